```python
import math
import jax
import jax.numpy as jnp
from jax import lax
import numpy as np

D_MODEL = 2048
BATCH = 4
SEQ = 2048
DEPTH = 4
DEC_BATCH = 8
DEC_SEQ = 1
PAST_LEN = 16384
PAGE_SIZE = 128

N_MIXERS = 3
LAYER_MIXER = tuple(i % N_MIXERS for i in range(DEPTH))
N_NSA = LAYER_MIXER.count(0)
N_CONV = LAYER_MIXER.count(1)
N_SSM = LAYER_MIXER.count(2)

N_HEADS = 16
HEAD_DIM = D_MODEL // N_HEADS
KV_GROUPS = 4
GROUP_SIZE = N_HEADS // KV_GROUPS
N_BRANCH = 3
CMP_BLOCK = 32
CMP_STRIDE = 16
CMP_RATIO = CMP_BLOCK // CMP_STRIDE
SEL_BLOCK = 64
SEL_RATIO = SEL_BLOCK // CMP_STRIDE
N_SELECT = 16
WINDOW = 512
FORCE_BONUS = 1.0e4
SEL_Q_BLOCK = 16
WIN_Q_BLOCK = 128

N_BUCKETS = 32
MAX_EXACT = 16
REL_MAX_DIST = 128

CONV_WIDTH = 31

SSM_GROUP_CH = 16
SSM_GROUPS = D_MODEL // SSM_GROUP_CH
SSM_STATE = 64
DT_MIN = 0.001
DT_MAX = 0.1

D_FF = 5632
FFN_CONV_WIDTH = 3

RMS_EPS = 1e-6
LN_EPS = 1e-5

kernel_name = "nsa_conformer_s5_hybrid_step"


def rms_norm(x, g):
    xf = x.astype(jnp.float32)
    y = xf * lax.rsqrt(jnp.mean(xf * xf, axis=-1, keepdims=True) + RMS_EPS)
    return (y * g.astype(jnp.float32)).astype(x.dtype)


def layer_norm(x, g, b):
    xf = x.astype(jnp.float32)
    mu = jnp.mean(xf, axis=-1, keepdims=True)
    var = jnp.mean(jnp.square(xf - mu), axis=-1, keepdims=True)
    y = (xf - mu) * lax.rsqrt(var + LN_EPS) * g.astype(jnp.float32) + b.astype(jnp.float32)
    return y.astype(x.dtype)


def t5_bucket(rel):
    n = jnp.maximum(rel, 0)
    nf = jnp.maximum(n, MAX_EXACT).astype(jnp.float32)
    big = MAX_EXACT + (jnp.log(nf / MAX_EXACT) / math.log(REL_MAX_DIST / MAX_EXACT)
                       * (N_BUCKETS - MAX_EXACT)).astype(jnp.int32)
    return jnp.where(n < MAX_EXACT, n, jnp.minimum(big, N_BUCKETS - 1))


def masked_softmax(logits, mask):
    logits = jnp.where(mask, logits.astype(jnp.float32), -1e30)
    return jnp.where(mask, jax.nn.softmax(logits, axis=-1), 0.0)


def attend_shared(q, k, v, qpos, kpos, mask, rel_bias):
    nq, nk = q.shape[1], k.shape[1]
    bias = rel_bias[t5_bucket(qpos[:, None] - kpos[None, :])]
    bias = bias.reshape(nq, nk, KV_GROUPS, GROUP_SIZE).transpose(2, 3, 0, 1)
    logits = (jnp.einsum('bqgrd,bngd->bgrqn', q, k).astype(jnp.float32) * HEAD_DIM ** -0.5
              + bias.astype(jnp.float32))
    p = masked_softmax(logits, mask)
    o = jnp.einsum('bgrqn,bngd->bqgrd', p.astype(v.dtype), v)
    return o, p


def attend_gathered(q, kv, qpos, kpos, ok, rel_bias):
    g_idx = jnp.arange(KV_GROUPS)[None, :, None, None]
    bias = rel_bias.reshape(N_BUCKETS, KV_GROUPS, GROUP_SIZE)[t5_bucket(qpos[None, None, :, None] - kpos), g_idx]
    bias = jnp.moveaxis(bias, -1, 2)
    logits = (jnp.einsum('bqgrd,bgqnd->bgrqn', q, kv[..., 0, :]).astype(jnp.float32) * HEAD_DIM ** -0.5
              + bias.astype(jnp.float32))
    mask = (ok & (kpos <= qpos[None, None, :, None]))[:, :, None]
    p = masked_softmax(logits, mask)
    return jnp.einsum('bgrqn,bgqnd->bqgrd', p.astype(kv.dtype), kv[..., 1, :])


def window_mask(qpos, kpos):
    k, qq = kpos[None, :], qpos[:, None]
    return (k >= 0) & (k <= qq) & (k >= qq - WINDOW)


def compress_kv(kv, pe, w1, w2):
    b, l = kv.shape[:2]
    n_chunk = l // CMP_STRIDE
    n_blk = n_chunk - CMP_RATIO + 1
    ch = kv.reshape(b, n_chunk, CMP_STRIDE, KV_GROUPS, 2, HEAD_DIM)
    h = 0.0
    for j in range(CMP_RATIO):
        sl = slice(j * CMP_STRIDE, (j + 1) * CMP_STRIDE)
        h = h + jnp.einsum('bnsgcd,csde->bngce', ch[:, j:j + n_blk] + pe[sl][:, None], w1[:, sl])
    return jnp.einsum('bngce,ced->bngcd', jax.nn.gelu(h), w2)


def select_blocks(imp, qpos, n_sel_blocks):
    n_cmp = imp.shape[-1]
    coef = np.convolve(np.ones(SEL_RATIO), np.ones(CMP_RATIO)).astype(np.float32)
    imp_pad = jnp.pad(imp, ((0, 0), (0, 0), (0, 0), (CMP_RATIO - 1, SEL_RATIO * n_sel_blocks - n_cmp)))
    gidx = SEL_RATIO * np.arange(n_sel_blocks)[:, None] + np.arange(coef.shape[0])[None, :]
    p_sel = jnp.einsum('bgqjo,o->bgqj', imp_pad[..., gidx], jnp.asarray(coef))
    j = jnp.arange(n_sel_blocks)[None, :]
    cur = (qpos // SEL_BLOCK)[:, None]
    valid = j <= cur
    forced = (j == 0) | (j == cur) | (j == cur - 1)
    score = jnp.where(valid, p_sel + jnp.where(forced, FORCE_BONUS, 0.0), -jnp.inf)
    vals, idx = lax.top_k(score, min(N_SELECT, n_sel_blocks))
    return idx, jnp.isfinite(vals)


def sel_positions(idx):
    pos = idx[..., None] * SEL_BLOCK + jnp.arange(SEL_BLOCK, dtype=idx.dtype)
    return pos.reshape(idx.shape[:-1] + (-1,))


def unblock(o):
    return jnp.moveaxis(o, 0, 1).reshape((o.shape[1], -1) + o.shape[3:])


def nsa_project(x, w_q, w_kv, w_gate):
    b, t, _ = x.shape
    q = (x @ w_q).reshape(b, t, KV_GROUPS, GROUP_SIZE, HEAD_DIM)
    kv = (x @ w_kv).reshape(b, t, N_BRANCH, KV_GROUPS, 2, HEAD_DIM)
    gate = jax.nn.sigmoid((x @ w_gate).astype(jnp.float32))
    gate = gate.reshape(b, t, KV_GROUPS, GROUP_SIZE, N_BRANCH).astype(x.dtype)
    return q, kv[:, :, 0], kv[:, :, 1], kv[:, :, 2], gate


def nsa_combine(o_cmp, o_slc, o_win, gate, w_o):
    b, t = o_cmp.shape[:2]
    o = o_cmp * gate[..., 0:1] + o_slc * gate[..., 1:2] + o_win * gate[..., 2:3]
    return o.reshape(b, t, N_HEADS * HEAD_DIM) @ w_o


def nsa_prompt(x, rel_bias, w_q, w_kv, cmp_pe, cmp_w1, cmp_w2, w_gate, w_o):
    b, t, _ = x.shape
    q, kv_cmp, kv_slc, kv_win, gate = nsa_project(x, w_q, w_kv, w_gate)
    qpos = jnp.arange(t, dtype=jnp.int32)
    kc = compress_kv(kv_cmp, cmp_pe, cmp_w1, cmp_w2)
    cpos = jnp.arange(kc.shape[1], dtype=jnp.int32) * CMP_STRIDE + (CMP_BLOCK - 1)
    o_cmp, p_cmp = attend_shared(q, kc[..., 0, :], kc[..., 1, :], qpos, cpos,
                                 cpos[None, :] <= qpos[:, None], rel_bias)
    idx, ok = select_blocks(p_cmp.sum(axis=2), qpos, -(-t // SEL_BLOCK))
    bi = jnp.arange(b)[:, None, None, None]
    gi = jnp.arange(KV_GROUPS)[None, :, None, None]

    def sel_chunk(c):
        s0 = c * SEL_Q_BLOCK
        qc = lax.dynamic_slice_in_dim(q, s0, SEL_Q_BLOCK, axis=1)
        ic = lax.dynamic_slice_in_dim(idx, s0, SEL_Q_BLOCK, axis=2)
        okc = lax.dynamic_slice_in_dim(ok, s0, SEL_Q_BLOCK, axis=2)
        kpos = sel_positions(ic)
        kv = kv_slc[bi, kpos, gi]
        return attend_gathered(qc, kv, s0 + jnp.arange(SEL_Q_BLOCK, dtype=jnp.int32), kpos,
                               jnp.repeat(okc, SEL_BLOCK, axis=-1), rel_bias)

    o_slc = unblock(lax.map(sel_chunk, jnp.arange(t // SEL_Q_BLOCK, dtype=jnp.int32)))
    kv_pad = jnp.pad(kv_win, ((0, 0), (WINDOW, 0), (0, 0), (0, 0), (0, 0)))
    span = WINDOW + WIN_Q_BLOCK

    def win_block(c):
        s0 = c * WIN_Q_BLOCK
        qc = lax.dynamic_slice_in_dim(q, s0, WIN_Q_BLOCK, axis=1)
        kvc = lax.dynamic_slice_in_dim(kv_pad, s0, span, axis=1)
        qp = s0 + jnp.arange(WIN_Q_BLOCK, dtype=jnp.int32)
        kp = s0 - WINDOW + jnp.arange(span, dtype=jnp.int32)
        return attend_shared(qc, kvc[..., 0, :], kvc[..., 1, :], qp, kp, window_mask(qp, kp), rel_bias)[0]

    o_win = unblock(lax.map(win_block, jnp.arange(t // WIN_Q_BLOCK, dtype=jnp.int32)))
    y = nsa_combine(o_cmp, o_slc, o_win, gate, w_o)
    return y, kv_cmp, kv_slc, kv_win[:, t - min(WINDOW, t):]


def nsa_sample(x, page_table, pool_cmp, pool_slc, win_buf, rel_bias,
               w_q, w_kv, cmp_pe, cmp_w1, cmp_w2, w_gate, w_o):
    b, s, _ = x.shape
    q, kv_cmp, kv_slc, kv_win, gate = nsa_project(x, w_q, w_kv, w_gate)
    qpos = PAST_LEN + jnp.arange(s, dtype=jnp.int32)
    total = PAST_LEN + s
    past_cmp = pool_cmp[page_table].reshape(b, PAST_LEN, KV_GROUPS, 2, HEAD_DIM)
    full = jnp.concatenate([past_cmp, kv_cmp], axis=1)
    full = jnp.pad(full, ((0, 0), (0, (-total) % CMP_STRIDE), (0, 0), (0, 0), (0, 0)))
    kc = compress_kv(full, cmp_pe, cmp_w1, cmp_w2)
    cpos = jnp.arange(kc.shape[1], dtype=jnp.int32) * CMP_STRIDE + (CMP_BLOCK - 1)
    o_cmp, p_cmp = attend_shared(q, kc[..., 0, :], kc[..., 1, :], qpos, cpos,
                                 cpos[None, :] <= qpos[:, None], rel_bias)
    idx, ok = select_blocks(p_cmp.sum(axis=2), qpos, -(-total // SEL_BLOCK))
    kpos = sel_positions(idx)
    bi = jnp.arange(b)[:, None, None, None]
    gi = jnp.arange(KV_GROUPS)[None, :, None, None]
    lp = jnp.minimum(kpos, PAST_LEN - 1)
    kv_past = pool_slc[page_table[bi, lp // PAGE_SIZE], lp % PAGE_SIZE, gi]
    kv_new = kv_slc[bi, jnp.clip(kpos - PAST_LEN, 0, s - 1), gi]
    kv_sel = jnp.where((kpos < PAST_LEN)[..., None, None], kv_past, kv_new)
    o_slc = attend_gathered(q, kv_sel, qpos, kpos, jnp.repeat(ok, SEL_BLOCK, axis=-1), rel_bias)
    win = jnp.concatenate([win_buf, kv_win], axis=1)
    kp = PAST_LEN - win_buf.shape[1] + jnp.arange(win.shape[1], dtype=jnp.int32)
    o_win = attend_shared(q, win[..., 0, :], win[..., 1, :], qpos, kp, window_mask(qpos, kp), rel_bias)[0]
    y = nsa_combine(o_cmp, o_slc, o_win, gate, w_o)
    return y, kv_cmp, kv_slc, win[:, win.shape[1] - min(WINDOW, total):]


def causal_dwconv(u, hist, w, bias):
    full = jnp.concatenate([hist, u], axis=1)
    y = lax.conv_general_dilated(full, w[:, None, :], window_strides=(1,), padding='VALID',
                                 dimension_numbers=('NWC', 'WIO', 'NWC'), feature_group_count=u.shape[-1])
    return y + bias, full[:, full.shape[1] - (w.shape[0] - 1):]


def conformer_conv(x, hist, w_pw1, dw, dw_b, ln_g, ln_b, w_pw2):
    a = x @ w_pw1
    u = a[..., :D_MODEL] * jax.nn.sigmoid(a[..., D_MODEL:])
    h, new_hist = causal_dwconv(u, hist, dw, dw_b)
    h = jax.nn.silu(layer_norm(h, ln_g, ln_b))
    return h @ w_pw2, new_hist


def s5_mixer(x, h_re, h_im, a_re, a_im, log_dt, b_re, b_im, c_re, c_im, d_skip, w_glu):
    bsz, t, _ = x.shape
    f32 = jnp.float32
    dt = jnp.exp(log_dt.astype(f32))[:, None]
    ar, ai = a_re.astype(f32), a_im.astype(f32)
    mag = jnp.exp(ar * dt)
    abar_re, abar_im = mag * jnp.cos(ai * dt), mag * jnp.sin(ai * dt)
    den = ar * ar + ai * ai
    coef_re = ((abar_re - 1.0) * ar + abar_im * ai) / den
    coef_im = (abar_im * ar - (abar_re - 1.0) * ai) / den
    br, bim = b_re.astype(f32), b_im.astype(f32)
    bb_re = coef_re[..., None] * br - coef_im[..., None] * bim
    bb_im = coef_re[..., None] * bim + coef_im[..., None] * br
    u = x.astype(f32).reshape(bsz, t, SSM_GROUPS, SSM_GROUP_CH)
    bu_re = jnp.einsum('btgc,gpc->tbgp', u, bb_re)
    bu_im = jnp.einsum('btgc,gpc->tbgp', u, bb_im)
    h0r, h0i = h_re.astype(f32), h_im.astype(f32)
    bu_re = bu_re.at[0].add(abar_re * h0r - abar_im * h0i)
    bu_im = bu_im.at[0].add(abar_re * h0i + abar_im * h0r)
    a_seq_re = jnp.broadcast_to(abar_re, (t, 1) + abar_re.shape)
    a_seq_im = jnp.broadcast_to(abar_im, (t, 1) + abar_im.shape)

    def combine(e1, e2):
        a1r, a1i, b1r, b1i = e1
        a2r, a2i, b2r, b2i = e2
        return (a2r * a1r - a2i * a1i, a2r * a1i + a2i * a1r,
                a2r * b1r - a2i * b1i + b2r, a2r * b1i + a2i * b1r + b2i)

    _, _, hs_re, hs_im = lax.associative_scan(combine, (a_seq_re, a_seq_im, bu_re, bu_im), axis=0)
    y = (jnp.einsum('tbgp,gcp->btgc', hs_re, c_re.astype(f32))
         - jnp.einsum('tbgp,gcp->btgc', hs_im, c_im.astype(f32)))
    y = y.reshape(bsz, t, D_MODEL) + d_skip.astype(f32) * x.astype(f32)
    z = y.astype(x.dtype) @ w_glu
    return z[..., :D_MODEL] * jax.nn.sigmoid(z[..., D_MODEL:]), hs_re[-1], hs_im[-1]


def conv_ffn(x, hist, w_up, dw, dw_b, w_down):
    up = x @ w_up
    g, new_hist = causal_dwconv(up[..., :D_FF], hist, dw, dw_b)
    return (jax.nn.gelu(g) * up[..., D_FF:]) @ w_down, new_hist


def setup_inputs(seed: int = 0) -> dict:
    key = jax.random.key(seed)
    ks = iter(jax.random.split(key, 64))

    def nrm(shape, scale):
        return scale * jax.random.normal(next(ks), shape, jnp.float32)

    n_pages = PAST_LEN // PAGE_SIZE
    n_used = DEC_BATCH * n_pages
    n_phys = n_used + max(1, n_used // 4)
    wb = min(WINDOW, PAST_LEN)
    D = D_MODEL
    inp = {}
    inp['x_prompt'] = nrm((BATCH, SEQ, D), 1.0)
    inp['x_sample'] = nrm((DEC_BATCH, DEC_SEQ, D), 1.0)
    inp['cache_cmp_kv'] = nrm((N_NSA, n_phys, PAGE_SIZE, KV_GROUPS, 2, HEAD_DIM), 1.0)
    inp['cache_slc_kv'] = nrm((N_NSA, n_phys, PAGE_SIZE, KV_GROUPS, 2, HEAD_DIM), 1.0)
    inp['cache_win_kv'] = nrm((N_NSA, DEC_BATCH, wb, KV_GROUPS, 2, HEAD_DIM), 1.0)
    inp['state_conv'] = nrm((N_CONV, DEC_BATCH, CONV_WIDTH - 1, D), 0.5)
    inp['state_ssm_re'] = nrm((N_SSM, DEC_BATCH, SSM_GROUPS, SSM_STATE), 0.3)
    inp['state_ssm_im'] = nrm((N_SSM, DEC_BATCH, SSM_GROUPS, SSM_STATE), 0.3)
    inp['state_ffn_conv'] = nrm((DEPTH, DEC_BATCH, FFN_CONV_WIDTH - 1, D_FF), 1.0)
    inp['page_table'] = jax.random.permutation(next(ks), n_phys)[:n_used].reshape(DEC_BATCH, n_pages).astype(jnp.int32)
    inp['norm_gain'] = 1.0 + nrm((DEPTH, 4, D), 0.05)
    inp['rel_bias'] = nrm((N_BUCKETS, N_HEADS), 0.5)
    inp['nsa_w_q'] = nrm((N_NSA, D, N_HEADS * HEAD_DIM), D ** -0.5)
    inp['nsa_w_kv'] = nrm((N_NSA, D, N_BRANCH * KV_GROUPS * 2 * HEAD_DIM), D ** -0.5)
    inp['nsa_cmp_pe'] = nrm((N_NSA, CMP_BLOCK, 2, HEAD_DIM), 0.5)
    inp['nsa_cmp_w1'] = nrm((N_NSA, 2, CMP_BLOCK, HEAD_DIM, HEAD_DIM), (CMP_BLOCK * HEAD_DIM) ** -0.5)
    inp['nsa_cmp_w2'] = nrm((N_NSA, 2, HEAD_DIM, HEAD_DIM), HEAD_DIM ** -0.5)
    inp['nsa_w_gate'] = nrm((N_NSA, D, N_HEADS * N_BRANCH), D ** -0.5)
    inp['nsa_w_o'] = nrm((N_NSA, N_HEADS * HEAD_DIM, D), (N_HEADS * HEAD_DIM) ** -0.5)
    inp['conv_w_pw1'] = nrm((N_CONV, D, 2 * D), D ** -0.5)
    inp['conv_dw'] = nrm((N_CONV, CONV_WIDTH, D), CONV_WIDTH ** -0.5)
    inp['conv_dw_b'] = nrm((N_CONV, D), 0.02)
    inp['conv_ln_g'] = 1.0 + nrm((N_CONV, D), 0.05)
    inp['conv_ln_b'] = nrm((N_CONV, D), 0.02)
    inp['conv_w_pw2'] = nrm((N_CONV, D, D), D ** -0.5)
    n_idx = jnp.arange(SSM_STATE, dtype=jnp.float32)
    inp['ssm_a_re'] = -0.5 + nrm((N_SSM, SSM_GROUPS, SSM_STATE), 0.01)
    inp['ssm_a_im'] = math.pi * n_idx + nrm((N_SSM, SSM_GROUPS, SSM_STATE), 0.01)
    inp['ssm_log_dt'] = jax.random.uniform(next(ks), (N_SSM, SSM_GROUPS), jnp.float32,
                                           math.log(DT_MIN), math.log(DT_MAX))
    inp['ssm_b_re'] = nrm((N_SSM, SSM_GROUPS, SSM_STATE, SSM_GROUP_CH), (2 * SSM_GROUP_CH) ** -0.5)
    inp['ssm_b_im'] = nrm((N_SSM, SSM_GROUPS, SSM_STATE, SSM_GROUP_CH), (2 * SSM_GROUP_CH) ** -0.5)
    inp['ssm_c_re'] = nrm((N_SSM, SSM_GROUPS, SSM_GROUP_CH, SSM_STATE), (2 * SSM_STATE) ** -0.5)
    inp['ssm_c_im'] = nrm((N_SSM, SSM_GROUPS, SSM_GROUP_CH, SSM_STATE), (2 * SSM_STATE) ** -0.5)
    inp['ssm_d'] = nrm((N_SSM, D), 1.0)
    inp['ssm_w_glu'] = nrm((N_SSM, D, 2 * D), D ** -0.5)
    inp['ffn_w_up'] = nrm((DEPTH, D, 2 * D_FF), D ** -0.5)
    inp['ffn_dw'] = nrm((DEPTH, FFN_CONV_WIDTH, D_FF), FFN_CONV_WIDTH ** -0.5)
    inp['ffn_dw_b'] = nrm((DEPTH, D_FF), 0.02)
    inp['ffn_w_down'] = nrm((DEPTH, D_FF, D), D_FF ** -0.5)
    return inp


def reference(x_prompt, x_sample, cache_cmp_kv, cache_slc_kv, cache_win_kv, state_conv, state_ssm_re,
              state_ssm_im, state_ffn_conv, page_table, norm_gain, rel_bias, nsa_w_q, nsa_w_kv, nsa_cmp_pe,
              nsa_cmp_w1, nsa_cmp_w2, nsa_w_gate, nsa_w_o, conv_w_pw1, conv_dw, conv_dw_b, conv_ln_g, conv_ln_b,
              conv_w_pw2, ssm_a_re, ssm_a_im, ssm_log_dt, ssm_b_re, ssm_b_im, ssm_c_re, ssm_c_im, ssm_d,
              ssm_w_glu, ffn_w_up, ffn_dw, ffn_dw_b, ffn_w_down):
    xp, xs = x_prompt, x_sample
    bp = xp.shape[0]
    cmp_p, cmp_s, slc_p, slc_s, win_p, win_s = [], [], [], [], [], []
    conv_p, conv_s = [], []
    ssm_re_p, ssm_re_s, ssm_im_p, ssm_im_s = [], [], [], []
    ffn_p, ffn_s = [], []
    for i in range(DEPTH):
        m = LAYER_MIXER[i]
        j = LAYER_MIXER[:i].count(m)
        hp = rms_norm(xp, norm_gain[i, 0])
        hs = rms_norm(xs, norm_gain[i, 0])
        if m == 0:
            w = (nsa_w_q[j], nsa_w_kv[j], nsa_cmp_pe[j], nsa_cmp_w1[j], nsa_cmp_w2[j], nsa_w_gate[j], nsa_w_o[j])
            mp, kc_p, ks_p, kw_p = nsa_prompt(hp, rel_bias, *w)
            ms, kc_s, ks_s, kw_s = nsa_sample(hs, page_table, cache_cmp_kv[j], cache_slc_kv[j],
                                              cache_win_kv[j], rel_bias, *w)
            cmp_p.append(kc_p); cmp_s.append(kc_s)
            slc_p.append(ks_p); slc_s.append(ks_s)
            win_p.append(kw_p); win_s.append(kw_s)
        elif m == 1:
            w = (conv_w_pw1[j], conv_dw[j], conv_dw_b[j], conv_ln_g[j], conv_ln_b[j], conv_w_pw2[j])
            mp, cp = conformer_conv(hp, jnp.zeros((bp, CONV_WIDTH - 1, D_MODEL), hp.dtype), *w)
            ms, cs = conformer_conv(hs, state_conv[j], *w)
            conv_p.append(cp); conv_s.append(cs)
        else:
            w = (ssm_a_re[j], ssm_a_im[j], ssm_log_dt[j], ssm_b_re[j], ssm_b_im[j], ssm_c_re[j], ssm_c_im[j],
                 ssm_d[j], ssm_w_glu[j])
            zero_state = jnp.zeros((bp, SSM_GROUPS, SSM_STATE), jnp.float32)
            mp, hr_p, hi_p = s5_mixer(hp, zero_state, zero_state, *w)
            ms, hr_s, hi_s = s5_mixer(hs, state_ssm_re[j], state_ssm_im[j], *w)
            ssm_re_p.append(hr_p); ssm_re_s.append(hr_s)
            ssm_im_p.append(hi_p); ssm_im_s.append(hi_s)
        xp = xp + rms_norm(mp, norm_gain[i, 1])
        xs = xs + rms_norm(ms, norm_gain[i, 1])
        wf = (ffn_w_up[i], ffn_dw[i], ffn_dw_b[i], ffn_w_down[i])
        fp, fh_p = conv_ffn(rms_norm(xp, norm_gain[i, 2]),
                            jnp.zeros((bp, FFN_CONV_WIDTH - 1, D_FF), xp.dtype), *wf)
        fs, fh_s = conv_ffn(rms_norm(xs, norm_gain[i, 2]), state_ffn_conv[i], *wf)
        ffn_p.append(fh_p); ffn_s.append(fh_s)
        xp = xp + rms_norm(fp, norm_gain[i, 3])
        xs = xs + rms_norm(fs, norm_gain[i, 3])
    return (xp, xs,
            jnp.stack(cmp_p), jnp.stack(cmp_s), jnp.stack(slc_p), jnp.stack(slc_s),
            jnp.stack(win_p), jnp.stack(win_s), jnp.stack(conv_p), jnp.stack(conv_s),
            jnp.stack(ssm_re_p), jnp.stack(ssm_re_s), jnp.stack(ssm_im_p), jnp.stack(ssm_im_s),
            jnp.stack(ffn_p), jnp.stack(ffn_s))
```

```python
import functools
import math

import numpy as np
import jax
import jax.numpy as jnp
from jax import lax
from jax.experimental import pallas as pl
from jax.experimental.pallas import tpu as pltpu

F32 = jnp.float32
BF16 = jnp.bfloat16

D_MODEL = 2048
DEPTH = 4
PAST_LEN = 16384
PAGE_SIZE = 128
N_HEADS = 16
HEAD_DIM = 128
KV_GROUPS = 4
GROUP_SIZE = 4
KV_COLS = KV_GROUPS * 2 * HEAD_DIM
CMP_BLOCK = 32
CMP_STRIDE = 16
SEL_BLOCK = 64
SEL_RATIO = SEL_BLOCK // CMP_STRIDE
N_SELECT = 16
WINDOW = 512
FORCE_BONUS = 1.0e4
N_BUCKETS = 32
MAX_EXACT = 16
REL_MAX_DIST = 128
CONV_WIDTH = 31
SSM_GROUP_CH = 16
SSM_GROUPS = 128
SSM_STATE = 64
SSM_DIM = SSM_GROUPS * SSM_STATE
D_FF = 5632
RMS_EPS = 1e-6
LN_EPS = 1e-5
SCALE = HEAD_DIM ** -0.5
NEG = -1e30

LANES = 128
SUBLANES = 8
VMEM_LIMIT = 56 * 1024 * 1024
ATT_TILE = 128
N_SLAB = SSM_DIM // LANES
SLAB_GRP = N_SLAB // SUBLANES

HIGHEST = lax.Precision.HIGHEST
NT_DIMS = (((1,), (1,)), ((), ()))


def _params(*sem):
    return pltpu.CompilerParams(dimension_semantics=sem, vmem_limit_bytes=VMEM_LIMIT)


def _rms(x, g):
    return x * lax.rsqrt(jnp.mean(x * x, axis=-1, keepdims=True) + RMS_EPS) * g


def _rmsnorm_kernel(x_ref, g_ref, o_ref):
    o_ref[...] = _rms(x_ref[...], g_ref[...]).astype(o_ref.dtype)


def rmsnorm_cast(x, g, tm):
    r, d = x.shape
    return pl.pallas_call(
        _rmsnorm_kernel,
        grid=(r // tm,),
        in_specs=[pl.BlockSpec((tm, d), lambda m: (m, 0)), pl.BlockSpec((1, d), lambda m: (0, 0))],
        out_specs=pl.BlockSpec((tm, d), lambda m: (m, 0)),
        out_shape=jax.ShapeDtypeStruct((r, d), BF16),
        compiler_params=_params("parallel"),
        name="rmsnorm",
    )(x, g.reshape(1, d))


def _mm_kernel(a_ref, w_ref, o_ref, *, act):
    r = jnp.dot(a_ref[...], w_ref[...], preferred_element_type=F32)
    if act == "sigmoid":
        r = jax.nn.sigmoid(r)
    o_ref[...] = r.astype(o_ref.dtype)


def matmul(a, w, *, n_split, out_dtype, act, tm, tn, name):
    r, k = a.shape
    n = w.shape[1]
    per = (n // n_split) // tn
    return pl.pallas_call(
        functools.partial(_mm_kernel, act=act),
        grid=(r // tm, n // tn),
        in_specs=[pl.BlockSpec((tm, k), lambda m, j: (m, 0)), pl.BlockSpec((k, tn), lambda m, j: (0, j))],
        out_specs=pl.BlockSpec((None, tm, tn), lambda m, j: (j // per, m, j % per)),
        out_shape=jax.ShapeDtypeStruct((n_split, r, n // n_split), out_dtype),
        compiler_params=_params("parallel", "arbitrary"),
        name=name,
    )(a, w)


def _glu_mm_kernel(a_ref, wa_ref, wb_ref, o_ref):
    a = a_ref[...]
    lin = jnp.dot(a, wa_ref[...], preferred_element_type=F32)
    gate = jnp.dot(a, wb_ref[...], preferred_element_type=F32)
    o_ref[...] = lin * jax.nn.sigmoid(gate)


def glu_matmul(a, w, *, tm, tn, name):
    r, k = a.shape
    n = w.shape[1] // 2
    nb = n // tn
    return pl.pallas_call(
        _glu_mm_kernel,
        grid=(r // tm, nb),
        in_specs=[pl.BlockSpec((tm, k), lambda m, j: (m, 0)),
                  pl.BlockSpec((k, tn), lambda m, j: (0, j)),
                  pl.BlockSpec((k, tn), lambda m, j: (0, j + nb))],
        out_specs=pl.BlockSpec((tm, tn), lambda m, j: (m, j)),
        out_shape=jax.ShapeDtypeStruct((r, n), F32),
        compiler_params=_params("parallel", "arbitrary"),
        name=name,
    )(a, w, w)


def _proj_res_kernel(a_ref, w_ref, x_ref, gp_ref, gn_ref, xo_ref, ho_ref, acc_ref, *, glu, nk):
    k = pl.program_id(1)

    @pl.when(k == 0)
    def _():
        acc_ref[...] = jnp.zeros_like(acc_ref)

    acc_ref[...] += jnp.dot(a_ref[...], w_ref[...], preferred_element_type=F32)

    @pl.when(k == nk - 1)
    def _():
        y = acc_ref[...]
        if glu:
            d = y.shape[1] // 2
            y = y[:, :d] * jax.nn.sigmoid(y[:, d:])
        xn = x_ref[...] + _rms(y, gp_ref[...])
        xo_ref[...] = xn
        ho_ref[...] = _rms(xn, gn_ref[...]).astype(ho_ref.dtype)


def proj_res(a, w, x, g_post, g_next, *, glu, tm, tk, name):
    r, kdim = a.shape
    n = w.shape[1]
    d = x.shape[1]
    nk = kdim // tk
    return pl.pallas_call(
        functools.partial(_proj_res_kernel, glu=glu, nk=nk),
        grid=(r // tm, nk),
        in_specs=[pl.BlockSpec((tm, tk), lambda m, k: (m, k)),
                  pl.BlockSpec((tk, n), lambda m, k: (k, 0)),
                  pl.BlockSpec((tm, d), lambda m, k: (m, 0)),
                  pl.BlockSpec((1, d), lambda m, k: (0, 0)),
                  pl.BlockSpec((1, d), lambda m, k: (0, 0))],
        out_specs=[pl.BlockSpec((tm, d), lambda m, k: (m, 0)),
                   pl.BlockSpec((tm, d), lambda m, k: (m, 0))],
        out_shape=[jax.ShapeDtypeStruct((r, d), F32), jax.ShapeDtypeStruct((r, d), BF16)],
        scratch_shapes=[pltpu.VMEM((tm, n), F32)],
        compiler_params=_params("parallel", "arbitrary"),
        name=name,
    )(a, w, x, g_post.reshape(1, d), g_next.reshape(1, d))


def _ffn_up_kernel(a_ref, p1_ref, p2_ref, wg_ref, wv_ref, dw_ref, db_ref, act_ref, hist_ref, *, decode, tiles_per_seq):
    a = a_ref[...]
    gate = jnp.dot(a, wg_ref[...], preferred_element_type=F32)
    val = jnp.dot(a, wv_ref[...], preferred_element_type=F32)
    tm = gate.shape[0]
    if decode:
        g2, g1 = p2_ref[...], p1_ref[...]
        hist_ref[...] = gate
    else:
        halo = jnp.dot(p1_ref[...], wg_ref[...], preferred_element_type=F32)
        halo = jnp.where(pl.program_id(0) % tiles_per_seq == 0, 0.0, halo)
        h7, h6 = halo[7:8, :], halo[6:7, :]
        row = lax.broadcasted_iota(jnp.int32, gate.shape, 0)
        g1 = jnp.where(row == 0, h7, pltpu.roll(gate, 1, 0))
        g2 = jnp.where(row == 0, h6, jnp.where(row == 1, h7, pltpu.roll(gate, 2, 0)))
        hist_ref[...] = gate[tm - SUBLANES:, :]
    g = dw_ref[0:1, :] * g2 + dw_ref[1:2, :] * g1 + dw_ref[2:3, :] * gate + db_ref[...]
    act_ref[...] = (jax.nn.gelu(g) * val).astype(act_ref.dtype)


def ffn_up(h, w_up, dw, db, *, hist=None, seq_len, tm, tn):
    r, k = h.shape
    nb = D_FF // tn
    decode = hist is not None
    if decode:
        p1, p2 = hist[:, 1, :], hist[:, 0, :]
        p_specs = [pl.BlockSpec((tm, tn), lambda m, j: (m, j)), pl.BlockSpec((tm, tn), lambda m, j: (m, j))]
        hrows = tm
    else:
        p1 = p2 = h
        blk = tm // SUBLANES
        p_specs = [pl.BlockSpec((SUBLANES, k), lambda m, j: (jnp.maximum(m * blk - 1, 0), 0)),
                   pl.BlockSpec((SUBLANES, k), lambda m, j: (0, 0))]
        hrows = SUBLANES
    return pl.pallas_call(
        functools.partial(_ffn_up_kernel, decode=decode, tiles_per_seq=max(seq_len // tm, 1)),
        grid=(r // tm, nb),
        in_specs=[pl.BlockSpec((tm, k), lambda m, j: (m, 0))] + p_specs + [
            pl.BlockSpec((k, tn), lambda m, j: (0, j)),
            pl.BlockSpec((k, tn), lambda m, j: (0, j + nb)),
            pl.BlockSpec((3, tn), lambda m, j: (0, j)),
            pl.BlockSpec((1, tn), lambda m, j: (0, j))],
        out_specs=[pl.BlockSpec((tm, tn), lambda m, j: (m, j)),
                   pl.BlockSpec((None, hrows, tn), lambda m, j: (m, 0, j))],
        out_shape=[jax.ShapeDtypeStruct((r, D_FF), BF16),
                   jax.ShapeDtypeStruct((r // tm, hrows, D_FF), F32)],
        compiler_params=_params("parallel", "arbitrary"),
        name="ffn_up",
    )(h, p1, p2, w_up, w_up, dw, db.reshape(1, D_FF))


N_KV_SLABS = KV_GROUPS * 2


def _cmp_partial_slab(x_ref, c, pe_ref, w1_ref, n_chunks):
    acc_a = jnp.zeros((n_chunks, HEAD_DIM), F32)
    acc_b = jnp.zeros((n_chunks, HEAD_DIM), F32)
    for s in range(CMP_STRIDE):
        xs = x_ref[pl.ds(s, n_chunks, stride=CMP_STRIDE), :]
        xa = (xs + pe_ref[c, s:s + 1, :]).astype(BF16)
        xb = (xs + pe_ref[c, CMP_STRIDE + s:CMP_STRIDE + s + 1, :]).astype(BF16)
        acc_a += jnp.dot(xa, w1_ref[c, s], preferred_element_type=F32)
        acc_b += jnp.dot(xb, w1_ref[c, CMP_STRIDE + s], preferred_element_type=F32)
    return acc_a, acc_b


def _cmp1_kernel(x_ref, pe_ref, w1_ref, a_ref, b_ref, *, n_chunks):
    c = pl.program_id(1) % 2
    a_ref[...], b_ref[...] = _cmp_partial_slab(x_ref, c, pe_ref, w1_ref, n_chunks)


def cmp_partial_rows(x, pe, w1, rows):
    nb = x.shape[0] // rows
    n_chunks = rows // CMP_STRIDE
    out = jax.ShapeDtypeStruct((nb, n_chunks, KV_COLS), F32)
    return pl.pallas_call(
        functools.partial(_cmp1_kernel, n_chunks=n_chunks),
        grid=(nb, N_KV_SLABS),
        in_specs=[pl.BlockSpec((rows, HEAD_DIM), lambda b, sl: (b, sl)),
                  pl.BlockSpec(pe.shape, lambda b, sl: (0, 0, 0)),
                  pl.BlockSpec(w1.shape, lambda b, sl: (0, 0, 0, 0))],
        out_specs=[pl.BlockSpec((None, n_chunks, HEAD_DIM), lambda b, sl: (b, 0, sl))] * 2,
        out_shape=[out, out],
        compiler_params=_params("parallel", "arbitrary"),
        name="cmp_partial",
    )(x, pe, w1)


PAGES_PER_STEP = 4
PAGES_PER_GROUP = 16
STEPS_PER_GROUP = PAGES_PER_GROUP // PAGES_PER_STEP
GROUP_ROWS = PAGES_PER_GROUP * PAGE_SIZE


def _cmp1_paged_kernel(pt_ref, *refs):
    page_refs = refs[:PAGES_PER_STEP]
    pe_ref, w1_ref, a_ref, b_ref, buf_ref = refs[PAGES_PER_STEP:]
    slot = pl.program_id(1) % STEPS_PER_GROUP
    for i, p_ref in enumerate(page_refs):
        start = pl.multiple_of((slot * PAGES_PER_STEP + i) * PAGE_SIZE, PAGE_SIZE)
        for slab in range(N_KV_SLABS):
            buf_ref[slab, pl.ds(start, PAGE_SIZE), :] = p_ref[:, slab * HEAD_DIM:(slab + 1) * HEAD_DIM]

    @pl.when(slot == STEPS_PER_GROUP - 1)
    def _():
        for slab in range(N_KV_SLABS):
            cols = slice(slab * HEAD_DIM, (slab + 1) * HEAD_DIM)
            a_ref[:, cols], b_ref[:, cols] = _cmp_partial_slab(buf_ref.at[slab], slab % 2, pe_ref, w1_ref,
                                                               GROUP_ROWS // CMP_STRIDE)


def cmp_partial_paged(pool, n_phys, page_table, layer, pe, w1):
    bsz, n_pages = page_table.shape
    n_steps = n_pages // PAGES_PER_STEP
    n_chunks = GROUP_ROWS // CMP_STRIDE
    out = jax.ShapeDtypeStruct((bsz, n_pages // PAGES_PER_GROUP, n_chunks, KV_COLS), F32)

    def page_spec(i):
        return pl.BlockSpec((None, PAGE_SIZE, KV_COLS),
                            lambda b, s, pt: (layer * n_phys + pt[b, s * PAGES_PER_STEP + i], 0, 0))

    out_spec = pl.BlockSpec((None, None, n_chunks, KV_COLS), lambda b, s, pt: (b, s // STEPS_PER_GROUP, 0, 0))
    a, b = pl.pallas_call(
        _cmp1_paged_kernel,
        grid_spec=pltpu.PrefetchScalarGridSpec(
            num_scalar_prefetch=1,
            grid=(bsz, n_steps),
            in_specs=[page_spec(i) for i in range(PAGES_PER_STEP)] + [
                pl.BlockSpec(pe.shape, lambda b, s, pt: (0, 0, 0)),
                pl.BlockSpec(w1.shape, lambda b, s, pt: (0, 0, 0, 0))],
            out_specs=[out_spec, out_spec],
            scratch_shapes=[pltpu.VMEM((N_KV_SLABS, GROUP_ROWS, HEAD_DIM), F32)]),
        out_shape=[out, out],
        compiler_params=_params("parallel", "arbitrary"),
        name="cmp_partial_paged",
    )(page_table, *([pool] * PAGES_PER_STEP), pe, w1)
    return a.reshape(bsz, -1, KV_COLS), b.reshape(bsz, -1, KV_COLS)


def _cmp2_kernel(a_ref, b_ref, bx_ref, w2_ref, o_ref):
    n = a_ref.shape[0]
    row = lax.broadcasted_iota(jnp.int32, (n, HEAD_DIM), 0)
    for slab in range(KV_GROUPS * 2):
        c = slab % 2
        cols = slice(slab * HEAD_DIM, (slab + 1) * HEAD_DIM)
        nxt = pltpu.roll(b_ref[:, cols], n - 1, 0)
        nxt = jnp.where(row == n - 1, bx_ref[0:1, cols], nxt)
        h = jax.nn.gelu(a_ref[:, cols] + nxt).astype(BF16)
        o_ref[:, cols] = jnp.dot(h, w2_ref[c], preferred_element_type=F32).astype(o_ref.dtype)


def cmp_finish(a, b, b_next, w2):
    nb, n, _ = a.shape
    return pl.pallas_call(
        _cmp2_kernel,
        grid=(nb,),
        in_specs=[pl.BlockSpec((None, n, KV_COLS), lambda i: (i, 0, 0)),
                  pl.BlockSpec((None, n, KV_COLS), lambda i: (i, 0, 0)),
                  pl.BlockSpec((None, SUBLANES, KV_COLS), lambda i: (i, 0, 0)),
                  pl.BlockSpec(w2.shape, lambda i: (0, 0, 0))],
        out_specs=pl.BlockSpec((None, n, KV_COLS), lambda i: (i, 0, 0)),
        out_shape=jax.ShapeDtypeStruct((nb, n, KV_COLS), BF16),
        compiler_params=_params("parallel"),
        name="cmp_finish",
    )(a, b, b_next, w2)


def _softmax_rows(s, mask):
    s = jnp.where(mask, s, NEG)
    m = jnp.max(s, axis=-1, keepdims=True)
    e = jnp.where(mask, jnp.exp(s - m), 0.0)
    l = jnp.sum(e, axis=-1, keepdims=True)
    return jnp.where(l > 0.0, e / jnp.where(l > 0.0, l, 1.0), 0.0)


def _cmp_attn_kernel(q_ref, kc_ref, bias_ref, msel_ref, exp_ref, o_ref, mask_ref, *, n_sel):
    qt = pl.program_id(1)
    tq, n_cmp = q_ref.shape[0], kc_ref.shape[0]
    seq = mask_ref.shape[-1]
    qpos = qt * tq + lax.broadcasted_iota(jnp.int32, (tq, n_cmp), 0)
    cpos = lax.broadcasted_iota(jnp.int32, (tq, n_cmp), 1) * CMP_STRIDE + (CMP_BLOCK - 1)
    cmask = cpos <= qpos
    qrow = qt * tq + lax.broadcasted_iota(jnp.int32, (tq, n_sel), 0)
    j = lax.broadcasted_iota(jnp.int32, (tq, n_sel), 1)
    cur = qrow // SEL_BLOCK
    valid = j <= cur
    forced = (j == 0) | (j == cur) | (j == cur - 1)
    key = lax.broadcasted_iota(jnp.int32, (tq, seq), 1)
    causal = key <= qt * tq + lax.broadcasted_iota(jnp.int32, (tq, seq), 0)
    for g in range(KV_GROUPS):
        k = kc_ref[:, g * 2 * HEAD_DIM:(g * 2 + 1) * HEAD_DIM]
        v = kc_ref[:, (g * 2 + 1) * HEAD_DIM:(g * 2 + 2) * HEAD_DIM]
        imp = jnp.zeros((tq, n_cmp), F32)
        for r in range(GROUP_SIZE):
            h = g * GROUP_SIZE + r
            cols = slice(h * HEAD_DIM, (h + 1) * HEAD_DIM)
            s = lax.dot_general(q_ref[:, cols], k, NT_DIMS, preferred_element_type=F32) * SCALE + bias_ref[h]
            p = _softmax_rows(s, cmask)
            o_ref[:, cols] = jnp.dot(p.astype(BF16), v, preferred_element_type=F32)
            imp = imp + p
        p_sel = jnp.dot(imp, msel_ref[...], precision=HIGHEST, preferred_element_type=F32)
        score = jnp.where(valid, p_sel + jnp.where(forced, FORCE_BONUS, 0.0), -jnp.inf)
        rank = jnp.zeros((tq, n_sel), jnp.int32)
        for i in range(n_sel):
            si = score[:, i:i + 1]
            beats = (si > score) | ((si == score) & (i < j))
            rank = rank + beats.astype(jnp.int32)
        sel = jnp.where((rank < N_SELECT) & valid, 1.0, 0.0).astype(BF16)
        allowed = jnp.dot(sel, exp_ref[...], preferred_element_type=F32)
        mask_ref[g] = jnp.where(causal & (allowed > 0.5), 1.0, 0.0).astype(mask_ref.dtype)


def cmp_attention(q, kc, bias_cmp, msel, expand, bsz, seq):
    tq = ATT_TILE
    nqt = seq // tq
    n_cmp = kc.shape[1]
    n_sel = msel.shape[1]
    return pl.pallas_call(
        functools.partial(_cmp_attn_kernel, n_sel=n_sel),
        grid=(bsz, nqt),
        in_specs=[pl.BlockSpec((tq, D_MODEL), lambda b, t: (b * nqt + t, 0)),
                  pl.BlockSpec((None, n_cmp, KV_COLS), lambda b, t: (b, 0, 0)),
                  pl.BlockSpec((N_HEADS, tq, n_cmp), lambda b, t: (0, t, 0)),
                  pl.BlockSpec(msel.shape, lambda b, t: (0, 0)),
                  pl.BlockSpec(expand.shape, lambda b, t: (0, 0))],
        out_specs=[pl.BlockSpec((tq, D_MODEL), lambda b, t: (b * nqt + t, 0)),
                   pl.BlockSpec((None, KV_GROUPS, tq, seq), lambda b, t: (b, 0, t, 0))],
        out_shape=[jax.ShapeDtypeStruct((bsz * seq, D_MODEL), F32),
                   jax.ShapeDtypeStruct((bsz, KV_GROUPS, seq, seq), BF16)],
        compiler_params=_params("parallel", "arbitrary"),
        name="cmp_attention",
    )(q, kc, bias_cmp, msel, expand)


def _flash_kernel(*refs, heads, window, n_steps):
    if window:
        q_ref, kv_ref, tb_ref, o_ref, m_ref, l_ref, acc_ref = refs
        qt, step = pl.program_id(1), pl.program_id(2)
        dist = n_steps - 1 - step
        active = qt - dist >= 0
        head0 = 0
    else:
        q_ref, kv_ref, tb_ref, mask_ref, o_ref, m_ref, l_ref, acc_ref = refs
        g, qt, step = pl.program_id(1), pl.program_id(2), pl.program_id(3)
        dist = qt - step
        active = dist >= 0
        head0 = g * GROUP_SIZE
    tq = q_ref.shape[0]

    @pl.when(step == 0)
    def _():
        m_ref[...] = jnp.full_like(m_ref, NEG)
        l_ref[...] = jnp.zeros_like(l_ref)
        acc_ref[...] = jnp.zeros_like(acc_ref)

    @pl.when(active)
    def _():
        tbi = jnp.minimum(dist, 2)
        if window:
            rel = (lax.broadcasted_iota(jnp.int32, (tq, tq), 0) - lax.broadcasted_iota(jnp.int32, (tq, tq), 1)
                   + dist * tq)
            mask = (rel >= 0) & (rel <= WINDOW)
        else:
            mask = mask_ref[...].astype(F32) > 0.5
        for gl in range(heads // GROUP_SIZE):
            k = kv_ref[:, gl * 2 * HEAD_DIM:(gl * 2 + 1) * HEAD_DIM].astype(BF16)
            v = kv_ref[:, (gl * 2 + 1) * HEAD_DIM:(gl * 2 + 2) * HEAD_DIM].astype(BF16)
            for r in range(GROUP_SIZE):
                hh = gl * GROUP_SIZE + r
                cols = slice(hh * HEAD_DIM, (hh + 1) * HEAD_DIM)
                s = lax.dot_general(q_ref[:, cols], k, NT_DIMS, preferred_element_type=F32) * SCALE
                s = jnp.where(mask, s + tb_ref[tbi, head0 + hh], NEG)
                m_prev = m_ref[hh]
                m_new = jnp.maximum(m_prev, jnp.max(s, axis=-1, keepdims=True))
                alpha = jnp.exp(m_prev - m_new)
                p = jnp.where(mask, jnp.exp(s - m_new), 0.0)
                l_ref[hh] = alpha * l_ref[hh] + jnp.sum(p, axis=-1, keepdims=True)
                acc_ref[:, cols] = alpha * acc_ref[:, cols] + jnp.dot(p.astype(BF16), v, preferred_element_type=F32)
                m_ref[hh] = m_new

    @pl.when(step == n_steps - 1)
    def _():
        for hh in range(heads):
            cols = slice(hh * HEAD_DIM, (hh + 1) * HEAD_DIM)
            l = l_ref[hh]
            o_ref[:, cols] = jnp.where(l > 0.0, acc_ref[:, cols] / jnp.where(l > 0.0, l, 1.0), 0.0)


def _flash_scratch(heads):
    return [pltpu.VMEM((heads, ATT_TILE, 1), F32), pltpu.VMEM((heads, ATT_TILE, 1), F32),
            pltpu.VMEM((ATT_TILE, heads * HEAD_DIM), F32)]


def slc_attention(q, kv, mask, tb, bsz, seq):
    tq = ATT_TILE
    nqt = seq // tq
    gw = GROUP_SIZE * HEAD_DIM
    return pl.pallas_call(
        functools.partial(_flash_kernel, heads=GROUP_SIZE, window=False, n_steps=nqt),
        grid=(bsz, KV_GROUPS, nqt, nqt),
        in_specs=[pl.BlockSpec((tq, gw), lambda b, g, t, s: (b * nqt + t, g)),
                  pl.BlockSpec((tq, 2 * HEAD_DIM), lambda b, g, t, s: (b * nqt + jnp.minimum(s, t), g)),
                  pl.BlockSpec(tb.shape, lambda b, g, t, s: (0, 0, 0, 0)),
                  pl.BlockSpec((None, None, tq, tq), lambda b, g, t, s: (b, g, t, jnp.minimum(s, t)))],
        out_specs=pl.BlockSpec((tq, gw), lambda b, g, t, s: (b * nqt + t, g)),
        out_shape=jax.ShapeDtypeStruct((bsz * seq, D_MODEL), F32),
        scratch_shapes=_flash_scratch(GROUP_SIZE),
        compiler_params=_params("parallel", "parallel", "parallel", "arbitrary"),
        name="slc_attention",
    )(q, kv, tb, mask)


def win_attention(q, kv, tb, bsz, seq):
    tq = ATT_TILE
    nqt = seq // tq
    n_steps = WINDOW // tq + 1
    return pl.pallas_call(
        functools.partial(_flash_kernel, heads=N_HEADS, window=True, n_steps=n_steps),
        grid=(bsz, nqt, n_steps),
        in_specs=[pl.BlockSpec((tq, D_MODEL), lambda b, t, s: (b * nqt + t, 0)),
                  pl.BlockSpec((tq, KV_COLS), lambda b, t, s: (b * nqt + jnp.maximum(t - (n_steps - 1) + s, 0), 0)),
                  pl.BlockSpec(tb.shape, lambda b, t, s: (0, 0, 0, 0))],
        out_specs=pl.BlockSpec((tq, D_MODEL), lambda b, t, s: (b * nqt + t, 0)),
        out_shape=jax.ShapeDtypeStruct((bsz * seq, D_MODEL), F32),
        scratch_shapes=_flash_scratch(N_HEADS),
        compiler_params=_params("parallel", "parallel", "arbitrary"),
        name="win_attention",
    )(q, kv, tb)


def _combine_kernel(oc_ref, os_ref, ow_ref, g_ref, o_ref):
    gate = g_ref[...]
    for h in range(N_HEADS):
        cols = slice(h * HEAD_DIM, (h + 1) * HEAD_DIM)
        o = (oc_ref[:, cols] * gate[:, 3 * h:3 * h + 1] + os_ref[:, cols] * gate[:, 3 * h + 1:3 * h + 2]
             + ow_ref[:, cols] * gate[:, 3 * h + 2:3 * h + 3])
        o_ref[:, cols] = o.astype(o_ref.dtype)


def nsa_combine(o_cmp, o_slc, o_win, gate, tm):
    r = o_cmp.shape[0]
    spec = pl.BlockSpec((tm, D_MODEL), lambda m: (m, 0))
    return pl.pallas_call(
        _combine_kernel,
        grid=(r // tm,),
        in_specs=[spec, spec, spec, pl.BlockSpec((tm, LANES), lambda m: (m, 0))],
        out_specs=spec,
        out_shape=jax.ShapeDtypeStruct((r, D_MODEL), BF16),
        compiler_params=_params("parallel"),
        name="nsa_combine",
    )(o_cmp, o_slc, o_win, gate)


N_SEL_S = -(-(PAST_LEN + 1) // SEL_BLOCK)
N_SEL_S_PAD = 384
N_CMP_S = PAST_LEN // CMP_STRIDE


def _group_rows(parts, hgrp):
    out = parts[0]
    for g in range(1, KV_GROUPS):
        out = jnp.where(hgrp == g, parts[g], out)
    return out


def _sample_attn_kernel(q_ref, kc_ref, bc_ref, gsum_ref, msel_ref, win_ref, new_ref, bw_ref, bn_ref,
                        oc_ref, ow_ref, idx_ref):
    q = q_ref[...]
    hgrp = lax.broadcasted_iota(jnp.int32, (N_HEADS, 1), 0) // GROUP_SIZE

    def kcol(g):
        return slice(g * 2 * HEAD_DIM, (g * 2 + 1) * HEAD_DIM)

    def vcol(g):
        return slice((g * 2 + 1) * HEAD_DIM, (g * 2 + 2) * HEAD_DIM)

    s = _group_rows([lax.dot_general(q, kc_ref[:, kcol(g)], NT_DIMS, preferred_element_type=F32)
                     for g in range(KV_GROUPS)], hgrp)
    s = s * SCALE + bc_ref[...]
    n = lax.broadcasted_iota(jnp.int32, s.shape, 1)
    p = _softmax_rows(s, n * CMP_STRIDE + (CMP_BLOCK - 1) <= PAST_LEN)
    pb = p.astype(BF16)
    oc_ref[...] = _group_rows([jnp.dot(pb, kc_ref[:, vcol(g)], preferred_element_type=F32)
                               for g in range(KV_GROUPS)], hgrp)
    imp = jnp.dot(gsum_ref[...], p, precision=HIGHEST, preferred_element_type=F32)
    p_sel = jnp.dot(imp, msel_ref[...], precision=HIGHEST, preferred_element_type=F32)
    j = lax.broadcasted_iota(jnp.int32, p_sel.shape, 1)
    cur = PAST_LEN // SEL_BLOCK
    forced = (j == 0) | (j == cur) | (j == cur - 1)
    score = jnp.where(j <= cur, p_sel + jnp.where(forced, FORCE_BONUS, 0.0), -jnp.inf)
    lane = lax.broadcasted_iota(jnp.int32, idx_ref.shape, 1)
    idx = jnp.zeros(idx_ref.shape, F32)
    jf = j.astype(F32)
    for kk in range(N_SELECT):
        mx = jnp.max(score, axis=-1, keepdims=True)
        pick = jnp.min(jnp.where(score == mx, jf, float(N_SEL_S_PAD)), axis=-1, keepdims=True)
        idx = jnp.where(lane == kk, pick, idx)
        score = jnp.where(jf == pick, -jnp.inf, score)
    idx_ref[...] = idx.astype(jnp.int32)
    sw = _group_rows([lax.dot_general(q, win_ref[:, kcol(g)].astype(BF16), NT_DIMS, preferred_element_type=F32)
                      for g in range(KV_GROUPS)], hgrp)
    sw = sw * SCALE + bw_ref[...]
    qf = q.astype(F32)
    sn = _group_rows([jnp.sum(qf * new_ref[:, kcol(g)].astype(BF16).astype(F32), axis=-1, keepdims=True)
                      for g in range(KV_GROUPS)], hgrp)
    sn = sn * SCALE + bn_ref[:, 0:1]
    m = jnp.maximum(jnp.max(sw, axis=-1, keepdims=True), sn)
    ew, en = jnp.exp(sw - m), jnp.exp(sn - m)
    l = jnp.sum(ew, axis=-1, keepdims=True) + en
    pw, pn = (ew / l).astype(BF16), (en / l).astype(BF16).astype(F32)
    ow = _group_rows([jnp.dot(pw, win_ref[:, vcol(g)].astype(BF16), preferred_element_type=F32)
                      + pn * new_ref[:, vcol(g)].astype(BF16).astype(F32) for g in range(KV_GROUPS)], hgrp)
    ow_ref[...] = ow


def sample_attention(q3, kc, bias_c, gsum, msel, win_pool, layer, kv_win_new, bias_w, bias_new):
    bsz = q3.shape[0]
    wb = win_pool.shape[1]
    o = jax.ShapeDtypeStruct((bsz, N_HEADS, HEAD_DIM), F32)
    full2 = lambda b: (0, 0)
    return pl.pallas_call(
        _sample_attn_kernel,
        grid=(bsz,),
        in_specs=[pl.BlockSpec((None, N_HEADS, HEAD_DIM), lambda b: (b, 0, 0)),
                  pl.BlockSpec((None, N_CMP_S, KV_COLS), lambda b: (b, 0, 0)),
                  pl.BlockSpec(bias_c.shape, full2),
                  pl.BlockSpec(gsum.shape, full2),
                  pl.BlockSpec(msel.shape, full2),
                  pl.BlockSpec((None, wb, KV_COLS), lambda b: (layer * bsz + b, 0, 0)),
                  pl.BlockSpec((None, 1, KV_COLS), lambda b: (b, 0, 0)),
                  pl.BlockSpec(bias_w.shape, full2),
                  pl.BlockSpec(bias_new.shape, full2)],
        out_specs=[pl.BlockSpec((None, N_HEADS, HEAD_DIM), lambda b: (b, 0, 0)),
                   pl.BlockSpec((None, N_HEADS, HEAD_DIM), lambda b: (b, 0, 0)),
                   pl.BlockSpec((None, SUBLANES, LANES), lambda b: (b, 0, 0))],
        out_shape=[o, o, jax.ShapeDtypeStruct((bsz, SUBLANES, LANES), jnp.int32)],
        compiler_params=_params("parallel"),
        name="sample_attention",
    )(q3, kc, bias_c, gsum, msel, win_pool, kv_win_new, bias_w, bias_new)


def _sample_slc_kernel(idx_ref, pt_ref, q_ref, blk_ref, new_ref, bias_ref, o_ref, m_ref, l_ref, acc_ref):
    b, g, kk = pl.program_id(0), pl.program_id(1), pl.program_id(2)
    j = idx_ref[b, g, kk]
    is_new = j >= PAST_LEN // SEL_BLOCK

    @pl.when(kk == 0)
    def _():
        m_ref[...] = jnp.full_like(m_ref, NEG)
        l_ref[...] = jnp.zeros_like(l_ref)
        acc_ref[...] = jnp.zeros_like(acc_ref)

    new = jnp.broadcast_to(new_ref[...], blk_ref.shape)
    kv = jnp.where(is_new, new, blk_ref[...]).astype(BF16)
    k, v = kv[:, :HEAD_DIM], kv[:, HEAD_DIM:]
    s = lax.dot_general(q_ref[...], k, NT_DIMS, preferred_element_type=F32) * SCALE + bias_ref[...]
    kpos = j * SEL_BLOCK + lax.broadcasted_iota(jnp.int32, s.shape, 1)
    mask = kpos <= PAST_LEN
    s = jnp.where(mask, s, NEG)
    m_prev = m_ref[...]
    m_new = jnp.maximum(m_prev, jnp.max(s, axis=-1, keepdims=True))
    alpha = jnp.exp(m_prev - m_new)
    p = jnp.where(mask, jnp.exp(s - m_new), 0.0)
    l_ref[...] = alpha * l_ref[...] + jnp.sum(p, axis=-1, keepdims=True)
    acc_ref[...] = alpha * acc_ref[...] + jnp.dot(p.astype(BF16), v, preferred_element_type=F32)
    m_ref[...] = m_new

    @pl.when(kk == N_SELECT - 1)
    def _():
        o_ref[...] = acc_ref[...] / l_ref[...]


def sample_slc_attention(idx, page_table, q3, pool, n_phys, layer, kv_slc_new, bias_blk):
    bsz = q3.shape[0]
    half_per_page = PAGE_SIZE // SEL_BLOCK
    n_half = n_phys * half_per_page
    last_past = PAST_LEN // SEL_BLOCK - 1

    def blk_map(b, g, kk, idx_r, pt_r):
        j = jnp.minimum(idx_r[b, g, kk], last_past)
        return (layer * n_half + pt_r[b, j // half_per_page] * half_per_page + j % half_per_page, 0, g)

    return pl.pallas_call(
        _sample_slc_kernel,
        grid_spec=pltpu.PrefetchScalarGridSpec(
            num_scalar_prefetch=2,
            grid=(bsz, KV_GROUPS, N_SELECT),
            in_specs=[pl.BlockSpec((None, N_HEADS, HEAD_DIM), lambda b, g, kk, i, p: (b, 0, 0)),
                      pl.BlockSpec((None, SEL_BLOCK, 2 * HEAD_DIM), blk_map),
                      pl.BlockSpec((None, 1, 2 * HEAD_DIM), lambda b, g, kk, i, p: (b, 0, g)),
                      pl.BlockSpec((None, N_HEADS, SEL_BLOCK), lambda b, g, kk, i, p: (i[b, g, kk], 0, 0))],
            out_specs=pl.BlockSpec((None, None, N_HEADS, HEAD_DIM), lambda b, g, kk, i, p: (b, g, 0, 0)),
            scratch_shapes=[pltpu.VMEM((N_HEADS, 1), F32), pltpu.VMEM((N_HEADS, 1), F32),
                            pltpu.VMEM((N_HEADS, HEAD_DIM), F32)]),
        out_shape=jax.ShapeDtypeStruct((bsz, KV_GROUPS, N_HEADS, HEAD_DIM), F32),
        compiler_params=_params("parallel", "parallel", "arbitrary"),
        name="sample_slc_attention",
    )(idx, page_table, q3, pool, kv_slc_new, bias_blk)


CONV_TILE = 128
CONV_HALO = 32
CONV_ROWS = 64


def _dwconv_ln_kernel(u_ref, halo_ref, w_ref, b_ref, g_ref, beta_ref, o_ref, buf_ref, y_ref, *, tiles_per_seq):
    first = pl.program_id(0) % tiles_per_seq == 0
    buf_ref[0:CONV_HALO, :] = jnp.where(first, 0.0, halo_ref[...])
    buf_ref[CONV_HALO:, :] = u_ref[...]
    lead = CONV_HALO - (CONV_WIDTH - 1)
    for r0 in range(0, CONV_TILE, CONV_ROWS):
        for c0 in range(0, D_MODEL, LANES):
            cols = slice(c0, c0 + LANES)
            acc = jnp.broadcast_to(b_ref[:, cols], (CONV_ROWS, LANES))
            for k in range(CONV_WIDTH):
                acc = acc + w_ref[k:k + 1, cols] * buf_ref[pl.ds(r0 + lead + k, CONV_ROWS), cols]
            y_ref[r0:r0 + CONV_ROWS, cols] = acc
    y = y_ref[...]
    mu = jnp.mean(y, axis=-1, keepdims=True)
    var = jnp.mean(jnp.square(y - mu), axis=-1, keepdims=True)
    y = (y - mu) * lax.rsqrt(var + LN_EPS) * g_ref[...] + beta_ref[...]
    o_ref[...] = (y * jax.nn.sigmoid(y)).astype(o_ref.dtype)


def dwconv_ln(u, w, b, ln_g, ln_b, seq):
    r, d = u.shape
    tiles_per_seq = seq // CONV_TILE
    ratio = CONV_TILE // CONV_HALO
    vec = lambda m: (0, 0)
    return pl.pallas_call(
        functools.partial(_dwconv_ln_kernel, tiles_per_seq=tiles_per_seq),
        grid=(r // CONV_TILE,),
        in_specs=[pl.BlockSpec((CONV_TILE, d), lambda m: (m, 0)),
                  pl.BlockSpec((CONV_HALO, d), lambda m: (jnp.maximum(m * ratio - 1, 0), 0)),
                  pl.BlockSpec((CONV_WIDTH, d), vec), pl.BlockSpec((1, d), vec),
                  pl.BlockSpec((1, d), vec), pl.BlockSpec((1, d), vec)],
        out_specs=pl.BlockSpec((CONV_TILE, d), lambda m: (m, 0)),
        out_shape=jax.ShapeDtypeStruct((r, d), BF16),
        scratch_shapes=[pltpu.VMEM((CONV_HALO + CONV_TILE, d), F32), pltpu.VMEM((CONV_TILE, d), F32)],
        compiler_params=_params("parallel"),
        name="dwconv_ln",
    )(u, u, w, b.reshape(1, d), ln_g.reshape(1, d), ln_b.reshape(1, d))


def _dwconv_ln_decode_kernel(u_ref, hist_ref, w_ref, b_ref, g_ref, beta_ref, o_ref):
    y = b_ref[...] + w_ref[CONV_WIDTH - 1:CONV_WIDTH, :] * u_ref[...]
    for k in range(CONV_WIDTH - 1):
        y = y + w_ref[k:k + 1, :] * hist_ref[k]
    mu = jnp.mean(y, axis=-1, keepdims=True)
    var = jnp.mean(jnp.square(y - mu), axis=-1, keepdims=True)
    y = (y - mu) * lax.rsqrt(var + LN_EPS) * g_ref[...] + beta_ref[...]
    o_ref[...] = (y * jax.nn.sigmoid(y)).astype(o_ref.dtype)


def dwconv_ln_decode(u, hist_t, w, b, ln_g, ln_b):
    r, d = u.shape
    vec = lambda i: (0, 0)
    return pl.pallas_call(
        _dwconv_ln_decode_kernel,
        grid=(1,),
        in_specs=[pl.BlockSpec((r, d), vec), pl.BlockSpec(hist_t.shape, lambda i: (0, 0, 0)),
                  pl.BlockSpec((CONV_WIDTH, d), vec), pl.BlockSpec((1, d), vec),
                  pl.BlockSpec((1, d), vec), pl.BlockSpec((1, d), vec)],
        out_specs=pl.BlockSpec((r, d), vec),
        out_shape=jax.ShapeDtypeStruct((r, d), BF16),
        compiler_params=_params("arbitrary"),
        name="dwconv_ln_decode",
    )(u, hist_t, w, b.reshape(1, d), ln_g.reshape(1, d), ln_b.reshape(1, d))


S5_CHUNK = 256
IN_SLABS = D_MODEL // LANES
STATE_PER_IN = N_SLAB // IN_SLABS


def _s5_project_in(hb, wb_ref, store):
    half = STATE_PER_IN * LANES
    for i in range(IN_SLABS):
        res = jnp.dot(hb[:, i * LANES:(i + 1) * LANES], wb_ref[i], preferred_element_type=F32)
        for jj in range(STATE_PER_IN):
            store(i * STATE_PER_IN + jj, res[:, jj * LANES:(jj + 1) * LANES],
                  res[:, half + jj * LANES:half + (jj + 1) * LANES])


def _s5_project_out(load, cre_ref, cim_ref, hn, d_ref, y_ref):
    for i in range(IN_SLABS):
        cols = slice(i * LANES, (i + 1) * LANES)
        acc = d_ref[:, cols] * hn[:, cols]
        for jj in range(STATE_PER_IN):
            j = i * STATE_PER_IN + jj
            re, im = load(j)
            acc = acc + jnp.dot(re.astype(BF16), cre_ref[j], preferred_element_type=F32)
            acc = acc - jnp.dot(im.astype(BF16), cim_ref[j], preferred_element_type=F32)
        y_ref[:, cols] = acc.astype(y_ref.dtype)


def _s5_scan_kernel(x_ref, g_ref, wb_ref, ar_ref, ai_ref, cre_ref, cim_ref, d_ref,
                    y_ref, sr_ref, si_ref, bur_ref, bui_ref, hr_ref, hi_ref):
    tc = S5_CHUNK

    @pl.when(pl.program_id(1) == 0)
    def _():
        hr_ref[...] = jnp.zeros_like(hr_ref)
        hi_ref[...] = jnp.zeros_like(hi_ref)

    hn = _rms(x_ref[...], g_ref[...])

    def store(j, re, im):
        bur_ref[j * tc:(j + 1) * tc, :] = re
        bui_ref[j * tc:(j + 1) * tc, :] = im

    _s5_project_in(hn.astype(BF16), wb_ref, store)

    ar = [ar_ref[j8] for j8 in range(SLAB_GRP)]
    ai = [ai_ref[j8] for j8 in range(SLAB_GRP)]

    def step(t, carry):
        out = []
        for j8 in range(SLAB_GRP):
            hr, hi = carry[2 * j8], carry[2 * j8 + 1]
            rows = pl.ds(j8 * SUBLANES * tc + t, SUBLANES, stride=tc)
            nr = ar[j8] * hr - ai[j8] * hi + bur_ref[rows, :]
            ni = ar[j8] * hi + ai[j8] * hr + bui_ref[rows, :]
            bur_ref[rows, :] = nr
            bui_ref[rows, :] = ni
            out += [nr, ni]
        return tuple(out)

    init = []
    for j8 in range(SLAB_GRP):
        init += [hr_ref[j8], hi_ref[j8]]
    fin = lax.fori_loop(0, tc, step, tuple(init))
    for j8 in range(SLAB_GRP):
        hr_ref[j8] = fin[2 * j8]
        hi_ref[j8] = fin[2 * j8 + 1]
    sr_ref[...] = hr_ref[...]
    si_ref[...] = hi_ref[...]

    def load(j):
        return bur_ref[j * tc:(j + 1) * tc, :], bui_ref[j * tc:(j + 1) * tc, :]

    _s5_project_out(load, cre_ref, cim_ref, hn, d_ref, y_ref)


def s5_scan(x, g, wb, ar, ai, cre, cim, d_skip, bsz, seq):
    n_chunks = seq // S5_CHUNK
    st = jax.ShapeDtypeStruct((bsz, SLAB_GRP, SUBLANES, LANES), F32)
    st_spec = pl.BlockSpec((None, SLAB_GRP, SUBLANES, LANES), lambda b, c: (b, 0, 0, 0))
    vec = lambda b, c: (0, 0)
    c3 = lambda b, c: (0, 0, 0)
    return pl.pallas_call(
        _s5_scan_kernel,
        grid=(bsz, n_chunks),
        in_specs=[pl.BlockSpec((S5_CHUNK, D_MODEL), lambda b, c: (b * n_chunks + c, 0)),
                  pl.BlockSpec((1, D_MODEL), vec),
                  pl.BlockSpec(wb.shape, c3), pl.BlockSpec(ar.shape, c3), pl.BlockSpec(ai.shape, c3),
                  pl.BlockSpec(cre.shape, c3), pl.BlockSpec(cim.shape, c3),
                  pl.BlockSpec((1, D_MODEL), vec)],
        out_specs=[pl.BlockSpec((S5_CHUNK, D_MODEL), lambda b, c: (b * n_chunks + c, 0)), st_spec, st_spec],
        out_shape=[jax.ShapeDtypeStruct((bsz * seq, D_MODEL), BF16), st, st],
        scratch_shapes=[pltpu.VMEM((N_SLAB * S5_CHUNK, LANES), F32), pltpu.VMEM((N_SLAB * S5_CHUNK, LANES), F32),
                        pltpu.VMEM((SLAB_GRP, SUBLANES, LANES), F32), pltpu.VMEM((SLAB_GRP, SUBLANES, LANES), F32)],
        compiler_params=_params("parallel", "arbitrary"),
        name="s5_scan",
    )(x, g.reshape(1, -1), wb, ar, ai, cre, cim, d_skip.reshape(1, -1))


def _s5_decode_kernel(x_ref, g_ref, wb_ref, ar_ref, ai_ref, cre_ref, cim_ref, d_ref, h0r_ref, h0i_ref,
                      y_ref, sr_ref, si_ref):
    hn = _rms(x_ref[...], g_ref[...])

    def store(j, re, im):
        cols = slice(j * LANES, (j + 1) * LANES)
        ar, ai = ar_ref[:, cols], ai_ref[:, cols]
        hr, hi = h0r_ref[:, cols], h0i_ref[:, cols]
        sr_ref[:, cols] = ar * hr - ai * hi + re
        si_ref[:, cols] = ar * hi + ai * hr + im

    _s5_project_in(hn.astype(BF16), wb_ref, store)

    def load(j):
        cols = slice(j * LANES, (j + 1) * LANES)
        return sr_ref[:, cols], si_ref[:, cols]

    _s5_project_out(load, cre_ref, cim_ref, hn, d_ref, y_ref)


def s5_decode(x, g, wb, ar_row, ai_row, cre, cim, d_skip, h0r, h0i):
    r = x.shape[0]
    st = jax.ShapeDtypeStruct((r, SSM_DIM), F32)
    vec = lambda i: (0, 0)
    c3 = lambda i: (0, 0, 0)
    return pl.pallas_call(
        _s5_decode_kernel,
        grid=(1,),
        in_specs=[pl.BlockSpec((r, D_MODEL), vec), pl.BlockSpec((1, D_MODEL), vec),
                  pl.BlockSpec(wb.shape, c3), pl.BlockSpec((1, SSM_DIM), vec), pl.BlockSpec((1, SSM_DIM), vec),
                  pl.BlockSpec(cre.shape, c3), pl.BlockSpec(cim.shape, c3), pl.BlockSpec((1, D_MODEL), vec),
                  pl.BlockSpec((r, SSM_DIM), vec), pl.BlockSpec((r, SSM_DIM), vec)],
        out_specs=[pl.BlockSpec((r, D_MODEL), vec), pl.BlockSpec((r, SSM_DIM), vec), pl.BlockSpec((r, SSM_DIM), vec)],
        out_shape=[jax.ShapeDtypeStruct((r, D_MODEL), BF16), st, st],
        compiler_params=_params("arbitrary"),
        name="s5_decode",
    )(x, g.reshape(1, -1), wb, ar_row, ai_row, cre, cim, d_skip.reshape(1, -1), h0r, h0i)


def _t5_bucket(rel):
    n = jnp.maximum(rel, 0)
    nf = jnp.maximum(n, MAX_EXACT).astype(F32)
    big = MAX_EXACT + (jnp.log(nf / MAX_EXACT) / math.log(REL_MAX_DIST / MAX_EXACT)
                       * (N_BUCKETS - MAX_EXACT)).astype(jnp.int32)
    return jnp.where(n < MAX_EXACT, n, jnp.minimum(big, N_BUCKETS - 1))


def _bias_of(rel_bias, rel):
    return jnp.moveaxis(rel_bias[_t5_bucket(rel)], -1, 0)


def _selection_matrix(n_cmp, n_sel_pad):
    coef = np.convolve(np.ones(SEL_RATIO), np.ones(CMP_BLOCK // CMP_STRIDE)).astype(np.float32)
    m = np.zeros((n_cmp, n_sel_pad), np.float32)
    for j in range(n_sel_pad):
        for o in range(coef.shape[0]):
            n = SEL_RATIO * j + o - (CMP_BLOCK // CMP_STRIDE - 1)
            if 0 <= n < n_cmp:
                m[n, j] = coef[o]
    return m


def _s5_params(a_re, a_im, log_dt, b_re, b_im, c_re, c_im):
    dt = jnp.exp(log_dt.astype(F32))[:, None]
    ar, ai = a_re.astype(F32), a_im.astype(F32)
    mag = jnp.exp(ar * dt)
    abar_re, abar_im = mag * jnp.cos(ai * dt), mag * jnp.sin(ai * dt)
    den = ar * ar + ai * ai
    coef_re = ((abar_re - 1.0) * ar + abar_im * ai) / den
    coef_im = (abar_im * ar - (abar_re - 1.0) * ai) / den
    br, bim = b_re.astype(F32), b_im.astype(F32)
    bb_re = coef_re[..., None] * br - coef_im[..., None] * bim
    bb_im = coef_re[..., None] * bim + coef_im[..., None] * br
    gpi = LANES // SSM_GROUP_CH
    eye = jnp.eye(gpi, dtype=F32)

    def in_blocks(bb):
        t = bb.reshape(IN_SLABS, gpi, SSM_STATE, SSM_GROUP_CH)
        blk = jnp.einsum('sgpc,gh->sgchp', t, eye)
        return blk.reshape(IN_SLABS, LANES, gpi * SSM_STATE)

    wb = jnp.concatenate([in_blocks(bb_re), in_blocks(bb_im)], axis=-1).astype(BF16)
    gps = LANES // SSM_STATE
    ch_per_in = LANES

    def out_blocks(c):
        t = c.astype(F32).reshape(IN_SLABS, STATE_PER_IN, gps, SSM_GROUP_CH, SSM_STATE)
        sel = jnp.eye(STATE_PER_IN * gps, dtype=F32).reshape(STATE_PER_IN, gps, STATE_PER_IN * gps)
        blk = jnp.einsum('ijgcp,jgh->ijgphc', t, sel)
        return blk.reshape(N_SLAB, LANES, ch_per_in).astype(BF16)

    return abar_re, abar_im, wb, out_blocks(c_re), out_blocks(c_im)


PROMPT_TM = 512
TN = 512


def _row_tile(r):
    return PROMPT_TM if r % PROMPT_TM == 0 else r


def _nsa_project(h, wq, wkv, wg):
    tm = _row_tile(h.shape[0])
    q = matmul(h, wq, n_split=1, out_dtype=BF16, act=None, tm=tm, tn=TN, name="nsa_q")[0]
    kv = matmul(h, wkv, n_split=3, out_dtype=F32, act=None, tm=tm, tn=TN, name="nsa_kv")
    gate = matmul(h, wg, n_split=1, out_dtype=F32, act="sigmoid", tm=tm, tn=LANES, name="nsa_gate")[0]
    return q, kv, gate


def _nsa_layer(hp, hs, xp, xs, g_post, g_next, wts, tabs, caches, layer, page_table, bsz, seq):
    wq, wkv, wg, wo, pe, w1, w2 = wts
    cmp_pool, slc_pool, win_pool, n_phys = caches
    dec = hs.shape[0]
    q, kv, gate = _nsa_project(hp, wq, wkv, wg)
    part_a, part_b = cmp_partial_rows(kv[0], pe, w1, seq)
    kc = cmp_finish(part_a, part_b, jnp.zeros((bsz, SUBLANES, KV_COLS), F32), w2)
    o_cmp, mask = cmp_attention(q, kc, tabs["cmp"], tabs["msel"], tabs["expand"], bsz, seq)
    o_slc = slc_attention(q, kv[1], mask, tabs["tile"], bsz, seq)
    o_win = win_attention(q, kv[2], tabs["tile"], bsz, seq)
    o = nsa_combine(o_cmp, o_slc, o_win, gate, PROMPT_TM)
    xp, hp = proj_res(o, wo, xp, g_post, g_next, glu=False, tm=PROMPT_TM, tk=TN, name="nsa_out")
    qs, kvs, gate_s = _nsa_project(hs, wq, wkv, wg)
    past_a, past_b = cmp_partial_paged(cmp_pool, n_phys, page_table, layer, pe, w1)
    tail = jnp.pad(kvs[0][:, None, :], ((0, 0), (0, CMP_STRIDE - 1), (0, 0))).reshape(dec * CMP_STRIDE, KV_COLS)
    _, tail_b = cmp_partial_rows(tail, pe, w1, dec * CMP_STRIDE)
    b_next = jnp.pad(tail_b[0][:, None, :], ((0, 0), (0, SUBLANES - 1), (0, 0)))
    kc_s = cmp_finish(past_a, past_b, b_next, w2)
    oc_s, ow_s, idx = sample_attention(qs.reshape(dec, N_HEADS, HEAD_DIM), kc_s, tabs["cmp_s"], tabs["gsum"],
                                       tabs["msel_s"], win_pool, layer, kvs[2][:, None, :],
                                       tabs["win_s"], tabs["new_s"])
    q3 = qs.reshape(dec, N_HEADS, HEAD_DIM)
    os_all = sample_slc_attention(idx[:, :KV_GROUPS, :N_SELECT], page_table, q3, slc_pool, n_phys, layer,
                                  kvs[1][:, None, :], tabs["slc_s"])
    heads = np.arange(N_HEADS)
    os_s = os_all[:, heads // GROUP_SIZE, heads]
    o_s = nsa_combine(oc_s.reshape(dec, D_MODEL), os_s.reshape(dec, D_MODEL), ow_s.reshape(dec, D_MODEL), gate_s, dec)
    xs, hs = proj_res(o_s, wo, xs, g_post, g_next, glu=False, tm=dec, tk=TN, name="nsa_out_s")
    return xp, hp, xs, hs, kv, kvs


def _conv_layer(hp, hs, xp, xs, g_post, g_next, wts, state, bsz, seq):
    w_pw1, dw, dw_b, ln_g, ln_b, w_pw2 = wts
    dec = hs.shape[0]
    u = glu_matmul(hp, w_pw1, tm=PROMPT_TM, tn=TN, name="conv_pw1")
    hc = dwconv_ln(u, dw, dw_b, ln_g, ln_b, seq)
    xp, hp = proj_res(hc, w_pw2, xp, g_post, g_next, glu=False, tm=PROMPT_TM, tk=TN, name="conv_pw2")
    hist_p = u.reshape(bsz, seq, D_MODEL)[:, seq - (CONV_WIDTH - 1):]
    us = glu_matmul(hs, w_pw1, tm=dec, tn=TN, name="conv_pw1_s")
    hc_s = dwconv_ln_decode(us, jnp.swapaxes(state, 0, 1), dw, dw_b, ln_g, ln_b)
    xs, hs = proj_res(hc_s, w_pw2, xs, g_post, g_next, glu=False, tm=dec, tk=TN, name="conv_pw2_s")
    hist_s = jnp.concatenate([state[:, 1:], us[:, None, :]], axis=1)
    return xp, hp, xs, hs, hist_p, hist_s


def _s5_layer(xp, xs, g_pre, g_post, g_next, wts, state_re, state_im, bsz, seq):
    a_re, a_im, log_dt, b_re, b_im, c_re, c_im, d_skip, w_glu = wts
    dec = xs.shape[0]
    abar_re, abar_im, wb, cre, cim = _s5_params(a_re, a_im, log_dt, b_re, b_im, c_re, c_im)
    slab_shape = (SLAB_GRP, SUBLANES, LANES)
    y, sr, si = s5_scan(xp, g_pre, wb, abar_re.reshape(slab_shape), abar_im.reshape(slab_shape), cre, cim,
                        d_skip, bsz, seq)
    xp, hp = proj_res(y, w_glu, xp, g_post, g_next, glu=True, tm=PROMPT_TM, tk=256, name="s5_glu")
    ys, sr_s, si_s = s5_decode(xs, g_pre, wb, abar_re.reshape(1, SSM_DIM), abar_im.reshape(1, SSM_DIM), cre, cim,
                               d_skip, state_re.reshape(dec, SSM_DIM), state_im.reshape(dec, SSM_DIM))
    xs, hs = proj_res(ys, w_glu, xs, g_post, g_next, glu=True, tm=dec, tk=256, name="s5_glu_s")
    gp = (SSM_GROUPS, SSM_STATE)
    return (xp, hp, xs, hs, sr.reshape((bsz,) + gp), si.reshape((bsz,) + gp),
            sr_s.reshape((dec,) + gp), si_s.reshape((dec,) + gp))


def _ffn_layer(hp, hs, xp, xs, g_post, g_next, wts, state, bsz, seq):
    w_up, dw, dw_b, w_down = wts
    dec = hs.shape[0]
    act, hist = ffn_up(hp, w_up, dw, dw_b, seq_len=seq, tm=PROMPT_TM, tn=TN)
    xp, hp = proj_res(act, w_down, xp, g_post, g_next, glu=False, tm=PROMPT_TM, tk=TN, name="ffn_down")
    tiles = seq // PROMPT_TM
    hist_p = hist.reshape(bsz, tiles, SUBLANES, D_FF)[:, tiles - 1, SUBLANES - 2:, :]
    act_s, gate_s = ffn_up(hs, w_up, dw, dw_b, hist=state, seq_len=1, tm=dec, tn=TN)
    xs, hs = proj_res(act_s, w_down, xs, g_post, g_next, glu=False, tm=dec, tk=TN, name="ffn_down_s")
    hist_s = jnp.concatenate([state[:, 1:], gate_s.reshape(dec, 1, D_FF)], axis=1)
    return xp, hp, xs, hs, hist_p, hist_s


def _bias_tables(rel_bias, seq):
    tq = ATT_TILE
    i = jnp.arange(tq, dtype=jnp.int32)
    tile = jnp.stack([_bias_of(rel_bias, d * tq + i[:, None] - i[None, :]) for d in range(3)])
    n_cmp = seq // CMP_STRIDE
    cpos = jnp.arange(n_cmp, dtype=jnp.int32) * CMP_STRIDE + (CMP_BLOCK - 1)
    qpos = jnp.arange(seq, dtype=jnp.int32)
    n_sel = -(-seq // SEL_BLOCK)
    key = np.arange(seq)
    expand = (key[None, :] // SEL_BLOCK == np.arange(n_sel)[:, None]).astype(np.float32)
    cpos_s = jnp.arange(N_CMP_S, dtype=jnp.int32) * CMP_STRIDE + (CMP_BLOCK - 1)
    wb = min(WINDOW, PAST_LEN)
    kpos_s = jnp.arange(N_SEL_S * SEL_BLOCK, dtype=jnp.int32)
    slc_s = _bias_of(rel_bias, PAST_LEN - kpos_s).reshape(N_HEADS, N_SEL_S, SEL_BLOCK)
    gsum = (np.arange(N_HEADS)[None, :] // GROUP_SIZE == np.arange(SUBLANES)[:, None]).astype(np.float32)
    return {
        "tile": tile,
        "cmp": _bias_of(rel_bias, qpos[:, None] - cpos[None, :]),
        "msel": jnp.asarray(_selection_matrix(n_cmp, n_sel)),
        "expand": jnp.asarray(expand, BF16),
        "cmp_s": _bias_of(rel_bias, PAST_LEN - cpos_s),
        "msel_s": jnp.asarray(_selection_matrix(N_CMP_S, N_SEL_S_PAD)),
        "gsum": jnp.asarray(gsum),
        "win_s": _bias_of(rel_bias, wb - jnp.arange(wb, dtype=jnp.int32)),
        "new_s": jnp.broadcast_to(_bias_of(rel_bias, jnp.zeros((1,), jnp.int32)), (N_HEADS, LANES)),
        "slc_s": jnp.swapaxes(slc_s, 0, 1),
    }


def kernel(x_prompt, x_sample, cache_cmp_kv, cache_slc_kv, cache_win_kv, state_conv, state_ssm_re, state_ssm_im, state_ffn_conv, page_table, norm_gain, rel_bias, nsa_w_q, nsa_w_kv, nsa_cmp_pe, nsa_cmp_w1, nsa_cmp_w2, nsa_w_gate, nsa_w_o, conv_w_pw1, conv_dw, conv_dw_b, conv_ln_g, conv_ln_b, conv_w_pw2, ssm_a_re, ssm_a_im, ssm_log_dt, ssm_b_re, ssm_b_im, ssm_c_re, ssm_c_im, ssm_d, ssm_w_glu, ffn_w_up, ffn_dw, ffn_dw_b, ffn_w_down):
    bsz, seq, d = x_prompt.shape
    dec, dec_seq, _ = x_sample.shape
    assert dec_seq == 1 and d == D_MODEL and seq % PROMPT_TM == 0 and seq // CMP_STRIDE == LANES
    n_nsa = cache_cmp_kv.shape[0]
    n_phys = cache_cmp_kv.shape[1]
    xp = x_prompt.reshape(bsz * seq, d)
    xs = x_sample.reshape(dec, d)
    tabs = _bias_tables(rel_bias, seq)
    cmp_pool = cache_cmp_kv.reshape(n_nsa * n_phys, PAGE_SIZE, KV_COLS)
    slc_pool = cache_slc_kv.reshape(n_nsa * n_phys * (PAGE_SIZE // SEL_BLOCK), SEL_BLOCK, KV_COLS)
    win_pool = cache_win_kv.reshape(n_nsa * dec, cache_win_kv.shape[2], KV_COLS)
    g_all = norm_gain.astype(F32)

    hp = rmsnorm_cast(xp, g_all[0, 0], PROMPT_TM)
    hs = rmsnorm_cast(xs, g_all[0, 0], dec)
    out = {k: [] for k in ("cmp_p", "cmp_s", "slc_p", "slc_s", "win_p", "win_s", "conv_p", "conv_s",
                           "re_p", "re_s", "im_p", "im_s", "ffn_p", "ffn_s")}
    counts = [0, 0, 0]
    for i in range(DEPTH):
        m = i % 3
        j = counts[m]
        counts[m] += 1
        g_post, g_ffn, g_ffn_post = g_all[i, 1], g_all[i, 2], g_all[i, 3]
        g_next = g_all[i + 1, 0] if i + 1 < DEPTH else g_all[i, 0]
        if m == 0:
            wg = jnp.pad(nsa_w_gate[j], ((0, 0), (0, LANES - nsa_w_gate.shape[-1]))).astype(BF16)
            wts = (nsa_w_q[j].astype(BF16), nsa_w_kv[j].astype(BF16), wg, nsa_w_o[j].astype(BF16),
                   jnp.swapaxes(nsa_cmp_pe[j], 0, 1).astype(F32), nsa_cmp_w1[j].astype(BF16),
                   nsa_cmp_w2[j].astype(BF16))
            xp, hp, xs, hs, kv, kvs = _nsa_layer(hp, hs, xp, xs, g_post, g_ffn, wts, tabs,
                                                 (cmp_pool, slc_pool, win_pool, n_phys), j, page_table, bsz, seq)
            shp = (bsz, seq, KV_GROUPS, 2, HEAD_DIM)
            shs = (dec, 1, KV_GROUPS, 2, HEAD_DIM)
            out["cmp_p"].append(kv[0].reshape(shp)); out["cmp_s"].append(kvs[0].reshape(shs))
            out["slc_p"].append(kv[1].reshape(shp)); out["slc_s"].append(kvs[1].reshape(shs))
            out["win_p"].append(kv[2].reshape(shp)[:, seq - min(WINDOW, seq):])
            win_full = jnp.concatenate([cache_win_kv[j], kvs[2].reshape(shs)], axis=1)
            out["win_s"].append(win_full[:, win_full.shape[1] - min(WINDOW, PAST_LEN + 1):])
        elif m == 1:
            wts = (conv_w_pw1[j].astype(BF16), conv_dw[j], conv_dw_b[j], conv_ln_g[j], conv_ln_b[j],
                   conv_w_pw2[j].astype(BF16))
            xp, hp, xs, hs, cp, cs = _conv_layer(hp, hs, xp, xs, g_post, g_ffn, wts, state_conv[j], bsz, seq)
            out["conv_p"].append(cp); out["conv_s"].append(cs)
        else:
            wts = (ssm_a_re[j], ssm_a_im[j], ssm_log_dt[j], ssm_b_re[j], ssm_b_im[j], ssm_c_re[j], ssm_c_im[j],
                   ssm_d[j], ssm_w_glu[j].astype(BF16))
            xp, hp, xs, hs, rp, ip, rs, is_ = _s5_layer(xp, xs, g_all[i, 0], g_post, g_ffn, wts,
                                                        state_ssm_re[j], state_ssm_im[j], bsz, seq)
            out["re_p"].append(rp); out["im_p"].append(ip); out["re_s"].append(rs); out["im_s"].append(is_)
        wts = (ffn_w_up[i].astype(BF16), ffn_dw[i], ffn_dw_b[i], ffn_w_down[i].astype(BF16))
        xp, hp, xs, hs, fp, fs = _ffn_layer(hp, hs, xp, xs, g_ffn_post, g_next, wts, state_ffn_conv[i], bsz, seq)
        out["ffn_p"].append(fp); out["ffn_s"].append(fs)
    st = lambda k: jnp.stack(out[k])
    return (xp.reshape(bsz, seq, d), xs.reshape(dec, 1, d),
            st("cmp_p"), st("cmp_s"), st("slc_p"), st("slc_s"), st("win_p"), st("win_s"),
            st("conv_p"), st("conv_s"), st("re_p"), st("re_s"), st("im_p"), st("im_s"),
            st("ffn_p"), st("ffn_s"))
```

```python
import functools
import math

import numpy as np
import jax
import jax.numpy as jnp
from jax import lax
from jax.experimental import pallas as pl
from jax.experimental.pallas import tpu as pltpu

F32 = jnp.float32
BF16 = jnp.bfloat16

D_MODEL = 2048
DEPTH = 4
PAST_LEN = 16384
PAGE_SIZE = 128
N_HEADS = 16
HEAD_DIM = 128
KV_GROUPS = 4
GROUP_SIZE = 4
KV_COLS = KV_GROUPS * 2 * HEAD_DIM
CMP_BLOCK = 32
CMP_STRIDE = 16
SEL_BLOCK = 64
SEL_RATIO = SEL_BLOCK // CMP_STRIDE
N_SELECT = 16
WINDOW = 512
FORCE_BONUS = 1.0e4
N_BUCKETS = 32
MAX_EXACT = 16
REL_MAX_DIST = 128
CONV_WIDTH = 31
SSM_GROUP_CH = 16
SSM_GROUPS = 128
SSM_STATE = 64
SSM_DIM = SSM_GROUPS * SSM_STATE
D_FF = 5632
RMS_EPS = 1e-6
LN_EPS = 1e-5
SCALE = HEAD_DIM ** -0.5
NEG = -1e30

LANES = 128
SUBLANES = 8
VMEM_LIMIT = 56 * 1024 * 1024
ATT_TILE = 128
N_SLAB = SSM_DIM // LANES
SLAB_GRP = N_SLAB // SUBLANES

HIGHEST = lax.Precision.HIGHEST
NT_DIMS = (((1,), (1,)), ((), ()))


def _params(*sem):
    return pltpu.CompilerParams(dimension_semantics=sem, vmem_limit_bytes=VMEM_LIMIT)


def _rms(x, g):
    return x * lax.rsqrt(jnp.mean(x * x, axis=-1, keepdims=True) + RMS_EPS) * g


def _rmsnorm_kernel(x_ref, g_ref, o_ref):
    o_ref[...] = _rms(x_ref[...], g_ref[...]).astype(o_ref.dtype)


def rmsnorm_cast(x, g, tm):
    r, d = x.shape
    return pl.pallas_call(
        _rmsnorm_kernel,
        grid=(r // tm,),
        in_specs=[pl.BlockSpec((tm, d), lambda m: (m, 0)), pl.BlockSpec((1, d), lambda m: (0, 0))],
        out_specs=pl.BlockSpec((tm, d), lambda m: (m, 0)),
        out_shape=jax.ShapeDtypeStruct((r, d), BF16),
        compiler_params=_params("parallel"),
        name="rmsnorm",
    )(x, g.reshape(1, d))


def _mm_kernel(a_ref, w_ref, o_ref, *, act):
    r = jnp.dot(a_ref[...], w_ref[...], preferred_element_type=F32)
    if act == "sigmoid":
        r = jax.nn.sigmoid(r)
    o_ref[...] = r.astype(o_ref.dtype)


def matmul(a, w, *, n_split, out_dtype, act, tm, tn, name):
    r, k = a.shape
    n = w.shape[1]
    per = (n // n_split) // tn
    return pl.pallas_call(
        functools.partial(_mm_kernel, act=act),
        grid=(r // tm, n // tn),
        in_specs=[pl.BlockSpec((tm, k), lambda m, j: (m, 0)), pl.BlockSpec((k, tn), lambda m, j: (0, j))],
        out_specs=pl.BlockSpec((None, tm, tn), lambda m, j: (j // per, m, j % per)),
        out_shape=jax.ShapeDtypeStruct((n_split, r, n // n_split), out_dtype),
        compiler_params=_params("parallel", "arbitrary"),
        name=name,
    )(a, w)


def _glu_mm_kernel(a_ref, wa_ref, wb_ref, o_ref):
    a = a_ref[...]
    lin = jnp.dot(a, wa_ref[...], preferred_element_type=F32)
    gate = jnp.dot(a, wb_ref[...], preferred_element_type=F32)
    o_ref[...] = lin * jax.nn.sigmoid(gate)


def glu_matmul(a, w, *, tm, tn, name):
    r, k = a.shape
    n = w.shape[1] // 2
    nb = n // tn
    return pl.pallas_call(
        _glu_mm_kernel,
        grid=(r // tm, nb),
        in_specs=[pl.BlockSpec((tm, k), lambda m, j: (m, 0)),
                  pl.BlockSpec((k, tn), lambda m, j: (0, j)),
                  pl.BlockSpec((k, tn), lambda m, j: (0, j + nb))],
        out_specs=pl.BlockSpec((tm, tn), lambda m, j: (m, j)),
        out_shape=jax.ShapeDtypeStruct((r, n), F32),
        compiler_params=_params("parallel", "arbitrary"),
        name=name,
    )(a, w, w)


def _proj_res_kernel(a_ref, w_ref, x_ref, gp_ref, gn_ref, xo_ref, ho_ref, acc_ref, *, glu, nk):
    k = pl.program_id(1)

    @pl.when(k == 0)
    def _():
        acc_ref[...] = jnp.zeros_like(acc_ref)

    acc_ref[...] += jnp.dot(a_ref[...], w_ref[...], preferred_element_type=F32)

    @pl.when(k == nk - 1)
    def _():
        y = acc_ref[...]
        if glu:
            d = y.shape[1] // 2
            y = y[:, :d] * jax.nn.sigmoid(y[:, d:])
        xn = x_ref[...] + _rms(y, gp_ref[...])
        xo_ref[...] = xn
        ho_ref[...] = _rms(xn, gn_ref[...]).astype(ho_ref.dtype)


def proj_res(a, w, x, g_post, g_next, *, glu, tm, tk, name):
    r, kdim = a.shape
    n = w.shape[1]
    d = x.shape[1]
    nk = kdim // tk
    return pl.pallas_call(
        functools.partial(_proj_res_kernel, glu=glu, nk=nk),
        grid=(r // tm, nk),
        in_specs=[pl.BlockSpec((tm, tk), lambda m, k: (m, k)),
                  pl.BlockSpec((tk, n), lambda m, k: (k, 0)),
                  pl.BlockSpec((tm, d), lambda m, k: (m, 0)),
                  pl.BlockSpec((1, d), lambda m, k: (0, 0)),
                  pl.BlockSpec((1, d), lambda m, k: (0, 0))],
        out_specs=[pl.BlockSpec((tm, d), lambda m, k: (m, 0)),
                   pl.BlockSpec((tm, d), lambda m, k: (m, 0))],
        out_shape=[jax.ShapeDtypeStruct((r, d), F32), jax.ShapeDtypeStruct((r, d), BF16)],
        scratch_shapes=[pltpu.VMEM((tm, n), F32)],
        compiler_params=_params("parallel", "arbitrary"),
        name=name,
    )(a, w, x, g_post.reshape(1, d), g_next.reshape(1, d))


def _ffn_up_kernel(a_ref, p1_ref, p2_ref, wg_ref, wv_ref, dw_ref, db_ref, act_ref, hist_ref, *, decode, tiles_per_seq):
    a = a_ref[...]
    gate = jnp.dot(a, wg_ref[...], preferred_element_type=F32)
    val = jnp.dot(a, wv_ref[...], preferred_element_type=F32)
    tm = gate.shape[0]
    if decode:
        g2, g1 = p2_ref[...], p1_ref[...]
        hist_ref[...] = gate
    else:
        halo = jnp.dot(p1_ref[...], wg_ref[...], preferred_element_type=F32)
        halo = jnp.where(pl.program_id(0) % tiles_per_seq == 0, 0.0, halo)
        h7, h6 = halo[7:8, :], halo[6:7, :]
        row = lax.broadcasted_iota(jnp.int32, gate.shape, 0)
        g1 = jnp.where(row == 0, h7, pltpu.roll(gate, 1, 0))
        g2 = jnp.where(row == 0, h6, jnp.where(row == 1, h7, pltpu.roll(gate, 2, 0)))
        hist_ref[...] = gate[tm - SUBLANES:, :]
    g = dw_ref[0:1, :] * g2 + dw_ref[1:2, :] * g1 + dw_ref[2:3, :] * gate + db_ref[...]
    act_ref[...] = (jax.nn.gelu(g) * val).astype(act_ref.dtype)


def ffn_up(h, w_up, dw, db, *, hist=None, seq_len, tm, tn):
    r, k = h.shape
    nb = D_FF // tn
    decode = hist is not None
    if decode:
        p1, p2 = hist[:, 1, :], hist[:, 0, :]
        p_specs = [pl.BlockSpec((tm, tn), lambda m, j: (m, j)), pl.BlockSpec((tm, tn), lambda m, j: (m, j))]
        hrows = tm
    else:
        p1 = p2 = h
        blk = tm // SUBLANES
        p_specs = [pl.BlockSpec((SUBLANES, k), lambda m, j: (jnp.maximum(m * blk - 1, 0), 0)),
                   pl.BlockSpec((SUBLANES, k), lambda m, j: (0, 0))]
        hrows = SUBLANES
    return pl.pallas_call(
        functools.partial(_ffn_up_kernel, decode=decode, tiles_per_seq=max(seq_len // tm, 1)),
        grid=(r // tm, nb),
        in_specs=[pl.BlockSpec((tm, k), lambda m, j: (m, 0))] + p_specs + [
            pl.BlockSpec((k, tn), lambda m, j: (0, j)),
            pl.BlockSpec((k, tn), lambda m, j: (0, j + nb)),
            pl.BlockSpec((3, tn), lambda m, j: (0, j)),
            pl.BlockSpec((1, tn), lambda m, j: (0, j))],
        out_specs=[pl.BlockSpec((tm, tn), lambda m, j: (m, j)),
                   pl.BlockSpec((None, hrows, tn), lambda m, j: (m, 0, j))],
        out_shape=[jax.ShapeDtypeStruct((r, D_FF), BF16),
                   jax.ShapeDtypeStruct((r // tm, hrows, D_FF), F32)],
        compiler_params=_params("parallel", "arbitrary"),
        name="ffn_up",
    )(h, p1, p2, w_up, w_up, dw, db.reshape(1, D_FF))


N_KV_SLABS = KV_GROUPS * 2


def _cmp_partial_slab(x_ref, c, pe_ref, w1_ref, n_chunks, first=0, pitch=1, chunk_pitch=None):
    chunk_pitch = CMP_STRIDE * pitch if chunk_pitch is None else chunk_pitch
    acc_a = jnp.zeros((n_chunks, HEAD_DIM), F32)
    acc_b = jnp.zeros((n_chunks, HEAD_DIM), F32)
    for s in range(CMP_STRIDE):
        xs = x_ref[pl.ds(first + s * pitch, n_chunks, stride=chunk_pitch), :]
        xa = (xs + pe_ref[c, s:s + 1, :]).astype(BF16)
        xb = (xs + pe_ref[c, CMP_STRIDE + s:CMP_STRIDE + s + 1, :]).astype(BF16)
        acc_a += jnp.dot(xa, w1_ref[c, s], preferred_element_type=F32)
        acc_b += jnp.dot(xb, w1_ref[c, CMP_STRIDE + s], preferred_element_type=F32)
    return acc_a, acc_b


def _cmp1_kernel(x_ref, pe_ref, w1_ref, a_ref, b_ref, *, n_chunks):
    c = pl.program_id(1) % 2
    a_ref[...], b_ref[...] = _cmp_partial_slab(x_ref, c, pe_ref, w1_ref, n_chunks)


def cmp_partial_rows(x, pe, w1, rows):
    nb = x.shape[0] // rows
    n_chunks = rows // CMP_STRIDE
    out = jax.ShapeDtypeStruct((nb, n_chunks, KV_COLS), F32)
    return pl.pallas_call(
        functools.partial(_cmp1_kernel, n_chunks=n_chunks),
        grid=(nb, N_KV_SLABS),
        in_specs=[pl.BlockSpec((rows, HEAD_DIM), lambda b, sl: (b, sl)),
                  pl.BlockSpec(pe.shape, lambda b, sl: (0, 0, 0)),
                  pl.BlockSpec(w1.shape, lambda b, sl: (0, 0, 0, 0))],
        out_specs=[pl.BlockSpec((None, n_chunks, HEAD_DIM), lambda b, sl: (b, 0, sl))] * 2,
        out_shape=[out, out],
        compiler_params=_params("parallel", "arbitrary"),
        name="cmp_partial",
    )(x, pe, w1)


PAGES_PER_STEP = 16
CHUNKS_PER_PAGE = PAGE_SIZE // CMP_STRIDE
CHUNK_ROWS = CMP_STRIDE * N_KV_SLABS
CHUNK_PITCH = CHUNK_ROWS + 4
STEP_CHUNKS = PAGES_PER_STEP * CHUNKS_PER_PAGE


def _cmp1_paged_kernel(pt_ref, *refs):
    page_refs = refs[:PAGES_PER_STEP]
    pe_ref, w1_ref, a_ref, b_ref, buf_ref = refs[PAGES_PER_STEP:]
    for i, p_ref in enumerate(page_refs):
        for ch in range(CHUNKS_PER_PAGE):
            dst = (i * CHUNKS_PER_PAGE + ch) * CHUNK_PITCH
            buf_ref[dst:dst + CHUNK_ROWS, :] = p_ref[ch * CHUNK_ROWS:(ch + 1) * CHUNK_ROWS, :]
    for slab in range(N_KV_SLABS):
        cols = slice(slab * HEAD_DIM, (slab + 1) * HEAD_DIM)
        a_ref[:, cols], b_ref[:, cols] = _cmp_partial_slab(buf_ref, slab % 2, pe_ref, w1_ref, STEP_CHUNKS,
                                                           first=slab, pitch=N_KV_SLABS, chunk_pitch=CHUNK_PITCH)


def cmp_partial_paged(pool, n_phys, page_table, layer, pe, w1):
    bsz, n_pages = page_table.shape
    n_steps = n_pages // PAGES_PER_STEP
    n_chunks = STEP_CHUNKS
    out = jax.ShapeDtypeStruct((bsz, n_steps, n_chunks, KV_COLS), F32)

    def page_spec(i):
        return pl.BlockSpec((None, PAGE_SIZE * N_KV_SLABS, HEAD_DIM),
                            lambda b, s, pt: (layer * n_phys + pt[b, s * PAGES_PER_STEP + i], 0, 0))

    out_spec = pl.BlockSpec((None, None, n_chunks, KV_COLS), lambda b, s, pt: (b, s, 0, 0))
    a, b = pl.pallas_call(
        _cmp1_paged_kernel,
        grid_spec=pltpu.PrefetchScalarGridSpec(
            num_scalar_prefetch=1,
            grid=(bsz, n_steps),
            in_specs=[page_spec(i) for i in range(PAGES_PER_STEP)] + [
                pl.BlockSpec(pe.shape, lambda b, s, pt: (0, 0, 0)),
                pl.BlockSpec(w1.shape, lambda b, s, pt: (0, 0, 0, 0))],
            out_specs=[out_spec, out_spec],
            scratch_shapes=[pltpu.VMEM((STEP_CHUNKS * CHUNK_PITCH, HEAD_DIM), F32)]),
        out_shape=[out, out],
        compiler_params=_params("parallel", "arbitrary"),
        name="cmp_partial_paged",
    )(page_table, *([pool] * PAGES_PER_STEP), pe, w1)
    return a.reshape(bsz, -1, KV_COLS), b.reshape(bsz, -1, KV_COLS)


def _cmp2_kernel(a_ref, b_ref, bx_ref, w2_ref, o_ref):
    n = a_ref.shape[0]
    row = lax.broadcasted_iota(jnp.int32, (n, HEAD_DIM), 0)
    for slab in range(KV_GROUPS * 2):
        c = slab % 2
        cols = slice(slab * HEAD_DIM, (slab + 1) * HEAD_DIM)
        nxt = pltpu.roll(b_ref[:, cols], n - 1, 0)
        nxt = jnp.where(row == n - 1, bx_ref[0:1, cols], nxt)
        h = jax.nn.gelu(a_ref[:, cols] + nxt).astype(BF16)
        o_ref[:, cols] = jnp.dot(h, w2_ref[c], preferred_element_type=F32).astype(o_ref.dtype)


def cmp_finish(a, b, b_next, w2):
    nb, n, _ = a.shape
    return pl.pallas_call(
        _cmp2_kernel,
        grid=(nb,),
        in_specs=[pl.BlockSpec((None, n, KV_COLS), lambda i: (i, 0, 0)),
                  pl.BlockSpec((None, n, KV_COLS), lambda i: (i, 0, 0)),
                  pl.BlockSpec((None, SUBLANES, KV_COLS), lambda i: (i, 0, 0)),
                  pl.BlockSpec(w2.shape, lambda i: (0, 0, 0))],
        out_specs=pl.BlockSpec((None, n, KV_COLS), lambda i: (i, 0, 0)),
        out_shape=jax.ShapeDtypeStruct((nb, n, KV_COLS), BF16),
        compiler_params=_params("parallel"),
        name="cmp_finish",
    )(a, b, b_next, w2)


def _softmax_rows(s, mask):
    s = jnp.where(mask, s, NEG)
    m = jnp.max(s, axis=-1, keepdims=True)
    e = jnp.where(mask, jnp.exp(s - m), 0.0)
    l = jnp.sum(e, axis=-1, keepdims=True)
    return jnp.where(l > 0.0, e / jnp.where(l > 0.0, l, 1.0), 0.0)


def _cmp_attn_kernel(q_ref, kc_ref, bias_ref, msel_ref, exp_ref, o_ref, mask_ref, *, n_sel):
    qt = pl.program_id(1)
    tq, n_cmp = q_ref.shape[0], kc_ref.shape[0]
    seq = mask_ref.shape[-1]
    qpos = qt * tq + lax.broadcasted_iota(jnp.int32, (tq, n_cmp), 0)
    cpos = lax.broadcasted_iota(jnp.int32, (tq, n_cmp), 1) * CMP_STRIDE + (CMP_BLOCK - 1)
    cmask = cpos <= qpos
    qrow = qt * tq + lax.broadcasted_iota(jnp.int32, (tq, n_sel), 0)
    j = lax.broadcasted_iota(jnp.int32, (tq, n_sel), 1)
    cur = qrow // SEL_BLOCK
    valid = j <= cur
    forced = (j == 0) | (j == cur) | (j == cur - 1)
    key = lax.broadcasted_iota(jnp.int32, (tq, seq), 1)
    causal = key <= qt * tq + lax.broadcasted_iota(jnp.int32, (tq, seq), 0)
    for g in range(KV_GROUPS):
        k = kc_ref[:, g * 2 * HEAD_DIM:(g * 2 + 1) * HEAD_DIM]
        v = kc_ref[:, (g * 2 + 1) * HEAD_DIM:(g * 2 + 2) * HEAD_DIM]
        imp = jnp.zeros((tq, n_cmp), F32)
        for r in range(GROUP_SIZE):
            h = g * GROUP_SIZE + r
            cols = slice(h * HEAD_DIM, (h + 1) * HEAD_DIM)
            s = lax.dot_general(q_ref[:, cols], k, NT_DIMS, preferred_element_type=F32) * SCALE + bias_ref[h]
            p = _softmax_rows(s, cmask)
            o_ref[:, cols] = jnp.dot(p.astype(BF16), v, preferred_element_type=F32)
            imp = imp + p
        p_sel = jnp.dot(imp, msel_ref[...], precision=HIGHEST, preferred_element_type=F32)
        score = jnp.where(valid, p_sel + jnp.where(forced, FORCE_BONUS, 0.0), -jnp.inf)
        rank = jnp.zeros((tq, n_sel), jnp.int32)
        for i in range(n_sel):
            si = score[:, i:i + 1]
            beats = (si > score) | ((si == score) & (i < j))
            rank = rank + beats.astype(jnp.int32)
        sel = jnp.where((rank < N_SELECT) & valid, 1.0, 0.0).astype(BF16)
        allowed = jnp.dot(sel, exp_ref[...], preferred_element_type=F32)
        mask_ref[g] = jnp.where(causal & (allowed > 0.5), 0.0, NEG).astype(mask_ref.dtype)


def cmp_attention(q, kc, bias_cmp, msel, expand, bsz, seq):
    tq = ATT_TILE
    nqt = seq // tq
    n_cmp = kc.shape[1]
    n_sel = msel.shape[1]
    return pl.pallas_call(
        functools.partial(_cmp_attn_kernel, n_sel=n_sel),
        grid=(bsz, nqt),
        in_specs=[pl.BlockSpec((tq, D_MODEL), lambda b, t: (b * nqt + t, 0)),
                  pl.BlockSpec((None, n_cmp, KV_COLS), lambda b, t: (b, 0, 0)),
                  pl.BlockSpec((N_HEADS, tq, n_cmp), lambda b, t: (0, t, 0)),
                  pl.BlockSpec(msel.shape, lambda b, t: (0, 0)),
                  pl.BlockSpec(expand.shape, lambda b, t: (0, 0))],
        out_specs=[pl.BlockSpec((tq, D_MODEL), lambda b, t: (b * nqt + t, 0)),
                   pl.BlockSpec((None, KV_GROUPS, tq, seq), lambda b, t: (b, 0, t, 0))],
        out_shape=[jax.ShapeDtypeStruct((bsz * seq, D_MODEL), F32),
                   jax.ShapeDtypeStruct((bsz, KV_GROUPS, seq, seq), BF16)],
        compiler_params=_params("parallel", "arbitrary"),
        name="cmp_attention",
    )(q, kc, bias_cmp, msel, expand)


SLC_KEYS = 512
SLC_SUB = SLC_KEYS // ATT_TILE


def _slc_kernel(qt_ref, ks_ref, q_ref, kv_ref, tb_ref, mask_ref, o_ref, m_ref, l_ref, acc_ref):
    g, pair = pl.program_id(1), pl.program_id(2)
    qt, ks = qt_ref[pair], ks_ref[pair]

    @pl.when(ks == 0)
    def _():
        m_ref[...] = jnp.full_like(m_ref, NEG)
        l_ref[...] = jnp.zeros_like(l_ref)
        acc_ref[...] = jnp.zeros_like(acc_ref)

    madd = mask_ref[...].astype(F32)
    k = kv_ref[:, :HEAD_DIM].astype(BF16)
    v = kv_ref[:, HEAD_DIM:].astype(BF16)
    tbi = [jnp.clip(qt - (ks * SLC_SUB + c), 0, 2) for c in range(SLC_SUB)]
    q4 = jnp.concatenate([q_ref[:, r * HEAD_DIM:(r + 1) * HEAD_DIM] for r in range(GROUP_SIZE)], axis=0)
    badd = jnp.concatenate(
        [jnp.concatenate([tb_ref[tbi[c], g * GROUP_SIZE + r] for c in range(SLC_SUB)], axis=1) + madd
         for r in range(GROUP_SIZE)], axis=0)
    s = lax.dot_general(q4, k, NT_DIMS, preferred_element_type=F32) * SCALE + badd
    m_prev = m_ref[...]
    m_new = jnp.maximum(m_prev, jnp.max(s, axis=-1, keepdims=True))
    alpha = jnp.exp(m_prev - m_new)
    p = jnp.exp(s - m_new)
    l_ref[...] = alpha * l_ref[...] + jnp.sum(p, axis=-1, keepdims=True)
    acc_ref[...] = alpha * acc_ref[...] + jnp.dot(p.astype(BF16), v, preferred_element_type=F32)
    m_ref[...] = m_new

    @pl.when(ks == qt // SLC_SUB)
    def _():
        tq = q_ref.shape[0]
        l = l_ref[...]
        o = jnp.where(l > 0.0, acc_ref[...] / jnp.where(l > 0.0, l, 1.0), 0.0)
        for r in range(GROUP_SIZE):
            o_ref[:, r * HEAD_DIM:(r + 1) * HEAD_DIM] = o[r * tq:(r + 1) * tq, :]


def slc_attention(q, kv, mask, tb, bsz, seq):
    tq = ATT_TILE
    nqt = seq // tq
    nks = seq // SLC_KEYS
    gw = GROUP_SIZE * HEAD_DIM
    pairs = [(t, s) for t in range(nqt) for s in range(t // SLC_SUB + 1)]
    qt_of = jnp.asarray([p[0] for p in pairs], jnp.int32)
    ks_of = jnp.asarray([p[1] for p in pairs], jnp.int32)
    return pl.pallas_call(
        _slc_kernel,
        grid_spec=pltpu.PrefetchScalarGridSpec(
            num_scalar_prefetch=2,
            grid=(bsz, KV_GROUPS, len(pairs)),
            in_specs=[pl.BlockSpec((tq, gw), lambda b, g, p, qt, ks: (b * nqt + qt[p], g)),
                      pl.BlockSpec((SLC_KEYS, 2 * HEAD_DIM), lambda b, g, p, qt, ks: (b * nks + ks[p], g)),
                      pl.BlockSpec(tb.shape, lambda b, g, p, qt, ks: (0, 0, 0, 0)),
                      pl.BlockSpec((None, None, tq, SLC_KEYS), lambda b, g, p, qt, ks: (b, g, qt[p], ks[p]))],
            out_specs=pl.BlockSpec((tq, gw), lambda b, g, p, qt, ks: (b * nqt + qt[p], g)),
            scratch_shapes=[pltpu.VMEM((GROUP_SIZE * tq, 1), F32), pltpu.VMEM((GROUP_SIZE * tq, 1), F32),
                            pltpu.VMEM((GROUP_SIZE * tq, HEAD_DIM), F32)]),
        out_shape=jax.ShapeDtypeStruct((bsz * seq, D_MODEL), F32),
        compiler_params=_params("parallel", "parallel", "arbitrary"),
        name="slc_attention",
    )(qt_of, ks_of, q, kv, tb, mask)


WIN_TILES = WINDOW // ATT_TILE + 1
WIN_SPAN = WIN_TILES * ATT_TILE


def _win_kernel(q_ref, *refs):
    kv_refs, (bias_ref, o_ref) = refs[:WIN_TILES], refs[WIN_TILES:]
    qt = pl.program_id(1)
    tq = q_ref.shape[0]
    row = lax.broadcasted_iota(jnp.int32, (tq, WIN_SPAN), 0)
    col = lax.broadcasted_iota(jnp.int32, (tq, WIN_SPAN), 1)
    back = row + WINDOW - col
    mask = (back >= 0) & (back <= WINDOW) & (qt * tq - WINDOW + col >= 0)
    madd = jnp.where(mask, 0.0, NEG)
    for g in range(KV_GROUPS):
        k = jnp.concatenate([r[:, g * 2 * HEAD_DIM:(g * 2 + 1) * HEAD_DIM].astype(BF16) for r in kv_refs], axis=0)
        v = jnp.concatenate([r[:, (g * 2 + 1) * HEAD_DIM:(g * 2 + 2) * HEAD_DIM].astype(BF16) for r in kv_refs], axis=0)
        heads = range(g * GROUP_SIZE, (g + 1) * GROUP_SIZE)
        q4 = jnp.concatenate([q_ref[:, h * HEAD_DIM:(h + 1) * HEAD_DIM] for h in heads], axis=0)
        badd = jnp.concatenate([bias_ref[h] + madd for h in heads], axis=0)
        s = lax.dot_general(q4, k, NT_DIMS, preferred_element_type=F32) * SCALE + badd
        e = jnp.exp(s - jnp.max(s, axis=-1, keepdims=True))
        l = jnp.sum(e, axis=-1, keepdims=True)
        o = jnp.dot(e.astype(BF16), v, preferred_element_type=F32) / l
        for r, h in enumerate(heads):
            o_ref[:, h * HEAD_DIM:(h + 1) * HEAD_DIM] = o[r * tq:(r + 1) * tq, :]


def win_attention(q, kv, bias_win, bsz, seq):
    tq = ATT_TILE
    nqt = seq // tq

    def kv_spec(i):
        return pl.BlockSpec((tq, KV_COLS), lambda b, t: (b * nqt + jnp.maximum(t - (WIN_TILES - 1) + i, 0), 0))

    return pl.pallas_call(
        _win_kernel,
        grid=(bsz, nqt),
        in_specs=[pl.BlockSpec((tq, D_MODEL), lambda b, t: (b * nqt + t, 0))]
        + [kv_spec(i) for i in range(WIN_TILES)]
        + [pl.BlockSpec(bias_win.shape, lambda b, t: (0, 0, 0))],
        out_specs=pl.BlockSpec((tq, D_MODEL), lambda b, t: (b * nqt + t, 0)),
        out_shape=jax.ShapeDtypeStruct((bsz * seq, D_MODEL), F32),
        compiler_params=_params("parallel", "arbitrary"),
        name="win_attention",
    )(q, *([kv] * WIN_TILES), bias_win)


def _combine_kernel(oc_ref, os_ref, ow_ref, g_ref, o_ref):
    gate = g_ref[...]
    for h in range(N_HEADS):
        cols = slice(h * HEAD_DIM, (h + 1) * HEAD_DIM)
        o = (oc_ref[:, cols] * gate[:, 3 * h:3 * h + 1] + os_ref[:, cols] * gate[:, 3 * h + 1:3 * h + 2]
             + ow_ref[:, cols] * gate[:, 3 * h + 2:3 * h + 3])
        o_ref[:, cols] = o.astype(o_ref.dtype)


def nsa_combine(o_cmp, o_slc, o_win, gate, tm):
    r = o_cmp.shape[0]
    spec = pl.BlockSpec((tm, D_MODEL), lambda m: (m, 0))
    return pl.pallas_call(
        _combine_kernel,
        grid=(r // tm,),
        in_specs=[spec, spec, spec, pl.BlockSpec((tm, LANES), lambda m: (m, 0))],
        out_specs=spec,
        out_shape=jax.ShapeDtypeStruct((r, D_MODEL), BF16),
        compiler_params=_params("parallel"),
        name="nsa_combine",
    )(o_cmp, o_slc, o_win, gate)


N_SEL_S = -(-(PAST_LEN + 1) // SEL_BLOCK)
N_SEL_S_PAD = 384
N_CMP_S = PAST_LEN // CMP_STRIDE


def _group_rows(parts, hgrp):
    out = parts[0]
    for g in range(1, KV_GROUPS):
        out = jnp.where(hgrp == g, parts[g], out)
    return out


def _sample_attn_kernel(q_ref, kc_ref, bc_ref, gsum_ref, msel_ref, win_ref, new_ref, bw_ref, bn_ref,
                        oc_ref, ow_ref, idx_ref):
    q = q_ref[...]
    hgrp = lax.broadcasted_iota(jnp.int32, (N_HEADS, 1), 0) // GROUP_SIZE

    def kcol(g):
        return slice(g * 2 * HEAD_DIM, (g * 2 + 1) * HEAD_DIM)

    def vcol(g):
        return slice((g * 2 + 1) * HEAD_DIM, (g * 2 + 2) * HEAD_DIM)

    s = _group_rows([lax.dot_general(q, kc_ref[:, kcol(g)], NT_DIMS, preferred_element_type=F32)
                     for g in range(KV_GROUPS)], hgrp)
    s = s * SCALE + bc_ref[...]
    n = lax.broadcasted_iota(jnp.int32, s.shape, 1)
    p = _softmax_rows(s, n * CMP_STRIDE + (CMP_BLOCK - 1) <= PAST_LEN)
    pb = p.astype(BF16)
    oc_ref[...] = _group_rows([jnp.dot(pb, kc_ref[:, vcol(g)], preferred_element_type=F32)
                               for g in range(KV_GROUPS)], hgrp)
    imp = jnp.dot(gsum_ref[...], p, precision=HIGHEST, preferred_element_type=F32)
    p_sel = jnp.dot(imp, msel_ref[...], precision=HIGHEST, preferred_element_type=F32)
    j = lax.broadcasted_iota(jnp.int32, p_sel.shape, 1)
    cur = PAST_LEN // SEL_BLOCK
    forced = (j == 0) | (j == cur) | (j == cur - 1)
    score = jnp.where(j <= cur, p_sel + jnp.where(forced, FORCE_BONUS, 0.0), -jnp.inf)
    lane = lax.broadcasted_iota(jnp.int32, idx_ref.shape, 1)
    idx = jnp.zeros(idx_ref.shape, F32)
    jf = j.astype(F32)
    for kk in range(N_SELECT):
        mx = jnp.max(score, axis=-1, keepdims=True)
        pick = jnp.min(jnp.where(score == mx, jf, float(N_SEL_S_PAD)), axis=-1, keepdims=True)
        idx = jnp.where(lane == kk, pick, idx)
        score = jnp.where(jf == pick, -jnp.inf, score)
    idx_ref[...] = idx.astype(jnp.int32)
    wb = win_ref.shape[0] // N_KV_SLABS

    def win_slab(slab):
        return win_ref[pl.ds(slab, wb, stride=N_KV_SLABS), :].astype(BF16)

    sw = _group_rows([lax.dot_general(q, win_slab(2 * g), NT_DIMS, preferred_element_type=F32)
                      for g in range(KV_GROUPS)], hgrp)
    sw = sw * SCALE + bw_ref[...]
    qf = q.astype(F32)
    sn = _group_rows([jnp.sum(qf * new_ref[:, kcol(g)].astype(BF16).astype(F32), axis=-1, keepdims=True)
                      for g in range(KV_GROUPS)], hgrp)
    sn = sn * SCALE + bn_ref[:, 0:1]
    m = jnp.maximum(jnp.max(sw, axis=-1, keepdims=True), sn)
    ew, en = jnp.exp(sw - m), jnp.exp(sn - m)
    l = jnp.sum(ew, axis=-1, keepdims=True) + en
    pw, pn = (ew / l).astype(BF16), (en / l).astype(BF16).astype(F32)
    ow = _group_rows([jnp.dot(pw, win_slab(2 * g + 1), preferred_element_type=F32)
                      + pn * new_ref[:, vcol(g)].astype(BF16).astype(F32) for g in range(KV_GROUPS)], hgrp)
    ow_ref[...] = ow


def sample_attention(q3, kc, bias_c, gsum, msel, win_pool, layer, kv_win_new, bias_w, bias_new):
    bsz = q3.shape[0]
    wb = win_pool.shape[1]
    o = jax.ShapeDtypeStruct((bsz, N_HEADS, HEAD_DIM), F32)
    full2 = lambda b: (0, 0)
    return pl.pallas_call(
        _sample_attn_kernel,
        grid=(bsz,),
        in_specs=[pl.BlockSpec((None, N_HEADS, HEAD_DIM), lambda b: (b, 0, 0)),
                  pl.BlockSpec((None, N_CMP_S, KV_COLS), lambda b: (b, 0, 0)),
                  pl.BlockSpec(bias_c.shape, full2),
                  pl.BlockSpec(gsum.shape, full2),
                  pl.BlockSpec(msel.shape, full2),
                  pl.BlockSpec((None, wb, HEAD_DIM), lambda b: (layer * bsz + b, 0, 0)),
                  pl.BlockSpec((None, 1, KV_COLS), lambda b: (b, 0, 0)),
                  pl.BlockSpec(bias_w.shape, full2),
                  pl.BlockSpec(bias_new.shape, full2)],
        out_specs=[pl.BlockSpec((None, N_HEADS, HEAD_DIM), lambda b: (b, 0, 0)),
                   pl.BlockSpec((None, N_HEADS, HEAD_DIM), lambda b: (b, 0, 0)),
                   pl.BlockSpec((None, SUBLANES, LANES), lambda b: (b, 0, 0))],
        out_shape=[o, o, jax.ShapeDtypeStruct((bsz, SUBLANES, LANES), jnp.int32)],
        compiler_params=_params("parallel"),
        name="sample_attention",
    )(q3, kc, bias_c, gsum, msel, win_pool, kv_win_new, bias_w, bias_new)


def _sample_slc_kernel(idx_ref, pt_ref, q_ref, blk_ref, new_ref, bias_ref, o_ref, m_ref, l_ref, acc_ref):
    b, g, kk = pl.program_id(0), pl.program_id(1), pl.program_id(2)
    j = idx_ref[b, g, kk]
    is_new = j >= PAST_LEN // SEL_BLOCK

    @pl.when(kk == 0)
    def _():
        m_ref[...] = jnp.full_like(m_ref, NEG)
        l_ref[...] = jnp.zeros_like(l_ref)
        acc_ref[...] = jnp.zeros_like(acc_ref)

    k_blk = blk_ref[pl.ds(2 * g, SEL_BLOCK, stride=N_KV_SLABS), :]
    v_blk = blk_ref[pl.ds(2 * g + 1, SEL_BLOCK, stride=N_KV_SLABS), :]
    k = jnp.where(is_new, jnp.broadcast_to(new_ref[:, :HEAD_DIM], k_blk.shape), k_blk).astype(BF16)
    v = jnp.where(is_new, jnp.broadcast_to(new_ref[:, HEAD_DIM:], v_blk.shape), v_blk).astype(BF16)
    s = lax.dot_general(q_ref[...], k, NT_DIMS, preferred_element_type=F32) * SCALE + bias_ref[...]
    kpos = j * SEL_BLOCK + lax.broadcasted_iota(jnp.int32, s.shape, 1)
    mask = kpos <= PAST_LEN
    s = jnp.where(mask, s, NEG)
    m_prev = m_ref[...]
    m_new = jnp.maximum(m_prev, jnp.max(s, axis=-1, keepdims=True))
    alpha = jnp.exp(m_prev - m_new)
    p = jnp.where(mask, jnp.exp(s - m_new), 0.0)
    l_ref[...] = alpha * l_ref[...] + jnp.sum(p, axis=-1, keepdims=True)
    acc_ref[...] = alpha * acc_ref[...] + jnp.dot(p.astype(BF16), v, preferred_element_type=F32)
    m_ref[...] = m_new

    @pl.when(kk == N_SELECT - 1)
    def _():
        o_ref[...] = acc_ref[...] / l_ref[...]


def sample_slc_attention(idx, page_table, q3, pool, n_phys, layer, kv_slc_new, bias_blk):
    bsz = q3.shape[0]
    half_per_page = PAGE_SIZE // SEL_BLOCK
    n_half = n_phys * half_per_page
    last_past = PAST_LEN // SEL_BLOCK - 1

    def blk_map(b, g, kk, idx_r, pt_r):
        j = jnp.minimum(idx_r[b, g, kk], last_past)
        return (layer * n_half + pt_r[b, j // half_per_page] * half_per_page + j % half_per_page, 0, 0)

    return pl.pallas_call(
        _sample_slc_kernel,
        grid_spec=pltpu.PrefetchScalarGridSpec(
            num_scalar_prefetch=2,
            grid=(bsz, KV_GROUPS, N_SELECT),
            in_specs=[pl.BlockSpec((None, N_HEADS, HEAD_DIM), lambda b, g, kk, i, p: (b, 0, 0)),
                      pl.BlockSpec((None, SEL_BLOCK * N_KV_SLABS, HEAD_DIM), blk_map),
                      pl.BlockSpec((None, 1, 2 * HEAD_DIM), lambda b, g, kk, i, p: (b, 0, g)),
                      pl.BlockSpec((None, N_HEADS, SEL_BLOCK), lambda b, g, kk, i, p: (i[b, g, kk], 0, 0))],
            out_specs=pl.BlockSpec((None, None, N_HEADS, HEAD_DIM), lambda b, g, kk, i, p: (b, g, 0, 0)),
            scratch_shapes=[pltpu.VMEM((N_HEADS, 1), F32), pltpu.VMEM((N_HEADS, 1), F32),
                            pltpu.VMEM((N_HEADS, HEAD_DIM), F32)]),
        out_shape=jax.ShapeDtypeStruct((bsz, KV_GROUPS, N_HEADS, HEAD_DIM), F32),
        compiler_params=_params("parallel", "parallel", "arbitrary"),
        name="sample_slc_attention",
    )(idx, page_table, q3, pool, kv_slc_new, bias_blk)


CONV_TILE = 128
CONV_HALO = 32
CONV_ROWS = 64


def _dwconv_ln_kernel(u_ref, halo_ref, w_ref, b_ref, g_ref, beta_ref, o_ref, buf_ref, y_ref, *, tiles_per_seq):
    first = pl.program_id(0) % tiles_per_seq == 0
    buf_ref[0:CONV_HALO, :] = jnp.where(first, 0.0, halo_ref[...])
    buf_ref[CONV_HALO:, :] = u_ref[...]
    lead = CONV_HALO - (CONV_WIDTH - 1)
    for r0 in range(0, CONV_TILE, CONV_ROWS):
        for c0 in range(0, D_MODEL, LANES):
            cols = slice(c0, c0 + LANES)
            acc = jnp.broadcast_to(b_ref[:, cols], (CONV_ROWS, LANES))
            for k in range(CONV_WIDTH):
                acc = acc + w_ref[k:k + 1, cols] * buf_ref[pl.ds(r0 + lead + k, CONV_ROWS), cols]
            y_ref[r0:r0 + CONV_ROWS, cols] = acc
    y = y_ref[...]
    mu = jnp.mean(y, axis=-1, keepdims=True)
    var = jnp.mean(jnp.square(y - mu), axis=-1, keepdims=True)
    y = (y - mu) * lax.rsqrt(var + LN_EPS) * g_ref[...] + beta_ref[...]
    o_ref[...] = (y * jax.nn.sigmoid(y)).astype(o_ref.dtype)


def dwconv_ln(u, w, b, ln_g, ln_b, seq):
    r, d = u.shape
    tiles_per_seq = seq // CONV_TILE
    ratio = CONV_TILE // CONV_HALO
    vec = lambda m: (0, 0)
    return pl.pallas_call(
        functools.partial(_dwconv_ln_kernel, tiles_per_seq=tiles_per_seq),
        grid=(r // CONV_TILE,),
        in_specs=[pl.BlockSpec((CONV_TILE, d), lambda m: (m, 0)),
                  pl.BlockSpec((CONV_HALO, d), lambda m: (jnp.maximum(m * ratio - 1, 0), 0)),
                  pl.BlockSpec((CONV_WIDTH, d), vec), pl.BlockSpec((1, d), vec),
                  pl.BlockSpec((1, d), vec), pl.BlockSpec((1, d), vec)],
        out_specs=pl.BlockSpec((CONV_TILE, d), lambda m: (m, 0)),
        out_shape=jax.ShapeDtypeStruct((r, d), BF16),
        scratch_shapes=[pltpu.VMEM((CONV_HALO + CONV_TILE, d), F32), pltpu.VMEM((CONV_TILE, d), F32)],
        compiler_params=_params("parallel"),
        name="dwconv_ln",
    )(u, u, w, b.reshape(1, d), ln_g.reshape(1, d), ln_b.reshape(1, d))


def _dwconv_ln_decode_kernel(u_ref, hist_ref, w_ref, b_ref, g_ref, beta_ref, o_ref):
    y = b_ref[...] + w_ref[CONV_WIDTH - 1:CONV_WIDTH, :] * u_ref[...]
    for k in range(CONV_WIDTH - 1):
        y = y + w_ref[k:k + 1, :] * hist_ref[k]
    mu = jnp.mean(y, axis=-1, keepdims=True)
    var = jnp.mean(jnp.square(y - mu), axis=-1, keepdims=True)
    y = (y - mu) * lax.rsqrt(var + LN_EPS) * g_ref[...] + beta_ref[...]
    o_ref[...] = (y * jax.nn.sigmoid(y)).astype(o_ref.dtype)


def dwconv_ln_decode(u, hist_t, w, b, ln_g, ln_b):
    r, d = u.shape
    vec = lambda i: (0, 0)
    return pl.pallas_call(
        _dwconv_ln_decode_kernel,
        grid=(1,),
        in_specs=[pl.BlockSpec((r, d), vec), pl.BlockSpec(hist_t.shape, lambda i: (0, 0, 0)),
                  pl.BlockSpec((CONV_WIDTH, d), vec), pl.BlockSpec((1, d), vec),
                  pl.BlockSpec((1, d), vec), pl.BlockSpec((1, d), vec)],
        out_specs=pl.BlockSpec((r, d), vec),
        out_shape=jax.ShapeDtypeStruct((r, d), BF16),
        compiler_params=_params("arbitrary"),
        name="dwconv_ln_decode",
    )(u, hist_t, w, b.reshape(1, d), ln_g.reshape(1, d), ln_b.reshape(1, d))


S5_CHUNK = 256
S5_PITCH = S5_CHUNK + 4
IN_SLABS = D_MODEL // LANES
STATE_PER_IN = N_SLAB // IN_SLABS


def _s5_project_in(hb, wb_ref, store):
    half = STATE_PER_IN * LANES
    for i in range(IN_SLABS):
        res = jnp.dot(hb[:, i * LANES:(i + 1) * LANES], wb_ref[i], preferred_element_type=F32)
        for jj in range(STATE_PER_IN):
            store(i * STATE_PER_IN + jj, res[:, jj * LANES:(jj + 1) * LANES],
                  res[:, half + jj * LANES:half + (jj + 1) * LANES])


def _s5_project_out(load, cre_ref, cim_ref, hn, d_ref, y_ref):
    for i in range(IN_SLABS):
        cols = slice(i * LANES, (i + 1) * LANES)
        acc = d_ref[:, cols] * hn[:, cols]
        for jj in range(STATE_PER_IN):
            j = i * STATE_PER_IN + jj
            re, im = load(j)
            acc = acc + jnp.dot(re.astype(BF16), cre_ref[j], preferred_element_type=F32)
            acc = acc - jnp.dot(im.astype(BF16), cim_ref[j], preferred_element_type=F32)
        y_ref[:, cols] = acc.astype(y_ref.dtype)


def _s5_scan_kernel(x_ref, g_ref, wb_ref, ar_ref, ai_ref, cre_ref, cim_ref, d_ref,
                    y_ref, sr_ref, si_ref, bur_ref, bui_ref, hr_ref, hi_ref):
    tc, pitch = S5_CHUNK, S5_PITCH

    @pl.when(pl.program_id(1) == 0)
    def _():
        hr_ref[...] = jnp.zeros_like(hr_ref)
        hi_ref[...] = jnp.zeros_like(hi_ref)

    hn = _rms(x_ref[...], g_ref[...])

    def store(j, re, im):
        bur_ref[j * pitch:j * pitch + tc, :] = re
        bui_ref[j * pitch:j * pitch + tc, :] = im

    _s5_project_in(hn.astype(BF16), wb_ref, store)

    ar = [ar_ref[j8] for j8 in range(SLAB_GRP)]
    ai = [ai_ref[j8] for j8 in range(SLAB_GRP)]

    def step(t, carry):
        out = []
        for j8 in range(SLAB_GRP):
            hr, hi = carry[2 * j8], carry[2 * j8 + 1]
            rows = pl.ds(j8 * SUBLANES * pitch + t, SUBLANES, stride=pitch)
            nr = ar[j8] * hr - ai[j8] * hi + bur_ref[rows, :]
            ni = ar[j8] * hi + ai[j8] * hr + bui_ref[rows, :]
            bur_ref[rows, :] = nr
            bui_ref[rows, :] = ni
            out += [nr, ni]
        return tuple(out)

    init = []
    for j8 in range(SLAB_GRP):
        init += [hr_ref[j8], hi_ref[j8]]
    fin = lax.fori_loop(0, tc, step, tuple(init))
    for j8 in range(SLAB_GRP):
        hr_ref[j8] = fin[2 * j8]
        hi_ref[j8] = fin[2 * j8 + 1]
    sr_ref[...] = hr_ref[...]
    si_ref[...] = hi_ref[...]

    def load(j):
        return bur_ref[j * pitch:j * pitch + tc, :], bui_ref[j * pitch:j * pitch + tc, :]

    _s5_project_out(load, cre_ref, cim_ref, hn, d_ref, y_ref)


def s5_scan(x, g, wb, ar, ai, cre, cim, d_skip, bsz, seq):
    n_chunks = seq // S5_CHUNK
    st = jax.ShapeDtypeStruct((bsz, SLAB_GRP, SUBLANES, LANES), F32)
    st_spec = pl.BlockSpec((None, SLAB_GRP, SUBLANES, LANES), lambda b, c: (b, 0, 0, 0))
    vec = lambda b, c: (0, 0)
    c3 = lambda b, c: (0, 0, 0)
    return pl.pallas_call(
        _s5_scan_kernel,
        grid=(bsz, n_chunks),
        in_specs=[pl.BlockSpec((S5_CHUNK, D_MODEL), lambda b, c: (b * n_chunks + c, 0)),
                  pl.BlockSpec((1, D_MODEL), vec),
                  pl.BlockSpec(wb.shape, c3), pl.BlockSpec(ar.shape, c3), pl.BlockSpec(ai.shape, c3),
                  pl.BlockSpec(cre.shape, c3), pl.BlockSpec(cim.shape, c3),
                  pl.BlockSpec((1, D_MODEL), vec)],
        out_specs=[pl.BlockSpec((S5_CHUNK, D_MODEL), lambda b, c: (b * n_chunks + c, 0)), st_spec, st_spec],
        out_shape=[jax.ShapeDtypeStruct((bsz * seq, D_MODEL), BF16), st, st],
        scratch_shapes=[pltpu.VMEM((N_SLAB * S5_PITCH, LANES), F32), pltpu.VMEM((N_SLAB * S5_PITCH, LANES), F32),
                        pltpu.VMEM((SLAB_GRP, SUBLANES, LANES), F32), pltpu.VMEM((SLAB_GRP, SUBLANES, LANES), F32)],
        compiler_params=_params("parallel", "arbitrary"),
        name="s5_scan",
    )(x, g.reshape(1, -1), wb, ar, ai, cre, cim, d_skip.reshape(1, -1))


def _s5_decode_kernel(x_ref, g_ref, wb_ref, ar_ref, ai_ref, cre_ref, cim_ref, d_ref, h0r_ref, h0i_ref,
                      y_ref, sr_ref, si_ref):
    hn = _rms(x_ref[...], g_ref[...])

    def store(j, re, im):
        cols = slice(j * LANES, (j + 1) * LANES)
        ar, ai = ar_ref[:, cols], ai_ref[:, cols]
        hr, hi = h0r_ref[:, cols], h0i_ref[:, cols]
        sr_ref[:, cols] = ar * hr - ai * hi + re
        si_ref[:, cols] = ar * hi + ai * hr + im

    _s5_project_in(hn.astype(BF16), wb_ref, store)

    def load(j):
        cols = slice(j * LANES, (j + 1) * LANES)
        return sr_ref[:, cols], si_ref[:, cols]

    _s5_project_out(load, cre_ref, cim_ref, hn, d_ref, y_ref)


def s5_decode(x, g, wb, ar_row, ai_row, cre, cim, d_skip, h0r, h0i):
    r = x.shape[0]
    st = jax.ShapeDtypeStruct((r, SSM_DIM), F32)
    vec = lambda i: (0, 0)
    c3 = lambda i: (0, 0, 0)
    return pl.pallas_call(
        _s5_decode_kernel,
        grid=(1,),
        in_specs=[pl.BlockSpec((r, D_MODEL), vec), pl.BlockSpec((1, D_MODEL), vec),
                  pl.BlockSpec(wb.shape, c3), pl.BlockSpec((1, SSM_DIM), vec), pl.BlockSpec((1, SSM_DIM), vec),
                  pl.BlockSpec(cre.shape, c3), pl.BlockSpec(cim.shape, c3), pl.BlockSpec((1, D_MODEL), vec),
                  pl.BlockSpec((r, SSM_DIM), vec), pl.BlockSpec((r, SSM_DIM), vec)],
        out_specs=[pl.BlockSpec((r, D_MODEL), vec), pl.BlockSpec((r, SSM_DIM), vec), pl.BlockSpec((r, SSM_DIM), vec)],
        out_shape=[jax.ShapeDtypeStruct((r, D_MODEL), BF16), st, st],
        compiler_params=_params("arbitrary"),
        name="s5_decode",
    )(x, g.reshape(1, -1), wb, ar_row, ai_row, cre, cim, d_skip.reshape(1, -1), h0r, h0i)


def _t5_bucket(rel):
    n = jnp.maximum(rel, 0)
    nf = jnp.maximum(n, MAX_EXACT).astype(F32)
    big = MAX_EXACT + (jnp.log(nf / MAX_EXACT) / math.log(REL_MAX_DIST / MAX_EXACT)
                       * (N_BUCKETS - MAX_EXACT)).astype(jnp.int32)
    return jnp.where(n < MAX_EXACT, n, jnp.minimum(big, N_BUCKETS - 1))


def _bias_of(rel_bias, rel):
    onehot = (_t5_bucket(rel)[..., None] == jnp.arange(N_BUCKETS, dtype=jnp.int32)).astype(F32)
    return jnp.einsum('...k,kh->h...', onehot, rel_bias.astype(F32), precision=HIGHEST)


def _selection_matrix(n_cmp, n_sel_pad):
    coef = np.convolve(np.ones(SEL_RATIO), np.ones(CMP_BLOCK // CMP_STRIDE)).astype(np.float32)
    m = np.zeros((n_cmp, n_sel_pad), np.float32)
    for j in range(n_sel_pad):
        for o in range(coef.shape[0]):
            n = SEL_RATIO * j + o - (CMP_BLOCK // CMP_STRIDE - 1)
            if 0 <= n < n_cmp:
                m[n, j] = coef[o]
    return m


def _s5_params(a_re, a_im, log_dt, b_re, b_im, c_re, c_im):
    dt = jnp.exp(log_dt.astype(F32))[:, None]
    ar, ai = a_re.astype(F32), a_im.astype(F32)
    mag = jnp.exp(ar * dt)
    abar_re, abar_im = mag * jnp.cos(ai * dt), mag * jnp.sin(ai * dt)
    den = ar * ar + ai * ai
    coef_re = ((abar_re - 1.0) * ar + abar_im * ai) / den
    coef_im = (abar_im * ar - (abar_re - 1.0) * ai) / den
    br, bim = b_re.astype(F32), b_im.astype(F32)
    bb_re = coef_re[..., None] * br - coef_im[..., None] * bim
    bb_im = coef_re[..., None] * bim + coef_im[..., None] * br
    gpi = LANES // SSM_GROUP_CH
    eye = jnp.eye(gpi, dtype=F32)

    def in_blocks(bb):
        t = bb.reshape(IN_SLABS, gpi, SSM_STATE, SSM_GROUP_CH)
        blk = jnp.einsum('sgpc,gh->sgchp', t, eye)
        return blk.reshape(IN_SLABS, LANES, gpi * SSM_STATE)

    wb = jnp.concatenate([in_blocks(bb_re), in_blocks(bb_im)], axis=-1).astype(BF16)
    gps = LANES // SSM_STATE
    ch_per_in = LANES

    def out_blocks(c):
        t = c.astype(F32).reshape(IN_SLABS, STATE_PER_IN, gps, SSM_GROUP_CH, SSM_STATE)
        sel = jnp.eye(STATE_PER_IN * gps, dtype=F32).reshape(STATE_PER_IN, gps, STATE_PER_IN * gps)
        blk = jnp.einsum('ijgcp,jgh->ijgphc', t, sel)
        return blk.reshape(N_SLAB, LANES, ch_per_in).astype(BF16)

    return abar_re, abar_im, wb, out_blocks(c_re), out_blocks(c_im)


PROMPT_TM = 512
TN = 512


def _row_tile(r):
    return PROMPT_TM if r % PROMPT_TM == 0 else r


def _nsa_project(h, wq, wkv, wg):
    tm = _row_tile(h.shape[0])
    q = matmul(h, wq, n_split=1, out_dtype=BF16, act=None, tm=tm, tn=TN, name="nsa_q")[0]
    kv = matmul(h, wkv, n_split=3, out_dtype=F32, act=None, tm=tm, tn=TN, name="nsa_kv")
    gate = matmul(h, wg, n_split=1, out_dtype=F32, act="sigmoid", tm=tm, tn=LANES, name="nsa_gate")[0]
    return q, kv, gate


def _nsa_layer(hp, hs, xp, xs, g_post, g_next, wts, tabs, caches, layer, page_table, bsz, seq):
    wq, wkv, wg, wo, pe, w1, w2 = wts
    cmp_pool, slc_pool, win_pool, n_phys = caches
    dec = hs.shape[0]
    q, kv, gate = _nsa_project(hp, wq, wkv, wg)
    part_a, part_b = cmp_partial_rows(kv[0], pe, w1, seq)
    kc = cmp_finish(part_a, part_b, jnp.zeros((bsz, SUBLANES, KV_COLS), F32), w2)
    o_cmp, mask = cmp_attention(q, kc, tabs["cmp"], tabs["msel"], tabs["expand"], bsz, seq)
    o_slc = slc_attention(q, kv[1], mask, tabs["tile"], bsz, seq)
    o_win = win_attention(q, kv[2], tabs["win"], bsz, seq)
    o = nsa_combine(o_cmp, o_slc, o_win, gate, PROMPT_TM)
    xp, hp = proj_res(o, wo, xp, g_post, g_next, glu=False, tm=PROMPT_TM, tk=TN, name="nsa_out")
    qs, kvs, gate_s = _nsa_project(hs, wq, wkv, wg)
    past_a, past_b = cmp_partial_paged(cmp_pool, n_phys, page_table, layer, pe, w1)
    tail = jnp.pad(kvs[0][:, None, :], ((0, 0), (0, CMP_STRIDE - 1), (0, 0))).reshape(dec * CMP_STRIDE, KV_COLS)
    _, tail_b = cmp_partial_rows(tail, pe, w1, dec * CMP_STRIDE)
    b_next = jnp.pad(tail_b[0][:, None, :], ((0, 0), (0, SUBLANES - 1), (0, 0)))
    kc_s = cmp_finish(past_a, past_b, b_next, w2)
    oc_s, ow_s, idx = sample_attention(qs.reshape(dec, N_HEADS, HEAD_DIM), kc_s, tabs["cmp_s"], tabs["gsum"],
                                       tabs["msel_s"], win_pool, layer, kvs[2][:, None, :],
                                       tabs["win_s"], tabs["new_s"])
    q3 = qs.reshape(dec, N_HEADS, HEAD_DIM)
    os_all = sample_slc_attention(idx[:, :KV_GROUPS, :N_SELECT], page_table, q3, slc_pool, n_phys, layer,
                                  kvs[1][:, None, :], tabs["slc_s"])
    os_s = jnp.stack([os_all[:, h // GROUP_SIZE, h] for h in range(N_HEADS)], axis=1)
    o_s = nsa_combine(oc_s.reshape(dec, D_MODEL), os_s.reshape(dec, D_MODEL), ow_s.reshape(dec, D_MODEL), gate_s, dec)
    xs, hs = proj_res(o_s, wo, xs, g_post, g_next, glu=False, tm=dec, tk=TN, name="nsa_out_s")
    return xp, hp, xs, hs, kv, kvs


def _conv_layer(hp, hs, xp, xs, g_post, g_next, wts, state, bsz, seq):
    w_pw1, dw, dw_b, ln_g, ln_b, w_pw2 = wts
    dec = hs.shape[0]
    u = glu_matmul(hp, w_pw1, tm=PROMPT_TM, tn=TN, name="conv_pw1")
    hc = dwconv_ln(u, dw, dw_b, ln_g, ln_b, seq)
    xp, hp = proj_res(hc, w_pw2, xp, g_post, g_next, glu=False, tm=PROMPT_TM, tk=TN, name="conv_pw2")
    hist_p = u.reshape(bsz, seq, D_MODEL)[:, seq - (CONV_WIDTH - 1):]
    us = glu_matmul(hs, w_pw1, tm=dec, tn=TN, name="conv_pw1_s")
    hc_s = dwconv_ln_decode(us, jnp.swapaxes(state, 0, 1), dw, dw_b, ln_g, ln_b)
    xs, hs = proj_res(hc_s, w_pw2, xs, g_post, g_next, glu=False, tm=dec, tk=TN, name="conv_pw2_s")
    hist_s = jnp.concatenate([state[:, 1:], us[:, None, :]], axis=1)
    return xp, hp, xs, hs, hist_p, hist_s


def _s5_layer(xp, xs, g_pre, g_post, g_next, wts, state_re, state_im, bsz, seq):
    a_re, a_im, log_dt, b_re, b_im, c_re, c_im, d_skip, w_glu = wts
    dec = xs.shape[0]
    abar_re, abar_im, wb, cre, cim = _s5_params(a_re, a_im, log_dt, b_re, b_im, c_re, c_im)
    slab_shape = (SLAB_GRP, SUBLANES, LANES)
    y, sr, si = s5_scan(xp, g_pre, wb, abar_re.reshape(slab_shape), abar_im.reshape(slab_shape), cre, cim,
                        d_skip, bsz, seq)
    xp, hp = proj_res(y, w_glu, xp, g_post, g_next, glu=True, tm=PROMPT_TM, tk=256, name="s5_glu")
    ys, sr_s, si_s = s5_decode(xs, g_pre, wb, abar_re.reshape(1, SSM_DIM), abar_im.reshape(1, SSM_DIM), cre, cim,
                               d_skip, state_re.reshape(dec, SSM_DIM), state_im.reshape(dec, SSM_DIM))
    xs, hs = proj_res(ys, w_glu, xs, g_post, g_next, glu=True, tm=dec, tk=256, name="s5_glu_s")
    gp = (SSM_GROUPS, SSM_STATE)
    return (xp, hp, xs, hs, sr.reshape((bsz,) + gp), si.reshape((bsz,) + gp),
            sr_s.reshape((dec,) + gp), si_s.reshape((dec,) + gp))


def _ffn_layer(hp, hs, xp, xs, g_post, g_next, wts, state, bsz, seq):
    w_up, dw, dw_b, w_down = wts
    dec = hs.shape[0]
    act, hist = ffn_up(hp, w_up, dw, dw_b, seq_len=seq, tm=PROMPT_TM, tn=TN)
    xp, hp = proj_res(act, w_down, xp, g_post, g_next, glu=False, tm=PROMPT_TM, tk=TN, name="ffn_down")
    tiles = seq // PROMPT_TM
    hist_p = hist.reshape(bsz, tiles, SUBLANES, D_FF)[:, tiles - 1, SUBLANES - 2:, :]
    act_s, gate_s = ffn_up(hs, w_up, dw, dw_b, hist=state, seq_len=1, tm=dec, tn=TN)
    xs, hs = proj_res(act_s, w_down, xs, g_post, g_next, glu=False, tm=dec, tk=TN, name="ffn_down_s")
    hist_s = jnp.concatenate([state[:, 1:], gate_s.reshape(dec, 1, D_FF)], axis=1)
    return xp, hp, xs, hs, hist_p, hist_s


def _bias_tables(rel_bias, seq):
    tq = ATT_TILE
    i = jnp.arange(tq, dtype=jnp.int32)
    tile = jnp.stack([_bias_of(rel_bias, d * tq + i[:, None] - i[None, :]) for d in range(3)])
    n_cmp = seq // CMP_STRIDE
    cpos = jnp.arange(n_cmp, dtype=jnp.int32) * CMP_STRIDE + (CMP_BLOCK - 1)
    qpos = jnp.arange(seq, dtype=jnp.int32)
    n_sel = -(-seq // SEL_BLOCK)
    key = np.arange(seq)
    expand = (key[None, :] // SEL_BLOCK == np.arange(n_sel)[:, None]).astype(np.float32)
    cpos_s = jnp.arange(N_CMP_S, dtype=jnp.int32) * CMP_STRIDE + (CMP_BLOCK - 1)
    wb = min(WINDOW, PAST_LEN)
    kpos_s = jnp.arange(N_SEL_S * SEL_BLOCK, dtype=jnp.int32)
    slc_s = _bias_of(rel_bias, PAST_LEN - kpos_s).reshape(N_HEADS, N_SEL_S, SEL_BLOCK)
    gsum = (np.arange(N_HEADS)[None, :] // GROUP_SIZE == np.arange(SUBLANES)[:, None]).astype(np.float32)
    return {
        "tile": tile,
        "win": _bias_of(rel_bias, i[:, None] + WINDOW - jnp.arange(WIN_SPAN, dtype=jnp.int32)[None, :]),
        "cmp": _bias_of(rel_bias, qpos[:, None] - cpos[None, :]),
        "msel": jnp.asarray(_selection_matrix(n_cmp, n_sel)),
        "expand": jnp.asarray(expand, BF16),
        "cmp_s": _bias_of(rel_bias, PAST_LEN - cpos_s),
        "msel_s": jnp.asarray(_selection_matrix(N_CMP_S, N_SEL_S_PAD)),
        "gsum": jnp.asarray(gsum),
        "win_s": _bias_of(rel_bias, wb - jnp.arange(wb, dtype=jnp.int32)),
        "new_s": jnp.broadcast_to(_bias_of(rel_bias, jnp.zeros((1,), jnp.int32)), (N_HEADS, LANES)),
        "slc_s": jnp.swapaxes(slc_s, 0, 1),
    }


def kernel(x_prompt, x_sample, cache_cmp_kv, cache_slc_kv, cache_win_kv, state_conv, state_ssm_re, state_ssm_im, state_ffn_conv, page_table, norm_gain, rel_bias, nsa_w_q, nsa_w_kv, nsa_cmp_pe, nsa_cmp_w1, nsa_cmp_w2, nsa_w_gate, nsa_w_o, conv_w_pw1, conv_dw, conv_dw_b, conv_ln_g, conv_ln_b, conv_w_pw2, ssm_a_re, ssm_a_im, ssm_log_dt, ssm_b_re, ssm_b_im, ssm_c_re, ssm_c_im, ssm_d, ssm_w_glu, ffn_w_up, ffn_dw, ffn_dw_b, ffn_w_down):
    bsz, seq, d = x_prompt.shape
    dec, dec_seq, _ = x_sample.shape
    assert dec_seq == 1 and d == D_MODEL and seq % PROMPT_TM == 0 and seq // CMP_STRIDE == LANES
    n_nsa = cache_cmp_kv.shape[0]
    n_phys = cache_cmp_kv.shape[1]
    xp = x_prompt.reshape(bsz * seq, d)
    xs = x_sample.reshape(dec, d)
    tabs = _bias_tables(rel_bias, seq)
    cmp_pool = cache_cmp_kv.reshape(n_nsa * n_phys, PAGE_SIZE * N_KV_SLABS, HEAD_DIM)
    slc_pool = cache_slc_kv.reshape(n_nsa * n_phys * (PAGE_SIZE // SEL_BLOCK), SEL_BLOCK * N_KV_SLABS, HEAD_DIM)
    win_pool = cache_win_kv.reshape(n_nsa * dec, cache_win_kv.shape[2] * N_KV_SLABS, HEAD_DIM)
    g_all = norm_gain.astype(F32)

    hp = rmsnorm_cast(xp, g_all[0, 0], PROMPT_TM)
    hs = rmsnorm_cast(xs, g_all[0, 0], dec)
    out = {k: [] for k in ("cmp_p", "cmp_s", "slc_p", "slc_s", "win_p", "win_s", "conv_p", "conv_s",
                           "re_p", "re_s", "im_p", "im_s", "ffn_p", "ffn_s")}
    counts = [0, 0, 0]
    for i in range(DEPTH):
        m = i % 3
        j = counts[m]
        counts[m] += 1
        g_post, g_ffn, g_ffn_post = g_all[i, 1], g_all[i, 2], g_all[i, 3]
        g_next = g_all[i + 1, 0] if i + 1 < DEPTH else g_all[i, 0]
        if m == 0:
            wg = jnp.pad(nsa_w_gate[j], ((0, 0), (0, LANES - nsa_w_gate.shape[-1]))).astype(BF16)
            wts = (nsa_w_q[j].astype(BF16), nsa_w_kv[j].astype(BF16), wg, nsa_w_o[j].astype(BF16),
                   jnp.swapaxes(nsa_cmp_pe[j], 0, 1).astype(F32), nsa_cmp_w1[j].astype(BF16),
                   nsa_cmp_w2[j].astype(BF16))
            xp, hp, xs, hs, kv, kvs = _nsa_layer(hp, hs, xp, xs, g_post, g_ffn, wts, tabs,
                                                 (cmp_pool, slc_pool, win_pool, n_phys), j, page_table, bsz, seq)
            shp = (bsz, seq, KV_GROUPS, 2, HEAD_DIM)
            shs = (dec, 1, KV_GROUPS, 2, HEAD_DIM)
            out["cmp_p"].append(kv[0].reshape(shp)); out["cmp_s"].append(kvs[0].reshape(shs))
            out["slc_p"].append(kv[1].reshape(shp)); out["slc_s"].append(kvs[1].reshape(shs))
            out["win_p"].append(kv[2].reshape(shp)[:, seq - min(WINDOW, seq):])
            win_full = jnp.concatenate([cache_win_kv[j], kvs[2].reshape(shs)], axis=1)
            out["win_s"].append(win_full[:, win_full.shape[1] - min(WINDOW, PAST_LEN + 1):])
        elif m == 1:
            wts = (conv_w_pw1[j].astype(BF16), conv_dw[j], conv_dw_b[j], conv_ln_g[j], conv_ln_b[j],
                   conv_w_pw2[j].astype(BF16))
            xp, hp, xs, hs, cp, cs = _conv_layer(hp, hs, xp, xs, g_post, g_ffn, wts, state_conv[j], bsz, seq)
            out["conv_p"].append(cp); out["conv_s"].append(cs)
        else:
            wts = (ssm_a_re[j], ssm_a_im[j], ssm_log_dt[j], ssm_b_re[j], ssm_b_im[j], ssm_c_re[j], ssm_c_im[j],
                   ssm_d[j], ssm_w_glu[j].astype(BF16))
            xp, hp, xs, hs, rp, ip, rs, is_ = _s5_layer(xp, xs, g_all[i, 0], g_post, g_ffn, wts,
                                                        state_ssm_re[j], state_ssm_im[j], bsz, seq)
            out["re_p"].append(rp); out["im_p"].append(ip); out["re_s"].append(rs); out["im_s"].append(is_)
        wts = (ffn_w_up[i].astype(BF16), ffn_dw[i], ffn_dw_b[i], ffn_w_down[i].astype(BF16))
        xp, hp, xs, hs, fp, fs = _ffn_layer(hp, hs, xp, xs, g_ffn_post, g_next, wts, state_ffn_conv[i], bsz, seq)
        out["ffn_p"].append(fp); out["ffn_s"].append(fs)
    st = lambda k: jnp.stack(out[k])
    return (xp.reshape(bsz, seq, d), xs.reshape(dec, 1, d),
            st("cmp_p"), st("cmp_s"), st("slc_p"), st("slc_s"), st("win_p"), st("win_s"),
            st("conv_p"), st("conv_s"), st("re_p"), st("re_s"), st("im_p"), st("im_s"),
            st("ffn_p"), st("ffn_s"))
```

```python
import functools
import math

import numpy as np
import jax
import jax.numpy as jnp
from jax import lax
from jax.experimental import pallas as pl
from jax.experimental.pallas import tpu as pltpu

F32 = jnp.float32
BF16 = jnp.bfloat16

D_MODEL = 2048
DEPTH = 4
PAST_LEN = 16384
PAGE_SIZE = 128
N_HEADS = 16
HEAD_DIM = 128
KV_GROUPS = 4
GROUP_SIZE = 4
KV_COLS = KV_GROUPS * 2 * HEAD_DIM
CMP_BLOCK = 32
CMP_STRIDE = 16
SEL_BLOCK = 64
SEL_RATIO = SEL_BLOCK // CMP_STRIDE
N_SELECT = 16
WINDOW = 512
FORCE_BONUS = 1.0e4
N_BUCKETS = 32
MAX_EXACT = 16
REL_MAX_DIST = 128
CONV_WIDTH = 31
SSM_GROUP_CH = 16
SSM_GROUPS = 128
SSM_STATE = 64
SSM_DIM = SSM_GROUPS * SSM_STATE
D_FF = 5632
RMS_EPS = 1e-6
LN_EPS = 1e-5
SCALE = HEAD_DIM ** -0.5
NEG = -1e30

LANES = 128
SUBLANES = 8
VMEM_LIMIT = 56 * 1024 * 1024
ATT_TILE = 128
N_SLAB = SSM_DIM // LANES
SLAB_GRP = N_SLAB // SUBLANES

HIGHEST = lax.Precision.HIGHEST
NT_DIMS = (((1,), (1,)), ((), ()))


def _params(*sem):
    return pltpu.CompilerParams(dimension_semantics=sem, vmem_limit_bytes=VMEM_LIMIT)


def _rms(x, g):
    return x * lax.rsqrt(jnp.mean(x * x, axis=-1, keepdims=True) + RMS_EPS) * g


def _rmsnorm_kernel(x_ref, g_ref, o_ref):
    o_ref[...] = _rms(x_ref[...], g_ref[...]).astype(o_ref.dtype)


def rmsnorm_cast(x, g, tm):
    r, d = x.shape
    return pl.pallas_call(
        _rmsnorm_kernel,
        grid=(r // tm,),
        in_specs=[pl.BlockSpec((tm, d), lambda m: (m, 0)), pl.BlockSpec((1, d), lambda m: (0, 0))],
        out_specs=pl.BlockSpec((tm, d), lambda m: (m, 0)),
        out_shape=jax.ShapeDtypeStruct((r, d), BF16),
        compiler_params=_params("parallel"),
        name="rmsnorm",
    )(x, g.reshape(1, d))


def _mm_kernel(a_ref, w_ref, o_ref, *, act):
    r = jnp.dot(a_ref[...], w_ref[...], preferred_element_type=F32)
    if act == "sigmoid":
        r = jax.nn.sigmoid(r)
    o_ref[...] = r.astype(o_ref.dtype)


def matmul(a, w, *, n_split, out_dtype, act, tm, tn, name):
    r, k = a.shape
    n = w.shape[1]
    per = (n // n_split) // tn
    return pl.pallas_call(
        functools.partial(_mm_kernel, act=act),
        grid=(r // tm, n // tn),
        in_specs=[pl.BlockSpec((tm, k), lambda m, j: (m, 0)), pl.BlockSpec((k, tn), lambda m, j: (0, j))],
        out_specs=pl.BlockSpec((None, tm, tn), lambda m, j: (j // per, m, j % per)),
        out_shape=jax.ShapeDtypeStruct((n_split, r, n // n_split), out_dtype),
        compiler_params=_params("parallel", "arbitrary"),
        name=name,
    )(a, w)


def _glu_mm_kernel(a_ref, wa_ref, wb_ref, o_ref):
    a = a_ref[...]
    lin = jnp.dot(a, wa_ref[...], preferred_element_type=F32)
    gate = jnp.dot(a, wb_ref[...], preferred_element_type=F32)
    o_ref[...] = lin * jax.nn.sigmoid(gate)


def glu_matmul(a, w, *, tm, tn, name):
    r, k = a.shape
    n = w.shape[1] // 2
    nb = n // tn
    return pl.pallas_call(
        _glu_mm_kernel,
        grid=(r // tm, nb),
        in_specs=[pl.BlockSpec((tm, k), lambda m, j: (m, 0)),
                  pl.BlockSpec((k, tn), lambda m, j: (0, j)),
                  pl.BlockSpec((k, tn), lambda m, j: (0, j + nb))],
        out_specs=pl.BlockSpec((tm, tn), lambda m, j: (m, j)),
        out_shape=jax.ShapeDtypeStruct((r, n), F32),
        compiler_params=_params("parallel", "arbitrary"),
        name=name,
    )(a, w, w)


def _proj_res_kernel(a_ref, w_ref, x_ref, gp_ref, gn_ref, xo_ref, ho_ref, *acc, glu, nk):
    def finish(y):
        if glu:
            d = y.shape[1] // 2
            y = y[:, :d] * jax.nn.sigmoid(y[:, d:])
        xn = x_ref[...] + _rms(y, gp_ref[...])
        xo_ref[...] = xn
        ho_ref[...] = _rms(xn, gn_ref[...]).astype(ho_ref.dtype)

    part = jnp.dot(a_ref[...], w_ref[...], preferred_element_type=F32)
    if nk == 1:
        finish(part)
        return
    acc_ref, = acc
    k = pl.program_id(1)

    @pl.when(k == 0)
    def _():
        acc_ref[...] = part

    @pl.when((k > 0) & (k < nk - 1))
    def _():
        acc_ref[...] += part

    @pl.when(k == nk - 1)
    def _():
        finish(acc_ref[...] + part)


def proj_res(a, w, x, g_post, g_next, *, glu, tm, tk, name):
    r, kdim = a.shape
    n = w.shape[1]
    d = x.shape[1]
    nk = kdim // tk
    return pl.pallas_call(
        functools.partial(_proj_res_kernel, glu=glu, nk=nk),
        grid=(r // tm, nk),
        in_specs=[pl.BlockSpec((tm, tk), lambda m, k: (m, k)),
                  pl.BlockSpec((tk, n), lambda m, k: (k, 0)),
                  pl.BlockSpec((tm, d), lambda m, k: (m, 0)),
                  pl.BlockSpec((1, d), lambda m, k: (0, 0)),
                  pl.BlockSpec((1, d), lambda m, k: (0, 0))],
        out_specs=[pl.BlockSpec((tm, d), lambda m, k: (m, 0)),
                   pl.BlockSpec((tm, d), lambda m, k: (m, 0))],
        out_shape=[jax.ShapeDtypeStruct((r, d), F32), jax.ShapeDtypeStruct((r, d), BF16)],
        scratch_shapes=[pltpu.VMEM((tm, n), F32)] if nk > 1 else [],
        compiler_params=_params("parallel", "arbitrary"),
        name=name,
    )(a, w, x, g_post.reshape(1, d), g_next.reshape(1, d))


def _ffn_up_kernel(a_ref, p1_ref, p2_ref, wg_ref, wv_ref, dw_ref, db_ref, act_ref, hist_ref, *, decode, tiles_per_seq):
    a = a_ref[...]
    gate = jnp.dot(a, wg_ref[...], preferred_element_type=F32)
    val = jnp.dot(a, wv_ref[...], preferred_element_type=F32)
    tm = gate.shape[0]
    if decode:
        g2, g1 = p2_ref[...], p1_ref[...]
        hist_ref[...] = gate
    else:
        halo = jnp.dot(p1_ref[...], wg_ref[...], preferred_element_type=F32)
        halo = jnp.where(pl.program_id(0) % tiles_per_seq == 0, 0.0, halo)
        h7, h6 = halo[7:8, :], halo[6:7, :]
        row = lax.broadcasted_iota(jnp.int32, gate.shape, 0)
        g1 = jnp.where(row == 0, h7, pltpu.roll(gate, 1, 0))
        g2 = jnp.where(row == 0, h6, jnp.where(row == 1, h7, pltpu.roll(gate, 2, 0)))
        hist_ref[...] = gate[tm - SUBLANES:, :]
    g = dw_ref[0:1, :] * g2 + dw_ref[1:2, :] * g1 + dw_ref[2:3, :] * gate + db_ref[...]
    act_ref[...] = (jax.nn.gelu(g) * val).astype(act_ref.dtype)


def ffn_up(h, w_up, dw, db, *, hist=None, seq_len, tm, tn):
    r, k = h.shape
    nb = D_FF // tn
    decode = hist is not None
    if decode:
        p1, p2 = hist[:, 1, :], hist[:, 0, :]
        p_specs = [pl.BlockSpec((tm, tn), lambda m, j: (m, j)), pl.BlockSpec((tm, tn), lambda m, j: (m, j))]
        hrows = tm
    else:
        p1 = p2 = h
        blk = tm // SUBLANES
        p_specs = [pl.BlockSpec((SUBLANES, k), lambda m, j: (jnp.maximum(m * blk - 1, 0), 0)),
                   pl.BlockSpec((SUBLANES, k), lambda m, j: (0, 0))]
        hrows = SUBLANES
    return pl.pallas_call(
        functools.partial(_ffn_up_kernel, decode=decode, tiles_per_seq=max(seq_len // tm, 1)),
        grid=(r // tm, nb),
        in_specs=[pl.BlockSpec((tm, k), lambda m, j: (m, 0))] + p_specs + [
            pl.BlockSpec((k, tn), lambda m, j: (0, j)),
            pl.BlockSpec((k, tn), lambda m, j: (0, j + nb)),
            pl.BlockSpec((3, tn), lambda m, j: (0, j)),
            pl.BlockSpec((1, tn), lambda m, j: (0, j))],
        out_specs=[pl.BlockSpec((tm, tn), lambda m, j: (m, j)),
                   pl.BlockSpec((None, hrows, tn), lambda m, j: (m, 0, j))],
        out_shape=[jax.ShapeDtypeStruct((r, D_FF), BF16),
                   jax.ShapeDtypeStruct((r // tm, hrows, D_FF), F32)],
        compiler_params=_params("parallel", "arbitrary"),
        name="ffn_up",
    )(h, p1, p2, w_up, w_up, dw, db.reshape(1, D_FF))


N_KV_SLABS = KV_GROUPS * 2


def _cmp_partial_slab(x_ref, c, pe_ref, w1_ref, n_chunks, first=0, pitch=1, chunk_pitch=None):
    chunk_pitch = CMP_STRIDE * pitch if chunk_pitch is None else chunk_pitch
    acc_a = jnp.zeros((n_chunks, HEAD_DIM), F32)
    acc_b = jnp.zeros((n_chunks, HEAD_DIM), F32)
    for s in range(CMP_STRIDE):
        xs = x_ref[pl.ds(first + s * pitch, n_chunks, stride=chunk_pitch), :]
        xa = (xs + pe_ref[c, s:s + 1, :]).astype(BF16)
        xb = (xs + pe_ref[c, CMP_STRIDE + s:CMP_STRIDE + s + 1, :]).astype(BF16)
        acc_a += jnp.dot(xa, w1_ref[c, s], preferred_element_type=F32)
        acc_b += jnp.dot(xb, w1_ref[c, CMP_STRIDE + s], preferred_element_type=F32)
    return acc_a, acc_b


def _cmp1_kernel(x_ref, pe_ref, w1_ref, a_ref, b_ref, *, n_chunks):
    c = pl.program_id(1) % 2
    a_ref[...], b_ref[...] = _cmp_partial_slab(x_ref, c, pe_ref, w1_ref, n_chunks)


def cmp_partial_rows(x, pe, w1, rows):
    nb = x.shape[0] // rows
    n_chunks = rows // CMP_STRIDE
    out = jax.ShapeDtypeStruct((nb, n_chunks, KV_COLS), F32)
    return pl.pallas_call(
        functools.partial(_cmp1_kernel, n_chunks=n_chunks),
        grid=(nb, N_KV_SLABS),
        in_specs=[pl.BlockSpec((rows, HEAD_DIM), lambda b, sl: (b, sl)),
                  pl.BlockSpec(pe.shape, lambda b, sl: (0, 0, 0)),
                  pl.BlockSpec(w1.shape, lambda b, sl: (0, 0, 0, 0))],
        out_specs=[pl.BlockSpec((None, n_chunks, HEAD_DIM), lambda b, sl: (b, 0, sl))] * 2,
        out_shape=[out, out],
        compiler_params=_params("parallel", "arbitrary"),
        name="cmp_partial",
    )(x, pe, w1)


PAGES_PER_STEP = 16
CHUNKS_PER_PAGE = PAGE_SIZE // CMP_STRIDE
CHUNK_ROWS = CMP_STRIDE * N_KV_SLABS
CHUNK_PITCH = CHUNK_ROWS + 4
STEP_CHUNKS = PAGES_PER_STEP * CHUNKS_PER_PAGE


def _cmp1_paged_kernel(pt_ref, *refs):
    page_refs = refs[:PAGES_PER_STEP]
    pe_ref, w1_ref, a_ref, b_ref, buf_ref = refs[PAGES_PER_STEP:]
    for i, p_ref in enumerate(page_refs):
        for ch in range(CHUNKS_PER_PAGE):
            dst = (i * CHUNKS_PER_PAGE + ch) * CHUNK_PITCH
            buf_ref[dst:dst + CHUNK_ROWS, :] = p_ref[ch * CHUNK_ROWS:(ch + 1) * CHUNK_ROWS, :]
    for slab in range(N_KV_SLABS):
        cols = slice(slab * HEAD_DIM, (slab + 1) * HEAD_DIM)
        a_ref[:, cols], b_ref[:, cols] = _cmp_partial_slab(buf_ref, slab % 2, pe_ref, w1_ref, STEP_CHUNKS,
                                                           first=slab, pitch=N_KV_SLABS, chunk_pitch=CHUNK_PITCH)


def cmp_partial_paged(pool, n_phys, page_table, layer, pe, w1):
    bsz, n_pages = page_table.shape
    n_steps = n_pages // PAGES_PER_STEP
    n_chunks = STEP_CHUNKS
    out = jax.ShapeDtypeStruct((bsz, n_steps, n_chunks, KV_COLS), F32)

    def page_spec(i):
        return pl.BlockSpec((None, PAGE_SIZE * N_KV_SLABS, HEAD_DIM),
                            lambda b, s, pt: (layer * n_phys + pt[b, s * PAGES_PER_STEP + i], 0, 0))

    out_spec = pl.BlockSpec((None, None, n_chunks, KV_COLS), lambda b, s, pt: (b, s, 0, 0))
    a, b = pl.pallas_call(
        _cmp1_paged_kernel,
        grid_spec=pltpu.PrefetchScalarGridSpec(
            num_scalar_prefetch=1,
            grid=(bsz, n_steps),
            in_specs=[page_spec(i) for i in range(PAGES_PER_STEP)] + [
                pl.BlockSpec(pe.shape, lambda b, s, pt: (0, 0, 0)),
                pl.BlockSpec(w1.shape, lambda b, s, pt: (0, 0, 0, 0))],
            out_specs=[out_spec, out_spec],
            scratch_shapes=[pltpu.VMEM((STEP_CHUNKS * CHUNK_PITCH, HEAD_DIM), F32)]),
        out_shape=[out, out],
        compiler_params=_params("parallel", "arbitrary"),
        name="cmp_partial_paged",
    )(page_table, *([pool] * PAGES_PER_STEP), pe, w1)
    return a.reshape(bsz, -1, KV_COLS), b.reshape(bsz, -1, KV_COLS)


def _cmp2_kernel(a_ref, b_ref, bx_ref, w2_ref, o_ref):
    n = a_ref.shape[0]
    row = lax.broadcasted_iota(jnp.int32, (n, HEAD_DIM), 0)
    for slab in range(KV_GROUPS * 2):
        c = slab % 2
        cols = slice(slab * HEAD_DIM, (slab + 1) * HEAD_DIM)
        nxt = pltpu.roll(b_ref[:, cols], n - 1, 0)
        nxt = jnp.where(row == n - 1, bx_ref[0:1, cols], nxt)
        h = jax.nn.gelu(a_ref[:, cols] + nxt).astype(BF16)
        o_ref[:, cols] = jnp.dot(h, w2_ref[c], preferred_element_type=F32).astype(o_ref.dtype)


def cmp_finish(a, b, b_next, w2):
    nb, n, _ = a.shape
    return pl.pallas_call(
        _cmp2_kernel,
        grid=(nb,),
        in_specs=[pl.BlockSpec((None, n, KV_COLS), lambda i: (i, 0, 0)),
                  pl.BlockSpec((None, n, KV_COLS), lambda i: (i, 0, 0)),
                  pl.BlockSpec((None, SUBLANES, KV_COLS), lambda i: (i, 0, 0)),
                  pl.BlockSpec(w2.shape, lambda i: (0, 0, 0))],
        out_specs=pl.BlockSpec((None, n, KV_COLS), lambda i: (i, 0, 0)),
        out_shape=jax.ShapeDtypeStruct((nb, n, KV_COLS), BF16),
        compiler_params=_params("parallel"),
        name="cmp_finish",
    )(a, b, b_next, w2)


def _softmax_rows(s, mask):
    s = jnp.where(mask, s, NEG)
    m = jnp.max(s, axis=-1, keepdims=True)
    e = jnp.where(mask, jnp.exp(s - m), 0.0)
    l = jnp.sum(e, axis=-1, keepdims=True)
    return jnp.where(l > 0.0, e / jnp.where(l > 0.0, l, 1.0), 0.0)


def _cmp_attn_kernel(q_ref, kc_ref, bias_ref, msel_ref, exp_ref, o_ref, mask_ref, *, n_sel):
    qt = pl.program_id(1)
    tq, n_cmp = q_ref.shape[0], kc_ref.shape[0]
    qpos = qt * tq + lax.broadcasted_iota(jnp.int32, (tq, n_cmp), 0)
    cpos = lax.broadcasted_iota(jnp.int32, (tq, n_cmp), 1) * CMP_STRIDE + (CMP_BLOCK - 1)
    cadd = jnp.where(cpos <= qpos, 0.0, NEG)
    j = lax.broadcasted_iota(jnp.int32, (n_sel, tq), 0)
    cur = (qt * tq + lax.broadcasted_iota(jnp.int32, (n_sel, tq), 1)) // SEL_BLOCK
    valid = j <= cur
    forced = (j == 0) | (j == cur) | (j == cur - 1)
    pad_rows = jnp.zeros((exp_ref.shape[0] - n_sel, tq), F32)
    diag = pl.ds(pl.multiple_of(qt * tq, tq), tq)
    causal = lax.broadcasted_iota(jnp.int32, (tq, tq), 1) <= lax.broadcasted_iota(jnp.int32, (tq, tq), 0)
    for g in range(KV_GROUPS):
        k = kc_ref[:, g * 2 * HEAD_DIM:(g * 2 + 1) * HEAD_DIM]
        v = kc_ref[:, (g * 2 + 1) * HEAD_DIM:(g * 2 + 2) * HEAD_DIM]
        heads = range(g * GROUP_SIZE, (g + 1) * GROUP_SIZE)
        q4 = jnp.concatenate([q_ref[:, h * HEAD_DIM:(h + 1) * HEAD_DIM] for h in heads], axis=0)
        badd = jnp.concatenate([bias_ref[h] + cadd for h in heads], axis=0)
        s = lax.dot_general(q4, k, NT_DIMS, preferred_element_type=F32) * SCALE + badd
        p = _softmax_rows(s, s > 0.5 * NEG)
        o4 = jnp.dot(p.astype(BF16), v, preferred_element_type=F32)
        imp = jnp.zeros((tq, n_cmp), F32)
        for r, h in enumerate(heads):
            o_ref[:, h * HEAD_DIM:(h + 1) * HEAD_DIM] = o4[r * tq:(r + 1) * tq, :]
            imp = imp + p[r * tq:(r + 1) * tq, :]
        p_sel = lax.dot_general(msel_ref[...], imp, NT_DIMS, precision=HIGHEST, preferred_element_type=F32)
        score = jnp.where(valid, p_sel + jnp.where(forced, FORCE_BONUS, 0.0), -jnp.inf)
        rank = jnp.zeros((n_sel, tq), jnp.int32)
        for i in range(n_sel):
            si = score[i:i + 1, :]
            beats = (si > score) | ((si == score) & (i < j))
            rank = rank + beats.astype(jnp.int32)
        sel_t = jnp.where((rank < N_SELECT) & valid, 1.0, 0.0)
        sel = jnp.concatenate([sel_t, pad_rows], axis=0).T.astype(BF16)
        allowed = jnp.dot(sel, exp_ref[...], preferred_element_type=F32)
        mask_ref[g] = ((allowed - 1.0) * -NEG).astype(mask_ref.dtype)
        allowed_d = jnp.dot(sel, exp_ref[:, diag], preferred_element_type=F32)
        mask_ref[g, :, diag] = jnp.where(causal, (allowed_d - 1.0) * -NEG, NEG).astype(mask_ref.dtype)


def cmp_attention(q, kc, bias_cmp, msel, expand, bsz, seq):
    tq = ATT_TILE
    nqt = seq // tq
    n_cmp = kc.shape[1]
    n_sel = msel.shape[0]
    return pl.pallas_call(
        functools.partial(_cmp_attn_kernel, n_sel=n_sel),
        grid=(bsz, nqt),
        in_specs=[pl.BlockSpec((tq, D_MODEL), lambda b, t: (b * nqt + t, 0)),
                  pl.BlockSpec((None, n_cmp, KV_COLS), lambda b, t: (b, 0, 0)),
                  pl.BlockSpec((N_HEADS, tq, n_cmp), lambda b, t: (0, t, 0)),
                  pl.BlockSpec(msel.shape, lambda b, t: (0, 0)),
                  pl.BlockSpec(expand.shape, lambda b, t: (0, 0))],
        out_specs=[pl.BlockSpec((tq, D_MODEL), lambda b, t: (b * nqt + t, 0)),
                   pl.BlockSpec((None, KV_GROUPS, tq, seq), lambda b, t: (b, 0, t, 0))],
        out_shape=[jax.ShapeDtypeStruct((bsz * seq, D_MODEL), F32),
                   jax.ShapeDtypeStruct((bsz, KV_GROUPS, seq, seq), BF16)],
        compiler_params=_params("parallel", "arbitrary"),
        name="cmp_attention",
    )(q, kc, bias_cmp, msel, expand)


SLC_KEYS = 512
SLC_SUB = SLC_KEYS // ATT_TILE
SLC_QUERIES = 128


def _slc_kernel(qt_ref, ks_ref, q_ref, kv_ref, tb_ref, mask_ref, o_ref, m_ref, l_ref, acc_ref):
    g, pair = pl.program_id(1), pl.program_id(2)
    qt, ks = qt_ref[pair], ks_ref[pair]

    @pl.when(ks == 0)
    def _():
        m_ref[...] = jnp.full_like(m_ref, NEG)
        l_ref[...] = jnp.zeros_like(l_ref)
        acc_ref[...] = jnp.zeros_like(acc_ref)

    tq = q_ref.shape[0]
    k = kv_ref[:, :HEAD_DIM].astype(BF16)
    v = kv_ref[:, HEAD_DIM:].astype(BF16)
    q4 = jnp.concatenate([q_ref[:, r * HEAD_DIM:(r + 1) * HEAD_DIM] for r in range(GROUP_SIZE)], axis=0)
    madd = [mask_ref[qs * ATT_TILE:(qs + 1) * ATT_TILE, :].astype(F32) for qs in range(tq // ATT_TILE)]
    rows = []
    for r in range(GROUP_SIZE):
        for qs in range(tq // ATT_TILE):
            dist = qt * (tq // ATT_TILE) + qs - ks * SLC_SUB
            bias = jnp.concatenate([tb_ref[jnp.clip(dist - c, 0, 2), g * GROUP_SIZE + r] for c in range(SLC_SUB)],
                                   axis=1)
            rows.append(bias + madd[qs])
    s = lax.dot_general(q4, k, NT_DIMS, preferred_element_type=F32) * SCALE + jnp.concatenate(rows, axis=0)
    m_prev = m_ref[...]
    m_new = jnp.maximum(m_prev, jnp.max(s, axis=-1, keepdims=True))
    alpha = jnp.exp(m_prev - m_new)
    p = jnp.exp(s - m_new)
    l_ref[...] = alpha * l_ref[...] + jnp.sum(p, axis=-1, keepdims=True)
    acc_ref[...] = alpha * acc_ref[...] + jnp.dot(p.astype(BF16), v, preferred_element_type=F32)
    m_ref[...] = m_new

    @pl.when(ks == ((qt + 1) * tq - 1) // SLC_KEYS)
    def _():
        l = l_ref[...]
        o = jnp.where(l > 0.0, acc_ref[...] / jnp.where(l > 0.0, l, 1.0), 0.0)
        for r in range(GROUP_SIZE):
            o_ref[:, r * HEAD_DIM:(r + 1) * HEAD_DIM] = o[r * tq:(r + 1) * tq, :]


def slc_attention(q, kv, mask, tb, bsz, seq):
    tq = SLC_QUERIES
    nqt = seq // tq
    nks = seq // SLC_KEYS
    gw = GROUP_SIZE * HEAD_DIM
    pairs = [(t, s) for t in range(nqt) for s in range(((t + 1) * tq - 1) // SLC_KEYS + 1)]
    qt_of = jnp.asarray([p[0] for p in pairs], jnp.int32)
    ks_of = jnp.asarray([p[1] for p in pairs], jnp.int32)
    return pl.pallas_call(
        _slc_kernel,
        grid_spec=pltpu.PrefetchScalarGridSpec(
            num_scalar_prefetch=2,
            grid=(bsz, KV_GROUPS, len(pairs)),
            in_specs=[pl.BlockSpec((tq, gw), lambda b, g, p, qt, ks: (b * nqt + qt[p], g)),
                      pl.BlockSpec((SLC_KEYS, 2 * HEAD_DIM), lambda b, g, p, qt, ks: (b * nks + ks[p], g)),
                      pl.BlockSpec(tb.shape, lambda b, g, p, qt, ks: (0, 0, 0, 0)),
                      pl.BlockSpec((None, None, tq, SLC_KEYS), lambda b, g, p, qt, ks: (b, g, qt[p], ks[p]))],
            out_specs=pl.BlockSpec((tq, gw), lambda b, g, p, qt, ks: (b * nqt + qt[p], g)),
            scratch_shapes=[pltpu.VMEM((GROUP_SIZE * tq, 1), F32), pltpu.VMEM((GROUP_SIZE * tq, 1), F32),
                            pltpu.VMEM((GROUP_SIZE * tq, HEAD_DIM), F32)]),
        out_shape=jax.ShapeDtypeStruct((bsz * seq, D_MODEL), F32),
        compiler_params=_params("parallel", "parallel", "arbitrary"),
        name="slc_attention",
    )(qt_of, ks_of, q, kv, tb, mask)


WIN_TILES = WINDOW // ATT_TILE + 1
WIN_SPAN = WIN_TILES * ATT_TILE


def _win_kernel(q_ref, *refs):
    kv_refs, (bias_ref, o_ref) = refs[:WIN_TILES], refs[WIN_TILES:]
    qt = pl.program_id(1)
    tq = q_ref.shape[0]
    row = lax.broadcasted_iota(jnp.int32, (tq, WIN_SPAN), 0)
    col = lax.broadcasted_iota(jnp.int32, (tq, WIN_SPAN), 1)
    back = row + WINDOW - col
    mask = (back >= 0) & (back <= WINDOW) & (qt * tq - WINDOW + col >= 0)
    madd = jnp.where(mask, 0.0, NEG)
    for g in range(KV_GROUPS):
        k = jnp.concatenate([r[:, g * 2 * HEAD_DIM:(g * 2 + 1) * HEAD_DIM].astype(BF16) for r in kv_refs], axis=0)
        v = jnp.concatenate([r[:, (g * 2 + 1) * HEAD_DIM:(g * 2 + 2) * HEAD_DIM].astype(BF16) for r in kv_refs], axis=0)
        heads = range(g * GROUP_SIZE, (g + 1) * GROUP_SIZE)
        q4 = jnp.concatenate([q_ref[:, h * HEAD_DIM:(h + 1) * HEAD_DIM] for h in heads], axis=0)
        badd = jnp.concatenate([bias_ref[h] + madd for h in heads], axis=0)
        s = lax.dot_general(q4, k, NT_DIMS, preferred_element_type=F32) * SCALE + badd
        e = jnp.exp(s - jnp.max(s, axis=-1, keepdims=True))
        l = jnp.sum(e, axis=-1, keepdims=True)
        o = jnp.dot(e.astype(BF16), v, preferred_element_type=F32) / l
        for r, h in enumerate(heads):
            o_ref[:, h * HEAD_DIM:(h + 1) * HEAD_DIM] = o[r * tq:(r + 1) * tq, :]


def win_attention(q, kv, bias_win, bsz, seq):
    tq = ATT_TILE
    nqt = seq // tq

    def kv_spec(i):
        return pl.BlockSpec((tq, KV_COLS), lambda b, t: (b * nqt + jnp.maximum(t - (WIN_TILES - 1) + i, 0), 0))

    return pl.pallas_call(
        _win_kernel,
        grid=(bsz, nqt),
        in_specs=[pl.BlockSpec((tq, D_MODEL), lambda b, t: (b * nqt + t, 0))]
        + [kv_spec(i) for i in range(WIN_TILES)]
        + [pl.BlockSpec(bias_win.shape, lambda b, t: (0, 0, 0))],
        out_specs=pl.BlockSpec((tq, D_MODEL), lambda b, t: (b * nqt + t, 0)),
        out_shape=jax.ShapeDtypeStruct((bsz * seq, D_MODEL), F32),
        compiler_params=_params("parallel", "arbitrary"),
        name="win_attention",
    )(q, *([kv] * WIN_TILES), bias_win)


def _combine_kernel(oc_ref, os_ref, ow_ref, g_ref, o_ref):
    gate = g_ref[...]
    for h in range(N_HEADS):
        cols = slice(h * HEAD_DIM, (h + 1) * HEAD_DIM)
        o = (oc_ref[:, cols] * gate[:, 3 * h:3 * h + 1] + os_ref[:, cols] * gate[:, 3 * h + 1:3 * h + 2]
             + ow_ref[:, cols] * gate[:, 3 * h + 2:3 * h + 3])
        o_ref[:, cols] = o.astype(o_ref.dtype)


def nsa_combine(o_cmp, o_slc, o_win, gate, tm):
    r = o_cmp.shape[0]
    spec = pl.BlockSpec((tm, D_MODEL), lambda m: (m, 0))
    return pl.pallas_call(
        _combine_kernel,
        grid=(r // tm,),
        in_specs=[spec, spec, spec, pl.BlockSpec((tm, LANES), lambda m: (m, 0))],
        out_specs=spec,
        out_shape=jax.ShapeDtypeStruct((r, D_MODEL), BF16),
        compiler_params=_params("parallel"),
        name="nsa_combine",
    )(o_cmp, o_slc, o_win, gate)


N_SEL_S = -(-(PAST_LEN + 1) // SEL_BLOCK)
N_SEL_S_PAD = 384
N_CMP_S = PAST_LEN // CMP_STRIDE


def _group_rows(parts, hgrp):
    out = parts[0]
    for g in range(1, KV_GROUPS):
        out = jnp.where(hgrp == g, parts[g], out)
    return out


def _sample_attn_kernel(q_ref, kc_ref, bc_ref, gsum_ref, msel_ref, win_ref, new_ref, bw_ref, bn_ref,
                        oc_ref, ow_ref, idx_ref):
    q = q_ref[...]
    hgrp = lax.broadcasted_iota(jnp.int32, (N_HEADS, 1), 0) // GROUP_SIZE

    def kcol(g):
        return slice(g * 2 * HEAD_DIM, (g * 2 + 1) * HEAD_DIM)

    def vcol(g):
        return slice((g * 2 + 1) * HEAD_DIM, (g * 2 + 2) * HEAD_DIM)

    s = _group_rows([lax.dot_general(q, kc_ref[:, kcol(g)], NT_DIMS, preferred_element_type=F32)
                     for g in range(KV_GROUPS)], hgrp)
    s = s * SCALE + bc_ref[...]
    n = lax.broadcasted_iota(jnp.int32, s.shape, 1)
    p = _softmax_rows(s, n * CMP_STRIDE + (CMP_BLOCK - 1) <= PAST_LEN)
    pb = p.astype(BF16)
    oc_ref[...] = _group_rows([jnp.dot(pb, kc_ref[:, vcol(g)], preferred_element_type=F32)
                               for g in range(KV_GROUPS)], hgrp)
    imp = jnp.dot(gsum_ref[...], p, precision=HIGHEST, preferred_element_type=F32)
    p_sel = jnp.dot(imp, msel_ref[...], precision=HIGHEST, preferred_element_type=F32)
    j = lax.broadcasted_iota(jnp.int32, p_sel.shape, 1)
    cur = PAST_LEN // SEL_BLOCK
    forced = (j == 0) | (j == cur) | (j == cur - 1)
    score = jnp.where(j <= cur, p_sel + jnp.where(forced, FORCE_BONUS, 0.0), -jnp.inf)
    lane = lax.broadcasted_iota(jnp.int32, idx_ref.shape, 1)
    idx = jnp.zeros(idx_ref.shape, F32)
    jf = j.astype(F32)
    for kk in range(N_SELECT):
        mx = jnp.max(score, axis=-1, keepdims=True)
        pick = jnp.min(jnp.where(score == mx, jf, float(N_SEL_S_PAD)), axis=-1, keepdims=True)
        idx = jnp.where(lane == kk, pick, idx)
        score = jnp.where(jf == pick, -jnp.inf, score)
    idx_ref[...] = idx.astype(jnp.int32)
    wb = win_ref.shape[0] // N_KV_SLABS

    def win_slab(slab):
        return win_ref[pl.ds(slab, wb, stride=N_KV_SLABS), :].astype(BF16)

    sw = _group_rows([lax.dot_general(q, win_slab(2 * g), NT_DIMS, preferred_element_type=F32)
                      for g in range(KV_GROUPS)], hgrp)
    sw = sw * SCALE + bw_ref[...]
    qf = q.astype(F32)
    sn = _group_rows([jnp.sum(qf * new_ref[:, kcol(g)].astype(BF16).astype(F32), axis=-1, keepdims=True)
                      for g in range(KV_GROUPS)], hgrp)
    sn = sn * SCALE + bn_ref[:, 0:1]
    m = jnp.maximum(jnp.max(sw, axis=-1, keepdims=True), sn)
    ew, en = jnp.exp(sw - m), jnp.exp(sn - m)
    l = jnp.sum(ew, axis=-1, keepdims=True) + en
    pw, pn = (ew / l).astype(BF16), (en / l).astype(BF16).astype(F32)
    ow = _group_rows([jnp.dot(pw, win_slab(2 * g + 1), preferred_element_type=F32)
                      + pn * new_ref[:, vcol(g)].astype(BF16).astype(F32) for g in range(KV_GROUPS)], hgrp)
    ow_ref[...] = ow


def sample_attention(q3, kc, bias_c, gsum, msel, win_pool, layer, kv_win_new, bias_w, bias_new):
    bsz = q3.shape[0]
    wb = win_pool.shape[1]
    o = jax.ShapeDtypeStruct((bsz, N_HEADS, HEAD_DIM), F32)
    full2 = lambda b: (0, 0)
    return pl.pallas_call(
        _sample_attn_kernel,
        grid=(bsz,),
        in_specs=[pl.BlockSpec((None, N_HEADS, HEAD_DIM), lambda b: (b, 0, 0)),
                  pl.BlockSpec((None, N_CMP_S, KV_COLS), lambda b: (b, 0, 0)),
                  pl.BlockSpec(bias_c.shape, full2),
                  pl.BlockSpec(gsum.shape, full2),
                  pl.BlockSpec(msel.shape, full2),
                  pl.BlockSpec((None, wb, HEAD_DIM), lambda b: (layer * bsz + b, 0, 0)),
                  pl.BlockSpec((None, 1, KV_COLS), lambda b: (b, 0, 0)),
                  pl.BlockSpec(bias_w.shape, full2),
                  pl.BlockSpec(bias_new.shape, full2)],
        out_specs=[pl.BlockSpec((None, N_HEADS, HEAD_DIM), lambda b: (b, 0, 0)),
                   pl.BlockSpec((None, N_HEADS, HEAD_DIM), lambda b: (b, 0, 0)),
                   pl.BlockSpec((None, SUBLANES, LANES), lambda b: (b, 0, 0))],
        out_shape=[o, o, jax.ShapeDtypeStruct((bsz, SUBLANES, LANES), jnp.int32)],
        compiler_params=_params("parallel"),
        name="sample_attention",
    )(q3, kc, bias_c, gsum, msel, win_pool, kv_win_new, bias_w, bias_new)


def _sample_slc_kernel(idx_ref, pt_ref, q_ref, *refs):
    blk_refs, (new_ref, bias_ref, o_ref) = refs[:N_SELECT], refs[N_SELECT:]
    b, g = pl.program_id(0), pl.program_id(1)
    ks, vs, biases = [], [], []
    lane = lax.broadcasted_iota(jnp.int32, (N_HEADS, SEL_BLOCK), 1)
    for kk, blk_ref in enumerate(blk_refs):
        j = idx_ref[b, g, kk]
        is_new = j >= PAST_LEN // SEL_BLOCK
        k_blk = blk_ref[pl.ds(2 * g, SEL_BLOCK, stride=N_KV_SLABS), :]
        v_blk = blk_ref[pl.ds(2 * g + 1, SEL_BLOCK, stride=N_KV_SLABS), :]
        ks.append(jnp.where(is_new, jnp.broadcast_to(new_ref[:, :HEAD_DIM], k_blk.shape), k_blk).astype(BF16))
        vs.append(jnp.where(is_new, jnp.broadcast_to(new_ref[:, HEAD_DIM:], v_blk.shape), v_blk).astype(BF16))
        biases.append(bias_ref[j] + jnp.where(j * SEL_BLOCK + lane <= PAST_LEN, 0.0, NEG))
    k, v = jnp.concatenate(ks, axis=0), jnp.concatenate(vs, axis=0)
    s = lax.dot_general(q_ref[...], k, NT_DIMS, preferred_element_type=F32) * SCALE + jnp.concatenate(biases, axis=1)
    p = _softmax_rows(s, s > 0.5 * NEG)
    o_ref[...] = jnp.dot(p.astype(BF16), v, preferred_element_type=F32)


def sample_slc_attention(idx, page_table, q3, pool, n_phys, layer, kv_slc_new, bias_blk):
    bsz = q3.shape[0]
    half_per_page = PAGE_SIZE // SEL_BLOCK
    n_half = n_phys * half_per_page
    last_past = PAST_LEN // SEL_BLOCK - 1

    def blk_spec(kk):
        def blk_map(b, g, idx_r, pt_r):
            j = jnp.minimum(idx_r[b, g, kk], last_past)
            return (layer * n_half + pt_r[b, j // half_per_page] * half_per_page + j % half_per_page, 0, 0)
        return pl.BlockSpec((None, SEL_BLOCK * N_KV_SLABS, HEAD_DIM), blk_map)

    return pl.pallas_call(
        _sample_slc_kernel,
        grid_spec=pltpu.PrefetchScalarGridSpec(
            num_scalar_prefetch=2,
            grid=(bsz, KV_GROUPS),
            in_specs=[pl.BlockSpec((None, N_HEADS, HEAD_DIM), lambda b, g, i, p: (b, 0, 0))]
            + [blk_spec(kk) for kk in range(N_SELECT)]
            + [pl.BlockSpec((None, 1, 2 * HEAD_DIM), lambda b, g, i, p: (b, 0, g)),
               pl.BlockSpec(bias_blk.shape, lambda b, g, i, p: (0, 0, 0))],
            out_specs=pl.BlockSpec((None, None, N_HEADS, HEAD_DIM), lambda b, g, i, p: (b, g, 0, 0))),
        out_shape=jax.ShapeDtypeStruct((bsz, KV_GROUPS, N_HEADS, HEAD_DIM), F32),
        compiler_params=_params("parallel", "arbitrary"),
        name="sample_slc_attention",
    )(idx, page_table, q3, *([pool] * N_SELECT), kv_slc_new, bias_blk)


CONV_TILE = 128
CONV_HALO = 32
CONV_ROWS = 64


def _dwconv_ln_kernel(u_ref, halo_ref, w_ref, b_ref, g_ref, beta_ref, o_ref, buf_ref, y_ref, *, tiles_per_seq):
    first = pl.program_id(0) % tiles_per_seq == 0
    buf_ref[0:CONV_HALO, :] = jnp.where(first, 0.0, halo_ref[...])
    buf_ref[CONV_HALO:, :] = u_ref[...]
    lead = CONV_HALO - (CONV_WIDTH - 1)
    for r0 in range(0, CONV_TILE, CONV_ROWS):
        for c0 in range(0, D_MODEL, LANES):
            cols = slice(c0, c0 + LANES)
            acc = jnp.broadcast_to(b_ref[:, cols], (CONV_ROWS, LANES))
            for k in range(CONV_WIDTH):
                acc = acc + w_ref[k:k + 1, cols] * buf_ref[pl.ds(r0 + lead + k, CONV_ROWS), cols]
            y_ref[r0:r0 + CONV_ROWS, cols] = acc
    y = y_ref[...]
    mu = jnp.mean(y, axis=-1, keepdims=True)
    var = jnp.mean(jnp.square(y - mu), axis=-1, keepdims=True)
    y = (y - mu) * lax.rsqrt(var + LN_EPS) * g_ref[...] + beta_ref[...]
    o_ref[...] = (y * jax.nn.sigmoid(y)).astype(o_ref.dtype)


def dwconv_ln(u, w, b, ln_g, ln_b, seq):
    r, d = u.shape
    tiles_per_seq = seq // CONV_TILE
    ratio = CONV_TILE // CONV_HALO
    vec = lambda m: (0, 0)
    return pl.pallas_call(
        functools.partial(_dwconv_ln_kernel, tiles_per_seq=tiles_per_seq),
        grid=(r // CONV_TILE,),
        in_specs=[pl.BlockSpec((CONV_TILE, d), lambda m: (m, 0)),
                  pl.BlockSpec((CONV_HALO, d), lambda m: (jnp.maximum(m * ratio - 1, 0), 0)),
                  pl.BlockSpec((CONV_WIDTH, d), vec), pl.BlockSpec((1, d), vec),
                  pl.BlockSpec((1, d), vec), pl.BlockSpec((1, d), vec)],
        out_specs=pl.BlockSpec((CONV_TILE, d), lambda m: (m, 0)),
        out_shape=jax.ShapeDtypeStruct((r, d), BF16),
        scratch_shapes=[pltpu.VMEM((CONV_HALO + CONV_TILE, d), F32), pltpu.VMEM((CONV_TILE, d), F32)],
        compiler_params=_params("parallel"),
        name="dwconv_ln",
    )(u, u, w, b.reshape(1, d), ln_g.reshape(1, d), ln_b.reshape(1, d))


def _dwconv_ln_decode_kernel(u_ref, hist_ref, w_ref, b_ref, g_ref, beta_ref, o_ref):
    y = b_ref[...] + w_ref[CONV_WIDTH - 1:CONV_WIDTH, :] * u_ref[...]
    for k in range(CONV_WIDTH - 1):
        y = y + w_ref[k:k + 1, :] * hist_ref[k]
    mu = jnp.mean(y, axis=-1, keepdims=True)
    var = jnp.mean(jnp.square(y - mu), axis=-1, keepdims=True)
    y = (y - mu) * lax.rsqrt(var + LN_EPS) * g_ref[...] + beta_ref[...]
    o_ref[...] = (y * jax.nn.sigmoid(y)).astype(o_ref.dtype)


def dwconv_ln_decode(u, hist_t, w, b, ln_g, ln_b):
    r, d = u.shape
    vec = lambda i: (0, 0)
    return pl.pallas_call(
        _dwconv_ln_decode_kernel,
        grid=(1,),
        in_specs=[pl.BlockSpec((r, d), vec), pl.BlockSpec(hist_t.shape, lambda i: (0, 0, 0)),
                  pl.BlockSpec((CONV_WIDTH, d), vec), pl.BlockSpec((1, d), vec),
                  pl.BlockSpec((1, d), vec), pl.BlockSpec((1, d), vec)],
        out_specs=pl.BlockSpec((r, d), vec),
        out_shape=jax.ShapeDtypeStruct((r, d), BF16),
        compiler_params=_params("arbitrary"),
        name="dwconv_ln_decode",
    )(u, hist_t, w, b.reshape(1, d), ln_g.reshape(1, d), ln_b.reshape(1, d))


S5_CHUNK = 256
S5_PITCH = S5_CHUNK + 4
IN_SLABS = D_MODEL // LANES
STATE_PER_IN = N_SLAB // IN_SLABS


def _s5_project_in(hb, wb_ref, store):
    half = STATE_PER_IN * LANES
    for i in range(IN_SLABS):
        res = jnp.dot(hb[:, i * LANES:(i + 1) * LANES], wb_ref[i], preferred_element_type=F32)
        for jj in range(STATE_PER_IN):
            store(i * STATE_PER_IN + jj, res[:, jj * LANES:(jj + 1) * LANES],
                  res[:, half + jj * LANES:half + (jj + 1) * LANES])


def _s5_project_out(load, cre_ref, cim_ref, hn, d_ref, y_ref):
    for i in range(IN_SLABS):
        cols = slice(i * LANES, (i + 1) * LANES)
        acc = d_ref[:, cols] * hn[:, cols]
        for jj in range(STATE_PER_IN):
            j = i * STATE_PER_IN + jj
            re, im = load(j)
            acc = acc + jnp.dot(re.astype(BF16), cre_ref[j], preferred_element_type=F32)
            acc = acc - jnp.dot(im.astype(BF16), cim_ref[j], preferred_element_type=F32)
        y_ref[:, cols] = acc.astype(y_ref.dtype)


def _s5_scan_kernel(x_ref, g_ref, wb_ref, ar_ref, ai_ref, cre_ref, cim_ref, d_ref,
                    y_ref, sr_ref, si_ref, bur_ref, bui_ref, hr_ref, hi_ref):
    tc, pitch = S5_CHUNK, S5_PITCH

    @pl.when(pl.program_id(1) == 0)
    def _():
        hr_ref[...] = jnp.zeros_like(hr_ref)
        hi_ref[...] = jnp.zeros_like(hi_ref)

    hn = _rms(x_ref[...], g_ref[...])

    def store(j, re, im):
        bur_ref[j * pitch:j * pitch + tc, :] = re
        bui_ref[j * pitch:j * pitch + tc, :] = im

    _s5_project_in(hn.astype(BF16), wb_ref, store)

    ar = [ar_ref[j8] for j8 in range(SLAB_GRP)]
    ai = [ai_ref[j8] for j8 in range(SLAB_GRP)]

    def step(t, carry):
        out = []
        for j8 in range(SLAB_GRP):
            hr, hi = carry[2 * j8], carry[2 * j8 + 1]
            rows = pl.ds(j8 * SUBLANES * pitch + t, SUBLANES, stride=pitch)
            nr = ar[j8] * hr - ai[j8] * hi + bur_ref[rows, :]
            ni = ar[j8] * hi + ai[j8] * hr + bui_ref[rows, :]
            bur_ref[rows, :] = nr
            bui_ref[rows, :] = ni
            out += [nr, ni]
        return tuple(out)

    init = []
    for j8 in range(SLAB_GRP):
        init += [hr_ref[j8], hi_ref[j8]]
    fin = lax.fori_loop(0, tc, step, tuple(init))
    for j8 in range(SLAB_GRP):
        hr_ref[j8] = fin[2 * j8]
        hi_ref[j8] = fin[2 * j8 + 1]
    sr_ref[...] = hr_ref[...]
    si_ref[...] = hi_ref[...]

    def load(j):
        return bur_ref[j * pitch:j * pitch + tc, :], bui_ref[j * pitch:j * pitch + tc, :]

    _s5_project_out(load, cre_ref, cim_ref, hn, d_ref, y_ref)


def s5_scan(x, g, wb, ar, ai, cre, cim, d_skip, bsz, seq):
    n_chunks = seq // S5_CHUNK
    st = jax.ShapeDtypeStruct((bsz, SLAB_GRP, SUBLANES, LANES), F32)
    st_spec = pl.BlockSpec((None, SLAB_GRP, SUBLANES, LANES), lambda b, c: (b, 0, 0, 0))
    vec = lambda b, c: (0, 0)
    c3 = lambda b, c: (0, 0, 0)
    return pl.pallas_call(
        _s5_scan_kernel,
        grid=(bsz, n_chunks),
        in_specs=[pl.BlockSpec((S5_CHUNK, D_MODEL), lambda b, c: (b * n_chunks + c, 0)),
                  pl.BlockSpec((1, D_MODEL), vec),
                  pl.BlockSpec(wb.shape, c3), pl.BlockSpec(ar.shape, c3), pl.BlockSpec(ai.shape, c3),
                  pl.BlockSpec(cre.shape, c3), pl.BlockSpec(cim.shape, c3),
                  pl.BlockSpec((1, D_MODEL), vec)],
        out_specs=[pl.BlockSpec((S5_CHUNK, D_MODEL), lambda b, c: (b * n_chunks + c, 0)), st_spec, st_spec],
        out_shape=[jax.ShapeDtypeStruct((bsz * seq, D_MODEL), BF16), st, st],
        scratch_shapes=[pltpu.VMEM((N_SLAB * S5_PITCH, LANES), F32), pltpu.VMEM((N_SLAB * S5_PITCH, LANES), F32),
                        pltpu.VMEM((SLAB_GRP, SUBLANES, LANES), F32), pltpu.VMEM((SLAB_GRP, SUBLANES, LANES), F32)],
        compiler_params=_params("parallel", "arbitrary"),
        name="s5_scan",
    )(x, g.reshape(1, -1), wb, ar, ai, cre, cim, d_skip.reshape(1, -1))


def _s5_decode_kernel(x_ref, g_ref, wb_ref, ar_ref, ai_ref, cre_ref, cim_ref, d_ref, h0r_ref, h0i_ref,
                      y_ref, sr_ref, si_ref):
    hn = _rms(x_ref[...], g_ref[...])

    def store(j, re, im):
        cols = slice(j * LANES, (j + 1) * LANES)
        ar, ai = ar_ref[:, cols], ai_ref[:, cols]
        hr, hi = h0r_ref[:, cols], h0i_ref[:, cols]
        sr_ref[:, cols] = ar * hr - ai * hi + re
        si_ref[:, cols] = ar * hi + ai * hr + im

    _s5_project_in(hn.astype(BF16), wb_ref, store)

    def load(j):
        cols = slice(j * LANES, (j + 1) * LANES)
        return sr_ref[:, cols], si_ref[:, cols]

    _s5_project_out(load, cre_ref, cim_ref, hn, d_ref, y_ref)


def s5_decode(x, g, wb, ar_row, ai_row, cre, cim, d_skip, h0r, h0i):
    r = x.shape[0]
    st = jax.ShapeDtypeStruct((r, SSM_DIM), F32)
    vec = lambda i: (0, 0)
    c3 = lambda i: (0, 0, 0)
    return pl.pallas_call(
        _s5_decode_kernel,
        grid=(1,),
        in_specs=[pl.BlockSpec((r, D_MODEL), vec), pl.BlockSpec((1, D_MODEL), vec),
                  pl.BlockSpec(wb.shape, c3), pl.BlockSpec((1, SSM_DIM), vec), pl.BlockSpec((1, SSM_DIM), vec),
                  pl.BlockSpec(cre.shape, c3), pl.BlockSpec(cim.shape, c3), pl.BlockSpec((1, D_MODEL), vec),
                  pl.BlockSpec((r, SSM_DIM), vec), pl.BlockSpec((r, SSM_DIM), vec)],
        out_specs=[pl.BlockSpec((r, D_MODEL), vec), pl.BlockSpec((r, SSM_DIM), vec), pl.BlockSpec((r, SSM_DIM), vec)],
        out_shape=[jax.ShapeDtypeStruct((r, D_MODEL), BF16), st, st],
        compiler_params=_params("arbitrary"),
        name="s5_decode",
    )(x, g.reshape(1, -1), wb, ar_row, ai_row, cre, cim, d_skip.reshape(1, -1), h0r, h0i)


def _t5_bucket(rel):
    n = jnp.maximum(rel, 0)
    nf = jnp.maximum(n, MAX_EXACT).astype(F32)
    big = MAX_EXACT + (jnp.log(nf / MAX_EXACT) / math.log(REL_MAX_DIST / MAX_EXACT)
                       * (N_BUCKETS - MAX_EXACT)).astype(jnp.int32)
    return jnp.where(n < MAX_EXACT, n, jnp.minimum(big, N_BUCKETS - 1))


def _bias_of(rel_bias, rel):
    onehot = (_t5_bucket(rel)[..., None] == jnp.arange(N_BUCKETS, dtype=jnp.int32)).astype(F32)
    return jnp.einsum('...k,kh->h...', onehot, rel_bias.astype(F32), precision=HIGHEST)


def _selection_matrix(n_cmp, n_sel_pad):
    coef = np.convolve(np.ones(SEL_RATIO), np.ones(CMP_BLOCK // CMP_STRIDE)).astype(np.float32)
    m = np.zeros((n_cmp, n_sel_pad), np.float32)
    for j in range(n_sel_pad):
        for o in range(coef.shape[0]):
            n = SEL_RATIO * j + o - (CMP_BLOCK // CMP_STRIDE - 1)
            if 0 <= n < n_cmp:
                m[n, j] = coef[o]
    return m


def _s5_params(a_re, a_im, log_dt, b_re, b_im, c_re, c_im):
    dt = jnp.exp(log_dt.astype(F32))[:, None]
    ar, ai = a_re.astype(F32), a_im.astype(F32)
    mag = jnp.exp(ar * dt)
    abar_re, abar_im = mag * jnp.cos(ai * dt), mag * jnp.sin(ai * dt)
    den = ar * ar + ai * ai
    coef_re = ((abar_re - 1.0) * ar + abar_im * ai) / den
    coef_im = (abar_im * ar - (abar_re - 1.0) * ai) / den
    br, bim = b_re.astype(F32), b_im.astype(F32)
    bb_re = coef_re[..., None] * br - coef_im[..., None] * bim
    bb_im = coef_re[..., None] * bim + coef_im[..., None] * br
    gpi = LANES // SSM_GROUP_CH
    eye = jnp.eye(gpi, dtype=F32)

    def in_blocks(bb):
        t = bb.reshape(IN_SLABS, gpi, SSM_STATE, SSM_GROUP_CH)
        blk = jnp.einsum('sgpc,gh->sgchp', t, eye)
        return blk.reshape(IN_SLABS, LANES, gpi * SSM_STATE)

    wb = jnp.concatenate([in_blocks(bb_re), in_blocks(bb_im)], axis=-1).astype(BF16)
    gps = LANES // SSM_STATE
    ch_per_in = LANES

    def out_blocks(c):
        t = c.astype(F32).reshape(IN_SLABS, STATE_PER_IN, gps, SSM_GROUP_CH, SSM_STATE)
        sel = jnp.eye(STATE_PER_IN * gps, dtype=F32).reshape(STATE_PER_IN, gps, STATE_PER_IN * gps)
        blk = jnp.einsum('ijgcp,jgh->ijgphc', t, sel)
        return blk.reshape(N_SLAB, LANES, ch_per_in).astype(BF16)

    return abar_re, abar_im, wb, out_blocks(c_re), out_blocks(c_im)


PROMPT_TM = 512
TN = 512
FFN_DOWN_TK = D_FF // 4
GLU_TM = 256
GLU_TK = 1024


def _row_tile(r):
    return PROMPT_TM if r % PROMPT_TM == 0 else r


def _nsa_project(h, wq, wkv, wg):
    tm = _row_tile(h.shape[0])
    q = matmul(h, wq, n_split=1, out_dtype=BF16, act=None, tm=tm, tn=TN, name="nsa_q")[0]
    kv = matmul(h, wkv, n_split=3, out_dtype=F32, act=None, tm=tm, tn=TN, name="nsa_kv")
    gate = matmul(h, wg, n_split=1, out_dtype=F32, act="sigmoid", tm=tm, tn=LANES, name="nsa_gate")[0]
    return q, kv, gate


def _nsa_layer(hp, hs, xp, xs, g_post, g_next, wts, tabs, caches, layer, page_table, bsz, seq):
    wq, wkv, wg, wo, pe, w1, w2 = wts
    cmp_pool, slc_pool, win_pool, n_phys = caches
    dec = hs.shape[0]
    q, kv, gate = _nsa_project(hp, wq, wkv, wg)
    part_a, part_b = cmp_partial_rows(kv[0], pe, w1, seq)
    kc = cmp_finish(part_a, part_b, jnp.zeros((bsz, SUBLANES, KV_COLS), F32), w2)
    o_cmp, mask = cmp_attention(q, kc, tabs["cmp"], tabs["msel"], tabs["expand"], bsz, seq)
    o_slc = slc_attention(q, kv[1], mask, tabs["tile"], bsz, seq)
    o_win = win_attention(q, kv[2], tabs["win"], bsz, seq)
    o = nsa_combine(o_cmp, o_slc, o_win, gate, PROMPT_TM)
    xp, hp = proj_res(o, wo, xp, g_post, g_next, glu=False, tm=PROMPT_TM, tk=D_MODEL, name="nsa_out")
    qs, kvs, gate_s = _nsa_project(hs, wq, wkv, wg)
    past_a, past_b = cmp_partial_paged(cmp_pool, n_phys, page_table, layer, pe, w1)
    tail = jnp.pad(kvs[0][:, None, :], ((0, 0), (0, CMP_STRIDE - 1), (0, 0))).reshape(dec * CMP_STRIDE, KV_COLS)
    _, tail_b = cmp_partial_rows(tail, pe, w1, dec * CMP_STRIDE)
    b_next = jnp.pad(tail_b[0][:, None, :], ((0, 0), (0, SUBLANES - 1), (0, 0)))
    kc_s = cmp_finish(past_a, past_b, b_next, w2)
    oc_s, ow_s, idx = sample_attention(qs.reshape(dec, N_HEADS, HEAD_DIM), kc_s, tabs["cmp_s"], tabs["gsum"],
                                       tabs["msel_s"], win_pool, layer, kvs[2][:, None, :],
                                       tabs["win_s"], tabs["new_s"])
    q3 = qs.reshape(dec, N_HEADS, HEAD_DIM)
    os_all = sample_slc_attention(idx[:, :KV_GROUPS, :N_SELECT], page_table, q3, slc_pool, n_phys, layer,
                                  kvs[1][:, None, :], tabs["slc_s"])
    os_s = jnp.stack([os_all[:, h // GROUP_SIZE, h] for h in range(N_HEADS)], axis=1)
    o_s = nsa_combine(oc_s.reshape(dec, D_MODEL), os_s.reshape(dec, D_MODEL), ow_s.reshape(dec, D_MODEL), gate_s, dec)
    xs, hs = proj_res(o_s, wo, xs, g_post, g_next, glu=False, tm=dec, tk=D_MODEL, name="nsa_out_s")
    return xp, hp, xs, hs, kv, kvs


def _conv_layer(hp, hs, xp, xs, g_post, g_next, wts, state, bsz, seq):
    w_pw1, dw, dw_b, ln_g, ln_b, w_pw2 = wts
    dec = hs.shape[0]
    u = glu_matmul(hp, w_pw1, tm=PROMPT_TM, tn=TN, name="conv_pw1")
    hc = dwconv_ln(u, dw, dw_b, ln_g, ln_b, seq)
    xp, hp = proj_res(hc, w_pw2, xp, g_post, g_next, glu=False, tm=PROMPT_TM, tk=D_MODEL, name="conv_pw2")
    hist_p = u.reshape(bsz, seq, D_MODEL)[:, seq - (CONV_WIDTH - 1):]
    us = glu_matmul(hs, w_pw1, tm=dec, tn=TN, name="conv_pw1_s")
    hc_s = dwconv_ln_decode(us, jnp.swapaxes(state, 0, 1), dw, dw_b, ln_g, ln_b)
    xs, hs = proj_res(hc_s, w_pw2, xs, g_post, g_next, glu=False, tm=dec, tk=D_MODEL, name="conv_pw2_s")
    hist_s = jnp.concatenate([state[:, 1:], us[:, None, :]], axis=1)
    return xp, hp, xs, hs, hist_p, hist_s


def _s5_layer(xp, xs, g_pre, g_post, g_next, wts, state_re, state_im, bsz, seq):
    a_re, a_im, log_dt, b_re, b_im, c_re, c_im, d_skip, w_glu = wts
    dec = xs.shape[0]
    abar_re, abar_im, wb, cre, cim = _s5_params(a_re, a_im, log_dt, b_re, b_im, c_re, c_im)
    slab_shape = (SLAB_GRP, SUBLANES, LANES)
    y, sr, si = s5_scan(xp, g_pre, wb, abar_re.reshape(slab_shape), abar_im.reshape(slab_shape), cre, cim,
                        d_skip, bsz, seq)
    xp, hp = proj_res(y, w_glu, xp, g_post, g_next, glu=True, tm=GLU_TM, tk=GLU_TK, name="s5_glu")
    ys, sr_s, si_s = s5_decode(xs, g_pre, wb, abar_re.reshape(1, SSM_DIM), abar_im.reshape(1, SSM_DIM), cre, cim,
                               d_skip, state_re.reshape(dec, SSM_DIM), state_im.reshape(dec, SSM_DIM))
    xs, hs = proj_res(ys, w_glu, xs, g_post, g_next, glu=True, tm=dec, tk=GLU_TK, name="s5_glu_s")
    gp = (SSM_GROUPS, SSM_STATE)
    return (xp, hp, xs, hs, sr.reshape((bsz,) + gp), si.reshape((bsz,) + gp),
            sr_s.reshape((dec,) + gp), si_s.reshape((dec,) + gp))


def _ffn_layer(hp, hs, xp, xs, g_post, g_next, wts, state, bsz, seq):
    w_up, dw, dw_b, w_down = wts
    dec = hs.shape[0]
    act, hist = ffn_up(hp, w_up, dw, dw_b, seq_len=seq, tm=PROMPT_TM, tn=TN)
    xp, hp = proj_res(act, w_down, xp, g_post, g_next, glu=False, tm=PROMPT_TM, tk=FFN_DOWN_TK, name="ffn_down")
    tiles = seq // PROMPT_TM
    hist_p = hist.reshape(bsz, tiles, SUBLANES, D_FF)[:, tiles - 1, SUBLANES - 2:, :]
    act_s, gate_s = ffn_up(hs, w_up, dw, dw_b, hist=state, seq_len=1, tm=dec, tn=TN)
    xs, hs = proj_res(act_s, w_down, xs, g_post, g_next, glu=False, tm=dec, tk=FFN_DOWN_TK, name="ffn_down_s")
    hist_s = jnp.concatenate([state[:, 1:], gate_s.reshape(dec, 1, D_FF)], axis=1)
    return xp, hp, xs, hs, hist_p, hist_s


def _bias_tables(rel_bias, seq):
    tq = ATT_TILE
    i = jnp.arange(tq, dtype=jnp.int32)
    tile = jnp.stack([_bias_of(rel_bias, d * tq + i[:, None] - i[None, :]) for d in range(3)])
    n_cmp = seq // CMP_STRIDE
    cpos = jnp.arange(n_cmp, dtype=jnp.int32) * CMP_STRIDE + (CMP_BLOCK - 1)
    qpos = jnp.arange(seq, dtype=jnp.int32)
    n_sel = -(-seq // SEL_BLOCK)
    key = np.arange(seq)
    expand = (key[None, :] // SEL_BLOCK == np.arange(LANES)[:, None]).astype(np.float32)
    cpos_s = jnp.arange(N_CMP_S, dtype=jnp.int32) * CMP_STRIDE + (CMP_BLOCK - 1)
    wb = min(WINDOW, PAST_LEN)
    kpos_s = jnp.arange(N_SEL_S * SEL_BLOCK, dtype=jnp.int32)
    slc_s = _bias_of(rel_bias, PAST_LEN - kpos_s).reshape(N_HEADS, N_SEL_S, SEL_BLOCK)
    gsum = (np.arange(N_HEADS)[None, :] // GROUP_SIZE == np.arange(SUBLANES)[:, None]).astype(np.float32)
    return {
        "tile": tile,
        "win": _bias_of(rel_bias, i[:, None] + WINDOW - jnp.arange(WIN_SPAN, dtype=jnp.int32)[None, :]),
        "cmp": _bias_of(rel_bias, qpos[:, None] - cpos[None, :]),
        "msel": jnp.asarray(_selection_matrix(n_cmp, n_sel).T),
        "expand": jnp.asarray(expand, BF16),
        "cmp_s": _bias_of(rel_bias, PAST_LEN - cpos_s),
        "msel_s": jnp.asarray(_selection_matrix(N_CMP_S, N_SEL_S_PAD)),
        "gsum": jnp.asarray(gsum),
        "win_s": _bias_of(rel_bias, wb - jnp.arange(wb, dtype=jnp.int32)),
        "new_s": jnp.broadcast_to(_bias_of(rel_bias, jnp.zeros((1,), jnp.int32)), (N_HEADS, LANES)),
        "slc_s": jnp.swapaxes(slc_s, 0, 1),
    }


def kernel(x_prompt, x_sample, cache_cmp_kv, cache_slc_kv, cache_win_kv, state_conv, state_ssm_re, state_ssm_im, state_ffn_conv, page_table, norm_gain, rel_bias, nsa_w_q, nsa_w_kv, nsa_cmp_pe, nsa_cmp_w1, nsa_cmp_w2, nsa_w_gate, nsa_w_o, conv_w_pw1, conv_dw, conv_dw_b, conv_ln_g, conv_ln_b, conv_w_pw2, ssm_a_re, ssm_a_im, ssm_log_dt, ssm_b_re, ssm_b_im, ssm_c_re, ssm_c_im, ssm_d, ssm_w_glu, ffn_w_up, ffn_dw, ffn_dw_b, ffn_w_down):
    bsz, seq, d = x_prompt.shape
    dec, dec_seq, _ = x_sample.shape
    assert dec_seq == 1 and d == D_MODEL and seq % PROMPT_TM == 0 and seq // CMP_STRIDE == LANES
    n_nsa = cache_cmp_kv.shape[0]
    n_phys = cache_cmp_kv.shape[1]
    xp = x_prompt.reshape(bsz * seq, d)
    xs = x_sample.reshape(dec, d)
    tabs = _bias_tables(rel_bias, seq)
    cmp_pool = cache_cmp_kv.reshape(n_nsa * n_phys, PAGE_SIZE * N_KV_SLABS, HEAD_DIM)
    slc_pool = cache_slc_kv.reshape(n_nsa * n_phys * (PAGE_SIZE // SEL_BLOCK), SEL_BLOCK * N_KV_SLABS, HEAD_DIM)
    win_pool = cache_win_kv.reshape(n_nsa * dec, cache_win_kv.shape[2] * N_KV_SLABS, HEAD_DIM)
    g_all = norm_gain.astype(F32)

    hp = rmsnorm_cast(xp, g_all[0, 0], PROMPT_TM)
    hs = rmsnorm_cast(xs, g_all[0, 0], dec)
    out = {k: [] for k in ("cmp_p", "cmp_s", "slc_p", "slc_s", "win_p", "win_s", "conv_p", "conv_s",
                           "re_p", "re_s", "im_p", "im_s", "ffn_p", "ffn_s")}
    counts = [0, 0, 0]
    for i in range(DEPTH):
        m = i % 3
        j = counts[m]
        counts[m] += 1
        g_post, g_ffn, g_ffn_post = g_all[i, 1], g_all[i, 2], g_all[i, 3]
        g_next = g_all[i + 1, 0] if i + 1 < DEPTH else g_all[i, 0]
        if m == 0:
            wg = jnp.pad(nsa_w_gate[j], ((0, 0), (0, LANES - nsa_w_gate.shape[-1]))).astype(BF16)
            wts = (nsa_w_q[j].astype(BF16), nsa_w_kv[j].astype(BF16), wg, nsa_w_o[j].astype(BF16),
                   jnp.swapaxes(nsa_cmp_pe[j], 0, 1).astype(F32), nsa_cmp_w1[j].astype(BF16),
                   nsa_cmp_w2[j].astype(BF16))
            xp, hp, xs, hs, kv, kvs = _nsa_layer(hp, hs, xp, xs, g_post, g_ffn, wts, tabs,
                                                 (cmp_pool, slc_pool, win_pool, n_phys), j, page_table, bsz, seq)
            shp = (bsz, seq, KV_GROUPS, 2, HEAD_DIM)
            shs = (dec, 1, KV_GROUPS, 2, HEAD_DIM)
            out["cmp_p"].append(kv[0].reshape(shp)); out["cmp_s"].append(kvs[0].reshape(shs))
            out["slc_p"].append(kv[1].reshape(shp)); out["slc_s"].append(kvs[1].reshape(shs))
            out["win_p"].append(kv[2].reshape(shp)[:, seq - min(WINDOW, seq):])
            win_full = jnp.concatenate([cache_win_kv[j], kvs[2].reshape(shs)], axis=1)
            out["win_s"].append(win_full[:, win_full.shape[1] - min(WINDOW, PAST_LEN + 1):])
        elif m == 1:
            wts = (conv_w_pw1[j].astype(BF16), conv_dw[j], conv_dw_b[j], conv_ln_g[j], conv_ln_b[j],
                   conv_w_pw2[j].astype(BF16))
            xp, hp, xs, hs, cp, cs = _conv_layer(hp, hs, xp, xs, g_post, g_ffn, wts, state_conv[j], bsz, seq)
            out["conv_p"].append(cp); out["conv_s"].append(cs)
        else:
            wts = (ssm_a_re[j], ssm_a_im[j], ssm_log_dt[j], ssm_b_re[j], ssm_b_im[j], ssm_c_re[j], ssm_c_im[j],
                   ssm_d[j], ssm_w_glu[j].astype(BF16))
            xp, hp, xs, hs, rp, ip, rs, is_ = _s5_layer(xp, xs, g_all[i, 0], g_post, g_ffn, wts,
                                                        state_ssm_re[j], state_ssm_im[j], bsz, seq)
            out["re_p"].append(rp); out["im_p"].append(ip); out["re_s"].append(rs); out["im_s"].append(is_)
        wts = (ffn_w_up[i].astype(BF16), ffn_dw[i], ffn_dw_b[i], ffn_w_down[i].astype(BF16))
        xp, hp, xs, hs, fp, fs = _ffn_layer(hp, hs, xp, xs, g_ffn_post, g_next, wts, state_ffn_conv[i], bsz, seq)
        out["ffn_p"].append(fp); out["ffn_s"].append(fs)
    st = lambda k: jnp.stack(out[k])
    return (xp.reshape(bsz, seq, d), xs.reshape(dec, 1, d),
            st("cmp_p"), st("cmp_s"), st("slc_p"), st("slc_s"), st("win_p"), st("win_s"),
            st("conv_p"), st("conv_s"), st("re_p"), st("re_s"), st("im_p"), st("im_s"),
            st("ffn_p"), st("ffn_s"))
```

```python
import functools
import math

import numpy as np
import jax
import jax.numpy as jnp
from jax import lax
from jax.experimental import pallas as pl
from jax.experimental.pallas import tpu as pltpu

F32 = jnp.float32
BF16 = jnp.bfloat16

D_MODEL = 2048
DEPTH = 4
PAST_LEN = 16384
PAGE_SIZE = 128
N_HEADS = 16
HEAD_DIM = 128
KV_GROUPS = 4
GROUP_SIZE = 4
KV_COLS = KV_GROUPS * 2 * HEAD_DIM
CMP_BLOCK = 32
CMP_STRIDE = 16
SEL_BLOCK = 64
SEL_RATIO = SEL_BLOCK // CMP_STRIDE
N_SELECT = 16
WINDOW = 512
FORCE_BONUS = 1.0e4
N_BUCKETS = 32
MAX_EXACT = 16
REL_MAX_DIST = 128
CONV_WIDTH = 31
SSM_GROUP_CH = 16
SSM_GROUPS = 128
SSM_STATE = 64
SSM_DIM = SSM_GROUPS * SSM_STATE
D_FF = 5632
RMS_EPS = 1e-6
LN_EPS = 1e-5
SCALE = HEAD_DIM ** -0.5
NEG = -1e30

LANES = 128
SUBLANES = 8
VMEM_LIMIT = 56 * 1024 * 1024
ATT_TILE = 128
N_SLAB = SSM_DIM // LANES
SLAB_GRP = N_SLAB // SUBLANES

HIGHEST = lax.Precision.HIGHEST
NT_DIMS = (((1,), (1,)), ((), ()))


def _params(*sem):
    return pltpu.CompilerParams(dimension_semantics=sem, vmem_limit_bytes=VMEM_LIMIT)


def _rms(x, g):
    return x * lax.rsqrt(jnp.mean(x * x, axis=-1, keepdims=True) + RMS_EPS) * g


def _rmsnorm_kernel(x_ref, g_ref, o_ref):
    o_ref[...] = _rms(x_ref[...], g_ref[...]).astype(o_ref.dtype)


def rmsnorm_cast(x, g, tm):
    r, d = x.shape
    return pl.pallas_call(
        _rmsnorm_kernel,
        grid=(r // tm,),
        in_specs=[pl.BlockSpec((tm, d), lambda m: (m, 0)), pl.BlockSpec((1, d), lambda m: (0, 0))],
        out_specs=pl.BlockSpec((tm, d), lambda m: (m, 0)),
        out_shape=jax.ShapeDtypeStruct((r, d), BF16),
        compiler_params=_params("parallel"),
        name="rmsnorm",
    )(x, g.reshape(1, d))


def _mm_kernel(a_ref, w_ref, o_ref, *, act):
    r = jnp.dot(a_ref[...], w_ref[...], preferred_element_type=F32)
    if act == "sigmoid":
        r = jax.nn.sigmoid(r)
    o_ref[...] = r.astype(o_ref.dtype)


def matmul(a, w, *, n_split, out_dtype, act, tm, tn, name):
    r, k = a.shape
    n = w.shape[1]
    per = (n // n_split) // tn
    return pl.pallas_call(
        functools.partial(_mm_kernel, act=act),
        grid=(r // tm, n // tn),
        in_specs=[pl.BlockSpec((tm, k), lambda m, j: (m, 0)), pl.BlockSpec((k, tn), lambda m, j: (0, j))],
        out_specs=pl.BlockSpec((None, tm, tn), lambda m, j: (j // per, m, j % per)),
        out_shape=jax.ShapeDtypeStruct((n_split, r, n // n_split), out_dtype),
        compiler_params=_params("parallel", "arbitrary"),
        name=name,
    )(a, w)


def _mm_nt_kernel(wt_ref, a_ref, o_ref):
    o_ref[...] = lax.dot_general(wt_ref[...], a_ref[...], NT_DIMS, preferred_element_type=F32).astype(o_ref.dtype)


def matmul_nt(wt, a, *, tm, name):
    n, k = wt.shape
    r = a.shape[0]
    return pl.pallas_call(
        _mm_nt_kernel,
        grid=(r // tm,),
        in_specs=[pl.BlockSpec((n, k), lambda m: (0, 0)), pl.BlockSpec((tm, k), lambda m: (m, 0))],
        out_specs=pl.BlockSpec((n, tm), lambda m: (0, m)),
        out_shape=jax.ShapeDtypeStruct((n, r), BF16),
        compiler_params=_params("parallel"),
        name=name,
    )(wt, a)


def _glu_mm_kernel(a_ref, wa_ref, wb_ref, o_ref):
    a = a_ref[...]
    lin = jnp.dot(a, wa_ref[...], preferred_element_type=F32)
    gate = jnp.dot(a, wb_ref[...], preferred_element_type=F32)
    o_ref[...] = lin * jax.nn.sigmoid(gate)


def glu_matmul(a, w, *, tm, tn, name):
    r, k = a.shape
    n = w.shape[1] // 2
    nb = n // tn
    return pl.pallas_call(
        _glu_mm_kernel,
        grid=(r // tm, nb),
        in_specs=[pl.BlockSpec((tm, k), lambda m, j: (m, 0)),
                  pl.BlockSpec((k, tn), lambda m, j: (0, j)),
                  pl.BlockSpec((k, tn), lambda m, j: (0, j + nb))],
        out_specs=pl.BlockSpec((tm, tn), lambda m, j: (m, j)),
        out_shape=jax.ShapeDtypeStruct((r, n), F32),
        compiler_params=_params("parallel", "arbitrary"),
        name=name,
    )(a, w, w)


def _proj_res_kernel(a_ref, w_ref, x_ref, gp_ref, gn_ref, xo_ref, ho_ref, *acc, glu, nk):
    def finish(y):
        if glu:
            d = y.shape[1] // 2
            y = y[:, :d] * jax.nn.sigmoid(y[:, d:])
        xn = x_ref[...] + _rms(y, gp_ref[...])
        xo_ref[...] = xn
        ho_ref[...] = _rms(xn, gn_ref[...]).astype(ho_ref.dtype)

    part = jnp.dot(a_ref[...], w_ref[...], preferred_element_type=F32)
    if nk == 1:
        finish(part)
        return
    acc_ref, = acc
    k = pl.program_id(1)

    @pl.when(k == 0)
    def _():
        acc_ref[...] = part

    @pl.when((k > 0) & (k < nk - 1))
    def _():
        acc_ref[...] += part

    @pl.when(k == nk - 1)
    def _():
        finish(acc_ref[...] + part)


def proj_res(a, w, x, g_post, g_next, *, glu, tm, tk, name):
    r, kdim = a.shape
    n = w.shape[1]
    d = x.shape[1]
    nk = kdim // tk
    return pl.pallas_call(
        functools.partial(_proj_res_kernel, glu=glu, nk=nk),
        grid=(r // tm, nk),
        in_specs=[pl.BlockSpec((tm, tk), lambda m, k: (m, k)),
                  pl.BlockSpec((tk, n), lambda m, k: (k, 0)),
                  pl.BlockSpec((tm, d), lambda m, k: (m, 0)),
                  pl.BlockSpec((1, d), lambda m, k: (0, 0)),
                  pl.BlockSpec((1, d), lambda m, k: (0, 0))],
        out_specs=[pl.BlockSpec((tm, d), lambda m, k: (m, 0)),
                   pl.BlockSpec((tm, d), lambda m, k: (m, 0))],
        out_shape=[jax.ShapeDtypeStruct((r, d), F32), jax.ShapeDtypeStruct((r, d), BF16)],
        scratch_shapes=[pltpu.VMEM((tm, n), F32)] if nk > 1 else [],
        compiler_params=_params("parallel", "arbitrary"),
        name=name,
    )(a, w, x, g_post.reshape(1, d), g_next.reshape(1, d))


def _ffn_up_kernel(a_ref, p1_ref, p2_ref, wg_ref, wv_ref, dw_ref, db_ref, act_ref, hist_ref, *, decode, tiles_per_seq):
    a = a_ref[...]
    gate = jnp.dot(a, wg_ref[...], preferred_element_type=F32)
    val = jnp.dot(a, wv_ref[...], preferred_element_type=F32)
    tm = gate.shape[0]
    if decode:
        g2, g1 = p2_ref[...], p1_ref[...]
        hist_ref[...] = gate
    else:
        halo = jnp.dot(p1_ref[...], wg_ref[...], preferred_element_type=F32)
        halo = jnp.where(pl.program_id(0) % tiles_per_seq == 0, 0.0, halo)
        h7, h6 = halo[7:8, :], halo[6:7, :]
        row = lax.broadcasted_iota(jnp.int32, gate.shape, 0)
        g1 = jnp.where(row == 0, h7, pltpu.roll(gate, 1, 0))
        g2 = jnp.where(row == 0, h6, jnp.where(row == 1, h7, pltpu.roll(gate, 2, 0)))
        hist_ref[...] = gate[tm - SUBLANES:, :]
    g = dw_ref[0:1, :] * g2 + dw_ref[1:2, :] * g1 + dw_ref[2:3, :] * gate + db_ref[...]
    act_ref[...] = (jax.nn.gelu(g) * val).astype(act_ref.dtype)


def ffn_up(h, w_up, dw, db, *, hist=None, seq_len, tm, tn):
    r, k = h.shape
    nb = D_FF // tn
    decode = hist is not None
    if decode:
        p1, p2 = hist[:, 1, :], hist[:, 0, :]
        p_specs = [pl.BlockSpec((tm, tn), lambda m, j: (m, j)), pl.BlockSpec((tm, tn), lambda m, j: (m, j))]
        hrows = tm
    else:
        p1 = p2 = h
        blk = tm // SUBLANES
        p_specs = [pl.BlockSpec((SUBLANES, k), lambda m, j: (jnp.maximum(m * blk - 1, 0), 0)),
                   pl.BlockSpec((SUBLANES, k), lambda m, j: (0, 0))]
        hrows = SUBLANES
    return pl.pallas_call(
        functools.partial(_ffn_up_kernel, decode=decode, tiles_per_seq=max(seq_len // tm, 1)),
        grid=(r // tm, nb),
        in_specs=[pl.BlockSpec((tm, k), lambda m, j: (m, 0))] + p_specs + [
            pl.BlockSpec((k, tn), lambda m, j: (0, j)),
            pl.BlockSpec((k, tn), lambda m, j: (0, j + nb)),
            pl.BlockSpec((3, tn), lambda m, j: (0, j)),
            pl.BlockSpec((1, tn), lambda m, j: (0, j))],
        out_specs=[pl.BlockSpec((tm, tn), lambda m, j: (m, j)),
                   pl.BlockSpec((None, hrows, tn), lambda m, j: (m, 0, j))],
        out_shape=[jax.ShapeDtypeStruct((r, D_FF), BF16),
                   jax.ShapeDtypeStruct((r // tm, hrows, D_FF), F32)],
        compiler_params=_params("parallel", "arbitrary"),
        name="ffn_up",
    )(h, p1, p2, w_up, w_up, dw, db.reshape(1, D_FF))


N_KV_SLABS = KV_GROUPS * 2


def _cmp_partial_slab(x_ref, c, pe_ref, w1_ref, n_chunks, first=0, pitch=1, chunk_pitch=None):
    chunk_pitch = CMP_STRIDE * pitch if chunk_pitch is None else chunk_pitch
    acc_a = jnp.zeros((n_chunks, HEAD_DIM), F32)
    acc_b = jnp.zeros((n_chunks, HEAD_DIM), F32)
    for s in range(CMP_STRIDE):
        xs = x_ref[pl.ds(first + s * pitch, n_chunks, stride=chunk_pitch), :]
        xa = (xs + pe_ref[c, s:s + 1, :]).astype(BF16)
        xb = (xs + pe_ref[c, CMP_STRIDE + s:CMP_STRIDE + s + 1, :]).astype(BF16)
        acc_a += jnp.dot(xa, w1_ref[c, s], preferred_element_type=F32)
        acc_b += jnp.dot(xb, w1_ref[c, CMP_STRIDE + s], preferred_element_type=F32)
    return acc_a, acc_b


def _cmp1_kernel(x_ref, pe_ref, w1_ref, a_ref, b_ref, *, n_chunks):
    c = pl.program_id(1) % 2
    a_ref[...], b_ref[...] = _cmp_partial_slab(x_ref, c, pe_ref, w1_ref, n_chunks)


def cmp_partial_rows(x, pe, w1, rows):
    nb = x.shape[0] // rows
    n_chunks = rows // CMP_STRIDE
    out = jax.ShapeDtypeStruct((nb, n_chunks, KV_COLS), F32)
    return pl.pallas_call(
        functools.partial(_cmp1_kernel, n_chunks=n_chunks),
        grid=(nb, N_KV_SLABS),
        in_specs=[pl.BlockSpec((rows, HEAD_DIM), lambda b, sl: (b, sl)),
                  pl.BlockSpec(pe.shape, lambda b, sl: (0, 0, 0)),
                  pl.BlockSpec(w1.shape, lambda b, sl: (0, 0, 0, 0))],
        out_specs=[pl.BlockSpec((None, n_chunks, HEAD_DIM), lambda b, sl: (b, 0, sl))] * 2,
        out_shape=[out, out],
        compiler_params=_params("parallel", "arbitrary"),
        name="cmp_partial",
    )(x, pe, w1)


PAGES_PER_STEP = 16
CHUNKS_PER_PAGE = PAGE_SIZE // CMP_STRIDE
CHUNK_ROWS = CMP_STRIDE * N_KV_SLABS
CHUNK_PITCH = CHUNK_ROWS + 4
STEP_CHUNKS = PAGES_PER_STEP * CHUNKS_PER_PAGE


def _cmp1_paged_kernel(pt_ref, *refs):
    page_refs = refs[:PAGES_PER_STEP]
    pe_ref, w1_ref, a_ref, b_ref, buf_ref = refs[PAGES_PER_STEP:]
    for i, p_ref in enumerate(page_refs):
        for ch in range(CHUNKS_PER_PAGE):
            dst = (i * CHUNKS_PER_PAGE + ch) * CHUNK_PITCH
            buf_ref[dst:dst + CHUNK_ROWS, :] = p_ref[ch * CHUNK_ROWS:(ch + 1) * CHUNK_ROWS, :]
    for slab in range(N_KV_SLABS):
        cols = slice(slab * HEAD_DIM, (slab + 1) * HEAD_DIM)
        a_ref[:, cols], b_ref[:, cols] = _cmp_partial_slab(buf_ref, slab % 2, pe_ref, w1_ref, STEP_CHUNKS,
                                                           first=slab, pitch=N_KV_SLABS, chunk_pitch=CHUNK_PITCH)


def cmp_partial_paged(pool, n_phys, page_table, layer, pe, w1):
    bsz, n_pages = page_table.shape
    n_steps = n_pages // PAGES_PER_STEP
    n_chunks = STEP_CHUNKS
    out = jax.ShapeDtypeStruct((bsz, n_steps, n_chunks, KV_COLS), F32)

    def page_spec(i):
        return pl.BlockSpec((None, PAGE_SIZE * N_KV_SLABS, HEAD_DIM),
                            lambda b, s, pt: (layer * n_phys + pt[b, s * PAGES_PER_STEP + i], 0, 0))

    out_spec = pl.BlockSpec((None, None, n_chunks, KV_COLS), lambda b, s, pt: (b, s, 0, 0))
    a, b = pl.pallas_call(
        _cmp1_paged_kernel,
        grid_spec=pltpu.PrefetchScalarGridSpec(
            num_scalar_prefetch=1,
            grid=(bsz, n_steps),
            in_specs=[page_spec(i) for i in range(PAGES_PER_STEP)] + [
                pl.BlockSpec(pe.shape, lambda b, s, pt: (0, 0, 0)),
                pl.BlockSpec(w1.shape, lambda b, s, pt: (0, 0, 0, 0))],
            out_specs=[out_spec, out_spec],
            scratch_shapes=[pltpu.VMEM((STEP_CHUNKS * CHUNK_PITCH, HEAD_DIM), F32)]),
        out_shape=[out, out],
        compiler_params=_params("parallel", "arbitrary"),
        name="cmp_partial_paged",
    )(page_table, *([pool] * PAGES_PER_STEP), pe, w1)
    return a.reshape(bsz, -1, KV_COLS), b.reshape(bsz, -1, KV_COLS)


def _cmp2_kernel(a_ref, b_ref, bx_ref, w2_ref, o_ref):
    n = a_ref.shape[0]
    row = lax.broadcasted_iota(jnp.int32, (n, HEAD_DIM), 0)
    for slab in range(KV_GROUPS * 2):
        c = slab % 2
        cols = slice(slab * HEAD_DIM, (slab + 1) * HEAD_DIM)
        nxt = pltpu.roll(b_ref[:, cols], n - 1, 0)
        nxt = jnp.where(row == n - 1, bx_ref[0:1, cols], nxt)
        h = jax.nn.gelu(a_ref[:, cols] + nxt).astype(BF16)
        o_ref[:, cols] = jnp.dot(h, w2_ref[c], preferred_element_type=F32).astype(o_ref.dtype)


def cmp_finish(a, b, b_next, w2):
    nb, n, _ = a.shape
    return pl.pallas_call(
        _cmp2_kernel,
        grid=(nb,),
        in_specs=[pl.BlockSpec((None, n, KV_COLS), lambda i: (i, 0, 0)),
                  pl.BlockSpec((None, n, KV_COLS), lambda i: (i, 0, 0)),
                  pl.BlockSpec((None, SUBLANES, KV_COLS), lambda i: (i, 0, 0)),
                  pl.BlockSpec(w2.shape, lambda i: (0, 0, 0))],
        out_specs=pl.BlockSpec((None, n, KV_COLS), lambda i: (i, 0, 0)),
        out_shape=jax.ShapeDtypeStruct((nb, n, KV_COLS), BF16),
        compiler_params=_params("parallel"),
        name="cmp_finish",
    )(a, b, b_next, w2)


def _softmax_rows(s, mask):
    s = jnp.where(mask, s, NEG)
    m = jnp.max(s, axis=-1, keepdims=True)
    e = jnp.where(mask, jnp.exp(s - m), 0.0)
    l = jnp.sum(e, axis=-1, keepdims=True)
    return jnp.where(l > 0.0, e / jnp.where(l > 0.0, l, 1.0), 0.0)


def _cmp_attn_kernel(q_ref, kc_ref, bias_ref, msel_ref, exp_ref, o_ref, mask_ref, *, n_sel):
    qt = pl.program_id(1)
    tq, n_cmp = q_ref.shape[0], kc_ref.shape[0]
    qpos = qt * tq + lax.broadcasted_iota(jnp.int32, (tq, n_cmp), 0)
    cpos = lax.broadcasted_iota(jnp.int32, (tq, n_cmp), 1) * CMP_STRIDE + (CMP_BLOCK - 1)
    cadd = jnp.where(cpos <= qpos, 0.0, NEG)
    j = lax.broadcasted_iota(jnp.int32, (n_sel, tq), 0)
    cur = (qt * tq + lax.broadcasted_iota(jnp.int32, (n_sel, tq), 1)) // SEL_BLOCK
    valid = j <= cur
    forced = (j == 0) | (j == cur) | (j == cur - 1)
    pad_rows = jnp.zeros((exp_ref.shape[1] - n_sel, tq), F32)
    diag = pl.ds(pl.multiple_of(qt * tq, tq), tq)
    causal = lax.broadcasted_iota(jnp.int32, (tq, tq), 0) <= lax.broadcasted_iota(jnp.int32, (tq, tq), 1)
    for g in range(KV_GROUPS):
        k = kc_ref[:, g * 2 * HEAD_DIM:(g * 2 + 1) * HEAD_DIM]
        v = kc_ref[:, (g * 2 + 1) * HEAD_DIM:(g * 2 + 2) * HEAD_DIM]
        heads = range(g * GROUP_SIZE, (g + 1) * GROUP_SIZE)
        q4 = jnp.concatenate([q_ref[:, h * HEAD_DIM:(h + 1) * HEAD_DIM] for h in heads], axis=0)
        badd = jnp.concatenate([bias_ref[h] + cadd for h in heads], axis=0)
        s = lax.dot_general(q4, k, NT_DIMS, preferred_element_type=F32) * SCALE + badd
        p = _softmax_rows(s, s > 0.5 * NEG)
        o4 = jnp.dot(p.astype(BF16), v, preferred_element_type=F32)
        imp = jnp.zeros((tq, n_cmp), F32)
        for r, h in enumerate(heads):
            o_ref[:, h * HEAD_DIM:(h + 1) * HEAD_DIM] = o4[r * tq:(r + 1) * tq, :]
            imp = imp + p[r * tq:(r + 1) * tq, :]
        p_sel = lax.dot_general(msel_ref[...], imp, NT_DIMS, precision=HIGHEST, preferred_element_type=F32)
        score = jnp.where(valid, p_sel + jnp.where(forced, FORCE_BONUS, 0.0), -jnp.inf)
        rank = jnp.zeros((n_sel, tq), jnp.int32)
        for i in range(n_sel):
            si = score[i:i + 1, :]
            beats = (si > score) | ((si == score) & (i < j))
            rank = rank + beats.astype(jnp.int32)
        sel_t = jnp.where((rank < N_SELECT) & valid, 1.0, 0.0)
        sel = jnp.concatenate([sel_t, pad_rows], axis=0).astype(BF16)
        allowed = jnp.dot(exp_ref[...], sel, preferred_element_type=F32)
        mask_ref[g] = ((allowed - 1.0) * -NEG).astype(mask_ref.dtype)
        allowed_d = jnp.dot(exp_ref[diag, :], sel, preferred_element_type=F32)
        mask_ref[g, diag, :] = jnp.where(causal, (allowed_d - 1.0) * -NEG, NEG).astype(mask_ref.dtype)


def cmp_attention(q, kc, bias_cmp, msel, expand, bsz, seq):
    tq = ATT_TILE
    nqt = seq // tq
    n_cmp = kc.shape[1]
    n_sel = msel.shape[0]
    return pl.pallas_call(
        functools.partial(_cmp_attn_kernel, n_sel=n_sel),
        grid=(bsz, nqt),
        in_specs=[pl.BlockSpec((tq, D_MODEL), lambda b, t: (b * nqt + t, 0)),
                  pl.BlockSpec((None, n_cmp, KV_COLS), lambda b, t: (b, 0, 0)),
                  pl.BlockSpec((N_HEADS, tq, n_cmp), lambda b, t: (0, t, 0)),
                  pl.BlockSpec(msel.shape, lambda b, t: (0, 0)),
                  pl.BlockSpec(expand.shape, lambda b, t: (0, 0))],
        out_specs=[pl.BlockSpec((tq, D_MODEL), lambda b, t: (b * nqt + t, 0)),
                   pl.BlockSpec((None, KV_GROUPS, seq, tq), lambda b, t: (b, 0, 0, t))],
        out_shape=[jax.ShapeDtypeStruct((bsz * seq, D_MODEL), F32),
                   jax.ShapeDtypeStruct((bsz, KV_GROUPS, seq, seq), BF16)],
        compiler_params=_params("parallel", "arbitrary"),
        name="cmp_attention",
    )(q, kc, bias_cmp, msel, expand)


SLC_KEYS = 512
SLC_SUB = SLC_KEYS // ATT_TILE
SLC_QUERIES = 256


def _slc_kernel(qt_ref, ks_ref, q_ref, k_ref, vt_ref, tbt_ref, mask_ref, o_ref, m_ref, l_ref, acc_ref):
    g, pair = pl.program_id(1), pl.program_id(2)
    qt, ks = qt_ref[pair], ks_ref[pair]
    tq = q_ref.shape[0]

    @pl.when(ks == 0)
    def _():
        m_ref[...] = jnp.full_like(m_ref, NEG)
        l_ref[...] = jnp.zeros_like(l_ref)
        acc_ref[...] = jnp.zeros_like(acc_ref)

    q4 = jnp.concatenate([q_ref[:, r * HEAD_DIM:(r + 1) * HEAD_DIM] for r in range(GROUP_SIZE)], axis=0)
    rows = []
    q_sub = tq // ATT_TILE
    for c in range(SLC_SUB):
        tbi = [jnp.clip(qt * q_sub + qs - (ks * SLC_SUB + c), 0, 2) for qs in range(q_sub)]
        madd = [mask_ref[c * ATT_TILE:(c + 1) * ATT_TILE, qs * ATT_TILE:(qs + 1) * ATT_TILE].astype(F32)
                for qs in range(q_sub)]
        rows.append(jnp.concatenate([tbt_ref[tbi[qs], g * GROUP_SIZE + r] + madd[qs]
                                     for r in range(GROUP_SIZE) for qs in range(q_sub)], axis=1))
    st = (lax.dot_general(k_ref[...].astype(BF16), q4, NT_DIMS, preferred_element_type=F32) * SCALE
          + jnp.concatenate(rows, axis=0))
    m_prev = m_ref[...]
    m_new = jnp.maximum(m_prev, jnp.max(st, axis=0, keepdims=True))
    alpha = jnp.exp(m_prev - m_new)
    pt = jnp.exp(st - m_new)
    l_ref[...] = alpha * l_ref[...] + jnp.sum(pt, axis=0, keepdims=True)
    acc_ref[...] = alpha * acc_ref[...] + jnp.dot(vt_ref[...], pt.astype(BF16), preferred_element_type=F32)
    m_ref[...] = m_new

    @pl.when(ks == ((qt + 1) * tq - 1) // SLC_KEYS)
    def _():
        ot = acc_ref[...] / l_ref[...]
        for r in range(GROUP_SIZE):
            o_ref[:, r * HEAD_DIM:(r + 1) * HEAD_DIM] = ot[:, r * tq:(r + 1) * tq].T


def slc_attention(q, kv, vt, mask, tbt, bsz, seq):
    tq = SLC_QUERIES
    nqt = seq // tq
    nks = seq // SLC_KEYS
    gw = GROUP_SIZE * HEAD_DIM
    pairs = [(t, s) for t in range(nqt) for s in range(((t + 1) * tq - 1) // SLC_KEYS + 1)]
    qt_of = jnp.asarray([p[0] for p in pairs], jnp.int32)
    ks_of = jnp.asarray([p[1] for p in pairs], jnp.int32)
    return pl.pallas_call(
        _slc_kernel,
        grid_spec=pltpu.PrefetchScalarGridSpec(
            num_scalar_prefetch=2,
            grid=(bsz, KV_GROUPS, len(pairs)),
            in_specs=[pl.BlockSpec((tq, gw), lambda b, g, p, qt, ks: (b * nqt + qt[p], g)),
                      pl.BlockSpec((SLC_KEYS, HEAD_DIM), lambda b, g, p, qt, ks: (b * nks + ks[p], 2 * g)),
                      pl.BlockSpec((HEAD_DIM, SLC_KEYS), lambda b, g, p, qt, ks: (g, b * nks + ks[p])),
                      pl.BlockSpec(tbt.shape, lambda b, g, p, qt, ks: (0, 0, 0, 0)),
                      pl.BlockSpec((None, None, SLC_KEYS, tq), lambda b, g, p, qt, ks: (b, g, ks[p], qt[p]))],
            out_specs=pl.BlockSpec((tq, gw), lambda b, g, p, qt, ks: (b * nqt + qt[p], g)),
            scratch_shapes=[pltpu.VMEM((1, GROUP_SIZE * tq), F32), pltpu.VMEM((1, GROUP_SIZE * tq), F32),
                            pltpu.VMEM((HEAD_DIM, GROUP_SIZE * tq), F32)]),
        out_shape=jax.ShapeDtypeStruct((bsz * seq, D_MODEL), F32),
        compiler_params=_params("parallel", "parallel", "arbitrary"),
        name="slc_attention",
    )(qt_of, ks_of, q, kv, vt, tbt, mask)


WIN_TILES = WINDOW // ATT_TILE + 1
WIN_SPAN = WIN_TILES * ATT_TILE


def _win_kernel(q_ref, *refs):
    kv_refs, (bias_ref, o_ref) = refs[:WIN_TILES], refs[WIN_TILES:]
    qt = pl.program_id(1)
    tq = q_ref.shape[0]
    row = lax.broadcasted_iota(jnp.int32, (tq, WIN_SPAN), 0)
    col = lax.broadcasted_iota(jnp.int32, (tq, WIN_SPAN), 1)
    back = row + WINDOW - col
    mask = (back >= 0) & (back <= WINDOW) & (qt * tq - WINDOW + col >= 0)
    madd = jnp.where(mask, 0.0, NEG)
    for g in range(KV_GROUPS):
        k = jnp.concatenate([r[:, g * 2 * HEAD_DIM:(g * 2 + 1) * HEAD_DIM].astype(BF16) for r in kv_refs], axis=0)
        v = jnp.concatenate([r[:, (g * 2 + 1) * HEAD_DIM:(g * 2 + 2) * HEAD_DIM].astype(BF16) for r in kv_refs], axis=0)
        heads = range(g * GROUP_SIZE, (g + 1) * GROUP_SIZE)
        q4 = jnp.concatenate([q_ref[:, h * HEAD_DIM:(h + 1) * HEAD_DIM] for h in heads], axis=0)
        badd = jnp.concatenate([bias_ref[h] + madd for h in heads], axis=0)
        s = lax.dot_general(q4, k, NT_DIMS, preferred_element_type=F32) * SCALE + badd
        e = jnp.exp(s - jnp.max(s, axis=-1, keepdims=True))
        l = jnp.sum(e, axis=-1, keepdims=True)
        o = jnp.dot(e.astype(BF16), v, preferred_element_type=F32) / l
        for r, h in enumerate(heads):
            o_ref[:, h * HEAD_DIM:(h + 1) * HEAD_DIM] = o[r * tq:(r + 1) * tq, :]


def win_attention(q, kv, bias_win, bsz, seq):
    tq = ATT_TILE
    nqt = seq // tq

    def kv_spec(i):
        return pl.BlockSpec((tq, KV_COLS), lambda b, t: (b * nqt + jnp.maximum(t - (WIN_TILES - 1) + i, 0), 0))

    return pl.pallas_call(
        _win_kernel,
        grid=(bsz, nqt),
        in_specs=[pl.BlockSpec((tq, D_MODEL), lambda b, t: (b * nqt + t, 0))]
        + [kv_spec(i) for i in range(WIN_TILES)]
        + [pl.BlockSpec(bias_win.shape, lambda b, t: (0, 0, 0))],
        out_specs=pl.BlockSpec((tq, D_MODEL), lambda b, t: (b * nqt + t, 0)),
        out_shape=jax.ShapeDtypeStruct((bsz * seq, D_MODEL), F32),
        compiler_params=_params("parallel", "arbitrary"),
        name="win_attention",
    )(q, *([kv] * WIN_TILES), bias_win)


def _combine_kernel(oc_ref, os_ref, ow_ref, g_ref, o_ref):
    gate = g_ref[...]
    for h in range(N_HEADS):
        cols = slice(h * HEAD_DIM, (h + 1) * HEAD_DIM)
        o = (oc_ref[:, cols] * gate[:, 3 * h:3 * h + 1] + os_ref[:, cols] * gate[:, 3 * h + 1:3 * h + 2]
             + ow_ref[:, cols] * gate[:, 3 * h + 2:3 * h + 3])
        o_ref[:, cols] = o.astype(o_ref.dtype)


def nsa_combine(o_cmp, o_slc, o_win, gate, tm):
    r = o_cmp.shape[0]
    spec = pl.BlockSpec((tm, D_MODEL), lambda m: (m, 0))
    return pl.pallas_call(
        _combine_kernel,
        grid=(r // tm,),
        in_specs=[spec, spec, spec, pl.BlockSpec((tm, LANES), lambda m: (m, 0))],
        out_specs=spec,
        out_shape=jax.ShapeDtypeStruct((r, D_MODEL), BF16),
        compiler_params=_params("parallel"),
        name="nsa_combine",
    )(o_cmp, o_slc, o_win, gate)


N_SEL_S = -(-(PAST_LEN + 1) // SEL_BLOCK)
N_SEL_S_PAD = 384
N_CMP_S = PAST_LEN // CMP_STRIDE


def _group_rows(parts, hgrp):
    out = parts[0]
    for g in range(1, KV_GROUPS):
        out = jnp.where(hgrp == g, parts[g], out)
    return out


def _sample_attn_kernel(q_ref, kc_ref, bc_ref, gsum_ref, msel_ref, win_ref, new_ref, bw_ref, bn_ref,
                        oc_ref, ow_ref, idx_ref):
    q = q_ref[...]
    hgrp = lax.broadcasted_iota(jnp.int32, (N_HEADS, 1), 0) // GROUP_SIZE

    def kcol(g):
        return slice(g * 2 * HEAD_DIM, (g * 2 + 1) * HEAD_DIM)

    def vcol(g):
        return slice((g * 2 + 1) * HEAD_DIM, (g * 2 + 2) * HEAD_DIM)

    s = _group_rows([lax.dot_general(q, kc_ref[:, kcol(g)], NT_DIMS, preferred_element_type=F32)
                     for g in range(KV_GROUPS)], hgrp)
    s = s * SCALE + bc_ref[...]
    n = lax.broadcasted_iota(jnp.int32, s.shape, 1)
    p = _softmax_rows(s, n * CMP_STRIDE + (CMP_BLOCK - 1) <= PAST_LEN)
    pb = p.astype(BF16)
    oc_ref[...] = _group_rows([jnp.dot(pb, kc_ref[:, vcol(g)], preferred_element_type=F32)
                               for g in range(KV_GROUPS)], hgrp)
    imp = jnp.dot(gsum_ref[...], p, precision=HIGHEST, preferred_element_type=F32)
    p_sel = jnp.dot(imp, msel_ref[...], precision=HIGHEST, preferred_element_type=F32)
    j = lax.broadcasted_iota(jnp.int32, p_sel.shape, 1)
    cur = PAST_LEN // SEL_BLOCK
    forced = (j == 0) | (j == cur) | (j == cur - 1)
    score = jnp.where(j <= cur, p_sel + jnp.where(forced, FORCE_BONUS, 0.0), -jnp.inf)
    lane = lax.broadcasted_iota(jnp.int32, idx_ref.shape, 1)
    idx = jnp.zeros(idx_ref.shape, F32)
    jf = j.astype(F32)
    for kk in range(N_SELECT):
        mx = jnp.max(score, axis=-1, keepdims=True)
        pick = jnp.min(jnp.where(score == mx, jf, float(N_SEL_S_PAD)), axis=-1, keepdims=True)
        idx = jnp.where(lane == kk, pick, idx)
        score = jnp.where(jf == pick, -jnp.inf, score)
    idx_ref[...] = idx.astype(jnp.int32)
    wb = win_ref.shape[0] // N_KV_SLABS

    def win_slab(slab):
        return win_ref[pl.ds(slab, wb, stride=N_KV_SLABS), :].astype(BF16)

    sw = _group_rows([lax.dot_general(q, win_slab(2 * g), NT_DIMS, preferred_element_type=F32)
                      for g in range(KV_GROUPS)], hgrp)
    sw = sw * SCALE + bw_ref[...]
    qf = q.astype(F32)
    sn = _group_rows([jnp.sum(qf * new_ref[:, kcol(g)].astype(BF16).astype(F32), axis=-1, keepdims=True)
                      for g in range(KV_GROUPS)], hgrp)
    sn = sn * SCALE + bn_ref[:, 0:1]
    m = jnp.maximum(jnp.max(sw, axis=-1, keepdims=True), sn)
    ew, en = jnp.exp(sw - m), jnp.exp(sn - m)
    l = jnp.sum(ew, axis=-1, keepdims=True) + en
    pw, pn = (ew / l).astype(BF16), (en / l).astype(BF16).astype(F32)
    ow = _group_rows([jnp.dot(pw, win_slab(2 * g + 1), preferred_element_type=F32)
                      + pn * new_ref[:, vcol(g)].astype(BF16).astype(F32) for g in range(KV_GROUPS)], hgrp)
    ow_ref[...] = ow


def sample_attention(q3, kc, bias_c, gsum, msel, win_pool, layer, kv_win_new, bias_w, bias_new):
    bsz = q3.shape[0]
    wb = win_pool.shape[1]
    o = jax.ShapeDtypeStruct((bsz, N_HEADS, HEAD_DIM), F32)
    full2 = lambda b: (0, 0)
    return pl.pallas_call(
        _sample_attn_kernel,
        grid=(bsz,),
        in_specs=[pl.BlockSpec((None, N_HEADS, HEAD_DIM), lambda b: (b, 0, 0)),
                  pl.BlockSpec((None, N_CMP_S, KV_COLS), lambda b: (b, 0, 0)),
                  pl.BlockSpec(bias_c.shape, full2),
                  pl.BlockSpec(gsum.shape, full2),
                  pl.BlockSpec(msel.shape, full2),
                  pl.BlockSpec((None, wb, HEAD_DIM), lambda b: (layer * bsz + b, 0, 0)),
                  pl.BlockSpec((None, 1, KV_COLS), lambda b: (b, 0, 0)),
                  pl.BlockSpec(bias_w.shape, full2),
                  pl.BlockSpec(bias_new.shape, full2)],
        out_specs=[pl.BlockSpec((None, N_HEADS, HEAD_DIM), lambda b: (b, 0, 0)),
                   pl.BlockSpec((None, N_HEADS, HEAD_DIM), lambda b: (b, 0, 0)),
                   pl.BlockSpec((None, SUBLANES, LANES), lambda b: (b, 0, 0))],
        out_shape=[o, o, jax.ShapeDtypeStruct((bsz, SUBLANES, LANES), jnp.int32)],
        compiler_params=_params("parallel"),
        name="sample_attention",
    )(q3, kc, bias_c, gsum, msel, win_pool, kv_win_new, bias_w, bias_new)


def _sample_slc_kernel(idx_ref, pt_ref, q_ref, *refs):
    blk_refs, (new_ref, bias_ref, o_ref) = refs[:N_SELECT], refs[N_SELECT:]
    b, g = pl.program_id(0), pl.program_id(1)
    ks, vs, biases = [], [], []
    lane = lax.broadcasted_iota(jnp.int32, (N_HEADS, SEL_BLOCK), 1)
    for kk, blk_ref in enumerate(blk_refs):
        j = idx_ref[b, g, kk]
        is_new = j >= PAST_LEN // SEL_BLOCK
        k_blk = blk_ref[pl.ds(2 * g, SEL_BLOCK, stride=N_KV_SLABS), :]
        v_blk = blk_ref[pl.ds(2 * g + 1, SEL_BLOCK, stride=N_KV_SLABS), :]
        ks.append(jnp.where(is_new, jnp.broadcast_to(new_ref[:, :HEAD_DIM], k_blk.shape), k_blk).astype(BF16))
        vs.append(jnp.where(is_new, jnp.broadcast_to(new_ref[:, HEAD_DIM:], v_blk.shape), v_blk).astype(BF16))
        biases.append(bias_ref[j] + jnp.where(j * SEL_BLOCK + lane <= PAST_LEN, 0.0, NEG))
    k, v = jnp.concatenate(ks, axis=0), jnp.concatenate(vs, axis=0)
    s = lax.dot_general(q_ref[...], k, NT_DIMS, preferred_element_type=F32) * SCALE + jnp.concatenate(biases, axis=1)
    p = _softmax_rows(s, s > 0.5 * NEG)
    o_ref[...] = jnp.dot(p.astype(BF16), v, preferred_element_type=F32)


def sample_slc_attention(idx, page_table, q3, pool, n_phys, layer, kv_slc_new, bias_blk):
    bsz = q3.shape[0]
    half_per_page = PAGE_SIZE // SEL_BLOCK
    n_half = n_phys * half_per_page
    last_past = PAST_LEN // SEL_BLOCK - 1

    def blk_spec(kk):
        def blk_map(b, g, idx_r, pt_r):
            j = jnp.minimum(idx_r[b, g, kk], last_past)
            return (layer * n_half + pt_r[b, j // half_per_page] * half_per_page + j % half_per_page, 0, 0)
        return pl.BlockSpec((None, SEL_BLOCK * N_KV_SLABS, HEAD_DIM), blk_map)

    return pl.pallas_call(
        _sample_slc_kernel,
        grid_spec=pltpu.PrefetchScalarGridSpec(
            num_scalar_prefetch=2,
            grid=(bsz, KV_GROUPS),
            in_specs=[pl.BlockSpec((None, N_HEADS, HEAD_DIM), lambda b, g, i, p: (b, 0, 0))]
            + [blk_spec(kk) for kk in range(N_SELECT)]
            + [pl.BlockSpec((None, 1, 2 * HEAD_DIM), lambda b, g, i, p: (b, 0, g)),
               pl.BlockSpec(bias_blk.shape, lambda b, g, i, p: (0, 0, 0))],
            out_specs=pl.BlockSpec((None, None, N_HEADS, HEAD_DIM), lambda b, g, i, p: (b, g, 0, 0))),
        out_shape=jax.ShapeDtypeStruct((bsz, KV_GROUPS, N_HEADS, HEAD_DIM), F32),
        compiler_params=_params("parallel", "arbitrary"),
        name="sample_slc_attention",
    )(idx, page_table, q3, *([pool] * N_SELECT), kv_slc_new, bias_blk)


CONV_TILE = 128
CONV_HALO = 32
CONV_ROWS = 64


def _dwconv_ln_kernel(u_ref, halo_ref, w_ref, b_ref, g_ref, beta_ref, o_ref, buf_ref, y_ref, *, tiles_per_seq):
    first = pl.program_id(0) % tiles_per_seq == 0
    buf_ref[0:CONV_HALO, :] = jnp.where(first, 0.0, halo_ref[...])
    buf_ref[CONV_HALO:, :] = u_ref[...]
    lead = CONV_HALO - (CONV_WIDTH - 1)
    for r0 in range(0, CONV_TILE, CONV_ROWS):
        for c0 in range(0, D_MODEL, LANES):
            cols = slice(c0, c0 + LANES)
            acc = jnp.broadcast_to(b_ref[:, cols], (CONV_ROWS, LANES))
            for k in range(CONV_WIDTH):
                acc = acc + w_ref[k:k + 1, cols] * buf_ref[pl.ds(r0 + lead + k, CONV_ROWS), cols]
            y_ref[r0:r0 + CONV_ROWS, cols] = acc
    y = y_ref[...]
    mu = jnp.mean(y, axis=-1, keepdims=True)
    var = jnp.mean(jnp.square(y - mu), axis=-1, keepdims=True)
    y = (y - mu) * lax.rsqrt(var + LN_EPS) * g_ref[...] + beta_ref[...]
    o_ref[...] = (y * jax.nn.sigmoid(y)).astype(o_ref.dtype)


def dwconv_ln(u, w, b, ln_g, ln_b, seq):
    r, d = u.shape
    tiles_per_seq = seq // CONV_TILE
    ratio = CONV_TILE // CONV_HALO
    vec = lambda m: (0, 0)
    return pl.pallas_call(
        functools.partial(_dwconv_ln_kernel, tiles_per_seq=tiles_per_seq),
        grid=(r // CONV_TILE,),
        in_specs=[pl.BlockSpec((CONV_TILE, d), lambda m: (m, 0)),
                  pl.BlockSpec((CONV_HALO, d), lambda m: (jnp.maximum(m * ratio - 1, 0), 0)),
                  pl.BlockSpec((CONV_WIDTH, d), vec), pl.BlockSpec((1, d), vec),
                  pl.BlockSpec((1, d), vec), pl.BlockSpec((1, d), vec)],
        out_specs=pl.BlockSpec((CONV_TILE, d), lambda m: (m, 0)),
        out_shape=jax.ShapeDtypeStruct((r, d), BF16),
        scratch_shapes=[pltpu.VMEM((CONV_HALO + CONV_TILE, d), F32), pltpu.VMEM((CONV_TILE, d), F32)],
        compiler_params=_params("parallel"),
        name="dwconv_ln",
    )(u, u, w, b.reshape(1, d), ln_g.reshape(1, d), ln_b.reshape(1, d))


def _dwconv_ln_decode_kernel(u_ref, hist_ref, w_ref, b_ref, g_ref, beta_ref, o_ref):
    y = b_ref[...] + w_ref[CONV_WIDTH - 1:CONV_WIDTH, :] * u_ref[...]
    for k in range(CONV_WIDTH - 1):
        y = y + w_ref[k:k + 1, :] * hist_ref[k]
    mu = jnp.mean(y, axis=-1, keepdims=True)
    var = jnp.mean(jnp.square(y - mu), axis=-1, keepdims=True)
    y = (y - mu) * lax.rsqrt(var + LN_EPS) * g_ref[...] + beta_ref[...]
    o_ref[...] = (y * jax.nn.sigmoid(y)).astype(o_ref.dtype)


def dwconv_ln_decode(u, hist_t, w, b, ln_g, ln_b):
    r, d = u.shape
    vec = lambda i: (0, 0)
    return pl.pallas_call(
        _dwconv_ln_decode_kernel,
        grid=(1,),
        in_specs=[pl.BlockSpec((r, d), vec), pl.BlockSpec(hist_t.shape, lambda i: (0, 0, 0)),
                  pl.BlockSpec((CONV_WIDTH, d), vec), pl.BlockSpec((1, d), vec),
                  pl.BlockSpec((1, d), vec), pl.BlockSpec((1, d), vec)],
        out_specs=pl.BlockSpec((r, d), vec),
        out_shape=jax.ShapeDtypeStruct((r, d), BF16),
        compiler_params=_params("arbitrary"),
        name="dwconv_ln_decode",
    )(u, hist_t, w, b.reshape(1, d), ln_g.reshape(1, d), ln_b.reshape(1, d))


S5_CHUNK = 256
S5_PITCH = S5_CHUNK + 4
IN_SLABS = D_MODEL // LANES
STATE_PER_IN = N_SLAB // IN_SLABS


def _s5_project_in(hb, wb_ref, store):
    half = STATE_PER_IN * LANES
    for i in range(IN_SLABS):
        res = jnp.dot(hb[:, i * LANES:(i + 1) * LANES], wb_ref[i], preferred_element_type=F32)
        for jj in range(STATE_PER_IN):
            store(i * STATE_PER_IN + jj, res[:, jj * LANES:(jj + 1) * LANES],
                  res[:, half + jj * LANES:half + (jj + 1) * LANES])


def _s5_project_out(load, cre_ref, cim_ref, hn, d_ref, y_ref):
    for i in range(IN_SLABS):
        cols = slice(i * LANES, (i + 1) * LANES)
        acc = d_ref[:, cols] * hn[:, cols]
        for jj in range(STATE_PER_IN):
            j = i * STATE_PER_IN + jj
            re, im = load(j)
            acc = acc + jnp.dot(re.astype(BF16), cre_ref[j], preferred_element_type=F32)
            acc = acc - jnp.dot(im.astype(BF16), cim_ref[j], preferred_element_type=F32)
        y_ref[:, cols] = acc.astype(y_ref.dtype)


def _s5_scan_kernel(x_ref, g_ref, wb_ref, ar_ref, ai_ref, cre_ref, cim_ref, d_ref,
                    y_ref, sr_ref, si_ref, bur_ref, bui_ref, hr_ref, hi_ref):
    tc, pitch = S5_CHUNK, S5_PITCH

    @pl.when(pl.program_id(1) == 0)
    def _():
        hr_ref[...] = jnp.zeros_like(hr_ref)
        hi_ref[...] = jnp.zeros_like(hi_ref)

    hn = _rms(x_ref[...], g_ref[...])

    def store(j, re, im):
        bur_ref[j * pitch:j * pitch + tc, :] = re
        bui_ref[j * pitch:j * pitch + tc, :] = im

    _s5_project_in(hn.astype(BF16), wb_ref, store)

    ar = [ar_ref[j8] for j8 in range(SLAB_GRP)]
    ai = [ai_ref[j8] for j8 in range(SLAB_GRP)]

    def step(t, carry):
        out = []
        for j8 in range(SLAB_GRP):
            hr, hi = carry[2 * j8], carry[2 * j8 + 1]
            rows = pl.ds(j8 * SUBLANES * pitch + t, SUBLANES, stride=pitch)
            nr = ar[j8] * hr - ai[j8] * hi + bur_ref[rows, :]
            ni = ar[j8] * hi + ai[j8] * hr + bui_ref[rows, :]
            bur_ref[rows, :] = nr
            bui_ref[rows, :] = ni
            out += [nr, ni]
        return tuple(out)

    init = []
    for j8 in range(SLAB_GRP):
        init += [hr_ref[j8], hi_ref[j8]]
    fin = lax.fori_loop(0, tc, step, tuple(init))
    for j8 in range(SLAB_GRP):
        hr_ref[j8] = fin[2 * j8]
        hi_ref[j8] = fin[2 * j8 + 1]
    sr_ref[...] = hr_ref[...]
    si_ref[...] = hi_ref[...]

    def load(j):
        return bur_ref[j * pitch:j * pitch + tc, :], bui_ref[j * pitch:j * pitch + tc, :]

    _s5_project_out(load, cre_ref, cim_ref, hn, d_ref, y_ref)


def s5_scan(x, g, wb, ar, ai, cre, cim, d_skip, bsz, seq):
    n_chunks = seq // S5_CHUNK
    st = jax.ShapeDtypeStruct((bsz, SLAB_GRP, SUBLANES, LANES), F32)
    st_spec = pl.BlockSpec((None, SLAB_GRP, SUBLANES, LANES), lambda b, c: (b, 0, 0, 0))
    vec = lambda b, c: (0, 0)
    c3 = lambda b, c: (0, 0, 0)
    return pl.pallas_call(
        _s5_scan_kernel,
        grid=(bsz, n_chunks),
        in_specs=[pl.BlockSpec((S5_CHUNK, D_MODEL), lambda b, c: (b * n_chunks + c, 0)),
                  pl.BlockSpec((1, D_MODEL), vec),
                  pl.BlockSpec(wb.shape, c3), pl.BlockSpec(ar.shape, c3), pl.BlockSpec(ai.shape, c3),
                  pl.BlockSpec(cre.shape, c3), pl.BlockSpec(cim.shape, c3),
                  pl.BlockSpec((1, D_MODEL), vec)],
        out_specs=[pl.BlockSpec((S5_CHUNK, D_MODEL), lambda b, c: (b * n_chunks + c, 0)), st_spec, st_spec],
        out_shape=[jax.ShapeDtypeStruct((bsz * seq, D_MODEL), BF16), st, st],
        scratch_shapes=[pltpu.VMEM((N_SLAB * S5_PITCH, LANES), F32), pltpu.VMEM((N_SLAB * S5_PITCH, LANES), F32),
                        pltpu.VMEM((SLAB_GRP, SUBLANES, LANES), F32), pltpu.VMEM((SLAB_GRP, SUBLANES, LANES), F32)],
        compiler_params=_params("parallel", "arbitrary"),
        name="s5_scan",
    )(x, g.reshape(1, -1), wb, ar, ai, cre, cim, d_skip.reshape(1, -1))


def _s5_decode_kernel(x_ref, g_ref, wb_ref, ar_ref, ai_ref, cre_ref, cim_ref, d_ref, h0r_ref, h0i_ref,
                      y_ref, sr_ref, si_ref):
    hn = _rms(x_ref[...], g_ref[...])

    def store(j, re, im):
        cols = slice(j * LANES, (j + 1) * LANES)
        ar, ai = ar_ref[:, cols], ai_ref[:, cols]
        hr, hi = h0r_ref[:, cols], h0i_ref[:, cols]
        sr_ref[:, cols] = ar * hr - ai * hi + re
        si_ref[:, cols] = ar * hi + ai * hr + im

    _s5_project_in(hn.astype(BF16), wb_ref, store)

    def load(j):
        cols = slice(j * LANES, (j + 1) * LANES)
        return sr_ref[:, cols], si_ref[:, cols]

    _s5_project_out(load, cre_ref, cim_ref, hn, d_ref, y_ref)


def s5_decode(x, g, wb, ar_row, ai_row, cre, cim, d_skip, h0r, h0i):
    r = x.shape[0]
    st = jax.ShapeDtypeStruct((r, SSM_DIM), F32)
    vec = lambda i: (0, 0)
    c3 = lambda i: (0, 0, 0)
    return pl.pallas_call(
        _s5_decode_kernel,
        grid=(1,),
        in_specs=[pl.BlockSpec((r, D_MODEL), vec), pl.BlockSpec((1, D_MODEL), vec),
                  pl.BlockSpec(wb.shape, c3), pl.BlockSpec((1, SSM_DIM), vec), pl.BlockSpec((1, SSM_DIM), vec),
                  pl.BlockSpec(cre.shape, c3), pl.BlockSpec(cim.shape, c3), pl.BlockSpec((1, D_MODEL), vec),
                  pl.BlockSpec((r, SSM_DIM), vec), pl.BlockSpec((r, SSM_DIM), vec)],
        out_specs=[pl.BlockSpec((r, D_MODEL), vec), pl.BlockSpec((r, SSM_DIM), vec), pl.BlockSpec((r, SSM_DIM), vec)],
        out_shape=[jax.ShapeDtypeStruct((r, D_MODEL), BF16), st, st],
        compiler_params=_params("arbitrary"),
        name="s5_decode",
    )(x, g.reshape(1, -1), wb, ar_row, ai_row, cre, cim, d_skip.reshape(1, -1), h0r, h0i)


def _t5_bucket(rel):
    n = jnp.maximum(rel, 0)
    nf = jnp.maximum(n, MAX_EXACT).astype(F32)
    big = MAX_EXACT + (jnp.log(nf / MAX_EXACT) / math.log(REL_MAX_DIST / MAX_EXACT)
                       * (N_BUCKETS - MAX_EXACT)).astype(jnp.int32)
    return jnp.where(n < MAX_EXACT, n, jnp.minimum(big, N_BUCKETS - 1))


def _bias_of(rel_bias, rel):
    onehot = (_t5_bucket(rel)[..., None] == jnp.arange(N_BUCKETS, dtype=jnp.int32)).astype(F32)
    return jnp.einsum('...k,kh->h...', onehot, rel_bias.astype(F32), precision=HIGHEST)


def _selection_matrix(n_cmp, n_sel_pad):
    coef = np.convolve(np.ones(SEL_RATIO), np.ones(CMP_BLOCK // CMP_STRIDE)).astype(np.float32)
    m = np.zeros((n_cmp, n_sel_pad), np.float32)
    for j in range(n_sel_pad):
        for o in range(coef.shape[0]):
            n = SEL_RATIO * j + o - (CMP_BLOCK // CMP_STRIDE - 1)
            if 0 <= n < n_cmp:
                m[n, j] = coef[o]
    return m


def _s5_params(a_re, a_im, log_dt, b_re, b_im, c_re, c_im):
    dt = jnp.exp(log_dt.astype(F32))[:, None]
    ar, ai = a_re.astype(F32), a_im.astype(F32)
    mag = jnp.exp(ar * dt)
    abar_re, abar_im = mag * jnp.cos(ai * dt), mag * jnp.sin(ai * dt)
    den = ar * ar + ai * ai
    coef_re = ((abar_re - 1.0) * ar + abar_im * ai) / den
    coef_im = (abar_im * ar - (abar_re - 1.0) * ai) / den
    br, bim = b_re.astype(F32), b_im.astype(F32)
    bb_re = coef_re[..., None] * br - coef_im[..., None] * bim
    bb_im = coef_re[..., None] * bim + coef_im[..., None] * br
    gpi = LANES // SSM_GROUP_CH
    eye = jnp.eye(gpi, dtype=F32)

    def in_blocks(bb):
        t = bb.reshape(IN_SLABS, gpi, SSM_STATE, SSM_GROUP_CH)
        blk = jnp.einsum('sgpc,gh->sgchp', t, eye)
        return blk.reshape(IN_SLABS, LANES, gpi * SSM_STATE)

    wb = jnp.concatenate([in_blocks(bb_re), in_blocks(bb_im)], axis=-1).astype(BF16)
    gps = LANES // SSM_STATE
    ch_per_in = LANES

    def out_blocks(c):
        t = c.astype(F32).reshape(IN_SLABS, STATE_PER_IN, gps, SSM_GROUP_CH, SSM_STATE)
        sel = jnp.eye(STATE_PER_IN * gps, dtype=F32).reshape(STATE_PER_IN, gps, STATE_PER_IN * gps)
        blk = jnp.einsum('ijgcp,jgh->ijgphc', t, sel)
        return blk.reshape(N_SLAB, LANES, ch_per_in).astype(BF16)

    return abar_re, abar_im, wb, out_blocks(c_re), out_blocks(c_im)


PROMPT_TM = 512
TN = 512
FFN_UP_TM = 1024
FFN_DOWN_TK = D_FF // 4
GLU_TM = 256
GLU_TK = 1024


def _row_tile(r):
    return PROMPT_TM if r % PROMPT_TM == 0 else r


def _nsa_project(h, wq, wkv, wg):
    tm = _row_tile(h.shape[0])
    q = matmul(h, wq, n_split=1, out_dtype=BF16, act=None, tm=tm, tn=TN, name="nsa_q")[0]
    kv = matmul(h, wkv, n_split=3, out_dtype=F32, act=None, tm=tm, tn=TN, name="nsa_kv")
    gate = matmul(h, wg, n_split=1, out_dtype=F32, act="sigmoid", tm=tm, tn=LANES, name="nsa_gate")[0]
    return q, kv, gate


def _nsa_layer(hp, hs, xp, xs, g_post, g_next, wts, tabs, caches, layer, page_table, bsz, seq):
    wq, wkv, wg, wo, pe, w1, w2, wvt = wts
    cmp_pool, slc_pool, win_pool, n_phys = caches
    dec = hs.shape[0]
    q, kv, gate = _nsa_project(hp, wq, wkv, wg)
    part_a, part_b = cmp_partial_rows(kv[0], pe, w1, seq)
    kc = cmp_finish(part_a, part_b, jnp.zeros((bsz, SUBLANES, KV_COLS), F32), w2)
    o_cmp, mask = cmp_attention(q, kc, tabs["cmp"], tabs["msel"], tabs["expand"], bsz, seq)
    vt = matmul_nt(wvt, hp, tm=PROMPT_TM, name="nsa_vt")
    o_slc = slc_attention(q, kv[1], vt, mask, tabs["tile_t"], bsz, seq)
    o_win = win_attention(q, kv[2], tabs["win"], bsz, seq)
    o = nsa_combine(o_cmp, o_slc, o_win, gate, PROMPT_TM)
    xp, hp = proj_res(o, wo, xp, g_post, g_next, glu=False, tm=PROMPT_TM, tk=D_MODEL, name="nsa_out")
    qs, kvs, gate_s = _nsa_project(hs, wq, wkv, wg)
    past_a, past_b = cmp_partial_paged(cmp_pool, n_phys, page_table, layer, pe, w1)
    tail = jnp.pad(kvs[0][:, None, :], ((0, 0), (0, CMP_STRIDE - 1), (0, 0))).reshape(dec * CMP_STRIDE, KV_COLS)
    _, tail_b = cmp_partial_rows(tail, pe, w1, dec * CMP_STRIDE)
    b_next = jnp.pad(tail_b[0][:, None, :], ((0, 0), (0, SUBLANES - 1), (0, 0)))
    kc_s = cmp_finish(past_a, past_b, b_next, w2)
    oc_s, ow_s, idx = sample_attention(qs.reshape(dec, N_HEADS, HEAD_DIM), kc_s, tabs["cmp_s"], tabs["gsum"],
                                       tabs["msel_s"], win_pool, layer, kvs[2][:, None, :],
                                       tabs["win_s"], tabs["new_s"])
    q3 = qs.reshape(dec, N_HEADS, HEAD_DIM)
    os_all = sample_slc_attention(idx[:, :KV_GROUPS, :N_SELECT], page_table, q3, slc_pool, n_phys, layer,
                                  kvs[1][:, None, :], tabs["slc_s"])
    os_s = jnp.stack([os_all[:, h // GROUP_SIZE, h] for h in range(N_HEADS)], axis=1)
    o_s = nsa_combine(oc_s.reshape(dec, D_MODEL), os_s.reshape(dec, D_MODEL), ow_s.reshape(dec, D_MODEL), gate_s, dec)
    xs, hs = proj_res(o_s, wo, xs, g_post, g_next, glu=False, tm=dec, tk=D_MODEL, name="nsa_out_s")
    return xp, hp, xs, hs, kv, kvs


def _conv_layer(hp, hs, xp, xs, g_post, g_next, wts, state, bsz, seq):
    w_pw1, dw, dw_b, ln_g, ln_b, w_pw2 = wts
    dec = hs.shape[0]
    u = glu_matmul(hp, w_pw1, tm=PROMPT_TM, tn=TN, name="conv_pw1")
    hc = dwconv_ln(u, dw, dw_b, ln_g, ln_b, seq)
    xp, hp = proj_res(hc, w_pw2, xp, g_post, g_next, glu=False, tm=PROMPT_TM, tk=D_MODEL, name="conv_pw2")
    hist_p = u.reshape(bsz, seq, D_MODEL)[:, seq - (CONV_WIDTH - 1):]
    us = glu_matmul(hs, w_pw1, tm=dec, tn=TN, name="conv_pw1_s")
    hc_s = dwconv_ln_decode(us, jnp.swapaxes(state, 0, 1), dw, dw_b, ln_g, ln_b)
    xs, hs = proj_res(hc_s, w_pw2, xs, g_post, g_next, glu=False, tm=dec, tk=D_MODEL, name="conv_pw2_s")
    hist_s = jnp.concatenate([state[:, 1:], us[:, None, :]], axis=1)
    return xp, hp, xs, hs, hist_p, hist_s


def _s5_layer(xp, xs, g_pre, g_post, g_next, wts, state_re, state_im, bsz, seq):
    a_re, a_im, log_dt, b_re, b_im, c_re, c_im, d_skip, w_glu = wts
    dec = xs.shape[0]
    abar_re, abar_im, wb, cre, cim = _s5_params(a_re, a_im, log_dt, b_re, b_im, c_re, c_im)
    slab_shape = (SLAB_GRP, SUBLANES, LANES)
    y, sr, si = s5_scan(xp, g_pre, wb, abar_re.reshape(slab_shape), abar_im.reshape(slab_shape), cre, cim,
                        d_skip, bsz, seq)
    xp, hp = proj_res(y, w_glu, xp, g_post, g_next, glu=True, tm=GLU_TM, tk=GLU_TK, name="s5_glu")
    ys, sr_s, si_s = s5_decode(xs, g_pre, wb, abar_re.reshape(1, SSM_DIM), abar_im.reshape(1, SSM_DIM), cre, cim,
                               d_skip, state_re.reshape(dec, SSM_DIM), state_im.reshape(dec, SSM_DIM))
    xs, hs = proj_res(ys, w_glu, xs, g_post, g_next, glu=True, tm=dec, tk=GLU_TK, name="s5_glu_s")
    gp = (SSM_GROUPS, SSM_STATE)
    return (xp, hp, xs, hs, sr.reshape((bsz,) + gp), si.reshape((bsz,) + gp),
            sr_s.reshape((dec,) + gp), si_s.reshape((dec,) + gp))


def _ffn_layer(hp, hs, xp, xs, g_post, g_next, wts, state, bsz, seq):
    w_up, dw, dw_b, w_down = wts
    dec = hs.shape[0]
    act, hist = ffn_up(hp, w_up, dw, dw_b, seq_len=seq, tm=FFN_UP_TM, tn=TN)
    xp, hp = proj_res(act, w_down, xp, g_post, g_next, glu=False, tm=PROMPT_TM, tk=FFN_DOWN_TK, name="ffn_down")
    tiles = seq // FFN_UP_TM
    hist_p = hist.reshape(bsz, tiles, SUBLANES, D_FF)[:, tiles - 1, SUBLANES - 2:, :]
    act_s, gate_s = ffn_up(hs, w_up, dw, dw_b, hist=state, seq_len=1, tm=dec, tn=TN)
    xs, hs = proj_res(act_s, w_down, xs, g_post, g_next, glu=False, tm=dec, tk=FFN_DOWN_TK, name="ffn_down_s")
    hist_s = jnp.concatenate([state[:, 1:], gate_s.reshape(dec, 1, D_FF)], axis=1)
    return xp, hp, xs, hs, hist_p, hist_s


def _slc_value_weight_t(w_kv):
    w = w_kv.reshape(D_MODEL, 3, KV_GROUPS, 2, HEAD_DIM)[:, 1, :, 1, :]
    return w.reshape(D_MODEL, KV_GROUPS * HEAD_DIM).T.astype(BF16)


def _bias_tables(rel_bias, seq):
    tq = ATT_TILE
    i = jnp.arange(tq, dtype=jnp.int32)
    tile = jnp.stack([_bias_of(rel_bias, d * tq + i[:, None] - i[None, :]) for d in range(3)])
    n_cmp = seq // CMP_STRIDE
    cpos = jnp.arange(n_cmp, dtype=jnp.int32) * CMP_STRIDE + (CMP_BLOCK - 1)
    qpos = jnp.arange(seq, dtype=jnp.int32)
    n_sel = -(-seq // SEL_BLOCK)
    key = np.arange(seq)
    expand = (key[:, None] // SEL_BLOCK == np.arange(LANES)[None, :]).astype(np.float32)
    cpos_s = jnp.arange(N_CMP_S, dtype=jnp.int32) * CMP_STRIDE + (CMP_BLOCK - 1)
    wb = min(WINDOW, PAST_LEN)
    kpos_s = jnp.arange(N_SEL_S * SEL_BLOCK, dtype=jnp.int32)
    slc_s = _bias_of(rel_bias, PAST_LEN - kpos_s).reshape(N_HEADS, N_SEL_S, SEL_BLOCK)
    gsum = (np.arange(N_HEADS)[None, :] // GROUP_SIZE == np.arange(SUBLANES)[:, None]).astype(np.float32)
    return {
        "tile_t": jnp.swapaxes(tile, -1, -2),
        "win": _bias_of(rel_bias, i[:, None] + WINDOW - jnp.arange(WIN_SPAN, dtype=jnp.int32)[None, :]),
        "cmp": _bias_of(rel_bias, qpos[:, None] - cpos[None, :]),
        "msel": jnp.asarray(_selection_matrix(n_cmp, n_sel).T),
        "expand": jnp.asarray(expand, BF16),
        "cmp_s": _bias_of(rel_bias, PAST_LEN - cpos_s),
        "msel_s": jnp.asarray(_selection_matrix(N_CMP_S, N_SEL_S_PAD)),
        "gsum": jnp.asarray(gsum),
        "win_s": _bias_of(rel_bias, wb - jnp.arange(wb, dtype=jnp.int32)),
        "new_s": jnp.broadcast_to(_bias_of(rel_bias, jnp.zeros((1,), jnp.int32)), (N_HEADS, LANES)),
        "slc_s": jnp.swapaxes(slc_s, 0, 1),
    }


def kernel(x_prompt, x_sample, cache_cmp_kv, cache_slc_kv, cache_win_kv, state_conv, state_ssm_re, state_ssm_im, state_ffn_conv, page_table, norm_gain, rel_bias, nsa_w_q, nsa_w_kv, nsa_cmp_pe, nsa_cmp_w1, nsa_cmp_w2, nsa_w_gate, nsa_w_o, conv_w_pw1, conv_dw, conv_dw_b, conv_ln_g, conv_ln_b, conv_w_pw2, ssm_a_re, ssm_a_im, ssm_log_dt, ssm_b_re, ssm_b_im, ssm_c_re, ssm_c_im, ssm_d, ssm_w_glu, ffn_w_up, ffn_dw, ffn_dw_b, ffn_w_down):
    bsz, seq, d = x_prompt.shape
    dec, dec_seq, _ = x_sample.shape
    assert dec_seq == 1 and d == D_MODEL and seq % PROMPT_TM == 0 and seq // CMP_STRIDE == LANES
    n_nsa = cache_cmp_kv.shape[0]
    n_phys = cache_cmp_kv.shape[1]
    xp = x_prompt.reshape(bsz * seq, d)
    xs = x_sample.reshape(dec, d)
    tabs = _bias_tables(rel_bias, seq)
    cmp_pool = cache_cmp_kv.reshape(n_nsa * n_phys, PAGE_SIZE * N_KV_SLABS, HEAD_DIM)
    slc_pool = cache_slc_kv.reshape(n_nsa * n_phys * (PAGE_SIZE // SEL_BLOCK), SEL_BLOCK * N_KV_SLABS, HEAD_DIM)
    win_pool = cache_win_kv.reshape(n_nsa * dec, cache_win_kv.shape[2] * N_KV_SLABS, HEAD_DIM)
    g_all = norm_gain.astype(F32)

    hp = rmsnorm_cast(xp, g_all[0, 0], PROMPT_TM)
    hs = rmsnorm_cast(xs, g_all[0, 0], dec)
    out = {k: [] for k in ("cmp_p", "cmp_s", "slc_p", "slc_s", "win_p", "win_s", "conv_p", "conv_s",
                           "re_p", "re_s", "im_p", "im_s", "ffn_p", "ffn_s")}
    counts = [0, 0, 0]
    for i in range(DEPTH):
        m = i % 3
        j = counts[m]
        counts[m] += 1
        g_post, g_ffn, g_ffn_post = g_all[i, 1], g_all[i, 2], g_all[i, 3]
        g_next = g_all[i + 1, 0] if i + 1 < DEPTH else g_all[i, 0]
        if m == 0:
            wg = jnp.pad(nsa_w_gate[j], ((0, 0), (0, LANES - nsa_w_gate.shape[-1]))).astype(BF16)
            wts = (nsa_w_q[j].astype(BF16), nsa_w_kv[j].astype(BF16), wg, nsa_w_o[j].astype(BF16),
                   jnp.swapaxes(nsa_cmp_pe[j], 0, 1).astype(F32), nsa_cmp_w1[j].astype(BF16),
                   nsa_cmp_w2[j].astype(BF16), _slc_value_weight_t(nsa_w_kv[j]))
            xp, hp, xs, hs, kv, kvs = _nsa_layer(hp, hs, xp, xs, g_post, g_ffn, wts, tabs,
                                                 (cmp_pool, slc_pool, win_pool, n_phys), j, page_table, bsz, seq)
            shp = (bsz, seq, KV_GROUPS, 2, HEAD_DIM)
            shs = (dec, 1, KV_GROUPS, 2, HEAD_DIM)
            out["cmp_p"].append(kv[0].reshape(shp)); out["cmp_s"].append(kvs[0].reshape(shs))
            out["slc_p"].append(kv[1].reshape(shp)); out["slc_s"].append(kvs[1].reshape(shs))
            out["win_p"].append(kv[2].reshape(shp)[:, seq - min(WINDOW, seq):])
            win_full = jnp.concatenate([cache_win_kv[j], kvs[2].reshape(shs)], axis=1)
            out["win_s"].append(win_full[:, win_full.shape[1] - min(WINDOW, PAST_LEN + 1):])
        elif m == 1:
            wts = (conv_w_pw1[j].astype(BF16), conv_dw[j], conv_dw_b[j], conv_ln_g[j], conv_ln_b[j],
                   conv_w_pw2[j].astype(BF16))
            xp, hp, xs, hs, cp, cs = _conv_layer(hp, hs, xp, xs, g_post, g_ffn, wts, state_conv[j], bsz, seq)
            out["conv_p"].append(cp); out["conv_s"].append(cs)
        else:
            wts = (ssm_a_re[j], ssm_a_im[j], ssm_log_dt[j], ssm_b_re[j], ssm_b_im[j], ssm_c_re[j], ssm_c_im[j],
                   ssm_d[j], ssm_w_glu[j].astype(BF16))
            xp, hp, xs, hs, rp, ip, rs, is_ = _s5_layer(xp, xs, g_all[i, 0], g_post, g_ffn, wts,
                                                        state_ssm_re[j], state_ssm_im[j], bsz, seq)
            out["re_p"].append(rp); out["im_p"].append(ip); out["re_s"].append(rs); out["im_s"].append(is_)
        wts = (ffn_w_up[i].astype(BF16), ffn_dw[i], ffn_dw_b[i], ffn_w_down[i].astype(BF16))
        xp, hp, xs, hs, fp, fs = _ffn_layer(hp, hs, xp, xs, g_ffn_post, g_next, wts, state_ffn_conv[i], bsz, seq)
        out["ffn_p"].append(fp); out["ffn_s"].append(fs)
    st = lambda k: jnp.stack(out[k])
    return (xp.reshape(bsz, seq, d), xs.reshape(dec, 1, d),
            st("cmp_p"), st("cmp_s"), st("slc_p"), st("slc_s"), st("win_p"), st("win_s"),
            st("conv_p"), st("conv_s"), st("re_p"), st("re_s"), st("im_p"), st("im_s"),
            st("ffn_p"), st("ffn_s"))
```

```python
import functools
import math

import numpy as np
import jax
import jax.numpy as jnp
from jax import lax
from jax.experimental import pallas as pl
from jax.experimental.pallas import tpu as pltpu

F32 = jnp.float32
BF16 = jnp.bfloat16

D_MODEL = 2048
DEPTH = 4
PAST_LEN = 16384
PAGE_SIZE = 128
N_HEADS = 16
HEAD_DIM = 128
KV_GROUPS = 4
GROUP_SIZE = 4
KV_COLS = KV_GROUPS * 2 * HEAD_DIM
CMP_BLOCK = 32
CMP_STRIDE = 16
SEL_BLOCK = 64
SEL_RATIO = SEL_BLOCK // CMP_STRIDE
N_SELECT = 16
WINDOW = 512
FORCE_BONUS = 1.0e4
N_BUCKETS = 32
MAX_EXACT = 16
REL_MAX_DIST = 128
CONV_WIDTH = 31
SSM_GROUP_CH = 16
SSM_GROUPS = 128
SSM_STATE = 64
SSM_DIM = SSM_GROUPS * SSM_STATE
D_FF = 5632
RMS_EPS = 1e-6
LN_EPS = 1e-5
SCALE = HEAD_DIM ** -0.5
NEG = -1e30

LANES = 128
SUBLANES = 8
VMEM_LIMIT = 56 * 1024 * 1024
ATT_TILE = 128
N_SLAB = SSM_DIM // LANES
SLAB_GRP = N_SLAB // SUBLANES

HIGHEST = lax.Precision.HIGHEST
NT_DIMS = (((1,), (1,)), ((), ()))


def _params(*sem):
    return pltpu.CompilerParams(dimension_semantics=sem, vmem_limit_bytes=VMEM_LIMIT)


def _rms(x, g):
    return x * lax.rsqrt(jnp.mean(x * x, axis=-1, keepdims=True) + RMS_EPS) * g


def _rmsnorm_kernel(x_ref, g_ref, o_ref):
    o_ref[...] = _rms(x_ref[...], g_ref[...]).astype(o_ref.dtype)


def rmsnorm_cast(x, g, tm):
    r, d = x.shape
    return pl.pallas_call(
        _rmsnorm_kernel,
        grid=(r // tm,),
        in_specs=[pl.BlockSpec((tm, d), lambda m: (m, 0)), pl.BlockSpec((1, d), lambda m: (0, 0))],
        out_specs=pl.BlockSpec((tm, d), lambda m: (m, 0)),
        out_shape=jax.ShapeDtypeStruct((r, d), BF16),
        compiler_params=_params("parallel"),
        name="rmsnorm",
    )(x, g.reshape(1, d))


def _mm_kernel(a_ref, w_ref, o_ref, *, act):
    r = jnp.dot(a_ref[...], w_ref[...], preferred_element_type=F32)
    if act == "sigmoid":
        r = jax.nn.sigmoid(r)
    o_ref[...] = r.astype(o_ref.dtype)


def matmul(a, w, *, n_split, out_dtype, act, tm, tn, name):
    r, k = a.shape
    n = w.shape[1]
    per = (n // n_split) // tn
    return pl.pallas_call(
        functools.partial(_mm_kernel, act=act),
        grid=(r // tm, n // tn),
        in_specs=[pl.BlockSpec((tm, k), lambda m, j: (m, 0)), pl.BlockSpec((k, tn), lambda m, j: (0, j))],
        out_specs=pl.BlockSpec((None, tm, tn), lambda m, j: (j // per, m, j % per)),
        out_shape=jax.ShapeDtypeStruct((n_split, r, n // n_split), out_dtype),
        compiler_params=_params("parallel", "arbitrary"),
        name=name,
    )(a, w)


def _value_t_kernel(a_ref, w_ref, o_ref):
    o_ref[...] = jnp.dot(a_ref[...], w_ref[...], preferred_element_type=F32).T.astype(o_ref.dtype)


def project_value_t(a, w_kv, branch, *, tm, name):
    r, k = a.shape
    first = branch * N_KV_SLABS + 1
    return pl.pallas_call(
        _value_t_kernel,
        grid=(r // tm, KV_GROUPS),
        in_specs=[pl.BlockSpec((tm, k), lambda m, g: (m, 0)),
                  pl.BlockSpec((k, HEAD_DIM), lambda m, g: (0, first + 2 * g))],
        out_specs=pl.BlockSpec((HEAD_DIM, tm), lambda m, g: (g, m)),
        out_shape=jax.ShapeDtypeStruct((KV_GROUPS * HEAD_DIM, r), BF16),
        compiler_params=_params("parallel", "arbitrary"),
        name=name,
    )(a, w_kv)


def _glu_mm_kernel(a_ref, wa_ref, wb_ref, o_ref):
    a = a_ref[...]
    lin = jnp.dot(a, wa_ref[...], preferred_element_type=F32)
    gate = jnp.dot(a, wb_ref[...], preferred_element_type=F32)
    o_ref[...] = lin * jax.nn.sigmoid(gate)


def glu_matmul(a, w, *, tm, tn, name):
    r, k = a.shape
    n = w.shape[1] // 2
    nb = n // tn
    return pl.pallas_call(
        _glu_mm_kernel,
        grid=(r // tm, nb),
        in_specs=[pl.BlockSpec((tm, k), lambda m, j: (m, 0)),
                  pl.BlockSpec((k, tn), lambda m, j: (0, j)),
                  pl.BlockSpec((k, tn), lambda m, j: (0, j + nb))],
        out_specs=pl.BlockSpec((tm, tn), lambda m, j: (m, j)),
        out_shape=jax.ShapeDtypeStruct((r, n), F32),
        compiler_params=_params("parallel", "arbitrary"),
        name=name,
    )(a, w, w)


def _proj_res_kernel(a_ref, w_ref, x_ref, gp_ref, gn_ref, xo_ref, ho_ref, *acc, glu, nk):
    def finish(y):
        if glu:
            d = y.shape[1] // 2
            y = y[:, :d] * jax.nn.sigmoid(y[:, d:])
        xn = x_ref[...] + _rms(y, gp_ref[...])
        xo_ref[...] = xn
        ho_ref[...] = _rms(xn, gn_ref[...]).astype(ho_ref.dtype)

    part = jnp.dot(a_ref[...], w_ref[...], preferred_element_type=F32)
    if nk == 1:
        finish(part)
        return
    acc_ref, = acc
    k = pl.program_id(1)

    @pl.when(k == 0)
    def _():
        acc_ref[...] = part

    @pl.when((k > 0) & (k < nk - 1))
    def _():
        acc_ref[...] += part

    @pl.when(k == nk - 1)
    def _():
        finish(acc_ref[...] + part)


def proj_res(a, w, x, g_post, g_next, *, glu, tm, tk, name):
    r, kdim = a.shape
    n = w.shape[1]
    d = x.shape[1]
    nk = kdim // tk
    return pl.pallas_call(
        functools.partial(_proj_res_kernel, glu=glu, nk=nk),
        grid=(r // tm, nk),
        in_specs=[pl.BlockSpec((tm, tk), lambda m, k: (m, k)),
                  pl.BlockSpec((tk, n), lambda m, k: (k, 0)),
                  pl.BlockSpec((tm, d), lambda m, k: (m, 0)),
                  pl.BlockSpec((1, d), lambda m, k: (0, 0)),
                  pl.BlockSpec((1, d), lambda m, k: (0, 0))],
        out_specs=[pl.BlockSpec((tm, d), lambda m, k: (m, 0)),
                   pl.BlockSpec((tm, d), lambda m, k: (m, 0))],
        out_shape=[jax.ShapeDtypeStruct((r, d), F32), jax.ShapeDtypeStruct((r, d), BF16)],
        scratch_shapes=[pltpu.VMEM((tm, n), F32)] if nk > 1 else [],
        compiler_params=_params("parallel", "arbitrary"),
        name=name,
    )(a, w, x, g_post.reshape(1, d), g_next.reshape(1, d))


def _ffn_up_kernel(a_ref, p1_ref, p2_ref, wg_ref, wv_ref, dw_ref, db_ref, act_ref, hist_ref, *, decode, tiles_per_seq):
    a = a_ref[...]
    wg = wg_ref[...].astype(BF16)
    gate = jnp.dot(a, wg, preferred_element_type=F32)
    val = jnp.dot(a, wv_ref[...].astype(BF16), preferred_element_type=F32)
    tm = gate.shape[0]
    if decode:
        g2, g1 = p2_ref[...], p1_ref[...]
        hist_ref[...] = gate
    else:
        halo = jnp.dot(p1_ref[...], wg, preferred_element_type=F32)
        halo = jnp.where(pl.program_id(0) % tiles_per_seq == 0, 0.0, halo)
        h7, h6 = halo[7:8, :], halo[6:7, :]
        row = lax.broadcasted_iota(jnp.int32, gate.shape, 0)
        g1 = jnp.where(row == 0, h7, pltpu.roll(gate, 1, 0))
        g2 = jnp.where(row == 0, h6, jnp.where(row == 1, h7, pltpu.roll(gate, 2, 0)))
        hist_ref[...] = gate[tm - SUBLANES:, :]
    g = dw_ref[0:1, :] * g2 + dw_ref[1:2, :] * g1 + dw_ref[2:3, :] * gate + db_ref[...]
    act_ref[...] = (jax.nn.gelu(g) * val).astype(act_ref.dtype)


def ffn_up(h, w_up, layer, dw, db, *, hist=None, seq_len, tm, tn):
    r, k = h.shape
    nb = D_FF // tn
    decode = hist is not None
    if decode:
        p1, p2 = hist[:, 1, :], hist[:, 0, :]
        p_specs = [pl.BlockSpec((tm, tn), lambda m, j: (m, j)), pl.BlockSpec((tm, tn), lambda m, j: (m, j))]
        hrows = tm
    else:
        p1 = p2 = h
        blk = tm // SUBLANES
        p_specs = [pl.BlockSpec((SUBLANES, k), lambda m, j: (jnp.maximum(m * blk - 1, 0), 0)),
                   pl.BlockSpec((SUBLANES, k), lambda m, j: (0, 0))]
        hrows = SUBLANES
    return pl.pallas_call(
        functools.partial(_ffn_up_kernel, decode=decode, tiles_per_seq=max(seq_len // tm, 1)),
        grid=(r // tm, nb),
        in_specs=[pl.BlockSpec((tm, k), lambda m, j: (m, 0))] + p_specs + [
            pl.BlockSpec((None, k, tn), lambda m, j: (layer, 0, j)),
            pl.BlockSpec((None, k, tn), lambda m, j: (layer, 0, j + nb)),
            pl.BlockSpec((3, tn), lambda m, j: (0, j)),
            pl.BlockSpec((1, tn), lambda m, j: (0, j))],
        out_specs=[pl.BlockSpec((tm, tn), lambda m, j: (m, j)),
                   pl.BlockSpec((None, hrows, tn), lambda m, j: (m, 0, j))],
        out_shape=[jax.ShapeDtypeStruct((r, D_FF), BF16),
                   jax.ShapeDtypeStruct((r // tm, hrows, D_FF), F32)],
        compiler_params=_params("parallel", "arbitrary"),
        name="ffn_up",
    )(h, p1, p2, w_up, w_up, dw, db.reshape(1, D_FF))


N_KV_SLABS = KV_GROUPS * 2


def _cmp_partial_slab(x_ref, c, pe_ref, w1_ref, n_chunks, first=0, pitch=1, chunk_pitch=None):
    chunk_pitch = CMP_STRIDE * pitch if chunk_pitch is None else chunk_pitch
    acc_a = jnp.zeros((n_chunks, HEAD_DIM), F32)
    acc_b = jnp.zeros((n_chunks, HEAD_DIM), F32)
    for s in range(CMP_STRIDE):
        xs = x_ref[pl.ds(first + s * pitch, n_chunks, stride=chunk_pitch), :]
        xa = (xs + pe_ref[c, s:s + 1, :]).astype(BF16)
        xb = (xs + pe_ref[c, CMP_STRIDE + s:CMP_STRIDE + s + 1, :]).astype(BF16)
        acc_a += jnp.dot(xa, w1_ref[c, s], preferred_element_type=F32)
        acc_b += jnp.dot(xb, w1_ref[c, CMP_STRIDE + s], preferred_element_type=F32)
    return acc_a, acc_b


def _cmp1_kernel(x_ref, pe_ref, w1_ref, a_ref, b_ref, *, n_chunks):
    c = pl.program_id(1) % 2
    a_ref[...], b_ref[...] = _cmp_partial_slab(x_ref, c, pe_ref, w1_ref, n_chunks)


def cmp_partial_rows(x, pe, w1, rows):
    nb = x.shape[0] // rows
    n_chunks = rows // CMP_STRIDE
    out = jax.ShapeDtypeStruct((nb, n_chunks, KV_COLS), F32)
    return pl.pallas_call(
        functools.partial(_cmp1_kernel, n_chunks=n_chunks),
        grid=(nb, N_KV_SLABS),
        in_specs=[pl.BlockSpec((rows, HEAD_DIM), lambda b, sl: (b, sl)),
                  pl.BlockSpec(pe.shape, lambda b, sl: (0, 0, 0)),
                  pl.BlockSpec(w1.shape, lambda b, sl: (0, 0, 0, 0))],
        out_specs=[pl.BlockSpec((None, n_chunks, HEAD_DIM), lambda b, sl: (b, 0, sl))] * 2,
        out_shape=[out, out],
        compiler_params=_params("parallel", "arbitrary"),
        name="cmp_partial",
    )(x, pe, w1)


PAGES_PER_STEP = 16
CHUNKS_PER_PAGE = PAGE_SIZE // CMP_STRIDE
CHUNK_ROWS = CMP_STRIDE * N_KV_SLABS
CHUNK_PITCH = CHUNK_ROWS + 4
STEP_CHUNKS = PAGES_PER_STEP * CHUNKS_PER_PAGE


def _cmp1_paged_kernel(pt_ref, *refs):
    page_refs = refs[:PAGES_PER_STEP]
    pe_ref, w1_ref, a_ref, b_ref, buf_ref = refs[PAGES_PER_STEP:]
    for i, p_ref in enumerate(page_refs):
        for ch in range(CHUNKS_PER_PAGE):
            dst = (i * CHUNKS_PER_PAGE + ch) * CHUNK_PITCH
            buf_ref[dst:dst + CHUNK_ROWS, :] = p_ref[ch * CHUNK_ROWS:(ch + 1) * CHUNK_ROWS, :]
    for slab in range(N_KV_SLABS):
        cols = slice(slab * HEAD_DIM, (slab + 1) * HEAD_DIM)
        a_ref[:, cols], b_ref[:, cols] = _cmp_partial_slab(buf_ref, slab % 2, pe_ref, w1_ref, STEP_CHUNKS,
                                                           first=slab, pitch=N_KV_SLABS, chunk_pitch=CHUNK_PITCH)


def cmp_partial_paged(pool, n_phys, page_table, layer, pe, w1):
    bsz, n_pages = page_table.shape
    n_steps = n_pages // PAGES_PER_STEP
    n_chunks = STEP_CHUNKS
    out = jax.ShapeDtypeStruct((bsz, n_steps, n_chunks, KV_COLS), F32)

    def page_spec(i):
        return pl.BlockSpec((None, PAGE_SIZE * N_KV_SLABS, HEAD_DIM),
                            lambda b, s, pt: (layer * n_phys + pt[b, s * PAGES_PER_STEP + i], 0, 0))

    out_spec = pl.BlockSpec((None, None, n_chunks, KV_COLS), lambda b, s, pt: (b, s, 0, 0))
    a, b = pl.pallas_call(
        _cmp1_paged_kernel,
        grid_spec=pltpu.PrefetchScalarGridSpec(
            num_scalar_prefetch=1,
            grid=(bsz, n_steps),
            in_specs=[page_spec(i) for i in range(PAGES_PER_STEP)] + [
                pl.BlockSpec(pe.shape, lambda b, s, pt: (0, 0, 0)),
                pl.BlockSpec(w1.shape, lambda b, s, pt: (0, 0, 0, 0))],
            out_specs=[out_spec, out_spec],
            scratch_shapes=[pltpu.VMEM((STEP_CHUNKS * CHUNK_PITCH, HEAD_DIM), F32)]),
        out_shape=[out, out],
        compiler_params=_params("parallel", "arbitrary"),
        name="cmp_partial_paged",
    )(page_table, *([pool] * PAGES_PER_STEP), pe, w1)
    return a.reshape(bsz, -1, KV_COLS), b.reshape(bsz, -1, KV_COLS)


def _cmp2_kernel(a_ref, b_ref, bx_ref, w2_ref, o_ref):
    n = a_ref.shape[0]
    row = lax.broadcasted_iota(jnp.int32, (n, HEAD_DIM), 0)
    for slab in range(KV_GROUPS * 2):
        c = slab % 2
        cols = slice(slab * HEAD_DIM, (slab + 1) * HEAD_DIM)
        nxt = pltpu.roll(b_ref[:, cols], n - 1, 0)
        nxt = jnp.where(row == n - 1, bx_ref[0:1, cols], nxt)
        h = jax.nn.gelu(a_ref[:, cols] + nxt).astype(BF16)
        o_ref[:, cols] = jnp.dot(h, w2_ref[c], preferred_element_type=F32).astype(o_ref.dtype)


def cmp_finish(a, b, b_next, w2):
    nb, n, _ = a.shape
    return pl.pallas_call(
        _cmp2_kernel,
        grid=(nb,),
        in_specs=[pl.BlockSpec((None, n, KV_COLS), lambda i: (i, 0, 0)),
                  pl.BlockSpec((None, n, KV_COLS), lambda i: (i, 0, 0)),
                  pl.BlockSpec((None, SUBLANES, KV_COLS), lambda i: (i, 0, 0)),
                  pl.BlockSpec(w2.shape, lambda i: (0, 0, 0))],
        out_specs=pl.BlockSpec((None, n, KV_COLS), lambda i: (i, 0, 0)),
        out_shape=jax.ShapeDtypeStruct((nb, n, KV_COLS), BF16),
        compiler_params=_params("parallel"),
        name="cmp_finish",
    )(a, b, b_next, w2)


def _softmax_rows(s, mask):
    s = jnp.where(mask, s, NEG)
    m = jnp.max(s, axis=-1, keepdims=True)
    e = jnp.where(mask, jnp.exp(s - m), 0.0)
    l = jnp.sum(e, axis=-1, keepdims=True)
    return jnp.where(l > 0.0, e / jnp.where(l > 0.0, l, 1.0), 0.0)


def _cmp_attn_kernel(q_ref, kc_ref, bias_ref, msel_ref, exp_ref, o_ref, mask_ref, *, n_sel):
    qt = pl.program_id(1)
    tq, n_cmp = q_ref.shape[0], kc_ref.shape[0]
    qpos = qt * tq + lax.broadcasted_iota(jnp.int32, (tq, n_cmp), 0)
    cpos = lax.broadcasted_iota(jnp.int32, (tq, n_cmp), 1) * CMP_STRIDE + (CMP_BLOCK - 1)
    cadd = jnp.where(cpos <= qpos, 0.0, NEG)
    j = lax.broadcasted_iota(jnp.int32, (n_sel, tq), 0)
    cur = (qt * tq + lax.broadcasted_iota(jnp.int32, (n_sel, tq), 1)) // SEL_BLOCK
    valid = j <= cur
    forced = (j == 0) | (j == cur) | (j == cur - 1)
    pad_rows = jnp.zeros((exp_ref.shape[1] - n_sel, tq), F32)
    diag = pl.ds(pl.multiple_of(qt * tq, tq), tq)
    causal = lax.broadcasted_iota(jnp.int32, (tq, tq), 0) <= lax.broadcasted_iota(jnp.int32, (tq, tq), 1)
    for g in range(KV_GROUPS):
        k = kc_ref[:, g * 2 * HEAD_DIM:(g * 2 + 1) * HEAD_DIM]
        v = kc_ref[:, (g * 2 + 1) * HEAD_DIM:(g * 2 + 2) * HEAD_DIM]
        heads = range(g * GROUP_SIZE, (g + 1) * GROUP_SIZE)
        q4 = jnp.concatenate([q_ref[:, h * HEAD_DIM:(h + 1) * HEAD_DIM] for h in heads], axis=0)
        badd = jnp.concatenate([bias_ref[h] + cadd for h in heads], axis=0)
        s = lax.dot_general(q4, k, NT_DIMS, preferred_element_type=F32) * SCALE + badd
        p = _softmax_rows(s, s > 0.5 * NEG)
        o4 = jnp.dot(p.astype(BF16), v, preferred_element_type=F32)
        imp = jnp.zeros((tq, n_cmp), F32)
        for r, h in enumerate(heads):
            o_ref[:, h * HEAD_DIM:(h + 1) * HEAD_DIM] = o4[r * tq:(r + 1) * tq, :]
            imp = imp + p[r * tq:(r + 1) * tq, :]
        p_sel = lax.dot_general(msel_ref[...], imp, NT_DIMS, precision=HIGHEST, preferred_element_type=F32)
        score = jnp.where(valid, p_sel + jnp.where(forced, FORCE_BONUS, 0.0), -jnp.inf)
        rank = jnp.zeros((n_sel, tq), jnp.int32)
        for i in range(n_sel):
            si = score[i:i + 1, :]
            beats = (si > score) | ((si == score) & (i < j))
            rank = rank + beats.astype(jnp.int32)
        sel_t = jnp.where((rank < N_SELECT) & valid, 1.0, 0.0)
        sel = jnp.concatenate([sel_t, pad_rows], axis=0).astype(BF16)
        allowed = jnp.dot(exp_ref[...], sel, preferred_element_type=F32)
        mask_ref[g] = ((allowed - 1.0) * -NEG).astype(mask_ref.dtype)
        allowed_d = jnp.dot(exp_ref[diag, :], sel, preferred_element_type=F32)
        mask_ref[g, diag, :] = jnp.where(causal, (allowed_d - 1.0) * -NEG, NEG).astype(mask_ref.dtype)


def cmp_attention(q, kc, bias_cmp, msel, expand, bsz, seq):
    tq = ATT_TILE
    nqt = seq // tq
    n_cmp = kc.shape[1]
    n_sel = msel.shape[0]
    return pl.pallas_call(
        functools.partial(_cmp_attn_kernel, n_sel=n_sel),
        grid=(bsz, nqt),
        in_specs=[pl.BlockSpec((tq, D_MODEL), lambda b, t: (b * nqt + t, 0)),
                  pl.BlockSpec((None, n_cmp, KV_COLS), lambda b, t: (b, 0, 0)),
                  pl.BlockSpec((N_HEADS, tq, n_cmp), lambda b, t: (0, t, 0)),
                  pl.BlockSpec(msel.shape, lambda b, t: (0, 0)),
                  pl.BlockSpec(expand.shape, lambda b, t: (0, 0))],
        out_specs=[pl.BlockSpec((tq, D_MODEL), lambda b, t: (b * nqt + t, 0)),
                   pl.BlockSpec((None, KV_GROUPS, seq, tq), lambda b, t: (b, 0, 0, t))],
        out_shape=[jax.ShapeDtypeStruct((bsz * seq, D_MODEL), F32),
                   jax.ShapeDtypeStruct((bsz, KV_GROUPS, seq, seq), BF16)],
        compiler_params=_params("parallel", "arbitrary"),
        name="cmp_attention",
    )(q, kc, bias_cmp, msel, expand)


SLC_KEYS = 512
SLC_SUB = SLC_KEYS // ATT_TILE
SLC_QUERIES = 256


def _slc_kernel(qt_ref, ks_ref, q_ref, k_ref, vt_ref, tbt_ref, mask_ref, o_ref, m_ref, l_ref, acc_ref):
    g, pair = pl.program_id(1), pl.program_id(2)
    qt, ks = qt_ref[pair], ks_ref[pair]
    tq = q_ref.shape[0]

    @pl.when(ks == 0)
    def _():
        m_ref[...] = jnp.full_like(m_ref, NEG)
        l_ref[...] = jnp.zeros_like(l_ref)
        acc_ref[...] = jnp.zeros_like(acc_ref)

    q4 = jnp.concatenate([q_ref[:, r * HEAD_DIM:(r + 1) * HEAD_DIM] for r in range(GROUP_SIZE)], axis=0)
    rows = []
    q_sub = tq // ATT_TILE
    for c in range(SLC_SUB):
        tbi = [jnp.clip(qt * q_sub + qs - (ks * SLC_SUB + c), 0, 2) for qs in range(q_sub)]
        madd = [mask_ref[c * ATT_TILE:(c + 1) * ATT_TILE, qs * ATT_TILE:(qs + 1) * ATT_TILE].astype(F32)
                for qs in range(q_sub)]
        rows.append(jnp.concatenate([tbt_ref[tbi[qs], g * GROUP_SIZE + r] + madd[qs]
                                     for r in range(GROUP_SIZE) for qs in range(q_sub)], axis=1))
    st = (lax.dot_general(k_ref[...].astype(BF16), q4, NT_DIMS, preferred_element_type=F32) * SCALE
          + jnp.concatenate(rows, axis=0))
    m_prev = m_ref[...]
    m_new = jnp.maximum(m_prev, jnp.max(st, axis=0, keepdims=True))
    alpha = jnp.exp(m_prev - m_new)
    pt = jnp.exp(st - m_new)
    l_ref[...] = alpha * l_ref[...] + jnp.sum(pt, axis=0, keepdims=True)
    acc_ref[...] = alpha * acc_ref[...] + jnp.dot(vt_ref[...], pt.astype(BF16), preferred_element_type=F32)
    m_ref[...] = m_new

    @pl.when(ks == ((qt + 1) * tq - 1) // SLC_KEYS)
    def _():
        ot = acc_ref[...] / l_ref[...]
        for r in range(GROUP_SIZE):
            o_ref[:, r * HEAD_DIM:(r + 1) * HEAD_DIM] = ot[:, r * tq:(r + 1) * tq].T


def slc_attention(q, kv, vt, mask, tbt, bsz, seq):
    tq = SLC_QUERIES
    nqt = seq // tq
    nks = seq // SLC_KEYS
    gw = GROUP_SIZE * HEAD_DIM
    pairs = [(t, s) for t in range(nqt) for s in range(((t + 1) * tq - 1) // SLC_KEYS + 1)]
    qt_of = jnp.asarray([p[0] for p in pairs], jnp.int32)
    ks_of = jnp.asarray([p[1] for p in pairs], jnp.int32)
    return pl.pallas_call(
        _slc_kernel,
        grid_spec=pltpu.PrefetchScalarGridSpec(
            num_scalar_prefetch=2,
            grid=(bsz, KV_GROUPS, len(pairs)),
            in_specs=[pl.BlockSpec((tq, gw), lambda b, g, p, qt, ks: (b * nqt + qt[p], g)),
                      pl.BlockSpec((SLC_KEYS, HEAD_DIM), lambda b, g, p, qt, ks: (b * nks + ks[p], 2 * g)),
                      pl.BlockSpec((HEAD_DIM, SLC_KEYS), lambda b, g, p, qt, ks: (g, b * nks + ks[p])),
                      pl.BlockSpec(tbt.shape, lambda b, g, p, qt, ks: (0, 0, 0, 0)),
                      pl.BlockSpec((None, None, SLC_KEYS, tq), lambda b, g, p, qt, ks: (b, g, ks[p], qt[p]))],
            out_specs=pl.BlockSpec((tq, gw), lambda b, g, p, qt, ks: (b * nqt + qt[p], g)),
            scratch_shapes=[pltpu.VMEM((1, GROUP_SIZE * tq), F32), pltpu.VMEM((1, GROUP_SIZE * tq), F32),
                            pltpu.VMEM((HEAD_DIM, GROUP_SIZE * tq), F32)]),
        out_shape=jax.ShapeDtypeStruct((bsz * seq, D_MODEL), F32),
        compiler_params=_params("parallel", "parallel", "arbitrary"),
        name="slc_attention",
    )(qt_of, ks_of, q, kv, vt, tbt, mask)


WIN_TILES = WINDOW // ATT_TILE + 1
WIN_SPAN = WIN_TILES * ATT_TILE


def _win_kernel(q_ref, *refs):
    kv_refs, (bias_ref, oc_ref, os_ref, gate_ref, o_ref) = refs[:WIN_TILES], refs[WIN_TILES:]
    gate = gate_ref[...]
    qt = pl.program_id(1)
    tq = q_ref.shape[0]
    row = lax.broadcasted_iota(jnp.int32, (tq, WIN_SPAN), 0)
    col = lax.broadcasted_iota(jnp.int32, (tq, WIN_SPAN), 1)
    back = row + WINDOW - col
    mask = (back >= 0) & (back <= WINDOW) & (qt * tq - WINDOW + col >= 0)
    madd = jnp.where(mask, 0.0, NEG)
    for g in range(KV_GROUPS):
        k = jnp.concatenate([r[:, g * 2 * HEAD_DIM:(g * 2 + 1) * HEAD_DIM].astype(BF16) for r in kv_refs], axis=0)
        v = jnp.concatenate([r[:, (g * 2 + 1) * HEAD_DIM:(g * 2 + 2) * HEAD_DIM].astype(BF16) for r in kv_refs], axis=0)
        heads = range(g * GROUP_SIZE, (g + 1) * GROUP_SIZE)
        q4 = jnp.concatenate([q_ref[:, h * HEAD_DIM:(h + 1) * HEAD_DIM] for h in heads], axis=0)
        badd = jnp.concatenate([bias_ref[h] + madd for h in heads], axis=0)
        s = lax.dot_general(q4, k, NT_DIMS, preferred_element_type=F32) * SCALE + badd
        e = jnp.exp(s - jnp.max(s, axis=-1, keepdims=True))
        l = jnp.sum(e, axis=-1, keepdims=True)
        o = jnp.dot(e.astype(BF16), v, preferred_element_type=F32) / l
        for r, h in enumerate(heads):
            cols = slice(h * HEAD_DIM, (h + 1) * HEAD_DIM)
            merged = (oc_ref[:, cols] * gate[:, 3 * h:3 * h + 1] + os_ref[:, cols] * gate[:, 3 * h + 1:3 * h + 2]
                      + o[r * tq:(r + 1) * tq, :] * gate[:, 3 * h + 2:3 * h + 3])
            o_ref[:, cols] = merged.astype(o_ref.dtype)


def win_attention(q, kv, bias_win, o_cmp, o_slc, gate, bsz, seq):
    tq = ATT_TILE
    nqt = seq // tq
    row_spec = pl.BlockSpec((tq, D_MODEL), lambda b, t: (b * nqt + t, 0))

    def kv_spec(i):
        return pl.BlockSpec((tq, KV_COLS), lambda b, t: (b * nqt + jnp.maximum(t - (WIN_TILES - 1) + i, 0), 0))

    return pl.pallas_call(
        _win_kernel,
        grid=(bsz, nqt),
        in_specs=[row_spec]
        + [kv_spec(i) for i in range(WIN_TILES)]
        + [pl.BlockSpec(bias_win.shape, lambda b, t: (0, 0, 0)), row_spec, row_spec,
           pl.BlockSpec((tq, LANES), lambda b, t: (b * nqt + t, 0))],
        out_specs=row_spec,
        out_shape=jax.ShapeDtypeStruct((bsz * seq, D_MODEL), BF16),
        compiler_params=_params("parallel", "arbitrary"),
        name="win_attention",
    )(q, *([kv] * WIN_TILES), bias_win, o_cmp, o_slc, gate)


def _combine_kernel(oc_ref, os_ref, ow_ref, g_ref, o_ref):
    gate = g_ref[...]
    for h in range(N_HEADS):
        cols = slice(h * HEAD_DIM, (h + 1) * HEAD_DIM)
        o = (oc_ref[:, cols] * gate[:, 3 * h:3 * h + 1] + os_ref[:, cols] * gate[:, 3 * h + 1:3 * h + 2]
             + ow_ref[:, cols] * gate[:, 3 * h + 2:3 * h + 3])
        o_ref[:, cols] = o.astype(o_ref.dtype)


def nsa_combine(o_cmp, o_slc, o_win, gate, tm):
    r = o_cmp.shape[0]
    spec = pl.BlockSpec((tm, D_MODEL), lambda m: (m, 0))
    return pl.pallas_call(
        _combine_kernel,
        grid=(r // tm,),
        in_specs=[spec, spec, spec, pl.BlockSpec((tm, LANES), lambda m: (m, 0))],
        out_specs=spec,
        out_shape=jax.ShapeDtypeStruct((r, D_MODEL), BF16),
        compiler_params=_params("parallel"),
        name="nsa_combine",
    )(o_cmp, o_slc, o_win, gate)


N_SEL_S = -(-(PAST_LEN + 1) // SEL_BLOCK)
N_SEL_S_PAD = 384
N_CMP_S = PAST_LEN // CMP_STRIDE


def _group_rows(parts, hgrp):
    out = parts[0]
    for g in range(1, KV_GROUPS):
        out = jnp.where(hgrp == g, parts[g], out)
    return out


def _sample_attn_kernel(q_ref, kc_ref, bc_ref, gsum_ref, msel_ref, win_ref, new_ref, bw_ref, bn_ref,
                        oc_ref, ow_ref, idx_ref):
    q = q_ref[...]
    hgrp = lax.broadcasted_iota(jnp.int32, (N_HEADS, 1), 0) // GROUP_SIZE

    def kcol(g):
        return slice(g * 2 * HEAD_DIM, (g * 2 + 1) * HEAD_DIM)

    def vcol(g):
        return slice((g * 2 + 1) * HEAD_DIM, (g * 2 + 2) * HEAD_DIM)

    s = _group_rows([lax.dot_general(q, kc_ref[:, kcol(g)], NT_DIMS, preferred_element_type=F32)
                     for g in range(KV_GROUPS)], hgrp)
    s = s * SCALE + bc_ref[...]
    n = lax.broadcasted_iota(jnp.int32, s.shape, 1)
    p = _softmax_rows(s, n * CMP_STRIDE + (CMP_BLOCK - 1) <= PAST_LEN)
    pb = p.astype(BF16)
    oc_ref[...] = _group_rows([jnp.dot(pb, kc_ref[:, vcol(g)], preferred_element_type=F32)
                               for g in range(KV_GROUPS)], hgrp)
    imp = jnp.dot(gsum_ref[...], p, precision=HIGHEST, preferred_element_type=F32)
    p_sel = jnp.dot(imp, msel_ref[...], precision=HIGHEST, preferred_element_type=F32)
    j = lax.broadcasted_iota(jnp.int32, p_sel.shape, 1)
    cur = PAST_LEN // SEL_BLOCK
    forced = (j == 0) | (j == cur) | (j == cur - 1)
    score = jnp.where(j <= cur, p_sel + jnp.where(forced, FORCE_BONUS, 0.0), -jnp.inf)
    lane = lax.broadcasted_iota(jnp.int32, idx_ref.shape, 1)
    idx = jnp.zeros(idx_ref.shape, F32)
    jf = j.astype(F32)
    for kk in range(N_SELECT):
        mx = jnp.max(score, axis=-1, keepdims=True)
        pick = jnp.min(jnp.where(score == mx, jf, float(N_SEL_S_PAD)), axis=-1, keepdims=True)
        idx = jnp.where(lane == kk, pick, idx)
        score = jnp.where(jf == pick, -jnp.inf, score)
    idx_ref[...] = idx.astype(jnp.int32)
    wb = win_ref.shape[0] // N_KV_SLABS

    def win_slab(slab):
        return win_ref[pl.ds(slab, wb, stride=N_KV_SLABS), :].astype(BF16)

    sw = _group_rows([lax.dot_general(q, win_slab(2 * g), NT_DIMS, preferred_element_type=F32)
                      for g in range(KV_GROUPS)], hgrp)
    sw = sw * SCALE + bw_ref[...]
    qf = q.astype(F32)
    sn = _group_rows([jnp.sum(qf * new_ref[:, kcol(g)].astype(BF16).astype(F32), axis=-1, keepdims=True)
                      for g in range(KV_GROUPS)], hgrp)
    sn = sn * SCALE + bn_ref[:, 0:1]
    m = jnp.maximum(jnp.max(sw, axis=-1, keepdims=True), sn)
    ew, en = jnp.exp(sw - m), jnp.exp(sn - m)
    l = jnp.sum(ew, axis=-1, keepdims=True) + en
    pw, pn = (ew / l).astype(BF16), (en / l).astype(BF16).astype(F32)
    ow = _group_rows([jnp.dot(pw, win_slab(2 * g + 1), preferred_element_type=F32)
                      + pn * new_ref[:, vcol(g)].astype(BF16).astype(F32) for g in range(KV_GROUPS)], hgrp)
    ow_ref[...] = ow


def sample_attention(q3, kc, bias_c, gsum, msel, win_pool, layer, kv_win_new, bias_w, bias_new):
    bsz = q3.shape[0]
    wb = win_pool.shape[1]
    o = jax.ShapeDtypeStruct((bsz, N_HEADS, HEAD_DIM), F32)
    full2 = lambda b: (0, 0)
    return pl.pallas_call(
        _sample_attn_kernel,
        grid=(bsz,),
        in_specs=[pl.BlockSpec((None, N_HEADS, HEAD_DIM), lambda b: (b, 0, 0)),
                  pl.BlockSpec((None, N_CMP_S, KV_COLS), lambda b: (b, 0, 0)),
                  pl.BlockSpec(bias_c.shape, full2),
                  pl.BlockSpec(gsum.shape, full2),
                  pl.BlockSpec(msel.shape, full2),
                  pl.BlockSpec((None, wb, HEAD_DIM), lambda b: (layer * bsz + b, 0, 0)),
                  pl.BlockSpec((None, 1, KV_COLS), lambda b: (b, 0, 0)),
                  pl.BlockSpec(bias_w.shape, full2),
                  pl.BlockSpec(bias_new.shape, full2)],
        out_specs=[pl.BlockSpec((None, N_HEADS, HEAD_DIM), lambda b: (b, 0, 0)),
                   pl.BlockSpec((None, N_HEADS, HEAD_DIM), lambda b: (b, 0, 0)),
                   pl.BlockSpec((None, SUBLANES, LANES), lambda b: (b, 0, 0))],
        out_shape=[o, o, jax.ShapeDtypeStruct((bsz, SUBLANES, LANES), jnp.int32)],
        compiler_params=_params("parallel"),
        name="sample_attention",
    )(q3, kc, bias_c, gsum, msel, win_pool, kv_win_new, bias_w, bias_new)


def _sample_slc_kernel(idx_ref, pt_ref, q_ref, *refs):
    blk_refs, (new_ref, bias_ref, o_ref) = refs[:N_SELECT], refs[N_SELECT:]
    b, g = pl.program_id(0), pl.program_id(1)
    ks, vs, biases = [], [], []
    lane = lax.broadcasted_iota(jnp.int32, (N_HEADS, SEL_BLOCK), 1)
    for kk, blk_ref in enumerate(blk_refs):
        j = idx_ref[b, g, kk]
        is_new = j >= PAST_LEN // SEL_BLOCK
        k_blk = blk_ref[pl.ds(2 * g, SEL_BLOCK, stride=N_KV_SLABS), :]
        v_blk = blk_ref[pl.ds(2 * g + 1, SEL_BLOCK, stride=N_KV_SLABS), :]
        ks.append(jnp.where(is_new, jnp.broadcast_to(new_ref[:, :HEAD_DIM], k_blk.shape), k_blk).astype(BF16))
        vs.append(jnp.where(is_new, jnp.broadcast_to(new_ref[:, HEAD_DIM:], v_blk.shape), v_blk).astype(BF16))
        biases.append(bias_ref[j] + jnp.where(j * SEL_BLOCK + lane <= PAST_LEN, 0.0, NEG))
    k, v = jnp.concatenate(ks, axis=0), jnp.concatenate(vs, axis=0)
    s = lax.dot_general(q_ref[...], k, NT_DIMS, preferred_element_type=F32) * SCALE + jnp.concatenate(biases, axis=1)
    p = _softmax_rows(s, s > 0.5 * NEG)
    o_ref[...] = jnp.dot(p.astype(BF16), v, preferred_element_type=F32)


def sample_slc_attention(idx, page_table, q3, pool, n_phys, layer, kv_slc_new, bias_blk):
    bsz = q3.shape[0]
    half_per_page = PAGE_SIZE // SEL_BLOCK
    n_half = n_phys * half_per_page
    last_past = PAST_LEN // SEL_BLOCK - 1

    def blk_spec(kk):
        def blk_map(b, g, idx_r, pt_r):
            j = jnp.minimum(idx_r[b, g, kk], last_past)
            return (layer * n_half + pt_r[b, j // half_per_page] * half_per_page + j % half_per_page, 0, 0)
        return pl.BlockSpec((None, SEL_BLOCK * N_KV_SLABS, HEAD_DIM), blk_map)

    return pl.pallas_call(
        _sample_slc_kernel,
        grid_spec=pltpu.PrefetchScalarGridSpec(
            num_scalar_prefetch=2,
            grid=(bsz, KV_GROUPS),
            in_specs=[pl.BlockSpec((None, N_HEADS, HEAD_DIM), lambda b, g, i, p: (b, 0, 0))]
            + [blk_spec(kk) for kk in range(N_SELECT)]
            + [pl.BlockSpec((None, 1, 2 * HEAD_DIM), lambda b, g, i, p: (b, 0, g)),
               pl.BlockSpec(bias_blk.shape, lambda b, g, i, p: (0, 0, 0))],
            out_specs=pl.BlockSpec((None, None, N_HEADS, HEAD_DIM), lambda b, g, i, p: (b, g, 0, 0))),
        out_shape=jax.ShapeDtypeStruct((bsz, KV_GROUPS, N_HEADS, HEAD_DIM), F32),
        compiler_params=_params("parallel", "arbitrary"),
        name="sample_slc_attention",
    )(idx, page_table, q3, *([pool] * N_SELECT), kv_slc_new, bias_blk)


CONV_TILE = 128
CONV_HALO = 32
CONV_ROWS = 64


def _dwconv_ln_kernel(u_ref, halo_ref, w_ref, b_ref, g_ref, beta_ref, o_ref, buf_ref, y_ref, sh_ref, *, tiles_per_seq):
    first = pl.program_id(0) % tiles_per_seq == 0
    buf_ref[0:CONV_HALO, :] = jnp.where(first, 0.0, halo_ref[...])
    buf_ref[CONV_HALO:, :] = u_ref[...]
    lead = CONV_HALO - (CONV_WIDTH - 1)
    for r0 in range(0, CONV_TILE, CONV_ROWS):
        for c0 in range(0, D_MODEL, LANES):
            cols = slice(c0, c0 + LANES)
            acc = jnp.broadcast_to(b_ref[:, cols], (CONV_ROWS, LANES))
            for res in range(SUBLANES):
                taps = [k for k in range(CONV_WIDTH) if (lead + k) % SUBLANES == res]
                span = max(lead + k - res for k in taps) + CONV_ROWS
                sh_ref[0:span, :] = buf_ref[r0 + res:r0 + res + span, cols]
                for k in taps:
                    off = lead + k - res
                    acc = acc + w_ref[k:k + 1, cols] * sh_ref[off:off + CONV_ROWS, :]
            y_ref[r0:r0 + CONV_ROWS, cols] = acc
    y = y_ref[...]
    mu = jnp.mean(y, axis=-1, keepdims=True)
    var = jnp.mean(jnp.square(y - mu), axis=-1, keepdims=True)
    y = (y - mu) * lax.rsqrt(var + LN_EPS) * g_ref[...] + beta_ref[...]
    o_ref[...] = (y * jax.nn.sigmoid(y)).astype(o_ref.dtype)


def dwconv_ln(u, w, b, ln_g, ln_b, seq):
    r, d = u.shape
    tiles_per_seq = seq // CONV_TILE
    ratio = CONV_TILE // CONV_HALO
    vec = lambda m: (0, 0)
    return pl.pallas_call(
        functools.partial(_dwconv_ln_kernel, tiles_per_seq=tiles_per_seq),
        grid=(r // CONV_TILE,),
        in_specs=[pl.BlockSpec((CONV_TILE, d), lambda m: (m, 0)),
                  pl.BlockSpec((CONV_HALO, d), lambda m: (jnp.maximum(m * ratio - 1, 0), 0)),
                  pl.BlockSpec((CONV_WIDTH, d), vec), pl.BlockSpec((1, d), vec),
                  pl.BlockSpec((1, d), vec), pl.BlockSpec((1, d), vec)],
        out_specs=pl.BlockSpec((CONV_TILE, d), lambda m: (m, 0)),
        out_shape=jax.ShapeDtypeStruct((r, d), BF16),
        scratch_shapes=[pltpu.VMEM((CONV_HALO + CONV_TILE, d), F32), pltpu.VMEM((CONV_TILE, d), F32),
                        pltpu.VMEM((CONV_HALO + CONV_ROWS, LANES), F32)],
        compiler_params=_params("parallel"),
        name="dwconv_ln",
    )(u, u, w, b.reshape(1, d), ln_g.reshape(1, d), ln_b.reshape(1, d))


def _dwconv_ln_decode_kernel(u_ref, hist_ref, w_ref, b_ref, g_ref, beta_ref, o_ref):
    y = b_ref[...] + w_ref[CONV_WIDTH - 1:CONV_WIDTH, :] * u_ref[...]
    for k in range(CONV_WIDTH - 1):
        y = y + w_ref[k:k + 1, :] * hist_ref[k]
    mu = jnp.mean(y, axis=-1, keepdims=True)
    var = jnp.mean(jnp.square(y - mu), axis=-1, keepdims=True)
    y = (y - mu) * lax.rsqrt(var + LN_EPS) * g_ref[...] + beta_ref[...]
    o_ref[...] = (y * jax.nn.sigmoid(y)).astype(o_ref.dtype)


def dwconv_ln_decode(u, hist_t, w, b, ln_g, ln_b):
    r, d = u.shape
    vec = lambda i: (0, 0)
    return pl.pallas_call(
        _dwconv_ln_decode_kernel,
        grid=(1,),
        in_specs=[pl.BlockSpec((r, d), vec), pl.BlockSpec(hist_t.shape, lambda i: (0, 0, 0)),
                  pl.BlockSpec((CONV_WIDTH, d), vec), pl.BlockSpec((1, d), vec),
                  pl.BlockSpec((1, d), vec), pl.BlockSpec((1, d), vec)],
        out_specs=pl.BlockSpec((r, d), vec),
        out_shape=jax.ShapeDtypeStruct((r, d), BF16),
        compiler_params=_params("arbitrary"),
        name="dwconv_ln_decode",
    )(u, hist_t, w, b.reshape(1, d), ln_g.reshape(1, d), ln_b.reshape(1, d))


S5_CHUNK = 256
S5_PITCH = S5_CHUNK + 4
IN_SLABS = D_MODEL // LANES
STATE_PER_IN = N_SLAB // IN_SLABS


def _s5_project_in(hb, wb_ref, store):
    half = STATE_PER_IN * LANES
    for i in range(IN_SLABS):
        res = jnp.dot(hb[:, i * LANES:(i + 1) * LANES], wb_ref[i], preferred_element_type=F32)
        for jj in range(STATE_PER_IN):
            store(i * STATE_PER_IN + jj, res[:, jj * LANES:(jj + 1) * LANES],
                  res[:, half + jj * LANES:half + (jj + 1) * LANES])


def _s5_project_out(load, cre_ref, cim_ref, hn, d_ref, y_ref):
    for i in range(IN_SLABS):
        cols = slice(i * LANES, (i + 1) * LANES)
        acc = d_ref[:, cols] * hn[:, cols]
        for jj in range(STATE_PER_IN):
            j = i * STATE_PER_IN + jj
            re, im = load(j)
            acc = acc + jnp.dot(re.astype(BF16), cre_ref[j], preferred_element_type=F32)
            acc = acc - jnp.dot(im.astype(BF16), cim_ref[j], preferred_element_type=F32)
        y_ref[:, cols] = acc.astype(y_ref.dtype)


def _s5_scan_kernel(x_ref, g_ref, wb_ref, ar_ref, ai_ref, cre_ref, cim_ref, d_ref,
                    y_ref, sr_ref, si_ref, bur_ref, bui_ref, hr_ref, hi_ref):
    tc, pitch = S5_CHUNK, S5_PITCH

    @pl.when(pl.program_id(1) == 0)
    def _():
        hr_ref[...] = jnp.zeros_like(hr_ref)
        hi_ref[...] = jnp.zeros_like(hi_ref)

    hn = _rms(x_ref[...], g_ref[...])

    def store(j, re, im):
        bur_ref[j * pitch:j * pitch + tc, :] = re
        bui_ref[j * pitch:j * pitch + tc, :] = im

    _s5_project_in(hn.astype(BF16), wb_ref, store)

    ar = [ar_ref[j8] for j8 in range(SLAB_GRP)]
    ai = [ai_ref[j8] for j8 in range(SLAB_GRP)]

    def step(t, carry):
        out = []
        for j8 in range(SLAB_GRP):
            hr, hi = carry[2 * j8], carry[2 * j8 + 1]
            rows = pl.ds(j8 * SUBLANES * pitch + t, SUBLANES, stride=pitch)
            nr = ar[j8] * hr - ai[j8] * hi + bur_ref[rows, :]
            ni = ar[j8] * hi + ai[j8] * hr + bui_ref[rows, :]
            bur_ref[rows, :] = nr
            bui_ref[rows, :] = ni
            out += [nr, ni]
        return tuple(out)

    init = []
    for j8 in range(SLAB_GRP):
        init += [hr_ref[j8], hi_ref[j8]]
    fin = lax.fori_loop(0, tc, step, tuple(init))
    for j8 in range(SLAB_GRP):
        hr_ref[j8] = fin[2 * j8]
        hi_ref[j8] = fin[2 * j8 + 1]
    sr_ref[...] = hr_ref[...]
    si_ref[...] = hi_ref[...]

    def load(j):
        return bur_ref[j * pitch:j * pitch + tc, :], bui_ref[j * pitch:j * pitch + tc, :]

    _s5_project_out(load, cre_ref, cim_ref, hn, d_ref, y_ref)


def s5_scan(x, g, wb, ar, ai, cre, cim, d_skip, bsz, seq):
    n_chunks = seq // S5_CHUNK
    st = jax.ShapeDtypeStruct((bsz, SLAB_GRP, SUBLANES, LANES), F32)
    st_spec = pl.BlockSpec((None, SLAB_GRP, SUBLANES, LANES), lambda b, c: (b, 0, 0, 0))
    vec = lambda b, c: (0, 0)
    c3 = lambda b, c: (0, 0, 0)
    return pl.pallas_call(
        _s5_scan_kernel,
        grid=(bsz, n_chunks),
        in_specs=[pl.BlockSpec((S5_CHUNK, D_MODEL), lambda b, c: (b * n_chunks + c, 0)),
                  pl.BlockSpec((1, D_MODEL), vec),
                  pl.BlockSpec(wb.shape, c3), pl.BlockSpec(ar.shape, c3), pl.BlockSpec(ai.shape, c3),
                  pl.BlockSpec(cre.shape, c3), pl.BlockSpec(cim.shape, c3),
                  pl.BlockSpec((1, D_MODEL), vec)],
        out_specs=[pl.BlockSpec((S5_CHUNK, D_MODEL), lambda b, c: (b * n_chunks + c, 0)), st_spec, st_spec],
        out_shape=[jax.ShapeDtypeStruct((bsz * seq, D_MODEL), BF16), st, st],
        scratch_shapes=[pltpu.VMEM((N_SLAB * S5_PITCH, LANES), F32), pltpu.VMEM((N_SLAB * S5_PITCH, LANES), F32),
                        pltpu.VMEM((SLAB_GRP, SUBLANES, LANES), F32), pltpu.VMEM((SLAB_GRP, SUBLANES, LANES), F32)],
        compiler_params=_params("parallel", "arbitrary"),
        name="s5_scan",
    )(x, g.reshape(1, -1), wb, ar, ai, cre, cim, d_skip.reshape(1, -1))


def _s5_decode_kernel(x_ref, g_ref, wb_ref, ar_ref, ai_ref, cre_ref, cim_ref, d_ref, h0r_ref, h0i_ref,
                      y_ref, sr_ref, si_ref):
    hn = _rms(x_ref[...], g_ref[...])

    def store(j, re, im):
        cols = slice(j * LANES, (j + 1) * LANES)
        ar, ai = ar_ref[:, cols], ai_ref[:, cols]
        hr, hi = h0r_ref[:, cols], h0i_ref[:, cols]
        sr_ref[:, cols] = ar * hr - ai * hi + re
        si_ref[:, cols] = ar * hi + ai * hr + im

    _s5_project_in(hn.astype(BF16), wb_ref, store)

    def load(j):
        cols = slice(j * LANES, (j + 1) * LANES)
        return sr_ref[:, cols], si_ref[:, cols]

    _s5_project_out(load, cre_ref, cim_ref, hn, d_ref, y_ref)


def s5_decode(x, g, wb, ar_row, ai_row, cre, cim, d_skip, h0r, h0i):
    r = x.shape[0]
    st = jax.ShapeDtypeStruct((r, SSM_DIM), F32)
    vec = lambda i: (0, 0)
    c3 = lambda i: (0, 0, 0)
    return pl.pallas_call(
        _s5_decode_kernel,
        grid=(1,),
        in_specs=[pl.BlockSpec((r, D_MODEL), vec), pl.BlockSpec((1, D_MODEL), vec),
                  pl.BlockSpec(wb.shape, c3), pl.BlockSpec((1, SSM_DIM), vec), pl.BlockSpec((1, SSM_DIM), vec),
                  pl.BlockSpec(cre.shape, c3), pl.BlockSpec(cim.shape, c3), pl.BlockSpec((1, D_MODEL), vec),
                  pl.BlockSpec((r, SSM_DIM), vec), pl.BlockSpec((r, SSM_DIM), vec)],
        out_specs=[pl.BlockSpec((r, D_MODEL), vec), pl.BlockSpec((r, SSM_DIM), vec), pl.BlockSpec((r, SSM_DIM), vec)],
        out_shape=[jax.ShapeDtypeStruct((r, D_MODEL), BF16), st, st],
        compiler_params=_params("arbitrary"),
        name="s5_decode",
    )(x, g.reshape(1, -1), wb, ar_row, ai_row, cre, cim, d_skip.reshape(1, -1), h0r, h0i)


def _t5_bucket(rel):
    n = jnp.maximum(rel, 0)
    nf = jnp.maximum(n, MAX_EXACT).astype(F32)
    big = MAX_EXACT + (jnp.log(nf / MAX_EXACT) / math.log(REL_MAX_DIST / MAX_EXACT)
                       * (N_BUCKETS - MAX_EXACT)).astype(jnp.int32)
    return jnp.where(n < MAX_EXACT, n, jnp.minimum(big, N_BUCKETS - 1))


def _bias_of(rel_bias, rel):
    onehot = (_t5_bucket(rel)[..., None] == jnp.arange(N_BUCKETS, dtype=jnp.int32)).astype(F32)
    return jnp.einsum('...k,kh->h...', onehot, rel_bias.astype(F32), precision=HIGHEST)


def _selection_matrix(n_cmp, n_sel_pad):
    coef = np.convolve(np.ones(SEL_RATIO), np.ones(CMP_BLOCK // CMP_STRIDE)).astype(np.float32)
    m = np.zeros((n_cmp, n_sel_pad), np.float32)
    for j in range(n_sel_pad):
        for o in range(coef.shape[0]):
            n = SEL_RATIO * j + o - (CMP_BLOCK // CMP_STRIDE - 1)
            if 0 <= n < n_cmp:
                m[n, j] = coef[o]
    return m


def _s5_params(a_re, a_im, log_dt, b_re, b_im, c_re, c_im):
    dt = jnp.exp(log_dt.astype(F32))[:, None]
    ar, ai = a_re.astype(F32), a_im.astype(F32)
    mag = jnp.exp(ar * dt)
    abar_re, abar_im = mag * jnp.cos(ai * dt), mag * jnp.sin(ai * dt)
    den = ar * ar + ai * ai
    coef_re = ((abar_re - 1.0) * ar + abar_im * ai) / den
    coef_im = (abar_im * ar - (abar_re - 1.0) * ai) / den
    br, bim = b_re.astype(F32), b_im.astype(F32)
    bb_re = coef_re[..., None] * br - coef_im[..., None] * bim
    bb_im = coef_re[..., None] * bim + coef_im[..., None] * br
    gpi = LANES // SSM_GROUP_CH
    eye = jnp.eye(gpi, dtype=F32)

    def in_blocks(bb):
        t = bb.reshape(IN_SLABS, gpi, SSM_STATE, SSM_GROUP_CH)
        blk = jnp.einsum('sgpc,gh->sgchp', t, eye)
        return blk.reshape(IN_SLABS, LANES, gpi * SSM_STATE)

    wb = jnp.concatenate([in_blocks(bb_re), in_blocks(bb_im)], axis=-1).astype(BF16)
    gps = LANES // SSM_STATE
    ch_per_in = LANES

    def out_blocks(c):
        t = c.astype(F32).reshape(IN_SLABS, STATE_PER_IN, gps, SSM_GROUP_CH, SSM_STATE)
        sel = jnp.eye(STATE_PER_IN * gps, dtype=F32).reshape(STATE_PER_IN, gps, STATE_PER_IN * gps)
        blk = jnp.einsum('ijgcp,jgh->ijgphc', t, sel)
        return blk.reshape(N_SLAB, LANES, ch_per_in).astype(BF16)

    return abar_re, abar_im, wb, out_blocks(c_re), out_blocks(c_im)


PROMPT_TM = 512
TN = 512
FFN_UP_TM = 1024
FFN_DOWN_TK = D_FF // 4
GLU_TM = 256
GLU_TK = D_MODEL


def _row_tile(r):
    return PROMPT_TM if r % PROMPT_TM == 0 else r


def _nsa_project(h, wq, wkv, wg):
    tm = _row_tile(h.shape[0])
    q = matmul(h, wq, n_split=1, out_dtype=BF16, act=None, tm=tm, tn=TN, name="nsa_q")[0]
    kv = matmul(h, wkv, n_split=3, out_dtype=F32, act=None, tm=tm, tn=TN, name="nsa_kv")
    gate = matmul(h, wg, n_split=1, out_dtype=F32, act="sigmoid", tm=tm, tn=LANES, name="nsa_gate")[0]
    return q, kv, gate


def _nsa_layer(hp, hs, xp, xs, g_post, g_next, wts, tabs, caches, layer, page_table, bsz, seq):
    wq, wkv, wg, wo, pe, w1, w2 = wts
    cmp_pool, slc_pool, win_pool, n_phys = caches
    dec = hs.shape[0]
    q, kv, gate = _nsa_project(hp, wq, wkv, wg)
    part_a, part_b = cmp_partial_rows(kv[0], pe, w1, seq)
    kc = cmp_finish(part_a, part_b, jnp.zeros((bsz, SUBLANES, KV_COLS), F32), w2)
    o_cmp, mask = cmp_attention(q, kc, tabs["cmp"], tabs["msel"], tabs["expand"], bsz, seq)
    vt = project_value_t(hp, wkv, 1, tm=PROMPT_TM, name="nsa_vt")
    o_slc = slc_attention(q, kv[1], vt, mask, tabs["tile_t"], bsz, seq)
    o = win_attention(q, kv[2], tabs["win"], o_cmp, o_slc, gate, bsz, seq)
    xp, hp = proj_res(o, wo, xp, g_post, g_next, glu=False, tm=PROMPT_TM, tk=D_MODEL, name="nsa_out")
    qs, kvs, gate_s = _nsa_project(hs, wq, wkv, wg)
    past_a, past_b = cmp_partial_paged(cmp_pool, n_phys, page_table, layer, pe, w1)
    tail = jnp.pad(kvs[0][:, None, :], ((0, 0), (0, CMP_STRIDE - 1), (0, 0))).reshape(dec * CMP_STRIDE, KV_COLS)
    _, tail_b = cmp_partial_rows(tail, pe, w1, dec * CMP_STRIDE)
    b_next = jnp.pad(tail_b[0][:, None, :], ((0, 0), (0, SUBLANES - 1), (0, 0)))
    kc_s = cmp_finish(past_a, past_b, b_next, w2)
    oc_s, ow_s, idx = sample_attention(qs.reshape(dec, N_HEADS, HEAD_DIM), kc_s, tabs["cmp_s"], tabs["gsum"],
                                       tabs["msel_s"], win_pool, layer, kvs[2][:, None, :],
                                       tabs["win_s"], tabs["new_s"])
    q3 = qs.reshape(dec, N_HEADS, HEAD_DIM)
    os_all = sample_slc_attention(idx[:, :KV_GROUPS, :N_SELECT], page_table, q3, slc_pool, n_phys, layer,
                                  kvs[1][:, None, :], tabs["slc_s"])
    os_s = jnp.stack([os_all[:, h // GROUP_SIZE, h] for h in range(N_HEADS)], axis=1)
    o_s = nsa_combine(oc_s.reshape(dec, D_MODEL), os_s.reshape(dec, D_MODEL), ow_s.reshape(dec, D_MODEL), gate_s, dec)
    xs, hs = proj_res(o_s, wo, xs, g_post, g_next, glu=False, tm=dec, tk=D_MODEL, name="nsa_out_s")
    return xp, hp, xs, hs, kv, kvs


def _conv_layer(hp, hs, xp, xs, g_post, g_next, wts, state, bsz, seq):
    w_pw1, dw, dw_b, ln_g, ln_b, w_pw2 = wts
    dec = hs.shape[0]
    u = glu_matmul(hp, w_pw1, tm=PROMPT_TM, tn=TN, name="conv_pw1")
    hc = dwconv_ln(u, dw, dw_b, ln_g, ln_b, seq)
    xp, hp = proj_res(hc, w_pw2, xp, g_post, g_next, glu=False, tm=PROMPT_TM, tk=D_MODEL, name="conv_pw2")
    hist_p = u.reshape(bsz, seq, D_MODEL)[:, seq - (CONV_WIDTH - 1):]
    us = glu_matmul(hs, w_pw1, tm=dec, tn=TN, name="conv_pw1_s")
    hc_s = dwconv_ln_decode(us, jnp.swapaxes(state, 0, 1), dw, dw_b, ln_g, ln_b)
    xs, hs = proj_res(hc_s, w_pw2, xs, g_post, g_next, glu=False, tm=dec, tk=D_MODEL, name="conv_pw2_s")
    hist_s = jnp.concatenate([state[:, 1:], us[:, None, :]], axis=1)
    return xp, hp, xs, hs, hist_p, hist_s


def _s5_layer(xp, xs, g_pre, g_post, g_next, wts, state_re, state_im, bsz, seq):
    a_re, a_im, log_dt, b_re, b_im, c_re, c_im, d_skip, w_glu = wts
    dec = xs.shape[0]
    abar_re, abar_im, wb, cre, cim = _s5_params(a_re, a_im, log_dt, b_re, b_im, c_re, c_im)
    slab_shape = (SLAB_GRP, SUBLANES, LANES)
    y, sr, si = s5_scan(xp, g_pre, wb, abar_re.reshape(slab_shape), abar_im.reshape(slab_shape), cre, cim,
                        d_skip, bsz, seq)
    xp, hp = proj_res(y, w_glu, xp, g_post, g_next, glu=True, tm=GLU_TM, tk=GLU_TK, name="s5_glu")
    ys, sr_s, si_s = s5_decode(xs, g_pre, wb, abar_re.reshape(1, SSM_DIM), abar_im.reshape(1, SSM_DIM), cre, cim,
                               d_skip, state_re.reshape(dec, SSM_DIM), state_im.reshape(dec, SSM_DIM))
    xs, hs = proj_res(ys, w_glu, xs, g_post, g_next, glu=True, tm=dec, tk=GLU_TK, name="s5_glu_s")
    gp = (SSM_GROUPS, SSM_STATE)
    return (xp, hp, xs, hs, sr.reshape((bsz,) + gp), si.reshape((bsz,) + gp),
            sr_s.reshape((dec,) + gp), si_s.reshape((dec,) + gp))


def _ffn_layer(hp, hs, xp, xs, g_post, g_next, wts, state, bsz, seq):
    w_up, layer, dw, dw_b, w_down = wts
    dec = hs.shape[0]
    act, hist = ffn_up(hp, w_up, layer, dw, dw_b, seq_len=seq, tm=FFN_UP_TM, tn=TN)
    xp, hp = proj_res(act, w_down, xp, g_post, g_next, glu=False, tm=PROMPT_TM, tk=FFN_DOWN_TK, name="ffn_down")
    tiles = seq // FFN_UP_TM
    hist_p = hist.reshape(bsz, tiles, SUBLANES, D_FF)[:, tiles - 1, SUBLANES - 2:, :]
    act_s, gate_s = ffn_up(hs, w_up, layer, dw, dw_b, hist=state, seq_len=1, tm=dec, tn=TN)
    xs, hs = proj_res(act_s, w_down, xs, g_post, g_next, glu=False, tm=dec, tk=FFN_DOWN_TK, name="ffn_down_s")
    hist_s = jnp.concatenate([state[:, 1:], gate_s.reshape(dec, 1, D_FF)], axis=1)
    return xp, hp, xs, hs, hist_p, hist_s


def _bias_tables(rel_bias, seq):
    tq = ATT_TILE
    i = jnp.arange(tq, dtype=jnp.int32)
    tile = jnp.stack([_bias_of(rel_bias, d * tq + i[:, None] - i[None, :]) for d in range(3)])
    n_cmp = seq // CMP_STRIDE
    cpos = jnp.arange(n_cmp, dtype=jnp.int32) * CMP_STRIDE + (CMP_BLOCK - 1)
    qpos = jnp.arange(seq, dtype=jnp.int32)
    n_sel = -(-seq // SEL_BLOCK)
    key = np.arange(seq)
    expand = (key[:, None] // SEL_BLOCK == np.arange(LANES)[None, :]).astype(np.float32)
    cpos_s = jnp.arange(N_CMP_S, dtype=jnp.int32) * CMP_STRIDE + (CMP_BLOCK - 1)
    wb = min(WINDOW, PAST_LEN)
    kpos_s = jnp.arange(N_SEL_S * SEL_BLOCK, dtype=jnp.int32)
    slc_s = _bias_of(rel_bias, PAST_LEN - kpos_s).reshape(N_HEADS, N_SEL_S, SEL_BLOCK)
    gsum = (np.arange(N_HEADS)[None, :] // GROUP_SIZE == np.arange(SUBLANES)[:, None]).astype(np.float32)
    return {
        "tile_t": jnp.swapaxes(tile, -1, -2),
        "win": _bias_of(rel_bias, i[:, None] + WINDOW - jnp.arange(WIN_SPAN, dtype=jnp.int32)[None, :]),
        "cmp": _bias_of(rel_bias, qpos[:, None] - cpos[None, :]),
        "msel": jnp.asarray(_selection_matrix(n_cmp, n_sel).T),
        "expand": jnp.asarray(expand, BF16),
        "cmp_s": _bias_of(rel_bias, PAST_LEN - cpos_s),
        "msel_s": jnp.asarray(_selection_matrix(N_CMP_S, N_SEL_S_PAD)),
        "gsum": jnp.asarray(gsum),
        "win_s": _bias_of(rel_bias, wb - jnp.arange(wb, dtype=jnp.int32)),
        "new_s": jnp.broadcast_to(_bias_of(rel_bias, jnp.zeros((1,), jnp.int32)), (N_HEADS, LANES)),
        "slc_s": jnp.swapaxes(slc_s, 0, 1),
    }


def kernel(x_prompt, x_sample, cache_cmp_kv, cache_slc_kv, cache_win_kv, state_conv, state_ssm_re, state_ssm_im, state_ffn_conv, page_table, norm_gain, rel_bias, nsa_w_q, nsa_w_kv, nsa_cmp_pe, nsa_cmp_w1, nsa_cmp_w2, nsa_w_gate, nsa_w_o, conv_w_pw1, conv_dw, conv_dw_b, conv_ln_g, conv_ln_b, conv_w_pw2, ssm_a_re, ssm_a_im, ssm_log_dt, ssm_b_re, ssm_b_im, ssm_c_re, ssm_c_im, ssm_d, ssm_w_glu, ffn_w_up, ffn_dw, ffn_dw_b, ffn_w_down):
    bsz, seq, d = x_prompt.shape
    dec, dec_seq, _ = x_sample.shape
    assert dec_seq == 1 and d == D_MODEL and seq % PROMPT_TM == 0 and seq // CMP_STRIDE == LANES
    n_nsa = cache_cmp_kv.shape[0]
    n_phys = cache_cmp_kv.shape[1]
    xp = x_prompt.reshape(bsz * seq, d)
    xs = x_sample.reshape(dec, d)
    tabs = _bias_tables(rel_bias, seq)
    cmp_pool = cache_cmp_kv.reshape(n_nsa * n_phys, PAGE_SIZE * N_KV_SLABS, HEAD_DIM)
    slc_pool = cache_slc_kv.reshape(n_nsa * n_phys * (PAGE_SIZE // SEL_BLOCK), SEL_BLOCK * N_KV_SLABS, HEAD_DIM)
    win_pool = cache_win_kv.reshape(n_nsa * dec, cache_win_kv.shape[2] * N_KV_SLABS, HEAD_DIM)
    g_all = norm_gain.astype(F32)

    hp = rmsnorm_cast(xp, g_all[0, 0], PROMPT_TM)
    hs = rmsnorm_cast(xs, g_all[0, 0], dec)
    out = {k: [] for k in ("cmp_p", "cmp_s", "slc_p", "slc_s", "win_p", "win_s", "conv_p", "conv_s",
                           "re_p", "re_s", "im_p", "im_s", "ffn_p", "ffn_s")}
    counts = [0, 0, 0]
    for i in range(DEPTH):
        m = i % 3
        j = counts[m]
        counts[m] += 1
        g_post, g_ffn, g_ffn_post = g_all[i, 1], g_all[i, 2], g_all[i, 3]
        g_next = g_all[i + 1, 0] if i + 1 < DEPTH else g_all[i, 0]
        if m == 0:
            wg = jnp.pad(nsa_w_gate[j], ((0, 0), (0, LANES - nsa_w_gate.shape[-1]))).astype(BF16)
            wts = (nsa_w_q[j].astype(BF16), nsa_w_kv[j].astype(BF16), wg, nsa_w_o[j].astype(BF16),
                   jnp.swapaxes(nsa_cmp_pe[j], 0, 1).astype(F32), nsa_cmp_w1[j].astype(BF16),
                   nsa_cmp_w2[j].astype(BF16))
            xp, hp, xs, hs, kv, kvs = _nsa_layer(hp, hs, xp, xs, g_post, g_ffn, wts, tabs,
                                                 (cmp_pool, slc_pool, win_pool, n_phys), j, page_table, bsz, seq)
            shp = (bsz, seq, KV_GROUPS, 2, HEAD_DIM)
            shs = (dec, 1, KV_GROUPS, 2, HEAD_DIM)
            out["cmp_p"].append(kv[0].reshape(shp)); out["cmp_s"].append(kvs[0].reshape(shs))
            out["slc_p"].append(kv[1].reshape(shp)); out["slc_s"].append(kvs[1].reshape(shs))
            out["win_p"].append(kv[2].reshape(shp)[:, seq - min(WINDOW, seq):])
            win_full = jnp.concatenate([cache_win_kv[j], kvs[2].reshape(shs)], axis=1)
            out["win_s"].append(win_full[:, win_full.shape[1] - min(WINDOW, PAST_LEN + 1):])
        elif m == 1:
            wts = (conv_w_pw1[j].astype(BF16), conv_dw[j], conv_dw_b[j], conv_ln_g[j], conv_ln_b[j],
                   conv_w_pw2[j].astype(BF16))
            xp, hp, xs, hs, cp, cs = _conv_layer(hp, hs, xp, xs, g_post, g_ffn, wts, state_conv[j], bsz, seq)
            out["conv_p"].append(cp); out["conv_s"].append(cs)
        else:
            wts = (ssm_a_re[j], ssm_a_im[j], ssm_log_dt[j], ssm_b_re[j], ssm_b_im[j], ssm_c_re[j], ssm_c_im[j],
                   ssm_d[j], ssm_w_glu[j].astype(BF16))
            xp, hp, xs, hs, rp, ip, rs, is_ = _s5_layer(xp, xs, g_all[i, 0], g_post, g_ffn, wts,
                                                        state_ssm_re[j], state_ssm_im[j], bsz, seq)
            out["re_p"].append(rp); out["im_p"].append(ip); out["re_s"].append(rs); out["im_s"].append(is_)
        wts = (ffn_w_up, i, ffn_dw[i], ffn_dw_b[i], ffn_w_down[i].astype(BF16))
        xp, hp, xs, hs, fp, fs = _ffn_layer(hp, hs, xp, xs, g_ffn_post, g_next, wts, state_ffn_conv[i], bsz, seq)
        out["ffn_p"].append(fp); out["ffn_s"].append(fs)
    st = lambda k: jnp.stack(out[k])
    return (xp.reshape(bsz, seq, d), xs.reshape(dec, 1, d),
            st("cmp_p"), st("cmp_s"), st("slc_p"), st("slc_s"), st("win_p"), st("win_s"),
            st("conv_p"), st("conv_s"), st("re_p"), st("re_s"), st("im_p"), st("im_s"),
            st("ffn_p"), st("ffn_s"))
```

```python
import functools
import math

import numpy as np
import jax
import jax.numpy as jnp
from jax import lax
from jax.experimental import pallas as pl
from jax.experimental.pallas import tpu as pltpu

F32 = jnp.float32
BF16 = jnp.bfloat16

D_MODEL = 2048
DEPTH = 4
PAST_LEN = 16384
PAGE_SIZE = 128
N_HEADS = 16
HEAD_DIM = 128
KV_GROUPS = 4
GROUP_SIZE = 4
KV_COLS = KV_GROUPS * 2 * HEAD_DIM
CMP_BLOCK = 32
CMP_STRIDE = 16
SEL_BLOCK = 64
SEL_RATIO = SEL_BLOCK // CMP_STRIDE
N_SELECT = 16
WINDOW = 512
FORCE_BONUS = 1.0e4
N_BUCKETS = 32
MAX_EXACT = 16
REL_MAX_DIST = 128
CONV_WIDTH = 31
SSM_GROUP_CH = 16
SSM_GROUPS = 128
SSM_STATE = 64
SSM_DIM = SSM_GROUPS * SSM_STATE
D_FF = 5632
RMS_EPS = 1e-6
LN_EPS = 1e-5
SCALE = HEAD_DIM ** -0.5
NEG = -1e30

LANES = 128
SUBLANES = 8
VMEM_LIMIT = 56 * 1024 * 1024
ATT_TILE = 128
N_SLAB = SSM_DIM // LANES
SLAB_GRP = N_SLAB // SUBLANES

HIGHEST = lax.Precision.HIGHEST
NT_DIMS = (((1,), (1,)), ((), ()))


def _params(*sem):
    return pltpu.CompilerParams(dimension_semantics=sem, vmem_limit_bytes=VMEM_LIMIT)


def _rms(x, g):
    return x * lax.rsqrt(jnp.mean(x * x, axis=-1, keepdims=True) + RMS_EPS) * g


def _rmsnorm_kernel(x_ref, g_ref, o_ref):
    o_ref[...] = _rms(x_ref[...], g_ref[...]).astype(o_ref.dtype)


def rmsnorm_cast(x, g, tm):
    r, d = x.shape
    return pl.pallas_call(
        _rmsnorm_kernel,
        grid=(r // tm,),
        in_specs=[pl.BlockSpec((tm, d), lambda m: (m, 0)), pl.BlockSpec((1, d), lambda m: (0, 0))],
        out_specs=pl.BlockSpec((tm, d), lambda m: (m, 0)),
        out_shape=jax.ShapeDtypeStruct((r, d), BF16),
        compiler_params=_params("parallel"),
        name="rmsnorm",
    )(x, g.reshape(1, d))


def _mm_kernel(a_ref, w_ref, o_ref, *, act):
    r = jnp.dot(a_ref[...], w_ref[...], preferred_element_type=F32)
    if act == "sigmoid":
        r = jax.nn.sigmoid(r)
    o_ref[...] = r.astype(o_ref.dtype)


def matmul(a, w, *, n_split, out_dtype, act, tm, tn, name):
    r, k = a.shape
    n = w.shape[1]
    per = (n // n_split) // tn
    return pl.pallas_call(
        functools.partial(_mm_kernel, act=act),
        grid=(r // tm, n // tn),
        in_specs=[pl.BlockSpec((tm, k), lambda m, j: (m, 0)), pl.BlockSpec((k, tn), lambda m, j: (0, j))],
        out_specs=pl.BlockSpec((None, tm, tn), lambda m, j: (j // per, m, j % per)),
        out_shape=jax.ShapeDtypeStruct((n_split, r, n // n_split), out_dtype),
        compiler_params=_params("parallel", "arbitrary"),
        name=name,
    )(a, w)


KV_TN = 512


def _kv_project_kernel(a_ref, w_ref, prev_cmp_ref, prev_slc_ref, kv_ref, cmp_ref, slc_ref):
    del prev_cmp_ref, prev_slc_ref
    n = pl.program_id(1)
    res = jnp.dot(a_ref[...], w_ref[...], preferred_element_type=F32)
    kv_ref[...] = res
    tm = res.shape[0]
    slabs = KV_TN // HEAD_DIM
    first = (n % 2) * slabs

    def scatter(out_ref):
        for sl in range(slabs):
            out_ref[pl.ds(first + sl, tm, stride=N_KV_SLABS), :] = res[:, sl * HEAD_DIM:(sl + 1) * HEAD_DIM]

    @pl.when(n < 2)
    def _():
        scatter(cmp_ref)

    @pl.when((n >= 2) & (n < 4))
    def _():
        scatter(slc_ref)


def kv_project(a, w_kv, prev_cmp, prev_slc, layer, *, tm):
    r, k = a.shape
    n_tiles = w_kv.shape[1] // KV_TN
    per = KV_COLS // KV_TN
    nat_spec = pl.BlockSpec((None, tm * N_KV_SLABS, HEAD_DIM), lambda m, j: (layer, m, 0))
    any_spec = pl.BlockSpec(memory_space=pl.ANY)
    return pl.pallas_call(
        _kv_project_kernel,
        grid=(r // tm, n_tiles),
        in_specs=[pl.BlockSpec((tm, k), lambda m, j: (m, 0)), pl.BlockSpec((k, KV_TN), lambda m, j: (0, j)),
                  any_spec, any_spec],
        out_specs=[pl.BlockSpec((None, tm, KV_TN), lambda m, j: (j // per, m, j % per)), nat_spec, nat_spec],
        out_shape=[jax.ShapeDtypeStruct((3, r, KV_COLS), F32),
                   jax.ShapeDtypeStruct(prev_cmp.shape, F32), jax.ShapeDtypeStruct(prev_slc.shape, F32)],
        input_output_aliases={2: 1, 3: 2},
        compiler_params=_params("parallel", "arbitrary"),
        name="nsa_kv",
    )(a, w_kv, prev_cmp, prev_slc)


def _value_t_kernel(a_ref, w_ref, o_ref):
    o_ref[...] = jnp.dot(a_ref[...], w_ref[...], preferred_element_type=F32).T.astype(o_ref.dtype)


def project_value_t(a, w_kv, branch, *, tm, name):
    r, k = a.shape
    first = branch * N_KV_SLABS + 1
    return pl.pallas_call(
        _value_t_kernel,
        grid=(r // tm, KV_GROUPS),
        in_specs=[pl.BlockSpec((tm, k), lambda m, g: (m, 0)),
                  pl.BlockSpec((k, HEAD_DIM), lambda m, g: (0, first + 2 * g))],
        out_specs=pl.BlockSpec((HEAD_DIM, tm), lambda m, g: (g, m)),
        out_shape=jax.ShapeDtypeStruct((KV_GROUPS * HEAD_DIM, r), BF16),
        compiler_params=_params("parallel", "arbitrary"),
        name=name,
    )(a, w_kv)


def _glu_mm_kernel(a_ref, wa_ref, wb_ref, o_ref):
    a = a_ref[...]
    lin = jnp.dot(a, wa_ref[...], preferred_element_type=F32)
    gate = jnp.dot(a, wb_ref[...], preferred_element_type=F32)
    o_ref[...] = lin * jax.nn.sigmoid(gate)


def glu_matmul(a, w, *, tm, tn, name):
    r, k = a.shape
    n = w.shape[1] // 2
    nb = n // tn
    return pl.pallas_call(
        _glu_mm_kernel,
        grid=(r // tm, nb),
        in_specs=[pl.BlockSpec((tm, k), lambda m, j: (m, 0)),
                  pl.BlockSpec((k, tn), lambda m, j: (0, j)),
                  pl.BlockSpec((k, tn), lambda m, j: (0, j + nb))],
        out_specs=pl.BlockSpec((tm, tn), lambda m, j: (m, j)),
        out_shape=jax.ShapeDtypeStruct((r, n), F32),
        compiler_params=_params("parallel", "arbitrary"),
        name=name,
    )(a, w, w)


def _proj_res_kernel(a_ref, w_ref, x_ref, gp_ref, gn_ref, xo_ref, ho_ref, *acc, glu, nk):
    def finish(y):
        if glu:
            d = y.shape[1] // 2
            y = y[:, :d] * jax.nn.sigmoid(y[:, d:])
        xn = x_ref[...] + _rms(y, gp_ref[...])
        xo_ref[...] = xn
        ho_ref[...] = _rms(xn, gn_ref[...]).astype(ho_ref.dtype)

    part = jnp.dot(a_ref[...], w_ref[...], preferred_element_type=F32)
    if nk == 1:
        finish(part)
        return
    acc_ref, = acc
    k = pl.program_id(1)

    @pl.when(k == 0)
    def _():
        acc_ref[...] = part

    @pl.when((k > 0) & (k < nk - 1))
    def _():
        acc_ref[...] += part

    @pl.when(k == nk - 1)
    def _():
        finish(acc_ref[...] + part)


def proj_res(a, w, x, g_post, g_next, *, glu, tm, tk, name):
    r, kdim = a.shape
    n = w.shape[1]
    d = x.shape[1]
    nk = kdim // tk
    return pl.pallas_call(
        functools.partial(_proj_res_kernel, glu=glu, nk=nk),
        grid=(r // tm, nk),
        in_specs=[pl.BlockSpec((tm, tk), lambda m, k: (m, k)),
                  pl.BlockSpec((tk, n), lambda m, k: (k, 0)),
                  pl.BlockSpec((tm, d), lambda m, k: (m, 0)),
                  pl.BlockSpec((1, d), lambda m, k: (0, 0)),
                  pl.BlockSpec((1, d), lambda m, k: (0, 0))],
        out_specs=[pl.BlockSpec((tm, d), lambda m, k: (m, 0)),
                   pl.BlockSpec((tm, d), lambda m, k: (m, 0))],
        out_shape=[jax.ShapeDtypeStruct((r, d), F32), jax.ShapeDtypeStruct((r, d), BF16)],
        scratch_shapes=[pltpu.VMEM((tm, n), F32)] if nk > 1 else [],
        compiler_params=_params("parallel", "arbitrary"),
        name=name,
    )(a, w, x, g_post.reshape(1, d), g_next.reshape(1, d))


def _ffn_up_kernel(a_ref, p1_ref, p2_ref, wg_ref, wv_ref, dw_ref, db_ref, act_ref, hist_ref, *, decode, tiles_per_seq):
    a = a_ref[...]
    wg = wg_ref[...].astype(BF16)
    gate = jnp.dot(a, wg, preferred_element_type=F32)
    val = jnp.dot(a, wv_ref[...].astype(BF16), preferred_element_type=F32)
    tm = gate.shape[0]
    if decode:
        g2, g1 = p2_ref[...], p1_ref[...]
        hist_ref[...] = gate
    else:
        halo = jnp.dot(p1_ref[...], wg, preferred_element_type=F32)
        halo = jnp.where(pl.program_id(0) % tiles_per_seq == 0, 0.0, halo)
        h7, h6 = halo[7:8, :], halo[6:7, :]
        row = lax.broadcasted_iota(jnp.int32, gate.shape, 0)
        g1 = jnp.where(row == 0, h7, pltpu.roll(gate, 1, 0))
        g2 = jnp.where(row == 0, h6, jnp.where(row == 1, h7, pltpu.roll(gate, 2, 0)))
        hist_ref[...] = gate[tm - SUBLANES:, :]
    g = dw_ref[0:1, :] * g2 + dw_ref[1:2, :] * g1 + dw_ref[2:3, :] * gate + db_ref[...]
    act_ref[...] = (jax.nn.gelu(g) * val).astype(act_ref.dtype)


def ffn_up(h, w_up, layer, dw, db, *, hist=None, seq_len, tm, tn):
    r, k = h.shape
    nb = D_FF // tn
    decode = hist is not None
    if decode:
        p1, p2 = hist[:, 1, :], hist[:, 0, :]
        p_specs = [pl.BlockSpec((tm, tn), lambda m, j: (m, j)), pl.BlockSpec((tm, tn), lambda m, j: (m, j))]
        hrows = tm
    else:
        p1 = p2 = h
        blk = tm // SUBLANES
        p_specs = [pl.BlockSpec((SUBLANES, k), lambda m, j: (jnp.maximum(m * blk - 1, 0), 0)),
                   pl.BlockSpec((SUBLANES, k), lambda m, j: (0, 0))]
        hrows = SUBLANES
    return pl.pallas_call(
        functools.partial(_ffn_up_kernel, decode=decode, tiles_per_seq=max(seq_len // tm, 1)),
        grid=(r // tm, nb),
        in_specs=[pl.BlockSpec((tm, k), lambda m, j: (m, 0))] + p_specs + [
            pl.BlockSpec((None, k, tn), lambda m, j: (layer, 0, j)),
            pl.BlockSpec((None, k, tn), lambda m, j: (layer, 0, j + nb)),
            pl.BlockSpec((3, tn), lambda m, j: (0, j)),
            pl.BlockSpec((1, tn), lambda m, j: (0, j))],
        out_specs=[pl.BlockSpec((tm, tn), lambda m, j: (m, j)),
                   pl.BlockSpec((None, hrows, tn), lambda m, j: (m, 0, j))],
        out_shape=[jax.ShapeDtypeStruct((r, D_FF), BF16),
                   jax.ShapeDtypeStruct((r // tm, hrows, D_FF), F32)],
        compiler_params=_params("parallel", "arbitrary"),
        name="ffn_up",
    )(h, p1, p2, w_up, w_up, dw, db.reshape(1, D_FF))


N_KV_SLABS = KV_GROUPS * 2


def _cmp_partial_slab(x_ref, c, pe_ref, w1_ref, n_chunks, first=0, pitch=1, chunk_pitch=None):
    chunk_pitch = CMP_STRIDE * pitch if chunk_pitch is None else chunk_pitch
    acc_a = jnp.zeros((n_chunks, HEAD_DIM), F32)
    acc_b = jnp.zeros((n_chunks, HEAD_DIM), F32)
    for s in range(CMP_STRIDE):
        xs = x_ref[pl.ds(first + s * pitch, n_chunks, stride=chunk_pitch), :]
        xa = (xs + pe_ref[c, s:s + 1, :]).astype(BF16)
        xb = (xs + pe_ref[c, CMP_STRIDE + s:CMP_STRIDE + s + 1, :]).astype(BF16)
        acc_a += jnp.dot(xa, w1_ref[c, s], preferred_element_type=F32)
        acc_b += jnp.dot(xb, w1_ref[c, CMP_STRIDE + s], preferred_element_type=F32)
    return acc_a, acc_b


def _cmp1_kernel(x_ref, pe_ref, w1_ref, a_ref, b_ref, *, n_chunks):
    c = pl.program_id(1) % 2
    a_ref[...], b_ref[...] = _cmp_partial_slab(x_ref, c, pe_ref, w1_ref, n_chunks)


def cmp_partial_rows(x, pe, w1, rows):
    nb = x.shape[0] // rows
    n_chunks = rows // CMP_STRIDE
    out = jax.ShapeDtypeStruct((nb, n_chunks, KV_COLS), F32)
    return pl.pallas_call(
        functools.partial(_cmp1_kernel, n_chunks=n_chunks),
        grid=(nb, N_KV_SLABS),
        in_specs=[pl.BlockSpec((rows, HEAD_DIM), lambda b, sl: (b, sl)),
                  pl.BlockSpec(pe.shape, lambda b, sl: (0, 0, 0)),
                  pl.BlockSpec(w1.shape, lambda b, sl: (0, 0, 0, 0))],
        out_specs=[pl.BlockSpec((None, n_chunks, HEAD_DIM), lambda b, sl: (b, 0, sl))] * 2,
        out_shape=[out, out],
        compiler_params=_params("parallel", "arbitrary"),
        name="cmp_partial",
    )(x, pe, w1)


PAGES_PER_STEP = 16
CHUNKS_PER_PAGE = PAGE_SIZE // CMP_STRIDE
CHUNK_ROWS = CMP_STRIDE * N_KV_SLABS
CHUNK_PITCH = CHUNK_ROWS + 4
STEP_CHUNKS = PAGES_PER_STEP * CHUNKS_PER_PAGE


def _cmp1_paged_kernel(pt_ref, *refs):
    page_refs = refs[:PAGES_PER_STEP]
    pe_ref, w1_ref, a_ref, b_ref, buf_ref = refs[PAGES_PER_STEP:]
    for i, p_ref in enumerate(page_refs):
        for ch in range(CHUNKS_PER_PAGE):
            dst = (i * CHUNKS_PER_PAGE + ch) * CHUNK_PITCH
            buf_ref[dst:dst + CHUNK_ROWS, :] = p_ref[ch * CHUNK_ROWS:(ch + 1) * CHUNK_ROWS, :]
    for slab in range(N_KV_SLABS):
        cols = slice(slab * HEAD_DIM, (slab + 1) * HEAD_DIM)
        a_ref[:, cols], b_ref[:, cols] = _cmp_partial_slab(buf_ref, slab % 2, pe_ref, w1_ref, STEP_CHUNKS,
                                                           first=slab, pitch=N_KV_SLABS, chunk_pitch=CHUNK_PITCH)


def cmp_partial_paged(pool, n_phys, page_table, layer, pe, w1):
    bsz, n_pages = page_table.shape
    n_steps = n_pages // PAGES_PER_STEP
    n_chunks = STEP_CHUNKS
    out = jax.ShapeDtypeStruct((bsz, n_steps, n_chunks, KV_COLS), F32)

    def page_spec(i):
        return pl.BlockSpec((None, PAGE_SIZE * N_KV_SLABS, HEAD_DIM),
                            lambda b, s, pt: (layer * n_phys + pt[b, s * PAGES_PER_STEP + i], 0, 0))

    out_spec = pl.BlockSpec((None, None, n_chunks, KV_COLS), lambda b, s, pt: (b, s, 0, 0))
    a, b = pl.pallas_call(
        _cmp1_paged_kernel,
        grid_spec=pltpu.PrefetchScalarGridSpec(
            num_scalar_prefetch=1,
            grid=(bsz, n_steps),
            in_specs=[page_spec(i) for i in range(PAGES_PER_STEP)] + [
                pl.BlockSpec(pe.shape, lambda b, s, pt: (0, 0, 0)),
                pl.BlockSpec(w1.shape, lambda b, s, pt: (0, 0, 0, 0))],
            out_specs=[out_spec, out_spec],
            scratch_shapes=[pltpu.VMEM((STEP_CHUNKS * CHUNK_PITCH, HEAD_DIM), F32)]),
        out_shape=[out, out],
        compiler_params=_params("parallel", "arbitrary"),
        name="cmp_partial_paged",
    )(page_table, *([pool] * PAGES_PER_STEP), pe, w1)
    return a.reshape(bsz, -1, KV_COLS), b.reshape(bsz, -1, KV_COLS)


def _cmp2_kernel(a_ref, b_ref, bx_ref, w2_ref, o_ref):
    n = a_ref.shape[0]
    row = lax.broadcasted_iota(jnp.int32, (n, HEAD_DIM), 0)
    for slab in range(KV_GROUPS * 2):
        c = slab % 2
        cols = slice(slab * HEAD_DIM, (slab + 1) * HEAD_DIM)
        nxt = pltpu.roll(b_ref[:, cols], n - 1, 0)
        nxt = jnp.where(row == n - 1, bx_ref[0:1, cols], nxt)
        h = jax.nn.gelu(a_ref[:, cols] + nxt).astype(BF16)
        o_ref[:, cols] = jnp.dot(h, w2_ref[c], preferred_element_type=F32).astype(o_ref.dtype)


def cmp_finish(a, b, b_next, w2):
    nb, n, _ = a.shape
    return pl.pallas_call(
        _cmp2_kernel,
        grid=(nb,),
        in_specs=[pl.BlockSpec((None, n, KV_COLS), lambda i: (i, 0, 0)),
                  pl.BlockSpec((None, n, KV_COLS), lambda i: (i, 0, 0)),
                  pl.BlockSpec((None, SUBLANES, KV_COLS), lambda i: (i, 0, 0)),
                  pl.BlockSpec(w2.shape, lambda i: (0, 0, 0))],
        out_specs=pl.BlockSpec((None, n, KV_COLS), lambda i: (i, 0, 0)),
        out_shape=jax.ShapeDtypeStruct((nb, n, KV_COLS), BF16),
        compiler_params=_params("parallel"),
        name="cmp_finish",
    )(a, b, b_next, w2)


def _softmax_rows(s, mask):
    s = jnp.where(mask, s, NEG)
    m = jnp.max(s, axis=-1, keepdims=True)
    e = jnp.where(mask, jnp.exp(s - m), 0.0)
    l = jnp.sum(e, axis=-1, keepdims=True)
    return jnp.where(l > 0.0, e / jnp.where(l > 0.0, l, 1.0), 0.0)


def _cmp_attn_kernel(q_ref, kc_ref, bias_ref, msel_ref, exp_ref, o_ref, mask_ref, *, n_sel):
    qt = pl.program_id(1)
    tq, n_cmp = q_ref.shape[0], kc_ref.shape[0]
    qpos = qt * tq + lax.broadcasted_iota(jnp.int32, (tq, n_cmp), 0)
    cpos = lax.broadcasted_iota(jnp.int32, (tq, n_cmp), 1) * CMP_STRIDE + (CMP_BLOCK - 1)
    cadd = jnp.where(cpos <= qpos, 0.0, NEG)
    j = lax.broadcasted_iota(jnp.int32, (n_sel, tq), 0)
    cur = (qt * tq + lax.broadcasted_iota(jnp.int32, (n_sel, tq), 1)) // SEL_BLOCK
    valid = j <= cur
    forced = (j == 0) | (j == cur) | (j == cur - 1)
    pad_rows = jnp.zeros((exp_ref.shape[1] - n_sel, tq), F32)
    diag = pl.ds(pl.multiple_of(qt * tq, tq), tq)
    causal = lax.broadcasted_iota(jnp.int32, (tq, tq), 0) <= lax.broadcasted_iota(jnp.int32, (tq, tq), 1)
    for g in range(KV_GROUPS):
        k = kc_ref[:, g * 2 * HEAD_DIM:(g * 2 + 1) * HEAD_DIM]
        v = kc_ref[:, (g * 2 + 1) * HEAD_DIM:(g * 2 + 2) * HEAD_DIM]
        heads = range(g * GROUP_SIZE, (g + 1) * GROUP_SIZE)
        q4 = jnp.concatenate([q_ref[:, h * HEAD_DIM:(h + 1) * HEAD_DIM] for h in heads], axis=0)
        badd = jnp.concatenate([bias_ref[h] + cadd for h in heads], axis=0)
        s = lax.dot_general(q4, k, NT_DIMS, preferred_element_type=F32) * SCALE + badd
        p = _softmax_rows(s, s > 0.5 * NEG)
        o4 = jnp.dot(p.astype(BF16), v, preferred_element_type=F32)
        imp = jnp.zeros((tq, n_cmp), F32)
        for r, h in enumerate(heads):
            o_ref[:, h * HEAD_DIM:(h + 1) * HEAD_DIM] = o4[r * tq:(r + 1) * tq, :]
            imp = imp + p[r * tq:(r + 1) * tq, :]
        p_sel = lax.dot_general(msel_ref[...], imp, NT_DIMS, precision=HIGHEST, preferred_element_type=F32)
        score = jnp.where(valid, p_sel + jnp.where(forced, FORCE_BONUS, 0.0), -jnp.inf)
        rank = jnp.zeros((n_sel, tq), jnp.int32)
        for i in range(n_sel):
            si = score[i:i + 1, :]
            beats = (si > score) | ((si == score) & (i < j))
            rank = rank + beats.astype(jnp.int32)
        sel_t = jnp.where((rank < N_SELECT) & valid, 1.0, 0.0)
        sel = jnp.concatenate([sel_t, pad_rows], axis=0).astype(BF16)
        allowed = jnp.dot(exp_ref[...], sel, preferred_element_type=F32)
        mask_ref[g] = ((allowed - 1.0) * -NEG).astype(mask_ref.dtype)
        allowed_d = jnp.dot(exp_ref[diag, :], sel, preferred_element_type=F32)
        mask_ref[g, diag, :] = jnp.where(causal, (allowed_d - 1.0) * -NEG, NEG).astype(mask_ref.dtype)


def cmp_attention(q, kc, bias_cmp, msel, expand, bsz, seq):
    tq = ATT_TILE
    nqt = seq // tq
    n_cmp = kc.shape[1]
    n_sel = msel.shape[0]
    return pl.pallas_call(
        functools.partial(_cmp_attn_kernel, n_sel=n_sel),
        grid=(bsz, nqt),
        in_specs=[pl.BlockSpec((tq, D_MODEL), lambda b, t: (b * nqt + t, 0)),
                  pl.BlockSpec((None, n_cmp, KV_COLS), lambda b, t: (b, 0, 0)),
                  pl.BlockSpec((N_HEADS, tq, n_cmp), lambda b, t: (0, t, 0)),
                  pl.BlockSpec(msel.shape, lambda b, t: (0, 0)),
                  pl.BlockSpec(expand.shape, lambda b, t: (0, 0))],
        out_specs=[pl.BlockSpec((tq, D_MODEL), lambda b, t: (b * nqt + t, 0)),
                   pl.BlockSpec((None, KV_GROUPS, seq, tq), lambda b, t: (b, 0, 0, t))],
        out_shape=[jax.ShapeDtypeStruct((bsz * seq, D_MODEL), F32),
                   jax.ShapeDtypeStruct((bsz, KV_GROUPS, seq, seq), BF16)],
        compiler_params=_params("parallel", "arbitrary"),
        name="cmp_attention",
    )(q, kc, bias_cmp, msel, expand)


SLC_KEYS = 512
SLC_SUB = SLC_KEYS // ATT_TILE
SLC_QUERIES = 256


def _slc_kernel(qt_ref, ks_ref, q_ref, k_ref, vt_ref, tbt_ref, mask_ref, o_ref, m_ref, l_ref, acc_ref):
    g, pair = pl.program_id(1), pl.program_id(2)
    qt, ks = qt_ref[pair], ks_ref[pair]
    tq = q_ref.shape[0]

    @pl.when(ks == 0)
    def _():
        m_ref[...] = jnp.full_like(m_ref, NEG)
        l_ref[...] = jnp.zeros_like(l_ref)
        acc_ref[...] = jnp.zeros_like(acc_ref)

    q4 = jnp.concatenate([q_ref[:, r * HEAD_DIM:(r + 1) * HEAD_DIM] for r in range(GROUP_SIZE)], axis=0)
    rows = []
    q_sub = tq // ATT_TILE
    for c in range(SLC_SUB):
        tbi = [jnp.clip(qt * q_sub + qs - (ks * SLC_SUB + c), 0, 2) for qs in range(q_sub)]
        madd = [mask_ref[c * ATT_TILE:(c + 1) * ATT_TILE, qs * ATT_TILE:(qs + 1) * ATT_TILE].astype(F32)
                for qs in range(q_sub)]
        rows.append(jnp.concatenate([tbt_ref[tbi[qs], g * GROUP_SIZE + r] + madd[qs]
                                     for r in range(GROUP_SIZE) for qs in range(q_sub)], axis=1))
    st = (lax.dot_general(k_ref[...].astype(BF16), q4, NT_DIMS, preferred_element_type=F32) * SCALE
          + jnp.concatenate(rows, axis=0))
    m_prev = m_ref[...]
    m_new = jnp.maximum(m_prev, jnp.max(st, axis=0, keepdims=True))
    alpha = jnp.exp(m_prev - m_new)
    pt = jnp.exp(st - m_new)
    l_ref[...] = alpha * l_ref[...] + jnp.sum(pt, axis=0, keepdims=True)
    acc_ref[...] = alpha * acc_ref[...] + jnp.dot(vt_ref[...], pt.astype(BF16), preferred_element_type=F32)
    m_ref[...] = m_new

    @pl.when(ks == ((qt + 1) * tq - 1) // SLC_KEYS)
    def _():
        ot = acc_ref[...] / l_ref[...]
        for r in range(GROUP_SIZE):
            o_ref[:, r * HEAD_DIM:(r + 1) * HEAD_DIM] = ot[:, r * tq:(r + 1) * tq].T


def slc_attention(q, kv, vt, mask, tbt, bsz, seq):
    tq = SLC_QUERIES
    nqt = seq // tq
    nks = seq // SLC_KEYS
    gw = GROUP_SIZE * HEAD_DIM
    pairs = [(t, s) for t in range(nqt) for s in range(((t + 1) * tq - 1) // SLC_KEYS + 1)]
    qt_of = jnp.asarray([p[0] for p in pairs], jnp.int32)
    ks_of = jnp.asarray([p[1] for p in pairs], jnp.int32)
    return pl.pallas_call(
        _slc_kernel,
        grid_spec=pltpu.PrefetchScalarGridSpec(
            num_scalar_prefetch=2,
            grid=(bsz, KV_GROUPS, len(pairs)),
            in_specs=[pl.BlockSpec((tq, gw), lambda b, g, p, qt, ks: (b * nqt + qt[p], g)),
                      pl.BlockSpec((SLC_KEYS, HEAD_DIM), lambda b, g, p, qt, ks: (b * nks + ks[p], 2 * g)),
                      pl.BlockSpec((HEAD_DIM, SLC_KEYS), lambda b, g, p, qt, ks: (g, b * nks + ks[p])),
                      pl.BlockSpec(tbt.shape, lambda b, g, p, qt, ks: (0, 0, 0, 0)),
                      pl.BlockSpec((None, None, SLC_KEYS, tq), lambda b, g, p, qt, ks: (b, g, ks[p], qt[p]))],
            out_specs=pl.BlockSpec((tq, gw), lambda b, g, p, qt, ks: (b * nqt + qt[p], g)),
            scratch_shapes=[pltpu.VMEM((1, GROUP_SIZE * tq), F32), pltpu.VMEM((1, GROUP_SIZE * tq), F32),
                            pltpu.VMEM((HEAD_DIM, GROUP_SIZE * tq), F32)]),
        out_shape=jax.ShapeDtypeStruct((bsz * seq, D_MODEL), F32),
        compiler_params=_params("parallel", "parallel", "arbitrary"),
        name="slc_attention",
    )(qt_of, ks_of, q, kv, vt, tbt, mask)


WIN_TILES = WINDOW // ATT_TILE + 1
WIN_SPAN = WIN_TILES * ATT_TILE


def _win_kernel(q_ref, *refs):
    kv_refs, (bias_ref, oc_ref, os_ref, gate_ref, o_ref) = refs[:WIN_TILES], refs[WIN_TILES:]
    gate = gate_ref[...]
    qt = pl.program_id(1)
    tq = q_ref.shape[0]
    row = lax.broadcasted_iota(jnp.int32, (tq, WIN_SPAN), 0)
    col = lax.broadcasted_iota(jnp.int32, (tq, WIN_SPAN), 1)
    back = row + WINDOW - col
    mask = (back >= 0) & (back <= WINDOW) & (qt * tq - WINDOW + col >= 0)
    madd = jnp.where(mask, 0.0, NEG)
    for g in range(KV_GROUPS):
        k = jnp.concatenate([r[:, g * 2 * HEAD_DIM:(g * 2 + 1) * HEAD_DIM].astype(BF16) for r in kv_refs], axis=0)
        v = jnp.concatenate([r[:, (g * 2 + 1) * HEAD_DIM:(g * 2 + 2) * HEAD_DIM].astype(BF16) for r in kv_refs], axis=0)
        heads = range(g * GROUP_SIZE, (g + 1) * GROUP_SIZE)
        q4 = jnp.concatenate([q_ref[:, h * HEAD_DIM:(h + 1) * HEAD_DIM] for h in heads], axis=0)
        badd = jnp.concatenate([bias_ref[h] + madd for h in heads], axis=0)
        s = lax.dot_general(q4, k, NT_DIMS, preferred_element_type=F32) * SCALE + badd
        e = jnp.exp(s - jnp.max(s, axis=-1, keepdims=True))
        l = jnp.sum(e, axis=-1, keepdims=True)
        o = jnp.dot(e.astype(BF16), v, preferred_element_type=F32) / l
        for r, h in enumerate(heads):
            cols = slice(h * HEAD_DIM, (h + 1) * HEAD_DIM)
            merged = (oc_ref[:, cols] * gate[:, 3 * h:3 * h + 1] + os_ref[:, cols] * gate[:, 3 * h + 1:3 * h + 2]
                      + o[r * tq:(r + 1) * tq, :] * gate[:, 3 * h + 2:3 * h + 3])
            o_ref[:, cols] = merged.astype(o_ref.dtype)


def win_attention(q, kv, bias_win, o_cmp, o_slc, gate, bsz, seq):
    tq = ATT_TILE
    nqt = seq // tq
    row_spec = pl.BlockSpec((tq, D_MODEL), lambda b, t: (b * nqt + t, 0))

    def kv_spec(i):
        return pl.BlockSpec((tq, KV_COLS), lambda b, t: (b * nqt + jnp.maximum(t - (WIN_TILES - 1) + i, 0), 0))

    return pl.pallas_call(
        _win_kernel,
        grid=(bsz, nqt),
        in_specs=[row_spec]
        + [kv_spec(i) for i in range(WIN_TILES)]
        + [pl.BlockSpec(bias_win.shape, lambda b, t: (0, 0, 0)), row_spec, row_spec,
           pl.BlockSpec((tq, LANES), lambda b, t: (b * nqt + t, 0))],
        out_specs=row_spec,
        out_shape=jax.ShapeDtypeStruct((bsz * seq, D_MODEL), BF16),
        compiler_params=_params("parallel", "arbitrary"),
        name="win_attention",
    )(q, *([kv] * WIN_TILES), bias_win, o_cmp, o_slc, gate)


def _combine_kernel(oc_ref, os_ref, ow_ref, g_ref, o_ref):
    gate = g_ref[...]
    for h in range(N_HEADS):
        cols = slice(h * HEAD_DIM, (h + 1) * HEAD_DIM)
        o = (oc_ref[:, cols] * gate[:, 3 * h:3 * h + 1] + os_ref[:, cols] * gate[:, 3 * h + 1:3 * h + 2]
             + ow_ref[:, cols] * gate[:, 3 * h + 2:3 * h + 3])
        o_ref[:, cols] = o.astype(o_ref.dtype)


def nsa_combine(o_cmp, o_slc, o_win, gate, tm):
    r = o_cmp.shape[0]
    spec = pl.BlockSpec((tm, D_MODEL), lambda m: (m, 0))
    return pl.pallas_call(
        _combine_kernel,
        grid=(r // tm,),
        in_specs=[spec, spec, spec, pl.BlockSpec((tm, LANES), lambda m: (m, 0))],
        out_specs=spec,
        out_shape=jax.ShapeDtypeStruct((r, D_MODEL), BF16),
        compiler_params=_params("parallel"),
        name="nsa_combine",
    )(o_cmp, o_slc, o_win, gate)


N_SEL_S = -(-(PAST_LEN + 1) // SEL_BLOCK)
N_SEL_S_PAD = 384
N_CMP_S = PAST_LEN // CMP_STRIDE


def _group_rows(parts, hgrp):
    out = parts[0]
    for g in range(1, KV_GROUPS):
        out = jnp.where(hgrp == g, parts[g], out)
    return out


def _sample_attn_kernel(q_ref, kc_ref, bc_ref, gsum_ref, msel_ref, win_ref, new_ref, bw_ref, bn_ref,
                        oc_ref, ow_ref, idx_ref):
    q = q_ref[...]
    hgrp = lax.broadcasted_iota(jnp.int32, (N_HEADS, 1), 0) // GROUP_SIZE

    def kcol(g):
        return slice(g * 2 * HEAD_DIM, (g * 2 + 1) * HEAD_DIM)

    def vcol(g):
        return slice((g * 2 + 1) * HEAD_DIM, (g * 2 + 2) * HEAD_DIM)

    s = _group_rows([lax.dot_general(q, kc_ref[:, kcol(g)], NT_DIMS, preferred_element_type=F32)
                     for g in range(KV_GROUPS)], hgrp)
    s = s * SCALE + bc_ref[...]
    n = lax.broadcasted_iota(jnp.int32, s.shape, 1)
    p = _softmax_rows(s, n * CMP_STRIDE + (CMP_BLOCK - 1) <= PAST_LEN)
    pb = p.astype(BF16)
    oc_ref[...] = _group_rows([jnp.dot(pb, kc_ref[:, vcol(g)], preferred_element_type=F32)
                               for g in range(KV_GROUPS)], hgrp)
    imp = jnp.dot(gsum_ref[...], p, precision=HIGHEST, preferred_element_type=F32)
    p_sel = jnp.dot(imp, msel_ref[...], precision=HIGHEST, preferred_element_type=F32)
    j = lax.broadcasted_iota(jnp.int32, p_sel.shape, 1)
    cur = PAST_LEN // SEL_BLOCK
    forced = (j == 0) | (j == cur) | (j == cur - 1)
    score = jnp.where(j <= cur, p_sel + jnp.where(forced, FORCE_BONUS, 0.0), -jnp.inf)
    lane = lax.broadcasted_iota(jnp.int32, idx_ref.shape, 1)
    idx = jnp.zeros(idx_ref.shape, F32)
    jf = j.astype(F32)
    for kk in range(N_SELECT):
        mx = jnp.max(score, axis=-1, keepdims=True)
        pick = jnp.min(jnp.where(score == mx, jf, float(N_SEL_S_PAD)), axis=-1, keepdims=True)
        idx = jnp.where(lane == kk, pick, idx)
        score = jnp.where(jf == pick, -jnp.inf, score)
    idx_ref[...] = idx.astype(jnp.int32)
    wb = win_ref.shape[0] // N_KV_SLABS

    def win_slab(slab):
        return win_ref[pl.ds(slab, wb, stride=N_KV_SLABS), :].astype(BF16)

    sw = _group_rows([lax.dot_general(q, win_slab(2 * g), NT_DIMS, preferred_element_type=F32)
                      for g in range(KV_GROUPS)], hgrp)
    sw = sw * SCALE + bw_ref[...]
    qf = q.astype(F32)
    sn = _group_rows([jnp.sum(qf * new_ref[:, kcol(g)].astype(BF16).astype(F32), axis=-1, keepdims=True)
                      for g in range(KV_GROUPS)], hgrp)
    sn = sn * SCALE + bn_ref[:, 0:1]
    m = jnp.maximum(jnp.max(sw, axis=-1, keepdims=True), sn)
    ew, en = jnp.exp(sw - m), jnp.exp(sn - m)
    l = jnp.sum(ew, axis=-1, keepdims=True) + en
    pw, pn = (ew / l).astype(BF16), (en / l).astype(BF16).astype(F32)
    ow = _group_rows([jnp.dot(pw, win_slab(2 * g + 1), preferred_element_type=F32)
                      + pn * new_ref[:, vcol(g)].astype(BF16).astype(F32) for g in range(KV_GROUPS)], hgrp)
    ow_ref[...] = ow


def sample_attention(q3, kc, bias_c, gsum, msel, win_pool, layer, kv_win_new, bias_w, bias_new):
    bsz = q3.shape[0]
    wb = win_pool.shape[1]
    o = jax.ShapeDtypeStruct((bsz, N_HEADS, HEAD_DIM), F32)
    full2 = lambda b: (0, 0)
    return pl.pallas_call(
        _sample_attn_kernel,
        grid=(bsz,),
        in_specs=[pl.BlockSpec((None, N_HEADS, HEAD_DIM), lambda b: (b, 0, 0)),
                  pl.BlockSpec((None, N_CMP_S, KV_COLS), lambda b: (b, 0, 0)),
                  pl.BlockSpec(bias_c.shape, full2),
                  pl.BlockSpec(gsum.shape, full2),
                  pl.BlockSpec(msel.shape, full2),
                  pl.BlockSpec((None, wb, HEAD_DIM), lambda b: (layer * bsz + b, 0, 0)),
                  pl.BlockSpec((None, 1, KV_COLS), lambda b: (b, 0, 0)),
                  pl.BlockSpec(bias_w.shape, full2),
                  pl.BlockSpec(bias_new.shape, full2)],
        out_specs=[pl.BlockSpec((None, N_HEADS, HEAD_DIM), lambda b: (b, 0, 0)),
                   pl.BlockSpec((None, N_HEADS, HEAD_DIM), lambda b: (b, 0, 0)),
                   pl.BlockSpec((None, SUBLANES, LANES), lambda b: (b, 0, 0))],
        out_shape=[o, o, jax.ShapeDtypeStruct((bsz, SUBLANES, LANES), jnp.int32)],
        compiler_params=_params("parallel"),
        name="sample_attention",
    )(q3, kc, bias_c, gsum, msel, win_pool, kv_win_new, bias_w, bias_new)


def _sample_slc_kernel(idx_ref, pt_ref, q_ref, *refs):
    blk_refs, (new_ref, bias_ref, o_ref) = refs[:N_SELECT], refs[N_SELECT:]
    b, g = pl.program_id(0), pl.program_id(1)
    ks, vs, biases = [], [], []
    lane = lax.broadcasted_iota(jnp.int32, (N_HEADS, SEL_BLOCK), 1)
    for kk, blk_ref in enumerate(blk_refs):
        j = idx_ref[b, g, kk]
        is_new = j >= PAST_LEN // SEL_BLOCK
        k_blk = blk_ref[pl.ds(2 * g, SEL_BLOCK, stride=N_KV_SLABS), :]
        v_blk = blk_ref[pl.ds(2 * g + 1, SEL_BLOCK, stride=N_KV_SLABS), :]
        ks.append(jnp.where(is_new, jnp.broadcast_to(new_ref[:, :HEAD_DIM], k_blk.shape), k_blk).astype(BF16))
        vs.append(jnp.where(is_new, jnp.broadcast_to(new_ref[:, HEAD_DIM:], v_blk.shape), v_blk).astype(BF16))
        biases.append(bias_ref[j] + jnp.where(j * SEL_BLOCK + lane <= PAST_LEN, 0.0, NEG))
    k, v = jnp.concatenate(ks, axis=0), jnp.concatenate(vs, axis=0)
    s = lax.dot_general(q_ref[...], k, NT_DIMS, preferred_element_type=F32) * SCALE + jnp.concatenate(biases, axis=1)
    p = _softmax_rows(s, s > 0.5 * NEG)
    o_ref[...] = jnp.dot(p.astype(BF16), v, preferred_element_type=F32)


def sample_slc_attention(idx, page_table, q3, pool, n_phys, layer, kv_slc_new, bias_blk):
    bsz = q3.shape[0]
    half_per_page = PAGE_SIZE // SEL_BLOCK
    n_half = n_phys * half_per_page
    last_past = PAST_LEN // SEL_BLOCK - 1

    def blk_spec(kk):
        def blk_map(b, g, idx_r, pt_r):
            j = jnp.minimum(idx_r[b, g, kk], last_past)
            return (layer * n_half + pt_r[b, j // half_per_page] * half_per_page + j % half_per_page, 0, 0)
        return pl.BlockSpec((None, SEL_BLOCK * N_KV_SLABS, HEAD_DIM), blk_map)

    return pl.pallas_call(
        _sample_slc_kernel,
        grid_spec=pltpu.PrefetchScalarGridSpec(
            num_scalar_prefetch=2,
            grid=(bsz, KV_GROUPS),
            in_specs=[pl.BlockSpec((None, N_HEADS, HEAD_DIM), lambda b, g, i, p: (b, 0, 0))]
            + [blk_spec(kk) for kk in range(N_SELECT)]
            + [pl.BlockSpec((None, 1, 2 * HEAD_DIM), lambda b, g, i, p: (b, 0, g)),
               pl.BlockSpec(bias_blk.shape, lambda b, g, i, p: (0, 0, 0))],
            out_specs=pl.BlockSpec((None, None, N_HEADS, HEAD_DIM), lambda b, g, i, p: (b, g, 0, 0))),
        out_shape=jax.ShapeDtypeStruct((bsz, KV_GROUPS, N_HEADS, HEAD_DIM), F32),
        compiler_params=_params("parallel", "arbitrary"),
        name="sample_slc_attention",
    )(idx, page_table, q3, *([pool] * N_SELECT), kv_slc_new, bias_blk)


CONV_TILE = 128
CONV_HALO = 32
CONV_ROWS = 64


def _dwconv_ln_kernel(u_ref, halo_ref, w_ref, b_ref, g_ref, beta_ref, o_ref, buf_ref, y_ref, sh_ref, *, tiles_per_seq):
    first = pl.program_id(0) % tiles_per_seq == 0
    buf_ref[0:CONV_HALO, :] = jnp.where(first, 0.0, halo_ref[...])
    buf_ref[CONV_HALO:, :] = u_ref[...]
    lead = CONV_HALO - (CONV_WIDTH - 1)
    for r0 in range(0, CONV_TILE, CONV_ROWS):
        for c0 in range(0, D_MODEL, LANES):
            cols = slice(c0, c0 + LANES)
            acc = jnp.broadcast_to(b_ref[:, cols], (CONV_ROWS, LANES))
            for res in range(SUBLANES):
                taps = [k for k in range(CONV_WIDTH) if (lead + k) % SUBLANES == res]
                span = max(lead + k - res for k in taps) + CONV_ROWS
                sh_ref[0:span, :] = buf_ref[r0 + res:r0 + res + span, cols]
                for k in taps:
                    off = lead + k - res
                    acc = acc + w_ref[k:k + 1, cols] * sh_ref[off:off + CONV_ROWS, :]
            y_ref[r0:r0 + CONV_ROWS, cols] = acc
    y = y_ref[...]
    mu = jnp.mean(y, axis=-1, keepdims=True)
    var = jnp.mean(jnp.square(y - mu), axis=-1, keepdims=True)
    y = (y - mu) * lax.rsqrt(var + LN_EPS) * g_ref[...] + beta_ref[...]
    o_ref[...] = (y * jax.nn.sigmoid(y)).astype(o_ref.dtype)


def dwconv_ln(u, w, b, ln_g, ln_b, seq):
    r, d = u.shape
    tiles_per_seq = seq // CONV_TILE
    ratio = CONV_TILE // CONV_HALO
    vec = lambda m: (0, 0)
    return pl.pallas_call(
        functools.partial(_dwconv_ln_kernel, tiles_per_seq=tiles_per_seq),
        grid=(r // CONV_TILE,),
        in_specs=[pl.BlockSpec((CONV_TILE, d), lambda m: (m, 0)),
                  pl.BlockSpec((CONV_HALO, d), lambda m: (jnp.maximum(m * ratio - 1, 0), 0)),
                  pl.BlockSpec((CONV_WIDTH, d), vec), pl.BlockSpec((1, d), vec),
                  pl.BlockSpec((1, d), vec), pl.BlockSpec((1, d), vec)],
        out_specs=pl.BlockSpec((CONV_TILE, d), lambda m: (m, 0)),
        out_shape=jax.ShapeDtypeStruct((r, d), BF16),
        scratch_shapes=[pltpu.VMEM((CONV_HALO + CONV_TILE, d), F32), pltpu.VMEM((CONV_TILE, d), F32),
                        pltpu.VMEM((CONV_HALO + CONV_ROWS, LANES), F32)],
        compiler_params=_params("parallel"),
        name="dwconv_ln",
    )(u, u, w, b.reshape(1, d), ln_g.reshape(1, d), ln_b.reshape(1, d))


def _dwconv_ln_decode_kernel(u_ref, hist_ref, w_ref, b_ref, g_ref, beta_ref, o_ref):
    y = b_ref[...] + w_ref[CONV_WIDTH - 1:CONV_WIDTH, :] * u_ref[...]
    for k in range(CONV_WIDTH - 1):
        y = y + w_ref[k:k + 1, :] * hist_ref[k]
    mu = jnp.mean(y, axis=-1, keepdims=True)
    var = jnp.mean(jnp.square(y - mu), axis=-1, keepdims=True)
    y = (y - mu) * lax.rsqrt(var + LN_EPS) * g_ref[...] + beta_ref[...]
    o_ref[...] = (y * jax.nn.sigmoid(y)).astype(o_ref.dtype)


def dwconv_ln_decode(u, hist_t, w, b, ln_g, ln_b):
    r, d = u.shape
    vec = lambda i: (0, 0)
    return pl.pallas_call(
        _dwconv_ln_decode_kernel,
        grid=(1,),
        in_specs=[pl.BlockSpec((r, d), vec), pl.BlockSpec(hist_t.shape, lambda i: (0, 0, 0)),
                  pl.BlockSpec((CONV_WIDTH, d), vec), pl.BlockSpec((1, d), vec),
                  pl.BlockSpec((1, d), vec), pl.BlockSpec((1, d), vec)],
        out_specs=pl.BlockSpec((r, d), vec),
        out_shape=jax.ShapeDtypeStruct((r, d), BF16),
        compiler_params=_params("arbitrary"),
        name="dwconv_ln_decode",
    )(u, hist_t, w, b.reshape(1, d), ln_g.reshape(1, d), ln_b.reshape(1, d))


S5_CHUNK = 256
S5_PITCH = S5_CHUNK + 4
IN_SLABS = D_MODEL // LANES
STATE_PER_IN = N_SLAB // IN_SLABS


def _s5_project_in(hb, wb_ref, store):
    half = STATE_PER_IN * LANES
    for i in range(IN_SLABS):
        res = jnp.dot(hb[:, i * LANES:(i + 1) * LANES], wb_ref[i], preferred_element_type=F32)
        for jj in range(STATE_PER_IN):
            store(i * STATE_PER_IN + jj, res[:, jj * LANES:(jj + 1) * LANES],
                  res[:, half + jj * LANES:half + (jj + 1) * LANES])


def _s5_project_out(load, cre_ref, cim_ref, hn, d_ref, y_ref):
    for i in range(IN_SLABS):
        cols = slice(i * LANES, (i + 1) * LANES)
        acc = d_ref[:, cols] * hn[:, cols]
        for jj in range(STATE_PER_IN):
            j = i * STATE_PER_IN + jj
            re, im = load(j)
            acc = acc + jnp.dot(re.astype(BF16), cre_ref[j], preferred_element_type=F32)
            acc = acc - jnp.dot(im.astype(BF16), cim_ref[j], preferred_element_type=F32)
        y_ref[:, cols] = acc.astype(y_ref.dtype)


def _s5_scan_kernel(x_ref, g_ref, wb_ref, ar_ref, ai_ref, cre_ref, cim_ref, d_ref,
                    y_ref, sr_ref, si_ref, bur_ref, bui_ref, hr_ref, hi_ref):
    tc, pitch = S5_CHUNK, S5_PITCH

    @pl.when(pl.program_id(1) == 0)
    def _():
        hr_ref[...] = jnp.zeros_like(hr_ref)
        hi_ref[...] = jnp.zeros_like(hi_ref)

    hn = _rms(x_ref[...], g_ref[...])

    def store(j, re, im):
        bur_ref[j * pitch:j * pitch + tc, :] = re
        bui_ref[j * pitch:j * pitch + tc, :] = im

    _s5_project_in(hn.astype(BF16), wb_ref, store)

    ar = [ar_ref[j8] for j8 in range(SLAB_GRP)]
    ai = [ai_ref[j8] for j8 in range(SLAB_GRP)]

    def step(t, carry):
        out = []
        for j8 in range(SLAB_GRP):
            hr, hi = carry[2 * j8], carry[2 * j8 + 1]
            rows = pl.ds(j8 * SUBLANES * pitch + t, SUBLANES, stride=pitch)
            nr = ar[j8] * hr - ai[j8] * hi + bur_ref[rows, :]
            ni = ar[j8] * hi + ai[j8] * hr + bui_ref[rows, :]
            bur_ref[rows, :] = nr
            bui_ref[rows, :] = ni
            out += [nr, ni]
        return tuple(out)

    init = []
    for j8 in range(SLAB_GRP):
        init += [hr_ref[j8], hi_ref[j8]]
    fin = lax.fori_loop(0, tc, step, tuple(init))
    for j8 in range(SLAB_GRP):
        hr_ref[j8] = fin[2 * j8]
        hi_ref[j8] = fin[2 * j8 + 1]
    sr_ref[...] = hr_ref[...]
    si_ref[...] = hi_ref[...]

    def load(j):
        return bur_ref[j * pitch:j * pitch + tc, :], bui_ref[j * pitch:j * pitch + tc, :]

    _s5_project_out(load, cre_ref, cim_ref, hn, d_ref, y_ref)


def s5_scan(x, g, wb, ar, ai, cre, cim, d_skip, bsz, seq):
    n_chunks = seq // S5_CHUNK
    st = jax.ShapeDtypeStruct((bsz, SLAB_GRP, SUBLANES, LANES), F32)
    st_spec = pl.BlockSpec((None, SLAB_GRP, SUBLANES, LANES), lambda b, c: (b, 0, 0, 0))
    vec = lambda b, c: (0, 0)
    c3 = lambda b, c: (0, 0, 0)
    return pl.pallas_call(
        _s5_scan_kernel,
        grid=(bsz, n_chunks),
        in_specs=[pl.BlockSpec((S5_CHUNK, D_MODEL), lambda b, c: (b * n_chunks + c, 0)),
                  pl.BlockSpec((1, D_MODEL), vec),
                  pl.BlockSpec(wb.shape, c3), pl.BlockSpec(ar.shape, c3), pl.BlockSpec(ai.shape, c3),
                  pl.BlockSpec(cre.shape, c3), pl.BlockSpec(cim.shape, c3),
                  pl.BlockSpec((1, D_MODEL), vec)],
        out_specs=[pl.BlockSpec((S5_CHUNK, D_MODEL), lambda b, c: (b * n_chunks + c, 0)), st_spec, st_spec],
        out_shape=[jax.ShapeDtypeStruct((bsz * seq, D_MODEL), BF16), st, st],
        scratch_shapes=[pltpu.VMEM((N_SLAB * S5_PITCH, LANES), F32), pltpu.VMEM((N_SLAB * S5_PITCH, LANES), F32),
                        pltpu.VMEM((SLAB_GRP, SUBLANES, LANES), F32), pltpu.VMEM((SLAB_GRP, SUBLANES, LANES), F32)],
        compiler_params=_params("parallel", "arbitrary"),
        name="s5_scan",
    )(x, g.reshape(1, -1), wb, ar, ai, cre, cim, d_skip.reshape(1, -1))


def _s5_decode_kernel(x_ref, g_ref, wb_ref, ar_ref, ai_ref, cre_ref, cim_ref, d_ref, h0r_ref, h0i_ref,
                      y_ref, sr_ref, si_ref):
    hn = _rms(x_ref[...], g_ref[...])

    def store(j, re, im):
        cols = slice(j * LANES, (j + 1) * LANES)
        ar, ai = ar_ref[:, cols], ai_ref[:, cols]
        hr, hi = h0r_ref[:, cols], h0i_ref[:, cols]
        sr_ref[:, cols] = ar * hr - ai * hi + re
        si_ref[:, cols] = ar * hi + ai * hr + im

    _s5_project_in(hn.astype(BF16), wb_ref, store)

    def load(j):
        cols = slice(j * LANES, (j + 1) * LANES)
        return sr_ref[:, cols], si_ref[:, cols]

    _s5_project_out(load, cre_ref, cim_ref, hn, d_ref, y_ref)


def s5_decode(x, g, wb, ar_row, ai_row, cre, cim, d_skip, h0r, h0i):
    r = x.shape[0]
    st = jax.ShapeDtypeStruct((r, SSM_DIM), F32)
    vec = lambda i: (0, 0)
    c3 = lambda i: (0, 0, 0)
    return pl.pallas_call(
        _s5_decode_kernel,
        grid=(1,),
        in_specs=[pl.BlockSpec((r, D_MODEL), vec), pl.BlockSpec((1, D_MODEL), vec),
                  pl.BlockSpec(wb.shape, c3), pl.BlockSpec((1, SSM_DIM), vec), pl.BlockSpec((1, SSM_DIM), vec),
                  pl.BlockSpec(cre.shape, c3), pl.BlockSpec(cim.shape, c3), pl.BlockSpec((1, D_MODEL), vec),
                  pl.BlockSpec((r, SSM_DIM), vec), pl.BlockSpec((r, SSM_DIM), vec)],
        out_specs=[pl.BlockSpec((r, D_MODEL), vec), pl.BlockSpec((r, SSM_DIM), vec), pl.BlockSpec((r, SSM_DIM), vec)],
        out_shape=[jax.ShapeDtypeStruct((r, D_MODEL), BF16), st, st],
        compiler_params=_params("arbitrary"),
        name="s5_decode",
    )(x, g.reshape(1, -1), wb, ar_row, ai_row, cre, cim, d_skip.reshape(1, -1), h0r, h0i)


def _t5_bucket(rel):
    n = jnp.maximum(rel, 0)
    nf = jnp.maximum(n, MAX_EXACT).astype(F32)
    big = MAX_EXACT + (jnp.log(nf / MAX_EXACT) / math.log(REL_MAX_DIST / MAX_EXACT)
                       * (N_BUCKETS - MAX_EXACT)).astype(jnp.int32)
    return jnp.where(n < MAX_EXACT, n, jnp.minimum(big, N_BUCKETS - 1))


def _bias_of(rel_bias, rel):
    onehot = (_t5_bucket(rel)[..., None] == jnp.arange(N_BUCKETS, dtype=jnp.int32)).astype(F32)
    return jnp.einsum('...k,kh->h...', onehot, rel_bias.astype(F32), precision=HIGHEST)


def _selection_matrix(n_cmp, n_sel_pad):
    coef = np.convolve(np.ones(SEL_RATIO), np.ones(CMP_BLOCK // CMP_STRIDE)).astype(np.float32)
    m = np.zeros((n_cmp, n_sel_pad), np.float32)
    for j in range(n_sel_pad):
        for o in range(coef.shape[0]):
            n = SEL_RATIO * j + o - (CMP_BLOCK // CMP_STRIDE - 1)
            if 0 <= n < n_cmp:
                m[n, j] = coef[o]
    return m


def _s5_params(a_re, a_im, log_dt, b_re, b_im, c_re, c_im):
    dt = jnp.exp(log_dt.astype(F32))[:, None]
    ar, ai = a_re.astype(F32), a_im.astype(F32)
    mag = jnp.exp(ar * dt)
    abar_re, abar_im = mag * jnp.cos(ai * dt), mag * jnp.sin(ai * dt)
    den = ar * ar + ai * ai
    coef_re = ((abar_re - 1.0) * ar + abar_im * ai) / den
    coef_im = (abar_im * ar - (abar_re - 1.0) * ai) / den
    br, bim = b_re.astype(F32), b_im.astype(F32)
    bb_re = coef_re[..., None] * br - coef_im[..., None] * bim
    bb_im = coef_re[..., None] * bim + coef_im[..., None] * br
    gpi = LANES // SSM_GROUP_CH
    eye = jnp.eye(gpi, dtype=F32)

    def in_blocks(bb):
        t = bb.reshape(IN_SLABS, gpi, SSM_STATE, SSM_GROUP_CH)
        blk = jnp.einsum('sgpc,gh->sgchp', t, eye)
        return blk.reshape(IN_SLABS, LANES, gpi * SSM_STATE)

    wb = jnp.concatenate([in_blocks(bb_re), in_blocks(bb_im)], axis=-1).astype(BF16)
    gps = LANES // SSM_STATE
    ch_per_in = LANES

    def out_blocks(c):
        t = c.astype(F32).reshape(IN_SLABS, STATE_PER_IN, gps, SSM_GROUP_CH, SSM_STATE)
        sel = jnp.eye(STATE_PER_IN * gps, dtype=F32).reshape(STATE_PER_IN, gps, STATE_PER_IN * gps)
        blk = jnp.einsum('ijgcp,jgh->ijgphc', t, sel)
        return blk.reshape(N_SLAB, LANES, ch_per_in).astype(BF16)

    return abar_re, abar_im, wb, out_blocks(c_re), out_blocks(c_im)


PROMPT_TM = 512
TN = 512
FFN_UP_TM = 1024
FFN_DOWN_TK = D_FF // 4
GLU_TM = 256
GLU_TK = D_MODEL


def _row_tile(r):
    return PROMPT_TM if r % PROMPT_TM == 0 else r


def _nsa_project(h, wq, wkv, wg):
    tm = _row_tile(h.shape[0])
    q = matmul(h, wq, n_split=1, out_dtype=BF16, act=None, tm=tm, tn=TN, name="nsa_q")[0]
    kv = matmul(h, wkv, n_split=3, out_dtype=F32, act=None, tm=tm, tn=TN, name="nsa_kv")
    gate = matmul(h, wg, n_split=1, out_dtype=F32, act="sigmoid", tm=tm, tn=LANES, name="nsa_gate")[0]
    return q, kv, gate


def _nsa_layer(hp, hs, xp, xs, g_post, g_next, wts, tabs, caches, ret, layer, page_table, bsz, seq):
    wq, wkv, wg, wo, pe, w1, w2 = wts
    cmp_pool, slc_pool, win_pool, n_phys = caches
    dec = hs.shape[0]
    q = matmul(hp, wq, n_split=1, out_dtype=BF16, act=None, tm=PROMPT_TM, tn=2 * TN, name="nsa_q")[0]
    kv, ret_cmp, ret_slc = kv_project(hp, wkv, ret[0], ret[1], layer, tm=PROMPT_TM)
    gate = matmul(hp, wg, n_split=1, out_dtype=F32, act="sigmoid", tm=PROMPT_TM, tn=LANES, name="nsa_gate")[0]
    part_a, part_b = cmp_partial_rows(kv[0], pe, w1, seq)
    kc = cmp_finish(part_a, part_b, jnp.zeros((bsz, SUBLANES, KV_COLS), F32), w2)
    o_cmp, mask = cmp_attention(q, kc, tabs["cmp"], tabs["msel"], tabs["expand"], bsz, seq)
    vt = project_value_t(hp, wkv, 1, tm=PROMPT_TM, name="nsa_vt")
    o_slc = slc_attention(q, kv[1], vt, mask, tabs["tile_t"], bsz, seq)
    o = win_attention(q, kv[2], tabs["win"], o_cmp, o_slc, gate, bsz, seq)
    xp, hp = proj_res(o, wo, xp, g_post, g_next, glu=False, tm=PROMPT_TM, tk=D_MODEL, name="nsa_out")
    qs, kvs, gate_s = _nsa_project(hs, wq, wkv, wg)
    past_a, past_b = cmp_partial_paged(cmp_pool, n_phys, page_table, layer, pe, w1)
    tail = jnp.pad(kvs[0][:, None, :], ((0, 0), (0, CMP_STRIDE - 1), (0, 0))).reshape(dec * CMP_STRIDE, KV_COLS)
    _, tail_b = cmp_partial_rows(tail, pe, w1, dec * CMP_STRIDE)
    b_next = jnp.pad(tail_b[0][:, None, :], ((0, 0), (0, SUBLANES - 1), (0, 0)))
    kc_s = cmp_finish(past_a, past_b, b_next, w2)
    oc_s, ow_s, idx = sample_attention(qs.reshape(dec, N_HEADS, HEAD_DIM), kc_s, tabs["cmp_s"], tabs["gsum"],
                                       tabs["msel_s"], win_pool, layer, kvs[2][:, None, :],
                                       tabs["win_s"], tabs["new_s"])
    q3 = qs.reshape(dec, N_HEADS, HEAD_DIM)
    os_all = sample_slc_attention(idx[:, :KV_GROUPS, :N_SELECT], page_table, q3, slc_pool, n_phys, layer,
                                  kvs[1][:, None, :], tabs["slc_s"])
    os_s = jnp.stack([os_all[:, h // GROUP_SIZE, h] for h in range(N_HEADS)], axis=1)
    o_s = nsa_combine(oc_s.reshape(dec, D_MODEL), os_s.reshape(dec, D_MODEL), ow_s.reshape(dec, D_MODEL), gate_s, dec)
    xs, hs = proj_res(o_s, wo, xs, g_post, g_next, glu=False, tm=dec, tk=D_MODEL, name="nsa_out_s")
    return xp, hp, xs, hs, kv, kvs, (ret_cmp, ret_slc)


def _conv_layer(hp, hs, xp, xs, g_post, g_next, wts, state, bsz, seq):
    w_pw1, dw, dw_b, ln_g, ln_b, w_pw2 = wts
    dec = hs.shape[0]
    u = glu_matmul(hp, w_pw1, tm=PROMPT_TM, tn=TN, name="conv_pw1")
    hc = dwconv_ln(u, dw, dw_b, ln_g, ln_b, seq)
    xp, hp = proj_res(hc, w_pw2, xp, g_post, g_next, glu=False, tm=PROMPT_TM, tk=D_MODEL, name="conv_pw2")
    hist_p = u.reshape(bsz, seq, D_MODEL)[:, seq - (CONV_WIDTH - 1):]
    us = glu_matmul(hs, w_pw1, tm=dec, tn=TN, name="conv_pw1_s")
    hc_s = dwconv_ln_decode(us, jnp.swapaxes(state, 0, 1), dw, dw_b, ln_g, ln_b)
    xs, hs = proj_res(hc_s, w_pw2, xs, g_post, g_next, glu=False, tm=dec, tk=D_MODEL, name="conv_pw2_s")
    hist_s = jnp.concatenate([state[:, 1:], us[:, None, :]], axis=1)
    return xp, hp, xs, hs, hist_p, hist_s


def _s5_layer(xp, xs, g_pre, g_post, g_next, wts, state_re, state_im, bsz, seq):
    a_re, a_im, log_dt, b_re, b_im, c_re, c_im, d_skip, w_glu = wts
    dec = xs.shape[0]
    abar_re, abar_im, wb, cre, cim = _s5_params(a_re, a_im, log_dt, b_re, b_im, c_re, c_im)
    slab_shape = (SLAB_GRP, SUBLANES, LANES)
    y, sr, si = s5_scan(xp, g_pre, wb, abar_re.reshape(slab_shape), abar_im.reshape(slab_shape), cre, cim,
                        d_skip, bsz, seq)
    xp, hp = proj_res(y, w_glu, xp, g_post, g_next, glu=True, tm=GLU_TM, tk=GLU_TK, name="s5_glu")
    ys, sr_s, si_s = s5_decode(xs, g_pre, wb, abar_re.reshape(1, SSM_DIM), abar_im.reshape(1, SSM_DIM), cre, cim,
                               d_skip, state_re.reshape(dec, SSM_DIM), state_im.reshape(dec, SSM_DIM))
    xs, hs = proj_res(ys, w_glu, xs, g_post, g_next, glu=True, tm=dec, tk=GLU_TK, name="s5_glu_s")
    gp = (SSM_GROUPS, SSM_STATE)
    return (xp, hp, xs, hs, sr.reshape((bsz,) + gp), si.reshape((bsz,) + gp),
            sr_s.reshape((dec,) + gp), si_s.reshape((dec,) + gp))


def _ffn_layer(hp, hs, xp, xs, g_post, g_next, wts, state, bsz, seq):
    w_up, layer, dw, dw_b, w_down = wts
    dec = hs.shape[0]
    act, hist = ffn_up(hp, w_up, layer, dw, dw_b, seq_len=seq, tm=FFN_UP_TM, tn=TN)
    xp, hp = proj_res(act, w_down, xp, g_post, g_next, glu=False, tm=PROMPT_TM, tk=FFN_DOWN_TK, name="ffn_down")
    tiles = seq // FFN_UP_TM
    hist_p = hist.reshape(bsz, tiles, SUBLANES, D_FF)[:, tiles - 1, SUBLANES - 2:, :]
    act_s, gate_s = ffn_up(hs, w_up, layer, dw, dw_b, hist=state, seq_len=1, tm=dec, tn=TN)
    xs, hs = proj_res(act_s, w_down, xs, g_post, g_next, glu=False, tm=dec, tk=FFN_DOWN_TK, name="ffn_down_s")
    hist_s = jnp.concatenate([state[:, 1:], gate_s.reshape(dec, 1, D_FF)], axis=1)
    return xp, hp, xs, hs, hist_p, hist_s


def _bias_tables(rel_bias, seq):
    tq = ATT_TILE
    i = jnp.arange(tq, dtype=jnp.int32)
    tile = jnp.stack([_bias_of(rel_bias, d * tq + i[:, None] - i[None, :]) for d in range(3)])
    n_cmp = seq // CMP_STRIDE
    cpos = jnp.arange(n_cmp, dtype=jnp.int32) * CMP_STRIDE + (CMP_BLOCK - 1)
    qpos = jnp.arange(seq, dtype=jnp.int32)
    n_sel = -(-seq // SEL_BLOCK)
    key = np.arange(seq)
    expand = (key[:, None] // SEL_BLOCK == np.arange(LANES)[None, :]).astype(np.float32)
    cpos_s = jnp.arange(N_CMP_S, dtype=jnp.int32) * CMP_STRIDE + (CMP_BLOCK - 1)
    wb = min(WINDOW, PAST_LEN)
    kpos_s = jnp.arange(N_SEL_S * SEL_BLOCK, dtype=jnp.int32)
    slc_s = _bias_of(rel_bias, PAST_LEN - kpos_s).reshape(N_HEADS, N_SEL_S, SEL_BLOCK)
    gsum = (np.arange(N_HEADS)[None, :] // GROUP_SIZE == np.arange(SUBLANES)[:, None]).astype(np.float32)
    return {
        "tile_t": jnp.swapaxes(tile, -1, -2),
        "win": _bias_of(rel_bias, i[:, None] + WINDOW - jnp.arange(WIN_SPAN, dtype=jnp.int32)[None, :]),
        "cmp": _bias_of(rel_bias, qpos[:, None] - cpos[None, :]),
        "msel": jnp.asarray(_selection_matrix(n_cmp, n_sel).T),
        "expand": jnp.asarray(expand, BF16),
        "cmp_s": _bias_of(rel_bias, PAST_LEN - cpos_s),
        "msel_s": jnp.asarray(_selection_matrix(N_CMP_S, N_SEL_S_PAD)),
        "gsum": jnp.asarray(gsum),
        "win_s": _bias_of(rel_bias, wb - jnp.arange(wb, dtype=jnp.int32)),
        "new_s": jnp.broadcast_to(_bias_of(rel_bias, jnp.zeros((1,), jnp.int32)), (N_HEADS, LANES)),
        "slc_s": jnp.swapaxes(slc_s, 0, 1),
    }


def kernel(x_prompt, x_sample, cache_cmp_kv, cache_slc_kv, cache_win_kv, state_conv, state_ssm_re, state_ssm_im, state_ffn_conv, page_table, norm_gain, rel_bias, nsa_w_q, nsa_w_kv, nsa_cmp_pe, nsa_cmp_w1, nsa_cmp_w2, nsa_w_gate, nsa_w_o, conv_w_pw1, conv_dw, conv_dw_b, conv_ln_g, conv_ln_b, conv_w_pw2, ssm_a_re, ssm_a_im, ssm_log_dt, ssm_b_re, ssm_b_im, ssm_c_re, ssm_c_im, ssm_d, ssm_w_glu, ffn_w_up, ffn_dw, ffn_dw_b, ffn_w_down):
    bsz, seq, d = x_prompt.shape
    dec, dec_seq, _ = x_sample.shape
    assert dec_seq == 1 and d == D_MODEL and seq % PROMPT_TM == 0 and seq // CMP_STRIDE == LANES
    n_nsa = cache_cmp_kv.shape[0]
    n_phys = cache_cmp_kv.shape[1]
    xp = x_prompt.reshape(bsz * seq, d)
    xs = x_sample.reshape(dec, d)
    tabs = _bias_tables(rel_bias, seq)
    cmp_pool = cache_cmp_kv.reshape(n_nsa * n_phys, PAGE_SIZE * N_KV_SLABS, HEAD_DIM)
    slc_pool = cache_slc_kv.reshape(n_nsa * n_phys * (PAGE_SIZE // SEL_BLOCK), SEL_BLOCK * N_KV_SLABS, HEAD_DIM)
    win_pool = cache_win_kv.reshape(n_nsa * dec, cache_win_kv.shape[2] * N_KV_SLABS, HEAD_DIM)
    g_all = norm_gain.astype(F32)

    hp = rmsnorm_cast(xp, g_all[0, 0], PROMPT_TM)
    hs = rmsnorm_cast(xs, g_all[0, 0], dec)
    ret = tuple(jnp.zeros((n_nsa, bsz * seq * N_KV_SLABS, HEAD_DIM), F32) for _ in range(2))
    ret_shape = (n_nsa, bsz, seq, KV_GROUPS, 2, HEAD_DIM)
    out = {k: [] for k in ("cmp_s", "slc_s", "win_p", "win_s", "conv_p", "conv_s",
                           "re_p", "re_s", "im_p", "im_s", "ffn_p", "ffn_s")}
    counts = [0, 0, 0]
    for i in range(DEPTH):
        m = i % 3
        j = counts[m]
        counts[m] += 1
        g_post, g_ffn, g_ffn_post = g_all[i, 1], g_all[i, 2], g_all[i, 3]
        g_next = g_all[i + 1, 0] if i + 1 < DEPTH else g_all[i, 0]
        if m == 0:
            wg = jnp.pad(nsa_w_gate[j], ((0, 0), (0, LANES - nsa_w_gate.shape[-1]))).astype(BF16)
            wts = (nsa_w_q[j].astype(BF16), nsa_w_kv[j].astype(BF16), wg, nsa_w_o[j].astype(BF16),
                   jnp.swapaxes(nsa_cmp_pe[j], 0, 1).astype(F32), nsa_cmp_w1[j].astype(BF16),
                   nsa_cmp_w2[j].astype(BF16))
            xp, hp, xs, hs, kv, kvs, ret = _nsa_layer(hp, hs, xp, xs, g_post, g_ffn, wts, tabs,
                                                      (cmp_pool, slc_pool, win_pool, n_phys), ret, j, page_table,
                                                      bsz, seq)
            shp = (bsz, seq, KV_GROUPS, 2, HEAD_DIM)
            shs = (dec, 1, KV_GROUPS, 2, HEAD_DIM)
            out["cmp_s"].append(kvs[0].reshape(shs))
            out["slc_s"].append(kvs[1].reshape(shs))
            out["win_p"].append(kv[2].reshape(shp)[:, seq - min(WINDOW, seq):])
            win_full = jnp.concatenate([cache_win_kv[j], kvs[2].reshape(shs)], axis=1)
            out["win_s"].append(win_full[:, win_full.shape[1] - min(WINDOW, PAST_LEN + 1):])
        elif m == 1:
            wts = (conv_w_pw1[j].astype(BF16), conv_dw[j], conv_dw_b[j], conv_ln_g[j], conv_ln_b[j],
                   conv_w_pw2[j].astype(BF16))
            xp, hp, xs, hs, cp, cs = _conv_layer(hp, hs, xp, xs, g_post, g_ffn, wts, state_conv[j], bsz, seq)
            out["conv_p"].append(cp); out["conv_s"].append(cs)
        else:
            wts = (ssm_a_re[j], ssm_a_im[j], ssm_log_dt[j], ssm_b_re[j], ssm_b_im[j], ssm_c_re[j], ssm_c_im[j],
                   ssm_d[j], ssm_w_glu[j].astype(BF16))
            xp, hp, xs, hs, rp, ip, rs, is_ = _s5_layer(xp, xs, g_all[i, 0], g_post, g_ffn, wts,
                                                        state_ssm_re[j], state_ssm_im[j], bsz, seq)
            out["re_p"].append(rp); out["im_p"].append(ip); out["re_s"].append(rs); out["im_s"].append(is_)
        wts = (ffn_w_up, i, ffn_dw[i], ffn_dw_b[i], ffn_w_down[i].astype(BF16))
        xp, hp, xs, hs, fp, fs = _ffn_layer(hp, hs, xp, xs, g_ffn_post, g_next, wts, state_ffn_conv[i], bsz, seq)
        out["ffn_p"].append(fp); out["ffn_s"].append(fs)
    st = lambda k: jnp.stack(out[k])
    return (xp.reshape(bsz, seq, d), xs.reshape(dec, 1, d),
            ret[0].reshape(ret_shape), st("cmp_s"), ret[1].reshape(ret_shape), st("slc_s"), st("win_p"), st("win_s"),
            st("conv_p"), st("conv_s"), st("re_p"), st("re_s"), st("im_p"), st("im_s"),
            st("ffn_p"), st("ffn_s"))
```

```python
import functools
import math

import numpy as np
import jax
import jax.numpy as jnp
from jax import lax
from jax.experimental import pallas as pl
from jax.experimental.pallas import tpu as pltpu

F32 = jnp.float32
BF16 = jnp.bfloat16

D_MODEL = 2048
DEPTH = 4
PAST_LEN = 16384
PAGE_SIZE = 128
N_HEADS = 16
HEAD_DIM = 128
KV_GROUPS = 4
GROUP_SIZE = 4
KV_COLS = KV_GROUPS * 2 * HEAD_DIM
CMP_BLOCK = 32
CMP_STRIDE = 16
SEL_BLOCK = 64
SEL_RATIO = SEL_BLOCK // CMP_STRIDE
N_SELECT = 16
WINDOW = 512
FORCE_BONUS = 1.0e4
N_BUCKETS = 32
MAX_EXACT = 16
REL_MAX_DIST = 128
CONV_WIDTH = 31
SSM_GROUP_CH = 16
SSM_GROUPS = 128
SSM_STATE = 64
SSM_DIM = SSM_GROUPS * SSM_STATE
D_FF = 5632
RMS_EPS = 1e-6
LN_EPS = 1e-5
SCALE = HEAD_DIM ** -0.5
NEG = -1e30

LANES = 128
SUBLANES = 8
VMEM_LIMIT = 56 * 1024 * 1024
ATT_TILE = 128
N_SLAB = SSM_DIM // LANES
SLAB_GRP = N_SLAB // SUBLANES

HIGHEST = lax.Precision.HIGHEST
NT_DIMS = (((1,), (1,)), ((), ()))


def _params(*sem):
    return pltpu.CompilerParams(dimension_semantics=sem, vmem_limit_bytes=VMEM_LIMIT)


def _rms(x, g):
    return x * lax.rsqrt(jnp.mean(x * x, axis=-1, keepdims=True) + RMS_EPS) * g


def _rmsnorm_kernel(x_ref, g_ref, o_ref):
    o_ref[...] = _rms(x_ref[...], g_ref[...]).astype(o_ref.dtype)


def rmsnorm_cast(x, g, tm):
    r, d = x.shape
    return pl.pallas_call(
        _rmsnorm_kernel,
        grid=(r // tm,),
        in_specs=[pl.BlockSpec((tm, d), lambda m: (m, 0)), pl.BlockSpec((1, d), lambda m: (0, 0))],
        out_specs=pl.BlockSpec((tm, d), lambda m: (m, 0)),
        out_shape=jax.ShapeDtypeStruct((r, d), BF16),
        compiler_params=_params("parallel"),
        name="rmsnorm",
    )(x, g.reshape(1, d))


def _mm_kernel(a_ref, w_ref, o_ref, *, act):
    r = jnp.dot(a_ref[...], w_ref[...], preferred_element_type=F32)
    if act == "sigmoid":
        r = jax.nn.sigmoid(r)
    o_ref[...] = r.astype(o_ref.dtype)


def matmul(a, w, *, n_split, out_dtype, act, tm, tn, name):
    r, k = a.shape
    n = w.shape[1]
    per = (n // n_split) // tn
    return pl.pallas_call(
        functools.partial(_mm_kernel, act=act),
        grid=(r // tm, n // tn),
        in_specs=[pl.BlockSpec((tm, k), lambda m, j: (m, 0)), pl.BlockSpec((k, tn), lambda m, j: (0, j))],
        out_specs=pl.BlockSpec((None, tm, tn), lambda m, j: (j // per, m, j % per)),
        out_shape=jax.ShapeDtypeStruct((n_split, r, n // n_split), out_dtype),
        compiler_params=_params("parallel", "arbitrary"),
        name=name,
    )(a, w)


KV_TN = 512


def _kv_project_kernel(a_ref, w_ref, prev_cmp_ref, prev_slc_ref, kv_ref, cmp_ref, slc_ref, vt_ref):
    del prev_cmp_ref, prev_slc_ref
    n = pl.program_id(1)
    res = jnp.dot(a_ref[...], w_ref[...], preferred_element_type=F32)
    kv_ref[...] = res
    tm = res.shape[0]
    slabs = KV_TN // HEAD_DIM
    first = (n % 2) * slabs

    @pl.when((n >= 2) & (n < 4))
    def _():
        for i in range(slabs // 2):
            v = res[:, (2 * i + 1) * HEAD_DIM:(2 * i + 2) * HEAD_DIM]
            vt_ref[i * HEAD_DIM:(i + 1) * HEAD_DIM, :] = v.T.astype(vt_ref.dtype)

    def scatter(out_ref):
        for sl in range(slabs):
            out_ref[pl.ds(first + sl, tm, stride=N_KV_SLABS), :] = res[:, sl * HEAD_DIM:(sl + 1) * HEAD_DIM]

    @pl.when(n < 2)
    def _():
        scatter(cmp_ref)

    @pl.when((n >= 2) & (n < 4))
    def _():
        scatter(slc_ref)


def kv_project(a, w_kv, prev_cmp, prev_slc, layer, *, tm):
    r, k = a.shape
    n_tiles = w_kv.shape[1] // KV_TN
    per = KV_COLS // KV_TN
    nat_spec = pl.BlockSpec((None, tm * N_KV_SLABS, HEAD_DIM), lambda m, j: (layer, m, 0))
    any_spec = pl.BlockSpec(memory_space=pl.ANY)
    return pl.pallas_call(
        _kv_project_kernel,
        grid=(r // tm, n_tiles),
        in_specs=[pl.BlockSpec((tm, k), lambda m, j: (m, 0)), pl.BlockSpec((k, KV_TN), lambda m, j: (0, j)),
                  any_spec, any_spec],
        out_specs=[pl.BlockSpec((None, tm, KV_TN), lambda m, j: (j // per, m, j % per)), nat_spec, nat_spec,
                   pl.BlockSpec((KV_TN // 2, tm), lambda m, j: (jnp.clip(j - 2, 0, 1), m))],
        out_shape=[jax.ShapeDtypeStruct((3, r, KV_COLS), F32),
                   jax.ShapeDtypeStruct(prev_cmp.shape, F32), jax.ShapeDtypeStruct(prev_slc.shape, F32),
                   jax.ShapeDtypeStruct((KV_GROUPS * HEAD_DIM, r), BF16)],
        input_output_aliases={2: 1, 3: 2},
        compiler_params=_params("parallel", "arbitrary"),
        name="nsa_kv",
    )(a, w_kv, prev_cmp, prev_slc)


def _glu_mm_kernel(a_ref, wa_ref, wb_ref, o_ref):
    a = a_ref[...]
    lin = jnp.dot(a, wa_ref[...], preferred_element_type=F32)
    gate = jnp.dot(a, wb_ref[...], preferred_element_type=F32)
    o_ref[...] = lin * jax.nn.sigmoid(gate)


def glu_matmul(a, w, *, tm, tn, name):
    r, k = a.shape
    n = w.shape[1] // 2
    nb = n // tn
    return pl.pallas_call(
        _glu_mm_kernel,
        grid=(r // tm, nb),
        in_specs=[pl.BlockSpec((tm, k), lambda m, j: (m, 0)),
                  pl.BlockSpec((k, tn), lambda m, j: (0, j)),
                  pl.BlockSpec((k, tn), lambda m, j: (0, j + nb))],
        out_specs=pl.BlockSpec((tm, tn), lambda m, j: (m, j)),
        out_shape=jax.ShapeDtypeStruct((r, n), F32),
        compiler_params=_params("parallel", "arbitrary"),
        name=name,
    )(a, w, w)


def _proj_res_kernel(a_ref, w_ref, x_ref, gp_ref, gn_ref, xo_ref, ho_ref, *acc, glu, nk):
    def finish(y):
        if glu:
            d = y.shape[1] // 2
            y = y[:, :d] * jax.nn.sigmoid(y[:, d:])
        xn = x_ref[...] + _rms(y, gp_ref[...])
        xo_ref[...] = xn
        ho_ref[...] = _rms(xn, gn_ref[...]).astype(ho_ref.dtype)

    part = jnp.dot(a_ref[...], w_ref[...], preferred_element_type=F32)
    if nk == 1:
        finish(part)
        return
    acc_ref, = acc
    k = pl.program_id(1)

    @pl.when(k == 0)
    def _():
        acc_ref[...] = part

    @pl.when((k > 0) & (k < nk - 1))
    def _():
        acc_ref[...] += part

    @pl.when(k == nk - 1)
    def _():
        finish(acc_ref[...] + part)


def proj_res(a, w, x, g_post, g_next, *, glu, tm, tk, name):
    r, kdim = a.shape
    n = w.shape[1]
    d = x.shape[1]
    nk = kdim // tk
    return pl.pallas_call(
        functools.partial(_proj_res_kernel, glu=glu, nk=nk),
        grid=(r // tm, nk),
        in_specs=[pl.BlockSpec((tm, tk), lambda m, k: (m, k)),
                  pl.BlockSpec((tk, n), lambda m, k: (k, 0)),
                  pl.BlockSpec((tm, d), lambda m, k: (m, 0)),
                  pl.BlockSpec((1, d), lambda m, k: (0, 0)),
                  pl.BlockSpec((1, d), lambda m, k: (0, 0))],
        out_specs=[pl.BlockSpec((tm, d), lambda m, k: (m, 0)),
                   pl.BlockSpec((tm, d), lambda m, k: (m, 0))],
        out_shape=[jax.ShapeDtypeStruct((r, d), F32), jax.ShapeDtypeStruct((r, d), BF16)],
        scratch_shapes=[pltpu.VMEM((tm, n), F32)] if nk > 1 else [],
        compiler_params=_params("parallel", "arbitrary"),
        name=name,
    )(a, w, x, g_post.reshape(1, d), g_next.reshape(1, d))


def _ffn_up_kernel(a_ref, p1_ref, p2_ref, wg_ref, wv_ref, dw_ref, db_ref, act_ref, hist_ref, *, decode, tiles_per_seq):
    a = a_ref[...]
    wg = wg_ref[...].astype(BF16)
    gate = jnp.dot(a, wg, preferred_element_type=F32)
    val = jnp.dot(a, wv_ref[...].astype(BF16), preferred_element_type=F32)
    tm = gate.shape[0]
    if decode:
        g2, g1 = p2_ref[...], p1_ref[...]
        hist_ref[...] = gate
    else:
        halo = jnp.dot(p1_ref[...], wg, preferred_element_type=F32)
        halo = jnp.where(pl.program_id(0) % tiles_per_seq == 0, 0.0, halo)
        h7, h6 = halo[7:8, :], halo[6:7, :]
        row = lax.broadcasted_iota(jnp.int32, gate.shape, 0)
        g1 = jnp.where(row == 0, h7, pltpu.roll(gate, 1, 0))
        g2 = jnp.where(row == 0, h6, jnp.where(row == 1, h7, pltpu.roll(gate, 2, 0)))
        hist_ref[...] = gate[tm - SUBLANES:, :]
    g = dw_ref[0:1, :] * g2 + dw_ref[1:2, :] * g1 + dw_ref[2:3, :] * gate + db_ref[...]
    act_ref[...] = (jax.nn.gelu(g) * val).astype(act_ref.dtype)


def ffn_up(h, w_up, layer, dw, db, *, hist=None, seq_len, tm, tn):
    r, k = h.shape
    nb = D_FF // tn
    decode = hist is not None
    if decode:
        p1, p2 = hist[:, 1, :], hist[:, 0, :]
        p_specs = [pl.BlockSpec((tm, tn), lambda m, j: (m, j)), pl.BlockSpec((tm, tn), lambda m, j: (m, j))]
        hrows = tm
    else:
        p1 = p2 = h
        blk = tm // SUBLANES
        p_specs = [pl.BlockSpec((SUBLANES, k), lambda m, j: (jnp.maximum(m * blk - 1, 0), 0)),
                   pl.BlockSpec((SUBLANES, k), lambda m, j: (0, 0))]
        hrows = SUBLANES
    return pl.pallas_call(
        functools.partial(_ffn_up_kernel, decode=decode, tiles_per_seq=max(seq_len // tm, 1)),
        grid=(r // tm, nb),
        in_specs=[pl.BlockSpec((tm, k), lambda m, j: (m, 0))] + p_specs + [
            pl.BlockSpec((None, k, tn), lambda m, j: (layer, 0, j)),
            pl.BlockSpec((None, k, tn), lambda m, j: (layer, 0, j + nb)),
            pl.BlockSpec((3, tn), lambda m, j: (0, j)),
            pl.BlockSpec((1, tn), lambda m, j: (0, j))],
        out_specs=[pl.BlockSpec((tm, tn), lambda m, j: (m, j)),
                   pl.BlockSpec((None, hrows, tn), lambda m, j: (m, 0, j))],
        out_shape=[jax.ShapeDtypeStruct((r, D_FF), BF16),
                   jax.ShapeDtypeStruct((r // tm, hrows, D_FF), F32)],
        compiler_params=_params("parallel", "arbitrary"),
        name="ffn_up",
    )(h, p1, p2, w_up, w_up, dw, db.reshape(1, D_FF))


N_KV_SLABS = KV_GROUPS * 2


def _cmp_partial_slab(x_ref, c, pe_ref, w1_ref, n_chunks, first=0, pitch=1, chunk_pitch=None):
    chunk_pitch = CMP_STRIDE * pitch if chunk_pitch is None else chunk_pitch
    acc_a = jnp.zeros((n_chunks, HEAD_DIM), F32)
    acc_b = jnp.zeros((n_chunks, HEAD_DIM), F32)
    for s in range(CMP_STRIDE):
        xs = x_ref[pl.ds(first + s * pitch, n_chunks, stride=chunk_pitch), :]
        xa = (xs + pe_ref[c, s:s + 1, :]).astype(BF16)
        xb = (xs + pe_ref[c, CMP_STRIDE + s:CMP_STRIDE + s + 1, :]).astype(BF16)
        acc_a += jnp.dot(xa, w1_ref[c, s], preferred_element_type=F32)
        acc_b += jnp.dot(xb, w1_ref[c, CMP_STRIDE + s], preferred_element_type=F32)
    return acc_a, acc_b


def _cmp1_kernel(x_ref, pe_ref, w1_ref, a_ref, b_ref, *, n_chunks):
    c = pl.program_id(1) % 2
    a_ref[...], b_ref[...] = _cmp_partial_slab(x_ref, c, pe_ref, w1_ref, n_chunks)


def cmp_partial_rows(x, pe, w1, rows):
    nb = x.shape[0] // rows
    n_chunks = rows // CMP_STRIDE
    out = jax.ShapeDtypeStruct((nb, n_chunks, KV_COLS), F32)
    return pl.pallas_call(
        functools.partial(_cmp1_kernel, n_chunks=n_chunks),
        grid=(nb, N_KV_SLABS),
        in_specs=[pl.BlockSpec((rows, HEAD_DIM), lambda b, sl: (b, sl)),
                  pl.BlockSpec(pe.shape, lambda b, sl: (0, 0, 0)),
                  pl.BlockSpec(w1.shape, lambda b, sl: (0, 0, 0, 0))],
        out_specs=[pl.BlockSpec((None, n_chunks, HEAD_DIM), lambda b, sl: (b, 0, sl))] * 2,
        out_shape=[out, out],
        compiler_params=_params("parallel", "arbitrary"),
        name="cmp_partial",
    )(x, pe, w1)


PAGES_PER_STEP = 16
CHUNKS_PER_PAGE = PAGE_SIZE // CMP_STRIDE
CHUNK_ROWS = CMP_STRIDE * N_KV_SLABS
CHUNK_PITCH = CHUNK_ROWS + 4
STEP_CHUNKS = PAGES_PER_STEP * CHUNKS_PER_PAGE


def _cmp1_paged_kernel(pt_ref, *refs):
    page_refs = refs[:PAGES_PER_STEP]
    pe_ref, w1_ref, a_ref, b_ref, buf_ref = refs[PAGES_PER_STEP:]
    for i, p_ref in enumerate(page_refs):
        for ch in range(CHUNKS_PER_PAGE):
            dst = (i * CHUNKS_PER_PAGE + ch) * CHUNK_PITCH
            buf_ref[dst:dst + CHUNK_ROWS, :] = p_ref[ch * CHUNK_ROWS:(ch + 1) * CHUNK_ROWS, :]
    for slab in range(N_KV_SLABS):
        cols = slice(slab * HEAD_DIM, (slab + 1) * HEAD_DIM)
        a_ref[:, cols], b_ref[:, cols] = _cmp_partial_slab(buf_ref, slab % 2, pe_ref, w1_ref, STEP_CHUNKS,
                                                           first=slab, pitch=N_KV_SLABS, chunk_pitch=CHUNK_PITCH)


def cmp_partial_paged(pool, n_phys, page_table, layer, pe, w1):
    bsz, n_pages = page_table.shape
    n_steps = n_pages // PAGES_PER_STEP
    n_chunks = STEP_CHUNKS
    out = jax.ShapeDtypeStruct((bsz, n_steps, n_chunks, KV_COLS), F32)

    def page_spec(i):
        return pl.BlockSpec((None, PAGE_SIZE * N_KV_SLABS, HEAD_DIM),
                            lambda b, s, pt: (layer * n_phys + pt[b, s * PAGES_PER_STEP + i], 0, 0))

    out_spec = pl.BlockSpec((None, None, n_chunks, KV_COLS), lambda b, s, pt: (b, s, 0, 0))
    a, b = pl.pallas_call(
        _cmp1_paged_kernel,
        grid_spec=pltpu.PrefetchScalarGridSpec(
            num_scalar_prefetch=1,
            grid=(bsz, n_steps),
            in_specs=[page_spec(i) for i in range(PAGES_PER_STEP)] + [
                pl.BlockSpec(pe.shape, lambda b, s, pt: (0, 0, 0)),
                pl.BlockSpec(w1.shape, lambda b, s, pt: (0, 0, 0, 0))],
            out_specs=[out_spec, out_spec],
            scratch_shapes=[pltpu.VMEM((STEP_CHUNKS * CHUNK_PITCH, HEAD_DIM), F32)]),
        out_shape=[out, out],
        compiler_params=_params("parallel", "arbitrary"),
        name="cmp_partial_paged",
    )(page_table, *([pool] * PAGES_PER_STEP), pe, w1)
    return a.reshape(bsz, -1, KV_COLS), b.reshape(bsz, -1, KV_COLS)


def _cmp2_kernel(a_ref, b_ref, bx_ref, w2_ref, o_ref):
    n = a_ref.shape[0]
    row = lax.broadcasted_iota(jnp.int32, (n, HEAD_DIM), 0)
    for slab in range(KV_GROUPS * 2):
        c = slab % 2
        cols = slice(slab * HEAD_DIM, (slab + 1) * HEAD_DIM)
        nxt = pltpu.roll(b_ref[:, cols], n - 1, 0)
        nxt = jnp.where(row == n - 1, bx_ref[0:1, cols], nxt)
        h = jax.nn.gelu(a_ref[:, cols] + nxt).astype(BF16)
        o_ref[:, cols] = jnp.dot(h, w2_ref[c], preferred_element_type=F32).astype(o_ref.dtype)


def cmp_finish(a, b, b_next, w2):
    nb, n, _ = a.shape
    return pl.pallas_call(
        _cmp2_kernel,
        grid=(nb,),
        in_specs=[pl.BlockSpec((None, n, KV_COLS), lambda i: (i, 0, 0)),
                  pl.BlockSpec((None, n, KV_COLS), lambda i: (i, 0, 0)),
                  pl.BlockSpec((None, SUBLANES, KV_COLS), lambda i: (i, 0, 0)),
                  pl.BlockSpec(w2.shape, lambda i: (0, 0, 0))],
        out_specs=pl.BlockSpec((None, n, KV_COLS), lambda i: (i, 0, 0)),
        out_shape=jax.ShapeDtypeStruct((nb, n, KV_COLS), BF16),
        compiler_params=_params("parallel"),
        name="cmp_finish",
    )(a, b, b_next, w2)


def _softmax_rows(s, mask):
    s = jnp.where(mask, s, NEG)
    m = jnp.max(s, axis=-1, keepdims=True)
    e = jnp.where(mask, jnp.exp(s - m), 0.0)
    l = jnp.sum(e, axis=-1, keepdims=True)
    return jnp.where(l > 0.0, e / jnp.where(l > 0.0, l, 1.0), 0.0)


def _cmp_attn_kernel(q_ref, kc_ref, bias_ref, msel_ref, exp_ref, o_ref, mask_ref, *, n_sel):
    qt = pl.program_id(1)
    tq, n_cmp = q_ref.shape[0], kc_ref.shape[0]
    qpos = qt * tq + lax.broadcasted_iota(jnp.int32, (tq, n_cmp), 0)
    cpos = lax.broadcasted_iota(jnp.int32, (tq, n_cmp), 1) * CMP_STRIDE + (CMP_BLOCK - 1)
    cadd = jnp.where(cpos <= qpos, 0.0, NEG)
    j = lax.broadcasted_iota(jnp.int32, (n_sel, tq), 0)
    cur = (qt * tq + lax.broadcasted_iota(jnp.int32, (n_sel, tq), 1)) // SEL_BLOCK
    valid = j <= cur
    forced = (j == 0) | (j == cur) | (j == cur - 1)
    pad_rows = jnp.zeros((exp_ref.shape[1] - n_sel, tq), F32)
    diag = pl.ds(pl.multiple_of(qt * tq, tq), tq)
    causal = lax.broadcasted_iota(jnp.int32, (tq, tq), 0) <= lax.broadcasted_iota(jnp.int32, (tq, tq), 1)
    for g in range(KV_GROUPS):
        k = kc_ref[:, g * 2 * HEAD_DIM:(g * 2 + 1) * HEAD_DIM]
        v = kc_ref[:, (g * 2 + 1) * HEAD_DIM:(g * 2 + 2) * HEAD_DIM]
        heads = range(g * GROUP_SIZE, (g + 1) * GROUP_SIZE)
        q4 = jnp.concatenate([q_ref[:, h * HEAD_DIM:(h + 1) * HEAD_DIM] for h in heads], axis=0)
        badd = jnp.concatenate([bias_ref[h] + cadd for h in heads], axis=0)
        s = lax.dot_general(q4, k, NT_DIMS, preferred_element_type=F32) * SCALE + badd
        p = _softmax_rows(s, s > 0.5 * NEG)
        o4 = jnp.dot(p.astype(BF16), v, preferred_element_type=F32)
        imp = jnp.zeros((tq, n_cmp), F32)
        for r, h in enumerate(heads):
            o_ref[:, h * HEAD_DIM:(h + 1) * HEAD_DIM] = o4[r * tq:(r + 1) * tq, :]
            imp = imp + p[r * tq:(r + 1) * tq, :]
        p_sel = lax.dot_general(msel_ref[...], imp, NT_DIMS, precision=HIGHEST, preferred_element_type=F32)
        score = jnp.where(valid, p_sel + jnp.where(forced, FORCE_BONUS, 0.0), -jnp.inf)
        rank = jnp.zeros((n_sel, tq), jnp.int32)
        for i in range(n_sel):
            si = score[i:i + 1, :]
            beats = (si > score) | ((si == score) & (i < j))
            rank = rank + beats.astype(jnp.int32)
        sel_t = jnp.where((rank < N_SELECT) & valid, 1.0, 0.0)
        sel = jnp.concatenate([sel_t, pad_rows], axis=0).astype(BF16)
        allowed = jnp.dot(exp_ref[...], sel, preferred_element_type=F32)
        mask_ref[g] = ((allowed - 1.0) * -NEG).astype(mask_ref.dtype)
        allowed_d = jnp.dot(exp_ref[diag, :], sel, preferred_element_type=F32)
        mask_ref[g, diag, :] = jnp.where(causal, (allowed_d - 1.0) * -NEG, NEG).astype(mask_ref.dtype)


def cmp_attention(q, kc, bias_cmp, msel, expand, bsz, seq):
    tq = ATT_TILE
    nqt = seq // tq
    n_cmp = kc.shape[1]
    n_sel = msel.shape[0]
    return pl.pallas_call(
        functools.partial(_cmp_attn_kernel, n_sel=n_sel),
        grid=(bsz, nqt),
        in_specs=[pl.BlockSpec((tq, D_MODEL), lambda b, t: (b * nqt + t, 0)),
                  pl.BlockSpec((None, n_cmp, KV_COLS), lambda b, t: (b, 0, 0)),
                  pl.BlockSpec((N_HEADS, tq, n_cmp), lambda b, t: (0, t, 0)),
                  pl.BlockSpec(msel.shape, lambda b, t: (0, 0)),
                  pl.BlockSpec(expand.shape, lambda b, t: (0, 0))],
        out_specs=[pl.BlockSpec((tq, D_MODEL), lambda b, t: (b * nqt + t, 0)),
                   pl.BlockSpec((None, KV_GROUPS, seq, tq), lambda b, t: (b, 0, 0, t))],
        out_shape=[jax.ShapeDtypeStruct((bsz * seq, D_MODEL), F32),
                   jax.ShapeDtypeStruct((bsz, KV_GROUPS, seq, seq), BF16)],
        compiler_params=_params("parallel", "arbitrary"),
        name="cmp_attention",
    )(q, kc, bias_cmp, msel, expand)


SLC_KEYS = 1024
SLC_SUB = SLC_KEYS // ATT_TILE
SLC_QUERIES = 256


def _slc_kernel(qt_ref, ks_ref, q_ref, k_ref, vt_ref, tbt_ref, mask_ref, o_ref, m_ref, l_ref, acc_ref):
    g, pair = pl.program_id(1), pl.program_id(2)
    qt, ks = qt_ref[pair], ks_ref[pair]
    tq = q_ref.shape[0]

    @pl.when(ks == 0)
    def _():
        m_ref[...] = jnp.full_like(m_ref, NEG)
        l_ref[...] = jnp.zeros_like(l_ref)
        acc_ref[...] = jnp.zeros_like(acc_ref)

    q4 = jnp.concatenate([q_ref[:, r * HEAD_DIM:(r + 1) * HEAD_DIM] for r in range(GROUP_SIZE)], axis=0)
    rows = []
    q_sub = tq // ATT_TILE
    for c in range(SLC_SUB):
        tbi = [jnp.clip(qt * q_sub + qs - (ks * SLC_SUB + c), 0, 2) for qs in range(q_sub)]
        madd = [mask_ref[c * ATT_TILE:(c + 1) * ATT_TILE, qs * ATT_TILE:(qs + 1) * ATT_TILE].astype(F32)
                for qs in range(q_sub)]
        rows.append(jnp.concatenate([tbt_ref[tbi[qs], g * GROUP_SIZE + r] + madd[qs]
                                     for r in range(GROUP_SIZE) for qs in range(q_sub)], axis=1))
    st = (lax.dot_general(k_ref[...].astype(BF16), q4, NT_DIMS, preferred_element_type=F32) * SCALE
          + jnp.concatenate(rows, axis=0))
    m_prev = m_ref[...]
    m_new = jnp.maximum(m_prev, jnp.max(st, axis=0, keepdims=True))
    alpha = jnp.exp(m_prev - m_new)
    pt = jnp.exp(st - m_new)
    l_ref[...] = alpha * l_ref[...] + jnp.sum(pt, axis=0, keepdims=True)
    acc_ref[...] = alpha * acc_ref[...] + jnp.dot(vt_ref[...], pt.astype(BF16), preferred_element_type=F32)
    m_ref[...] = m_new

    @pl.when(ks == ((qt + 1) * tq - 1) // SLC_KEYS)
    def _():
        ot = acc_ref[...] / l_ref[...]
        for r in range(GROUP_SIZE):
            o_ref[:, r * HEAD_DIM:(r + 1) * HEAD_DIM] = ot[:, r * tq:(r + 1) * tq].T


def slc_attention(q, kv, vt, mask, tbt, bsz, seq):
    tq = SLC_QUERIES
    nqt = seq // tq
    nks = seq // SLC_KEYS
    gw = GROUP_SIZE * HEAD_DIM
    pairs = [(t, s) for t in range(nqt) for s in range(((t + 1) * tq - 1) // SLC_KEYS + 1)]
    qt_of = jnp.asarray([p[0] for p in pairs], jnp.int32)
    ks_of = jnp.asarray([p[1] for p in pairs], jnp.int32)
    return pl.pallas_call(
        _slc_kernel,
        grid_spec=pltpu.PrefetchScalarGridSpec(
            num_scalar_prefetch=2,
            grid=(bsz, KV_GROUPS, len(pairs)),
            in_specs=[pl.BlockSpec((tq, gw), lambda b, g, p, qt, ks: (b * nqt + qt[p], g)),
                      pl.BlockSpec((SLC_KEYS, HEAD_DIM), lambda b, g, p, qt, ks: (b * nks + ks[p], 2 * g)),
                      pl.BlockSpec((HEAD_DIM, SLC_KEYS), lambda b, g, p, qt, ks: (g, b * nks + ks[p])),
                      pl.BlockSpec(tbt.shape, lambda b, g, p, qt, ks: (0, 0, 0, 0)),
                      pl.BlockSpec((None, None, SLC_KEYS, tq), lambda b, g, p, qt, ks: (b, g, ks[p], qt[p]))],
            out_specs=pl.BlockSpec((tq, gw), lambda b, g, p, qt, ks: (b * nqt + qt[p], g)),
            scratch_shapes=[pltpu.VMEM((1, GROUP_SIZE * tq), F32), pltpu.VMEM((1, GROUP_SIZE * tq), F32),
                            pltpu.VMEM((HEAD_DIM, GROUP_SIZE * tq), F32)]),
        out_shape=jax.ShapeDtypeStruct((bsz * seq, D_MODEL), F32),
        compiler_params=_params("parallel", "parallel", "arbitrary"),
        name="slc_attention",
    )(qt_of, ks_of, q, kv, vt, tbt, mask)


WIN_TILES = WINDOW // ATT_TILE + 1
WIN_SPAN = WIN_TILES * ATT_TILE


def _win_kernel(q_ref, *refs):
    kv_refs, (bias_ref, oc_ref, os_ref, gate_ref, o_ref) = refs[:WIN_TILES], refs[WIN_TILES:]
    gate = gate_ref[...]
    qt = pl.program_id(1)
    tq = q_ref.shape[0]
    row = lax.broadcasted_iota(jnp.int32, (tq, WIN_SPAN), 0)
    col = lax.broadcasted_iota(jnp.int32, (tq, WIN_SPAN), 1)
    back = row + WINDOW - col
    mask = (back >= 0) & (back <= WINDOW) & (qt * tq - WINDOW + col >= 0)
    madd = jnp.where(mask, 0.0, NEG)
    for g in range(KV_GROUPS):
        k = jnp.concatenate([r[:, g * 2 * HEAD_DIM:(g * 2 + 1) * HEAD_DIM].astype(BF16) for r in kv_refs], axis=0)
        v = jnp.concatenate([r[:, (g * 2 + 1) * HEAD_DIM:(g * 2 + 2) * HEAD_DIM].astype(BF16) for r in kv_refs], axis=0)
        heads = range(g * GROUP_SIZE, (g + 1) * GROUP_SIZE)
        q4 = jnp.concatenate([q_ref[:, h * HEAD_DIM:(h + 1) * HEAD_DIM] for h in heads], axis=0)
        badd = jnp.concatenate([bias_ref[h] + madd for h in heads], axis=0)
        s = lax.dot_general(q4, k, NT_DIMS, preferred_element_type=F32) * SCALE + badd
        e = jnp.exp(s - jnp.max(s, axis=-1, keepdims=True))
        l = jnp.sum(e, axis=-1, keepdims=True)
        o = jnp.dot(e.astype(BF16), v, preferred_element_type=F32) / l
        for r, h in enumerate(heads):
            cols = slice(h * HEAD_DIM, (h + 1) * HEAD_DIM)
            merged = (oc_ref[:, cols] * gate[:, 3 * h:3 * h + 1] + os_ref[:, cols] * gate[:, 3 * h + 1:3 * h + 2]
                      + o[r * tq:(r + 1) * tq, :] * gate[:, 3 * h + 2:3 * h + 3])
            o_ref[:, cols] = merged.astype(o_ref.dtype)


def win_attention(q, kv, bias_win, o_cmp, o_slc, gate, bsz, seq):
    tq = ATT_TILE
    nqt = seq // tq
    row_spec = pl.BlockSpec((tq, D_MODEL), lambda b, t: (b * nqt + t, 0))

    def kv_spec(i):
        return pl.BlockSpec((tq, KV_COLS), lambda b, t: (b * nqt + jnp.maximum(t - (WIN_TILES - 1) + i, 0), 0))

    return pl.pallas_call(
        _win_kernel,
        grid=(bsz, nqt),
        in_specs=[row_spec]
        + [kv_spec(i) for i in range(WIN_TILES)]
        + [pl.BlockSpec(bias_win.shape, lambda b, t: (0, 0, 0)), row_spec, row_spec,
           pl.BlockSpec((tq, LANES), lambda b, t: (b * nqt + t, 0))],
        out_specs=row_spec,
        out_shape=jax.ShapeDtypeStruct((bsz * seq, D_MODEL), BF16),
        compiler_params=_params("parallel", "arbitrary"),
        name="win_attention",
    )(q, *([kv] * WIN_TILES), bias_win, o_cmp, o_slc, gate)


def _combine_kernel(oc_ref, os_ref, ow_ref, g_ref, o_ref):
    gate = g_ref[...]
    for h in range(N_HEADS):
        cols = slice(h * HEAD_DIM, (h + 1) * HEAD_DIM)
        o = (oc_ref[:, cols] * gate[:, 3 * h:3 * h + 1] + os_ref[:, cols] * gate[:, 3 * h + 1:3 * h + 2]
             + ow_ref[:, cols] * gate[:, 3 * h + 2:3 * h + 3])
        o_ref[:, cols] = o.astype(o_ref.dtype)


def nsa_combine(o_cmp, o_slc, o_win, gate, tm):
    r = o_cmp.shape[0]
    spec = pl.BlockSpec((tm, D_MODEL), lambda m: (m, 0))
    return pl.pallas_call(
        _combine_kernel,
        grid=(r // tm,),
        in_specs=[spec, spec, spec, pl.BlockSpec((tm, LANES), lambda m: (m, 0))],
        out_specs=spec,
        out_shape=jax.ShapeDtypeStruct((r, D_MODEL), BF16),
        compiler_params=_params("parallel"),
        name="nsa_combine",
    )(o_cmp, o_slc, o_win, gate)


N_SEL_S = -(-(PAST_LEN + 1) // SEL_BLOCK)
N_SEL_S_PAD = 384
N_CMP_S = PAST_LEN // CMP_STRIDE


def _group_rows(parts, hgrp):
    out = parts[0]
    for g in range(1, KV_GROUPS):
        out = jnp.where(hgrp == g, parts[g], out)
    return out


def _sample_attn_kernel(q_ref, kc_ref, bc_ref, gsum_ref, msel_ref, win_ref, new_ref, bw_ref, bn_ref,
                        oc_ref, ow_ref, idx_ref):
    q = q_ref[...]
    hgrp = lax.broadcasted_iota(jnp.int32, (N_HEADS, 1), 0) // GROUP_SIZE

    def kcol(g):
        return slice(g * 2 * HEAD_DIM, (g * 2 + 1) * HEAD_DIM)

    def vcol(g):
        return slice((g * 2 + 1) * HEAD_DIM, (g * 2 + 2) * HEAD_DIM)

    s = _group_rows([lax.dot_general(q, kc_ref[:, kcol(g)], NT_DIMS, preferred_element_type=F32)
                     for g in range(KV_GROUPS)], hgrp)
    s = s * SCALE + bc_ref[...]
    n = lax.broadcasted_iota(jnp.int32, s.shape, 1)
    p = _softmax_rows(s, n * CMP_STRIDE + (CMP_BLOCK - 1) <= PAST_LEN)
    pb = p.astype(BF16)
    oc_ref[...] = _group_rows([jnp.dot(pb, kc_ref[:, vcol(g)], preferred_element_type=F32)
                               for g in range(KV_GROUPS)], hgrp)
    imp = jnp.dot(gsum_ref[...], p, precision=HIGHEST, preferred_element_type=F32)
    p_sel = jnp.dot(imp, msel_ref[...], precision=HIGHEST, preferred_element_type=F32)
    j = lax.broadcasted_iota(jnp.int32, p_sel.shape, 1)
    cur = PAST_LEN // SEL_BLOCK
    forced = (j == 0) | (j == cur) | (j == cur - 1)
    score = jnp.where(j <= cur, p_sel + jnp.where(forced, FORCE_BONUS, 0.0), -jnp.inf)
    lane = lax.broadcasted_iota(jnp.int32, idx_ref.shape, 1)
    idx = jnp.zeros(idx_ref.shape, F32)
    jf = j.astype(F32)
    for kk in range(N_SELECT):
        mx = jnp.max(score, axis=-1, keepdims=True)
        pick = jnp.min(jnp.where(score == mx, jf, float(N_SEL_S_PAD)), axis=-1, keepdims=True)
        idx = jnp.where(lane == kk, pick, idx)
        score = jnp.where(jf == pick, -jnp.inf, score)
    idx_ref[...] = idx.astype(jnp.int32)
    wb = win_ref.shape[0] // N_KV_SLABS

    def win_slab(slab):
        return win_ref[pl.ds(slab, wb, stride=N_KV_SLABS), :].astype(BF16)

    sw = _group_rows([lax.dot_general(q, win_slab(2 * g), NT_DIMS, preferred_element_type=F32)
                      for g in range(KV_GROUPS)], hgrp)
    sw = sw * SCALE + bw_ref[...]
    qf = q.astype(F32)
    sn = _group_rows([jnp.sum(qf * new_ref[:, kcol(g)].astype(BF16).astype(F32), axis=-1, keepdims=True)
                      for g in range(KV_GROUPS)], hgrp)
    sn = sn * SCALE + bn_ref[:, 0:1]
    m = jnp.maximum(jnp.max(sw, axis=-1, keepdims=True), sn)
    ew, en = jnp.exp(sw - m), jnp.exp(sn - m)
    l = jnp.sum(ew, axis=-1, keepdims=True) + en
    pw, pn = (ew / l).astype(BF16), (en / l).astype(BF16).astype(F32)
    ow = _group_rows([jnp.dot(pw, win_slab(2 * g + 1), preferred_element_type=F32)
                      + pn * new_ref[:, vcol(g)].astype(BF16).astype(F32) for g in range(KV_GROUPS)], hgrp)
    ow_ref[...] = ow


def sample_attention(q3, kc, bias_c, gsum, msel, win_pool, layer, kv_win_new, bias_w, bias_new):
    bsz = q3.shape[0]
    wb = win_pool.shape[1]
    o = jax.ShapeDtypeStruct((bsz, N_HEADS, HEAD_DIM), F32)
    full2 = lambda b: (0, 0)
    return pl.pallas_call(
        _sample_attn_kernel,
        grid=(bsz,),
        in_specs=[pl.BlockSpec((None, N_HEADS, HEAD_DIM), lambda b: (b, 0, 0)),
                  pl.BlockSpec((None, N_CMP_S, KV_COLS), lambda b: (b, 0, 0)),
                  pl.BlockSpec(bias_c.shape, full2),
                  pl.BlockSpec(gsum.shape, full2),
                  pl.BlockSpec(msel.shape, full2),
                  pl.BlockSpec((None, wb, HEAD_DIM), lambda b: (layer * bsz + b, 0, 0)),
                  pl.BlockSpec((None, 1, KV_COLS), lambda b: (b, 0, 0)),
                  pl.BlockSpec(bias_w.shape, full2),
                  pl.BlockSpec(bias_new.shape, full2)],
        out_specs=[pl.BlockSpec((None, N_HEADS, HEAD_DIM), lambda b: (b, 0, 0)),
                   pl.BlockSpec((None, N_HEADS, HEAD_DIM), lambda b: (b, 0, 0)),
                   pl.BlockSpec((None, SUBLANES, LANES), lambda b: (b, 0, 0))],
        out_shape=[o, o, jax.ShapeDtypeStruct((bsz, SUBLANES, LANES), jnp.int32)],
        compiler_params=_params("parallel"),
        name="sample_attention",
    )(q3, kc, bias_c, gsum, msel, win_pool, kv_win_new, bias_w, bias_new)


def _sample_slc_kernel(idx_ref, pt_ref, q_ref, *refs):
    blk_refs, (new_ref, bias_ref, o_ref) = refs[:N_SELECT], refs[N_SELECT:]
    b, g = pl.program_id(0), pl.program_id(1)
    ks, vs, biases = [], [], []
    lane = lax.broadcasted_iota(jnp.int32, (N_HEADS, SEL_BLOCK), 1)
    for kk, blk_ref in enumerate(blk_refs):
        j = idx_ref[b, g, kk]
        is_new = j >= PAST_LEN // SEL_BLOCK
        k_blk = blk_ref[pl.ds(2 * g, SEL_BLOCK, stride=N_KV_SLABS), :]
        v_blk = blk_ref[pl.ds(2 * g + 1, SEL_BLOCK, stride=N_KV_SLABS), :]
        ks.append(jnp.where(is_new, jnp.broadcast_to(new_ref[:, :HEAD_DIM], k_blk.shape), k_blk).astype(BF16))
        vs.append(jnp.where(is_new, jnp.broadcast_to(new_ref[:, HEAD_DIM:], v_blk.shape), v_blk).astype(BF16))
        biases.append(bias_ref[j] + jnp.where(j * SEL_BLOCK + lane <= PAST_LEN, 0.0, NEG))
    k, v = jnp.concatenate(ks, axis=0), jnp.concatenate(vs, axis=0)
    s = lax.dot_general(q_ref[...], k, NT_DIMS, preferred_element_type=F32) * SCALE + jnp.concatenate(biases, axis=1)
    p = _softmax_rows(s, s > 0.5 * NEG)
    o_ref[...] = jnp.dot(p.astype(BF16), v, preferred_element_type=F32)


def sample_slc_attention(idx, page_table, q3, pool, n_phys, layer, kv_slc_new, bias_blk):
    bsz = q3.shape[0]
    half_per_page = PAGE_SIZE // SEL_BLOCK
    n_half = n_phys * half_per_page
    last_past = PAST_LEN // SEL_BLOCK - 1

    def blk_spec(kk):
        def blk_map(b, g, idx_r, pt_r):
            j = jnp.minimum(idx_r[b, g, kk], last_past)
            return (layer * n_half + pt_r[b, j // half_per_page] * half_per_page + j % half_per_page, 0, 0)
        return pl.BlockSpec((None, SEL_BLOCK * N_KV_SLABS, HEAD_DIM), blk_map)

    return pl.pallas_call(
        _sample_slc_kernel,
        grid_spec=pltpu.PrefetchScalarGridSpec(
            num_scalar_prefetch=2,
            grid=(bsz, KV_GROUPS),
            in_specs=[pl.BlockSpec((None, N_HEADS, HEAD_DIM), lambda b, g, i, p: (b, 0, 0))]
            + [blk_spec(kk) for kk in range(N_SELECT)]
            + [pl.BlockSpec((None, 1, 2 * HEAD_DIM), lambda b, g, i, p: (b, 0, g)),
               pl.BlockSpec(bias_blk.shape, lambda b, g, i, p: (0, 0, 0))],
            out_specs=pl.BlockSpec((None, None, N_HEADS, HEAD_DIM), lambda b, g, i, p: (b, g, 0, 0))),
        out_shape=jax.ShapeDtypeStruct((bsz, KV_GROUPS, N_HEADS, HEAD_DIM), F32),
        compiler_params=_params("parallel", "arbitrary"),
        name="sample_slc_attention",
    )(idx, page_table, q3, *([pool] * N_SELECT), kv_slc_new, bias_blk)


CONV_TILE = 128
CONV_HALO = 32
CONV_ROWS = 64


def _dwconv_ln_kernel(u_ref, halo_ref, w_ref, b_ref, g_ref, beta_ref, o_ref, buf_ref, y_ref, sh_ref, *, tiles_per_seq):
    first = pl.program_id(0) % tiles_per_seq == 0
    buf_ref[0:CONV_HALO, :] = jnp.where(first, 0.0, halo_ref[...])
    buf_ref[CONV_HALO:, :] = u_ref[...]
    lead = CONV_HALO - (CONV_WIDTH - 1)
    for r0 in range(0, CONV_TILE, CONV_ROWS):
        for c0 in range(0, D_MODEL, LANES):
            cols = slice(c0, c0 + LANES)
            acc = jnp.broadcast_to(b_ref[:, cols], (CONV_ROWS, LANES))
            for res in range(SUBLANES):
                taps = [k for k in range(CONV_WIDTH) if (lead + k) % SUBLANES == res]
                span = max(lead + k - res for k in taps) + CONV_ROWS
                sh_ref[0:span, :] = buf_ref[r0 + res:r0 + res + span, cols]
                for k in taps:
                    off = lead + k - res
                    acc = acc + w_ref[k:k + 1, cols] * sh_ref[off:off + CONV_ROWS, :]
            y_ref[r0:r0 + CONV_ROWS, cols] = acc
    y = y_ref[...]
    mu = jnp.mean(y, axis=-1, keepdims=True)
    var = jnp.mean(jnp.square(y - mu), axis=-1, keepdims=True)
    y = (y - mu) * lax.rsqrt(var + LN_EPS) * g_ref[...] + beta_ref[...]
    o_ref[...] = (y * jax.nn.sigmoid(y)).astype(o_ref.dtype)


def dwconv_ln(u, w, b, ln_g, ln_b, seq):
    r, d = u.shape
    tiles_per_seq = seq // CONV_TILE
    ratio = CONV_TILE // CONV_HALO
    vec = lambda m: (0, 0)
    return pl.pallas_call(
        functools.partial(_dwconv_ln_kernel, tiles_per_seq=tiles_per_seq),
        grid=(r // CONV_TILE,),
        in_specs=[pl.BlockSpec((CONV_TILE, d), lambda m: (m, 0)),
                  pl.BlockSpec((CONV_HALO, d), lambda m: (jnp.maximum(m * ratio - 1, 0), 0)),
                  pl.BlockSpec((CONV_WIDTH, d), vec), pl.BlockSpec((1, d), vec),
                  pl.BlockSpec((1, d), vec), pl.BlockSpec((1, d), vec)],
        out_specs=pl.BlockSpec((CONV_TILE, d), lambda m: (m, 0)),
        out_shape=jax.ShapeDtypeStruct((r, d), BF16),
        scratch_shapes=[pltpu.VMEM((CONV_HALO + CONV_TILE, d), F32), pltpu.VMEM((CONV_TILE, d), F32),
                        pltpu.VMEM((CONV_HALO + CONV_ROWS, LANES), F32)],
        compiler_params=_params("parallel"),
        name="dwconv_ln",
    )(u, u, w, b.reshape(1, d), ln_g.reshape(1, d), ln_b.reshape(1, d))


def _dwconv_ln_decode_kernel(u_ref, hist_ref, w_ref, b_ref, g_ref, beta_ref, o_ref):
    y = b_ref[...] + w_ref[CONV_WIDTH - 1:CONV_WIDTH, :] * u_ref[...]
    for k in range(CONV_WIDTH - 1):
        y = y + w_ref[k:k + 1, :] * hist_ref[k]
    mu = jnp.mean(y, axis=-1, keepdims=True)
    var = jnp.mean(jnp.square(y - mu), axis=-1, keepdims=True)
    y = (y - mu) * lax.rsqrt(var + LN_EPS) * g_ref[...] + beta_ref[...]
    o_ref[...] = (y * jax.nn.sigmoid(y)).astype(o_ref.dtype)


def dwconv_ln_decode(u, hist_t, w, b, ln_g, ln_b):
    r, d = u.shape
    vec = lambda i: (0, 0)
    return pl.pallas_call(
        _dwconv_ln_decode_kernel,
        grid=(1,),
        in_specs=[pl.BlockSpec((r, d), vec), pl.BlockSpec(hist_t.shape, lambda i: (0, 0, 0)),
                  pl.BlockSpec((CONV_WIDTH, d), vec), pl.BlockSpec((1, d), vec),
                  pl.BlockSpec((1, d), vec), pl.BlockSpec((1, d), vec)],
        out_specs=pl.BlockSpec((r, d), vec),
        out_shape=jax.ShapeDtypeStruct((r, d), BF16),
        compiler_params=_params("arbitrary"),
        name="dwconv_ln_decode",
    )(u, hist_t, w, b.reshape(1, d), ln_g.reshape(1, d), ln_b.reshape(1, d))


S5_CHUNK = 256
S5_PITCH = S5_CHUNK + 4
IN_SLABS = D_MODEL // LANES
STATE_PER_IN = N_SLAB // IN_SLABS


def _s5_project_in(hb, wb_ref, store):
    half = STATE_PER_IN * LANES
    for i in range(IN_SLABS):
        res = jnp.dot(hb[:, i * LANES:(i + 1) * LANES], wb_ref[i], preferred_element_type=F32)
        for jj in range(STATE_PER_IN):
            store(i * STATE_PER_IN + jj, res[:, jj * LANES:(jj + 1) * LANES],
                  res[:, half + jj * LANES:half + (jj + 1) * LANES])


def _s5_project_out(load, cre_ref, cim_ref, hn, d_ref, y_ref):
    for i in range(IN_SLABS):
        cols = slice(i * LANES, (i + 1) * LANES)
        acc = d_ref[:, cols] * hn[:, cols]
        for jj in range(STATE_PER_IN):
            j = i * STATE_PER_IN + jj
            re, im = load(j)
            acc = acc + jnp.dot(re.astype(BF16), cre_ref[j], preferred_element_type=F32)
            acc = acc - jnp.dot(im.astype(BF16), cim_ref[j], preferred_element_type=F32)
        y_ref[:, cols] = acc.astype(y_ref.dtype)


def _s5_scan_kernel(x_ref, g_ref, wb_ref, ar_ref, ai_ref, cre_ref, cim_ref, d_ref,
                    y_ref, sr_ref, si_ref, bur_ref, bui_ref, hr_ref, hi_ref):
    tc, pitch = S5_CHUNK, S5_PITCH

    @pl.when(pl.program_id(1) == 0)
    def _():
        hr_ref[...] = jnp.zeros_like(hr_ref)
        hi_ref[...] = jnp.zeros_like(hi_ref)

    hn = _rms(x_ref[...], g_ref[...])

    def store(j, re, im):
        bur_ref[j * pitch:j * pitch + tc, :] = re
        bui_ref[j * pitch:j * pitch + tc, :] = im

    _s5_project_in(hn.astype(BF16), wb_ref, store)

    ar = [ar_ref[j8] for j8 in range(SLAB_GRP)]
    ai = [ai_ref[j8] for j8 in range(SLAB_GRP)]

    def step(t, carry):
        out = []
        for j8 in range(SLAB_GRP):
            hr, hi = carry[2 * j8], carry[2 * j8 + 1]
            rows = pl.ds(j8 * SUBLANES * pitch + t, SUBLANES, stride=pitch)
            nr = ar[j8] * hr - ai[j8] * hi + bur_ref[rows, :]
            ni = ar[j8] * hi + ai[j8] * hr + bui_ref[rows, :]
            bur_ref[rows, :] = nr
            bui_ref[rows, :] = ni
            out += [nr, ni]
        return tuple(out)

    init = []
    for j8 in range(SLAB_GRP):
        init += [hr_ref[j8], hi_ref[j8]]
    fin = lax.fori_loop(0, tc, step, tuple(init))
    for j8 in range(SLAB_GRP):
        hr_ref[j8] = fin[2 * j8]
        hi_ref[j8] = fin[2 * j8 + 1]
    sr_ref[...] = hr_ref[...]
    si_ref[...] = hi_ref[...]

    def load(j):
        return bur_ref[j * pitch:j * pitch + tc, :], bui_ref[j * pitch:j * pitch + tc, :]

    _s5_project_out(load, cre_ref, cim_ref, hn, d_ref, y_ref)


def s5_scan(x, g, wb, ar, ai, cre, cim, d_skip, bsz, seq):
    n_chunks = seq // S5_CHUNK
    st = jax.ShapeDtypeStruct((bsz, SLAB_GRP, SUBLANES, LANES), F32)
    st_spec = pl.BlockSpec((None, SLAB_GRP, SUBLANES, LANES), lambda b, c: (b, 0, 0, 0))
    vec = lambda b, c: (0, 0)
    c3 = lambda b, c: (0, 0, 0)
    return pl.pallas_call(
        _s5_scan_kernel,
        grid=(bsz, n_chunks),
        in_specs=[pl.BlockSpec((S5_CHUNK, D_MODEL), lambda b, c: (b * n_chunks + c, 0)),
                  pl.BlockSpec((1, D_MODEL), vec),
                  pl.BlockSpec(wb.shape, c3), pl.BlockSpec(ar.shape, c3), pl.BlockSpec(ai.shape, c3),
                  pl.BlockSpec(cre.shape, c3), pl.BlockSpec(cim.shape, c3),
                  pl.BlockSpec((1, D_MODEL), vec)],
        out_specs=[pl.BlockSpec((S5_CHUNK, D_MODEL), lambda b, c: (b * n_chunks + c, 0)), st_spec, st_spec],
        out_shape=[jax.ShapeDtypeStruct((bsz * seq, D_MODEL), BF16), st, st],
        scratch_shapes=[pltpu.VMEM((N_SLAB * S5_PITCH, LANES), F32), pltpu.VMEM((N_SLAB * S5_PITCH, LANES), F32),
                        pltpu.VMEM((SLAB_GRP, SUBLANES, LANES), F32), pltpu.VMEM((SLAB_GRP, SUBLANES, LANES), F32)],
        compiler_params=_params("parallel", "arbitrary"),
        name="s5_scan",
    )(x, g.reshape(1, -1), wb, ar, ai, cre, cim, d_skip.reshape(1, -1))


def _s5_decode_kernel(x_ref, g_ref, wb_ref, ar_ref, ai_ref, cre_ref, cim_ref, d_ref, h0r_ref, h0i_ref,
                      y_ref, sr_ref, si_ref):
    hn = _rms(x_ref[...], g_ref[...])

    def store(j, re, im):
        cols = slice(j * LANES, (j + 1) * LANES)
        ar, ai = ar_ref[:, cols], ai_ref[:, cols]
        hr, hi = h0r_ref[:, cols], h0i_ref[:, cols]
        sr_ref[:, cols] = ar * hr - ai * hi + re
        si_ref[:, cols] = ar * hi + ai * hr + im

    _s5_project_in(hn.astype(BF16), wb_ref, store)

    def load(j):
        cols = slice(j * LANES, (j + 1) * LANES)
        return sr_ref[:, cols], si_ref[:, cols]

    _s5_project_out(load, cre_ref, cim_ref, hn, d_ref, y_ref)


def s5_decode(x, g, wb, ar_row, ai_row, cre, cim, d_skip, h0r, h0i):
    r = x.shape[0]
    st = jax.ShapeDtypeStruct((r, SSM_DIM), F32)
    vec = lambda i: (0, 0)
    c3 = lambda i: (0, 0, 0)
    return pl.pallas_call(
        _s5_decode_kernel,
        grid=(1,),
        in_specs=[pl.BlockSpec((r, D_MODEL), vec), pl.BlockSpec((1, D_MODEL), vec),
                  pl.BlockSpec(wb.shape, c3), pl.BlockSpec((1, SSM_DIM), vec), pl.BlockSpec((1, SSM_DIM), vec),
                  pl.BlockSpec(cre.shape, c3), pl.BlockSpec(cim.shape, c3), pl.BlockSpec((1, D_MODEL), vec),
                  pl.BlockSpec((r, SSM_DIM), vec), pl.BlockSpec((r, SSM_DIM), vec)],
        out_specs=[pl.BlockSpec((r, D_MODEL), vec), pl.BlockSpec((r, SSM_DIM), vec), pl.BlockSpec((r, SSM_DIM), vec)],
        out_shape=[jax.ShapeDtypeStruct((r, D_MODEL), BF16), st, st],
        compiler_params=_params("arbitrary"),
        name="s5_decode",
    )(x, g.reshape(1, -1), wb, ar_row, ai_row, cre, cim, d_skip.reshape(1, -1), h0r, h0i)


def _t5_bucket(rel):
    n = jnp.maximum(rel, 0)
    nf = jnp.maximum(n, MAX_EXACT).astype(F32)
    big = MAX_EXACT + (jnp.log(nf / MAX_EXACT) / math.log(REL_MAX_DIST / MAX_EXACT)
                       * (N_BUCKETS - MAX_EXACT)).astype(jnp.int32)
    return jnp.where(n < MAX_EXACT, n, jnp.minimum(big, N_BUCKETS - 1))


def _bias_of(rel_bias, rel):
    onehot = (_t5_bucket(rel)[..., None] == jnp.arange(N_BUCKETS, dtype=jnp.int32)).astype(F32)
    return jnp.einsum('...k,kh->h...', onehot, rel_bias.astype(F32), precision=HIGHEST)


def _selection_matrix(n_cmp, n_sel_pad):
    coef = np.convolve(np.ones(SEL_RATIO), np.ones(CMP_BLOCK // CMP_STRIDE)).astype(np.float32)
    m = np.zeros((n_cmp, n_sel_pad), np.float32)
    for j in range(n_sel_pad):
        for o in range(coef.shape[0]):
            n = SEL_RATIO * j + o - (CMP_BLOCK // CMP_STRIDE - 1)
            if 0 <= n < n_cmp:
                m[n, j] = coef[o]
    return m


def _s5_params(a_re, a_im, log_dt, b_re, b_im, c_re, c_im):
    dt = jnp.exp(log_dt.astype(F32))[:, None]
    ar, ai = a_re.astype(F32), a_im.astype(F32)
    mag = jnp.exp(ar * dt)
    abar_re, abar_im = mag * jnp.cos(ai * dt), mag * jnp.sin(ai * dt)
    den = ar * ar + ai * ai
    coef_re = ((abar_re - 1.0) * ar + abar_im * ai) / den
    coef_im = (abar_im * ar - (abar_re - 1.0) * ai) / den
    br, bim = b_re.astype(F32), b_im.astype(F32)
    bb_re = coef_re[..., None] * br - coef_im[..., None] * bim
    bb_im = coef_re[..., None] * bim + coef_im[..., None] * br
    gpi = LANES // SSM_GROUP_CH
    eye = jnp.eye(gpi, dtype=F32)

    def in_blocks(bb):
        t = bb.reshape(IN_SLABS, gpi, SSM_STATE, SSM_GROUP_CH)
        blk = jnp.einsum('sgpc,gh->sgchp', t, eye)
        return blk.reshape(IN_SLABS, LANES, gpi * SSM_STATE)

    wb = jnp.concatenate([in_blocks(bb_re), in_blocks(bb_im)], axis=-1).astype(BF16)
    gps = LANES // SSM_STATE
    ch_per_in = LANES

    def out_blocks(c):
        t = c.astype(F32).reshape(IN_SLABS, STATE_PER_IN, gps, SSM_GROUP_CH, SSM_STATE)
        sel = jnp.eye(STATE_PER_IN * gps, dtype=F32).reshape(STATE_PER_IN, gps, STATE_PER_IN * gps)
        blk = jnp.einsum('ijgcp,jgh->ijgphc', t, sel)
        return blk.reshape(N_SLAB, LANES, ch_per_in).astype(BF16)

    return abar_re, abar_im, wb, out_blocks(c_re), out_blocks(c_im)


PROMPT_TM = 512
TN = 512
FFN_UP_TM = 1024
FFN_DOWN_TK = D_FF // 4
GLU_TM = 256
GLU_TK = D_MODEL


def _row_tile(r):
    return PROMPT_TM if r % PROMPT_TM == 0 else r


def _nsa_project(h, wq, wkv, wg):
    tm = _row_tile(h.shape[0])
    q = matmul(h, wq, n_split=1, out_dtype=BF16, act=None, tm=tm, tn=TN, name="nsa_q")[0]
    kv = matmul(h, wkv, n_split=3, out_dtype=F32, act=None, tm=tm, tn=TN, name="nsa_kv")
    gate = matmul(h, wg, n_split=1, out_dtype=F32, act="sigmoid", tm=tm, tn=LANES, name="nsa_gate")[0]
    return q, kv, gate


def _nsa_layer(hp, hs, xp, xs, g_post, g_next, wts, tabs, caches, ret, layer, page_table, bsz, seq):
    wq, wkv, wg, wo, pe, w1, w2 = wts
    cmp_pool, slc_pool, win_pool, n_phys = caches
    dec = hs.shape[0]
    q = matmul(hp, wq, n_split=1, out_dtype=BF16, act=None, tm=PROMPT_TM, tn=2 * TN, name="nsa_q")[0]
    kv, ret_cmp, ret_slc, vt = kv_project(hp, wkv, ret[0], ret[1], layer, tm=PROMPT_TM)
    gate = matmul(hp, wg, n_split=1, out_dtype=F32, act="sigmoid", tm=PROMPT_TM, tn=LANES, name="nsa_gate")[0]
    part_a, part_b = cmp_partial_rows(kv[0], pe, w1, seq)
    kc = cmp_finish(part_a, part_b, jnp.zeros((bsz, SUBLANES, KV_COLS), F32), w2)
    o_cmp, mask = cmp_attention(q, kc, tabs["cmp"], tabs["msel"], tabs["expand"], bsz, seq)
    o_slc = slc_attention(q, kv[1], vt, mask, tabs["tile_t"], bsz, seq)
    o = win_attention(q, kv[2], tabs["win"], o_cmp, o_slc, gate, bsz, seq)
    xp, hp = proj_res(o, wo, xp, g_post, g_next, glu=False, tm=PROMPT_TM, tk=D_MODEL, name="nsa_out")
    qs, kvs, gate_s = _nsa_project(hs, wq, wkv, wg)
    past_a, past_b = cmp_partial_paged(cmp_pool, n_phys, page_table, layer, pe, w1)
    tail = jnp.pad(kvs[0][:, None, :], ((0, 0), (0, CMP_STRIDE - 1), (0, 0))).reshape(dec * CMP_STRIDE, KV_COLS)
    _, tail_b = cmp_partial_rows(tail, pe, w1, dec * CMP_STRIDE)
    b_next = jnp.pad(tail_b[0][:, None, :], ((0, 0), (0, SUBLANES - 1), (0, 0)))
    kc_s = cmp_finish(past_a, past_b, b_next, w2)
    oc_s, ow_s, idx = sample_attention(qs.reshape(dec, N_HEADS, HEAD_DIM), kc_s, tabs["cmp_s"], tabs["gsum"],
                                       tabs["msel_s"], win_pool, layer, kvs[2][:, None, :],
                                       tabs["win_s"], tabs["new_s"])
    q3 = qs.reshape(dec, N_HEADS, HEAD_DIM)
    os_all = sample_slc_attention(idx[:, :KV_GROUPS, :N_SELECT], page_table, q3, slc_pool, n_phys, layer,
                                  kvs[1][:, None, :], tabs["slc_s"])
    os_s = jnp.stack([os_all[:, h // GROUP_SIZE, h] for h in range(N_HEADS)], axis=1)
    o_s = nsa_combine(oc_s.reshape(dec, D_MODEL), os_s.reshape(dec, D_MODEL), ow_s.reshape(dec, D_MODEL), gate_s, dec)
    xs, hs = proj_res(o_s, wo, xs, g_post, g_next, glu=False, tm=dec, tk=D_MODEL, name="nsa_out_s")
    return xp, hp, xs, hs, kv, kvs, (ret_cmp, ret_slc)


def _conv_layer(hp, hs, xp, xs, g_post, g_next, wts, state, bsz, seq):
    w_pw1, dw, dw_b, ln_g, ln_b, w_pw2 = wts
    dec = hs.shape[0]
    u = glu_matmul(hp, w_pw1, tm=PROMPT_TM, tn=TN, name="conv_pw1")
    hc = dwconv_ln(u, dw, dw_b, ln_g, ln_b, seq)
    xp, hp = proj_res(hc, w_pw2, xp, g_post, g_next, glu=False, tm=PROMPT_TM, tk=D_MODEL, name="conv_pw2")
    hist_p = u.reshape(bsz, seq, D_MODEL)[:, seq - (CONV_WIDTH - 1):]
    us = glu_matmul(hs, w_pw1, tm=dec, tn=TN, name="conv_pw1_s")
    hc_s = dwconv_ln_decode(us, jnp.swapaxes(state, 0, 1), dw, dw_b, ln_g, ln_b)
    xs, hs = proj_res(hc_s, w_pw2, xs, g_post, g_next, glu=False, tm=dec, tk=D_MODEL, name="conv_pw2_s")
    hist_s = jnp.concatenate([state[:, 1:], us[:, None, :]], axis=1)
    return xp, hp, xs, hs, hist_p, hist_s


def _s5_layer(xp, xs, g_pre, g_post, g_next, wts, state_re, state_im, bsz, seq):
    a_re, a_im, log_dt, b_re, b_im, c_re, c_im, d_skip, w_glu = wts
    dec = xs.shape[0]
    abar_re, abar_im, wb, cre, cim = _s5_params(a_re, a_im, log_dt, b_re, b_im, c_re, c_im)
    slab_shape = (SLAB_GRP, SUBLANES, LANES)
    y, sr, si = s5_scan(xp, g_pre, wb, abar_re.reshape(slab_shape), abar_im.reshape(slab_shape), cre, cim,
                        d_skip, bsz, seq)
    xp, hp = proj_res(y, w_glu, xp, g_post, g_next, glu=True, tm=GLU_TM, tk=GLU_TK, name="s5_glu")
    ys, sr_s, si_s = s5_decode(xs, g_pre, wb, abar_re.reshape(1, SSM_DIM), abar_im.reshape(1, SSM_DIM), cre, cim,
                               d_skip, state_re.reshape(dec, SSM_DIM), state_im.reshape(dec, SSM_DIM))
    xs, hs = proj_res(ys, w_glu, xs, g_post, g_next, glu=True, tm=dec, tk=GLU_TK, name="s5_glu_s")
    gp = (SSM_GROUPS, SSM_STATE)
    return (xp, hp, xs, hs, sr.reshape((bsz,) + gp), si.reshape((bsz,) + gp),
            sr_s.reshape((dec,) + gp), si_s.reshape((dec,) + gp))


def _ffn_layer(hp, hs, xp, xs, g_post, g_next, wts, state, bsz, seq):
    w_up, layer, dw, dw_b, w_down = wts
    dec = hs.shape[0]
    act, hist = ffn_up(hp, w_up, layer, dw, dw_b, seq_len=seq, tm=FFN_UP_TM, tn=TN)
    xp, hp = proj_res(act, w_down, xp, g_post, g_next, glu=False, tm=PROMPT_TM, tk=FFN_DOWN_TK, name="ffn_down")
    tiles = seq // FFN_UP_TM
    hist_p = hist.reshape(bsz, tiles, SUBLANES, D_FF)[:, tiles - 1, SUBLANES - 2:, :]
    act_s, gate_s = ffn_up(hs, w_up, layer, dw, dw_b, hist=state, seq_len=1, tm=dec, tn=TN)
    xs, hs = proj_res(act_s, w_down, xs, g_post, g_next, glu=False, tm=dec, tk=FFN_DOWN_TK, name="ffn_down_s")
    hist_s = jnp.concatenate([state[:, 1:], gate_s.reshape(dec, 1, D_FF)], axis=1)
    return xp, hp, xs, hs, hist_p, hist_s


def _bias_tables(rel_bias, seq):
    tq = ATT_TILE
    i = jnp.arange(tq, dtype=jnp.int32)
    tile = jnp.stack([_bias_of(rel_bias, d * tq + i[:, None] - i[None, :]) for d in range(3)])
    n_cmp = seq // CMP_STRIDE
    cpos = jnp.arange(n_cmp, dtype=jnp.int32) * CMP_STRIDE + (CMP_BLOCK - 1)
    qpos = jnp.arange(seq, dtype=jnp.int32)
    n_sel = -(-seq // SEL_BLOCK)
    key = np.arange(seq)
    expand = (key[:, None] // SEL_BLOCK == np.arange(LANES)[None, :]).astype(np.float32)
    cpos_s = jnp.arange(N_CMP_S, dtype=jnp.int32) * CMP_STRIDE + (CMP_BLOCK - 1)
    wb = min(WINDOW, PAST_LEN)
    kpos_s = jnp.arange(N_SEL_S * SEL_BLOCK, dtype=jnp.int32)
    slc_s = _bias_of(rel_bias, PAST_LEN - kpos_s).reshape(N_HEADS, N_SEL_S, SEL_BLOCK)
    gsum = (np.arange(N_HEADS)[None, :] // GROUP_SIZE == np.arange(SUBLANES)[:, None]).astype(np.float32)
    return {
        "tile_t": jnp.swapaxes(tile, -1, -2),
        "win": _bias_of(rel_bias, i[:, None] + WINDOW - jnp.arange(WIN_SPAN, dtype=jnp.int32)[None, :]),
        "cmp": _bias_of(rel_bias, qpos[:, None] - cpos[None, :]),
        "msel": jnp.asarray(_selection_matrix(n_cmp, n_sel).T),
        "expand": jnp.asarray(expand, BF16),
        "cmp_s": _bias_of(rel_bias, PAST_LEN - cpos_s),
        "msel_s": jnp.asarray(_selection_matrix(N_CMP_S, N_SEL_S_PAD)),
        "gsum": jnp.asarray(gsum),
        "win_s": _bias_of(rel_bias, wb - jnp.arange(wb, dtype=jnp.int32)),
        "new_s": jnp.broadcast_to(_bias_of(rel_bias, jnp.zeros((1,), jnp.int32)), (N_HEADS, LANES)),
        "slc_s": jnp.swapaxes(slc_s, 0, 1),
    }


def kernel(x_prompt, x_sample, cache_cmp_kv, cache_slc_kv, cache_win_kv, state_conv, state_ssm_re, state_ssm_im, state_ffn_conv, page_table, norm_gain, rel_bias, nsa_w_q, nsa_w_kv, nsa_cmp_pe, nsa_cmp_w1, nsa_cmp_w2, nsa_w_gate, nsa_w_o, conv_w_pw1, conv_dw, conv_dw_b, conv_ln_g, conv_ln_b, conv_w_pw2, ssm_a_re, ssm_a_im, ssm_log_dt, ssm_b_re, ssm_b_im, ssm_c_re, ssm_c_im, ssm_d, ssm_w_glu, ffn_w_up, ffn_dw, ffn_dw_b, ffn_w_down):
    bsz, seq, d = x_prompt.shape
    dec, dec_seq, _ = x_sample.shape
    assert dec_seq == 1 and d == D_MODEL and seq % PROMPT_TM == 0 and seq // CMP_STRIDE == LANES
    n_nsa = cache_cmp_kv.shape[0]
    n_phys = cache_cmp_kv.shape[1]
    xp = x_prompt.reshape(bsz * seq, d)
    xs = x_sample.reshape(dec, d)
    tabs = _bias_tables(rel_bias, seq)
    cmp_pool = cache_cmp_kv.reshape(n_nsa * n_phys, PAGE_SIZE * N_KV_SLABS, HEAD_DIM)
    slc_pool = cache_slc_kv.reshape(n_nsa * n_phys * (PAGE_SIZE // SEL_BLOCK), SEL_BLOCK * N_KV_SLABS, HEAD_DIM)
    win_pool = cache_win_kv.reshape(n_nsa * dec, cache_win_kv.shape[2] * N_KV_SLABS, HEAD_DIM)
    g_all = norm_gain.astype(F32)

    hp = rmsnorm_cast(xp, g_all[0, 0], PROMPT_TM)
    hs = rmsnorm_cast(xs, g_all[0, 0], dec)
    ret = tuple(jnp.zeros((n_nsa, bsz * seq * N_KV_SLABS, HEAD_DIM), F32) for _ in range(2))
    ret_shape = (n_nsa, bsz, seq, KV_GROUPS, 2, HEAD_DIM)
    out = {k: [] for k in ("cmp_s", "slc_s", "win_p", "win_s", "conv_p", "conv_s",
                           "re_p", "re_s", "im_p", "im_s", "ffn_p", "ffn_s")}
    counts = [0, 0, 0]
    for i in range(DEPTH):
        m = i % 3
        j = counts[m]
        counts[m] += 1
        g_post, g_ffn, g_ffn_post = g_all[i, 1], g_all[i, 2], g_all[i, 3]
        g_next = g_all[i + 1, 0] if i + 1 < DEPTH else g_all[i, 0]
        if m == 0:
            wg = jnp.pad(nsa_w_gate[j], ((0, 0), (0, LANES - nsa_w_gate.shape[-1]))).astype(BF16)
            wts = (nsa_w_q[j].astype(BF16), nsa_w_kv[j].astype(BF16), wg, nsa_w_o[j].astype(BF16),
                   jnp.swapaxes(nsa_cmp_pe[j], 0, 1).astype(F32), nsa_cmp_w1[j].astype(BF16),
                   nsa_cmp_w2[j].astype(BF16))
            xp, hp, xs, hs, kv, kvs, ret = _nsa_layer(hp, hs, xp, xs, g_post, g_ffn, wts, tabs,
                                                      (cmp_pool, slc_pool, win_pool, n_phys), ret, j, page_table,
                                                      bsz, seq)
            shp = (bsz, seq, KV_GROUPS, 2, HEAD_DIM)
            shs = (dec, 1, KV_GROUPS, 2, HEAD_DIM)
            out["cmp_s"].append(kvs[0].reshape(shs))
            out["slc_s"].append(kvs[1].reshape(shs))
            out["win_p"].append(kv[2].reshape(shp)[:, seq - min(WINDOW, seq):])
            win_full = jnp.concatenate([cache_win_kv[j], kvs[2].reshape(shs)], axis=1)
            out["win_s"].append(win_full[:, win_full.shape[1] - min(WINDOW, PAST_LEN + 1):])
        elif m == 1:
            wts = (conv_w_pw1[j].astype(BF16), conv_dw[j], conv_dw_b[j], conv_ln_g[j], conv_ln_b[j],
                   conv_w_pw2[j].astype(BF16))
            xp, hp, xs, hs, cp, cs = _conv_layer(hp, hs, xp, xs, g_post, g_ffn, wts, state_conv[j], bsz, seq)
            out["conv_p"].append(cp); out["conv_s"].append(cs)
        else:
            wts = (ssm_a_re[j], ssm_a_im[j], ssm_log_dt[j], ssm_b_re[j], ssm_b_im[j], ssm_c_re[j], ssm_c_im[j],
                   ssm_d[j], ssm_w_glu[j].astype(BF16))
            xp, hp, xs, hs, rp, ip, rs, is_ = _s5_layer(xp, xs, g_all[i, 0], g_post, g_ffn, wts,
                                                        state_ssm_re[j], state_ssm_im[j], bsz, seq)
            out["re_p"].append(rp); out["im_p"].append(ip); out["re_s"].append(rs); out["im_s"].append(is_)
        wts = (ffn_w_up, i, ffn_dw[i], ffn_dw_b[i], ffn_w_down[i].astype(BF16))
        xp, hp, xs, hs, fp, fs = _ffn_layer(hp, hs, xp, xs, g_ffn_post, g_next, wts, state_ffn_conv[i], bsz, seq)
        out["ffn_p"].append(fp); out["ffn_s"].append(fs)
    st = lambda k: jnp.stack(out[k])
    return (xp.reshape(bsz, seq, d), xs.reshape(dec, 1, d),
            ret[0].reshape(ret_shape), st("cmp_s"), ret[1].reshape(ret_shape), st("slc_s"), st("win_p"), st("win_s"),
            st("conv_p"), st("conv_s"), st("re_p"), st("re_s"), st("im_p"), st("im_s"),
            st("ffn_p"), st("ffn_s"))
```

```python
import functools
import math

import numpy as np
import jax
import jax.numpy as jnp
from jax import lax
from jax.experimental import pallas as pl
from jax.experimental.pallas import tpu as pltpu

F32 = jnp.float32
BF16 = jnp.bfloat16

D_MODEL = 2048
DEPTH = 4
PAST_LEN = 16384
PAGE_SIZE = 128
N_HEADS = 16
HEAD_DIM = 128
KV_GROUPS = 4
GROUP_SIZE = 4
KV_COLS = KV_GROUPS * 2 * HEAD_DIM
CMP_BLOCK = 32
CMP_STRIDE = 16
SEL_BLOCK = 64
SEL_RATIO = SEL_BLOCK // CMP_STRIDE
N_SELECT = 16
WINDOW = 512
FORCE_BONUS = 1.0e4
N_BUCKETS = 32
MAX_EXACT = 16
REL_MAX_DIST = 128
CONV_WIDTH = 31
SSM_GROUP_CH = 16
SSM_GROUPS = 128
SSM_STATE = 64
SSM_DIM = SSM_GROUPS * SSM_STATE
D_FF = 5632
RMS_EPS = 1e-6
LN_EPS = 1e-5
SCALE = HEAD_DIM ** -0.5
NEG = -1e30

LANES = 128
SUBLANES = 8
VMEM_LIMIT = 56 * 1024 * 1024
ATT_TILE = 128
N_SLAB = SSM_DIM // LANES
SLAB_GRP = N_SLAB // SUBLANES

HIGHEST = lax.Precision.HIGHEST
NT_DIMS = (((1,), (1,)), ((), ()))


def _params(*sem):
    return pltpu.CompilerParams(dimension_semantics=sem, vmem_limit_bytes=VMEM_LIMIT)


def _rms(x, g):
    return x * lax.rsqrt(jnp.mean(x * x, axis=-1, keepdims=True) + RMS_EPS) * g


def _rmsnorm_kernel(x_ref, g_ref, o_ref):
    o_ref[...] = _rms(x_ref[...], g_ref[...]).astype(o_ref.dtype)


def rmsnorm_cast(x, g, tm):
    r, d = x.shape
    return pl.pallas_call(
        _rmsnorm_kernel,
        grid=(r // tm,),
        in_specs=[pl.BlockSpec((tm, d), lambda m: (m, 0)), pl.BlockSpec((1, d), lambda m: (0, 0))],
        out_specs=pl.BlockSpec((tm, d), lambda m: (m, 0)),
        out_shape=jax.ShapeDtypeStruct((r, d), BF16),
        compiler_params=_params("parallel"),
        name="rmsnorm",
    )(x, g.reshape(1, d))


def _mm_kernel(a_ref, w_ref, o_ref, *, act):
    r = jnp.dot(a_ref[...], w_ref[...], preferred_element_type=F32)
    if act == "sigmoid":
        r = jax.nn.sigmoid(r)
    o_ref[...] = r.astype(o_ref.dtype)


def matmul(a, w, *, n_split, out_dtype, act, tm, tn, name):
    r, k = a.shape
    n = w.shape[1]
    per = (n // n_split) // tn
    return pl.pallas_call(
        functools.partial(_mm_kernel, act=act),
        grid=(r // tm, n // tn),
        in_specs=[pl.BlockSpec((tm, k), lambda m, j: (m, 0)), pl.BlockSpec((k, tn), lambda m, j: (0, j))],
        out_specs=pl.BlockSpec((None, tm, tn), lambda m, j: (j // per, m, j % per)),
        out_shape=jax.ShapeDtypeStruct((n_split, r, n // n_split), out_dtype),
        compiler_params=_params("parallel", "arbitrary"),
        name=name,
    )(a, w)


KV_TN = 512


def _kv_project_kernel(a_ref, w_ref, prev_cmp_ref, prev_slc_ref, kv_ref, cmp_ref, slc_ref, vt_ref):
    del prev_cmp_ref, prev_slc_ref
    n = pl.program_id(1)
    res = jnp.dot(a_ref[...], w_ref[...], preferred_element_type=F32)
    kv_ref[...] = res
    tm = res.shape[0]
    slabs = KV_TN // HEAD_DIM
    first = (n % 2) * slabs

    @pl.when((n >= 2) & (n < 4))
    def _():
        for i in range(slabs // 2):
            v = res[:, (2 * i + 1) * HEAD_DIM:(2 * i + 2) * HEAD_DIM]
            vt_ref[i * HEAD_DIM:(i + 1) * HEAD_DIM, :] = v.T.astype(vt_ref.dtype)

    def scatter(out_ref):
        for sl in range(slabs):
            out_ref[pl.ds(first + sl, tm, stride=N_KV_SLABS), :] = res[:, sl * HEAD_DIM:(sl + 1) * HEAD_DIM]

    @pl.when(n < 2)
    def _():
        scatter(cmp_ref)

    @pl.when((n >= 2) & (n < 4))
    def _():
        scatter(slc_ref)


def kv_project(a, w_kv, prev_cmp, prev_slc, layer, *, tm):
    r, k = a.shape
    n_tiles = w_kv.shape[1] // KV_TN
    per = KV_COLS // KV_TN
    nat_spec = pl.BlockSpec((None, tm * N_KV_SLABS, HEAD_DIM), lambda m, j: (layer, m, 0))
    any_spec = pl.BlockSpec(memory_space=pl.ANY)
    return pl.pallas_call(
        _kv_project_kernel,
        grid=(r // tm, n_tiles),
        in_specs=[pl.BlockSpec((tm, k), lambda m, j: (m, 0)), pl.BlockSpec((k, KV_TN), lambda m, j: (0, j)),
                  any_spec, any_spec],
        out_specs=[pl.BlockSpec((None, tm, KV_TN), lambda m, j: (j // per, m, j % per)), nat_spec, nat_spec,
                   pl.BlockSpec((KV_TN // 2, tm), lambda m, j: (jnp.clip(j - 2, 0, 1), m))],
        out_shape=[jax.ShapeDtypeStruct((3, r, KV_COLS), F32),
                   jax.ShapeDtypeStruct(prev_cmp.shape, F32), jax.ShapeDtypeStruct(prev_slc.shape, F32),
                   jax.ShapeDtypeStruct((KV_GROUPS * HEAD_DIM, r), BF16)],
        input_output_aliases={2: 1, 3: 2},
        compiler_params=_params("parallel", "arbitrary"),
        name="nsa_kv",
    )(a, w_kv, prev_cmp, prev_slc)


def _glu_mm_kernel(a_ref, wa_ref, wb_ref, o_ref):
    a = a_ref[...]
    lin = jnp.dot(a, wa_ref[...], preferred_element_type=F32)
    gate = jnp.dot(a, wb_ref[...], preferred_element_type=F32)
    o_ref[...] = lin * jax.nn.sigmoid(gate)


def glu_matmul(a, w, *, tm, tn, name):
    r, k = a.shape
    n = w.shape[1] // 2
    nb = n // tn
    return pl.pallas_call(
        _glu_mm_kernel,
        grid=(r // tm, nb),
        in_specs=[pl.BlockSpec((tm, k), lambda m, j: (m, 0)),
                  pl.BlockSpec((k, tn), lambda m, j: (0, j)),
                  pl.BlockSpec((k, tn), lambda m, j: (0, j + nb))],
        out_specs=pl.BlockSpec((tm, tn), lambda m, j: (m, j)),
        out_shape=jax.ShapeDtypeStruct((r, n), F32),
        compiler_params=_params("parallel", "arbitrary"),
        name=name,
    )(a, w, w)


def _proj_res_kernel(a_ref, w_ref, x_ref, gp_ref, gn_ref, xo_ref, ho_ref, *acc, glu, nk):
    def finish(y):
        if glu:
            d = y.shape[1] // 2
            y = y[:, :d] * jax.nn.sigmoid(y[:, d:])
        xn = x_ref[...] + _rms(y, gp_ref[...])
        xo_ref[...] = xn
        ho_ref[...] = _rms(xn, gn_ref[...]).astype(ho_ref.dtype)

    part = jnp.dot(a_ref[...], w_ref[...], preferred_element_type=F32)
    if nk == 1:
        finish(part)
        return
    acc_ref, = acc
    k = pl.program_id(1)

    @pl.when(k == 0)
    def _():
        acc_ref[...] = part

    @pl.when((k > 0) & (k < nk - 1))
    def _():
        acc_ref[...] += part

    @pl.when(k == nk - 1)
    def _():
        finish(acc_ref[...] + part)


def proj_res(a, w, x, g_post, g_next, *, glu, tm, tk, name):
    r, kdim = a.shape
    n = w.shape[1]
    d = x.shape[1]
    nk = kdim // tk
    return pl.pallas_call(
        functools.partial(_proj_res_kernel, glu=glu, nk=nk),
        grid=(r // tm, nk),
        in_specs=[pl.BlockSpec((tm, tk), lambda m, k: (m, k)),
                  pl.BlockSpec((tk, n), lambda m, k: (k, 0)),
                  pl.BlockSpec((tm, d), lambda m, k: (m, 0)),
                  pl.BlockSpec((1, d), lambda m, k: (0, 0)),
                  pl.BlockSpec((1, d), lambda m, k: (0, 0))],
        out_specs=[pl.BlockSpec((tm, d), lambda m, k: (m, 0)),
                   pl.BlockSpec((tm, d), lambda m, k: (m, 0))],
        out_shape=[jax.ShapeDtypeStruct((r, d), F32), jax.ShapeDtypeStruct((r, d), BF16)],
        scratch_shapes=[pltpu.VMEM((tm, n), F32)] if nk > 1 else [],
        compiler_params=_params("parallel", "arbitrary"),
        name=name,
    )(a, w, x, g_post.reshape(1, d), g_next.reshape(1, d))


def _ffn_up_kernel(a_ref, p1_ref, p2_ref, wg_ref, wv_ref, dw_ref, db_ref, act_ref, hist_ref, *, decode, tiles_per_seq):
    a = a_ref[...]
    wg = wg_ref[...].astype(BF16)
    gate = jnp.dot(a, wg, preferred_element_type=F32)
    val = jnp.dot(a, wv_ref[...].astype(BF16), preferred_element_type=F32)
    tm = gate.shape[0]
    if decode:
        g2, g1 = p2_ref[...], p1_ref[...]
        hist_ref[...] = gate
    else:
        halo = jnp.dot(p1_ref[...], wg, preferred_element_type=F32)
        halo = jnp.where(pl.program_id(0) % tiles_per_seq == 0, 0.0, halo)
        h7, h6 = halo[7:8, :], halo[6:7, :]
        row = lax.broadcasted_iota(jnp.int32, gate.shape, 0)
        g1 = jnp.where(row == 0, h7, pltpu.roll(gate, 1, 0))
        g2 = jnp.where(row == 0, h6, jnp.where(row == 1, h7, pltpu.roll(gate, 2, 0)))
        hist_ref[...] = gate[tm - SUBLANES:, :]
    g = dw_ref[0:1, :] * g2 + dw_ref[1:2, :] * g1 + dw_ref[2:3, :] * gate + db_ref[...]
    act_ref[...] = (jax.nn.gelu(g) * val).astype(act_ref.dtype)


def ffn_up(h, w_up, layer, dw, db, *, hist=None, seq_len, tm, tn):
    r, k = h.shape
    nb = D_FF // tn
    decode = hist is not None
    if decode:
        p1, p2 = hist[:, 1, :], hist[:, 0, :]
        p_specs = [pl.BlockSpec((tm, tn), lambda m, j: (m, j)), pl.BlockSpec((tm, tn), lambda m, j: (m, j))]
        hrows = tm
    else:
        p1 = p2 = h
        blk = tm // SUBLANES
        p_specs = [pl.BlockSpec((SUBLANES, k), lambda m, j: (jnp.maximum(m * blk - 1, 0), 0)),
                   pl.BlockSpec((SUBLANES, k), lambda m, j: (0, 0))]
        hrows = SUBLANES
    return pl.pallas_call(
        functools.partial(_ffn_up_kernel, decode=decode, tiles_per_seq=max(seq_len // tm, 1)),
        grid=(r // tm, nb),
        in_specs=[pl.BlockSpec((tm, k), lambda m, j: (m, 0))] + p_specs + [
            pl.BlockSpec((None, k, tn), lambda m, j: (layer, 0, j)),
            pl.BlockSpec((None, k, tn), lambda m, j: (layer, 0, j + nb)),
            pl.BlockSpec((3, tn), lambda m, j: (0, j)),
            pl.BlockSpec((1, tn), lambda m, j: (0, j))],
        out_specs=[pl.BlockSpec((tm, tn), lambda m, j: (m, j)),
                   pl.BlockSpec((None, hrows, tn), lambda m, j: (m, 0, j))],
        out_shape=[jax.ShapeDtypeStruct((r, D_FF), BF16),
                   jax.ShapeDtypeStruct((r // tm, hrows, D_FF), F32)],
        compiler_params=_params("parallel", "arbitrary"),
        name="ffn_up",
    )(h, p1, p2, w_up, w_up, dw, db.reshape(1, D_FF))


N_KV_SLABS = KV_GROUPS * 2


def _cmp_partial_slab(x_ref, c, pe_ref, w1_ref, n_chunks, first=0, pitch=1, chunk_pitch=None):
    chunk_pitch = CMP_STRIDE * pitch if chunk_pitch is None else chunk_pitch
    acc_a = jnp.zeros((n_chunks, HEAD_DIM), F32)
    acc_b = jnp.zeros((n_chunks, HEAD_DIM), F32)
    for s in range(CMP_STRIDE):
        xs = x_ref[pl.ds(first + s * pitch, n_chunks, stride=chunk_pitch), :]
        xa = (xs + pe_ref[c, s:s + 1, :]).astype(BF16)
        xb = (xs + pe_ref[c, CMP_STRIDE + s:CMP_STRIDE + s + 1, :]).astype(BF16)
        acc_a += jnp.dot(xa, w1_ref[c, s], preferred_element_type=F32)
        acc_b += jnp.dot(xb, w1_ref[c, CMP_STRIDE + s], preferred_element_type=F32)
    return acc_a, acc_b


def _cmp1_kernel(x_ref, pe_ref, w1_ref, a_ref, b_ref, *, n_chunks):
    c = pl.program_id(1) % 2
    a_ref[...], b_ref[...] = _cmp_partial_slab(x_ref, c, pe_ref, w1_ref, n_chunks)


def cmp_partial_rows(x, pe, w1, rows):
    nb = x.shape[0] // rows
    n_chunks = rows // CMP_STRIDE
    out = jax.ShapeDtypeStruct((nb, n_chunks, KV_COLS), F32)
    return pl.pallas_call(
        functools.partial(_cmp1_kernel, n_chunks=n_chunks),
        grid=(nb, N_KV_SLABS),
        in_specs=[pl.BlockSpec((rows, HEAD_DIM), lambda b, sl: (b, sl)),
                  pl.BlockSpec(pe.shape, lambda b, sl: (0, 0, 0)),
                  pl.BlockSpec(w1.shape, lambda b, sl: (0, 0, 0, 0))],
        out_specs=[pl.BlockSpec((None, n_chunks, HEAD_DIM), lambda b, sl: (b, 0, sl))] * 2,
        out_shape=[out, out],
        compiler_params=_params("parallel", "arbitrary"),
        name="cmp_partial",
    )(x, pe, w1)


PAGES_PER_STEP = 16
CHUNKS_PER_PAGE = PAGE_SIZE // CMP_STRIDE
CHUNK_ROWS = CMP_STRIDE * N_KV_SLABS
CHUNK_PITCH = CHUNK_ROWS + 4
STEP_CHUNKS = PAGES_PER_STEP * CHUNKS_PER_PAGE


def _cmp1_paged_kernel(pt_ref, *refs):
    page_refs = refs[:PAGES_PER_STEP]
    pe_ref, w1_ref, a_ref, b_ref, buf_ref = refs[PAGES_PER_STEP:]
    for i, p_ref in enumerate(page_refs):
        for ch in range(CHUNKS_PER_PAGE):
            dst = (i * CHUNKS_PER_PAGE + ch) * CHUNK_PITCH
            buf_ref[dst:dst + CHUNK_ROWS, :] = p_ref[ch * CHUNK_ROWS:(ch + 1) * CHUNK_ROWS, :]
    for slab in range(N_KV_SLABS):
        cols = slice(slab * HEAD_DIM, (slab + 1) * HEAD_DIM)
        a_ref[:, cols], b_ref[:, cols] = _cmp_partial_slab(buf_ref, slab % 2, pe_ref, w1_ref, STEP_CHUNKS,
                                                           first=slab, pitch=N_KV_SLABS, chunk_pitch=CHUNK_PITCH)


def cmp_partial_paged(pool, n_phys, page_table, layer, pe, w1):
    bsz, n_pages = page_table.shape
    n_steps = n_pages // PAGES_PER_STEP
    n_chunks = STEP_CHUNKS
    out = jax.ShapeDtypeStruct((bsz, n_steps, n_chunks, KV_COLS), F32)

    def page_spec(i):
        return pl.BlockSpec((None, PAGE_SIZE * N_KV_SLABS, HEAD_DIM),
                            lambda b, s, pt: (layer * n_phys + pt[b, s * PAGES_PER_STEP + i], 0, 0))

    out_spec = pl.BlockSpec((None, None, n_chunks, KV_COLS), lambda b, s, pt: (b, s, 0, 0))
    a, b = pl.pallas_call(
        _cmp1_paged_kernel,
        grid_spec=pltpu.PrefetchScalarGridSpec(
            num_scalar_prefetch=1,
            grid=(bsz, n_steps),
            in_specs=[page_spec(i) for i in range(PAGES_PER_STEP)] + [
                pl.BlockSpec(pe.shape, lambda b, s, pt: (0, 0, 0)),
                pl.BlockSpec(w1.shape, lambda b, s, pt: (0, 0, 0, 0))],
            out_specs=[out_spec, out_spec],
            scratch_shapes=[pltpu.VMEM((STEP_CHUNKS * CHUNK_PITCH, HEAD_DIM), F32)]),
        out_shape=[out, out],
        compiler_params=_params("parallel", "arbitrary"),
        name="cmp_partial_paged",
    )(page_table, *([pool] * PAGES_PER_STEP), pe, w1)
    return a.reshape(bsz, -1, KV_COLS), b.reshape(bsz, -1, KV_COLS)


def _cmp2_kernel(a_ref, b_ref, bx_ref, w2_ref, o_ref):
    n = a_ref.shape[0]
    row = lax.broadcasted_iota(jnp.int32, (n, HEAD_DIM), 0)
    for slab in range(KV_GROUPS * 2):
        c = slab % 2
        cols = slice(slab * HEAD_DIM, (slab + 1) * HEAD_DIM)
        nxt = pltpu.roll(b_ref[:, cols], n - 1, 0)
        nxt = jnp.where(row == n - 1, bx_ref[0:1, cols], nxt)
        h = jax.nn.gelu(a_ref[:, cols] + nxt).astype(BF16)
        o_ref[:, cols] = jnp.dot(h, w2_ref[c], preferred_element_type=F32).astype(o_ref.dtype)


def cmp_finish(a, b, b_next, w2):
    nb, n, _ = a.shape
    return pl.pallas_call(
        _cmp2_kernel,
        grid=(nb,),
        in_specs=[pl.BlockSpec((None, n, KV_COLS), lambda i: (i, 0, 0)),
                  pl.BlockSpec((None, n, KV_COLS), lambda i: (i, 0, 0)),
                  pl.BlockSpec((None, SUBLANES, KV_COLS), lambda i: (i, 0, 0)),
                  pl.BlockSpec(w2.shape, lambda i: (0, 0, 0))],
        out_specs=pl.BlockSpec((None, n, KV_COLS), lambda i: (i, 0, 0)),
        out_shape=jax.ShapeDtypeStruct((nb, n, KV_COLS), BF16),
        compiler_params=_params("parallel"),
        name="cmp_finish",
    )(a, b, b_next, w2)


def _softmax_rows(s, mask):
    s = jnp.where(mask, s, NEG)
    m = jnp.max(s, axis=-1, keepdims=True)
    e = jnp.where(mask, jnp.exp(s - m), 0.0)
    l = jnp.sum(e, axis=-1, keepdims=True)
    return jnp.where(l > 0.0, e / jnp.where(l > 0.0, l, 1.0), 0.0)


def _cmp_attn_kernel(q_ref, kc_ref, bias_ref, msel_ref, exp_ref, o_ref, mask_ref, *, n_sel):
    qt = pl.program_id(1)
    tq, n_cmp = q_ref.shape[0], kc_ref.shape[0]
    qpos = qt * tq + lax.broadcasted_iota(jnp.int32, (tq, n_cmp), 0)
    cpos = lax.broadcasted_iota(jnp.int32, (tq, n_cmp), 1) * CMP_STRIDE + (CMP_BLOCK - 1)
    cadd = jnp.where(cpos <= qpos, 0.0, NEG)
    j = lax.broadcasted_iota(jnp.int32, (n_sel, tq), 0)
    cur = (qt * tq + lax.broadcasted_iota(jnp.int32, (n_sel, tq), 1)) // SEL_BLOCK
    valid = j <= cur
    forced = (j == 0) | (j == cur) | (j == cur - 1)
    pad_rows = jnp.zeros((exp_ref.shape[1] - n_sel, tq), F32)
    diag = pl.ds(pl.multiple_of(qt * tq, tq), tq)
    causal = lax.broadcasted_iota(jnp.int32, (tq, tq), 0) <= lax.broadcasted_iota(jnp.int32, (tq, tq), 1)
    for g in range(KV_GROUPS):
        k = kc_ref[:, g * 2 * HEAD_DIM:(g * 2 + 1) * HEAD_DIM]
        v = kc_ref[:, (g * 2 + 1) * HEAD_DIM:(g * 2 + 2) * HEAD_DIM]
        heads = range(g * GROUP_SIZE, (g + 1) * GROUP_SIZE)
        q4 = jnp.concatenate([q_ref[:, h * HEAD_DIM:(h + 1) * HEAD_DIM] for h in heads], axis=0)
        badd = jnp.concatenate([bias_ref[h] + cadd for h in heads], axis=0)
        s = lax.dot_general(q4, k, NT_DIMS, preferred_element_type=F32) * SCALE + badd
        p = _softmax_rows(s, s > 0.5 * NEG)
        o4 = jnp.dot(p.astype(BF16), v, preferred_element_type=F32)
        imp = jnp.zeros((tq, n_cmp), F32)
        for r, h in enumerate(heads):
            o_ref[:, h * HEAD_DIM:(h + 1) * HEAD_DIM] = o4[r * tq:(r + 1) * tq, :]
            imp = imp + p[r * tq:(r + 1) * tq, :]
        p_sel = lax.dot_general(msel_ref[...], imp, NT_DIMS, precision=HIGHEST, preferred_element_type=F32)
        score = jnp.where(valid, p_sel + jnp.where(forced, FORCE_BONUS, 0.0), -jnp.inf)
        rank = jnp.zeros((n_sel, tq), jnp.int32)
        for i in range(n_sel):
            si = score[i:i + 1, :]
            beats = (si > score) | ((si == score) & (i < j))
            rank = rank + beats.astype(jnp.int32)
        sel_t = jnp.where((rank < N_SELECT) & valid, 1.0, 0.0)
        sel = jnp.concatenate([sel_t, pad_rows], axis=0).astype(BF16)
        allowed = jnp.dot(exp_ref[...], sel, preferred_element_type=F32)
        mask_ref[g] = ((allowed - 1.0) * -NEG).astype(mask_ref.dtype)
        allowed_d = jnp.dot(exp_ref[diag, :], sel, preferred_element_type=F32)
        mask_ref[g, diag, :] = jnp.where(causal, (allowed_d - 1.0) * -NEG, NEG).astype(mask_ref.dtype)


def cmp_attention(q, kc, bias_cmp, msel, expand, bsz, seq):
    tq = ATT_TILE
    nqt = seq // tq
    n_cmp = kc.shape[1]
    n_sel = msel.shape[0]
    return pl.pallas_call(
        functools.partial(_cmp_attn_kernel, n_sel=n_sel),
        grid=(bsz, nqt),
        in_specs=[pl.BlockSpec((tq, D_MODEL), lambda b, t: (b * nqt + t, 0)),
                  pl.BlockSpec((None, n_cmp, KV_COLS), lambda b, t: (b, 0, 0)),
                  pl.BlockSpec((N_HEADS, tq, n_cmp), lambda b, t: (0, t, 0)),
                  pl.BlockSpec(msel.shape, lambda b, t: (0, 0)),
                  pl.BlockSpec(expand.shape, lambda b, t: (0, 0))],
        out_specs=[pl.BlockSpec((tq, D_MODEL), lambda b, t: (b * nqt + t, 0)),
                   pl.BlockSpec((None, KV_GROUPS, seq, tq), lambda b, t: (b, 0, 0, t))],
        out_shape=[jax.ShapeDtypeStruct((bsz * seq, D_MODEL), F32),
                   jax.ShapeDtypeStruct((bsz, KV_GROUPS, seq, seq), BF16)],
        compiler_params=_params("parallel", "arbitrary"),
        name="cmp_attention",
    )(q, kc, bias_cmp, msel, expand)


SLC_KEYS = 1024
SLC_SUB = SLC_KEYS // ATT_TILE
SLC_QUERIES = 256


def _slc_kernel(qt_ref, ks_ref, q_ref, k_ref, vt_ref, tbt_ref, mask_ref, o_ref, m_ref, l_ref, acc_ref):
    g, pair = pl.program_id(1), pl.program_id(2)
    qt, ks = qt_ref[pair], ks_ref[pair]
    tq = q_ref.shape[0]

    @pl.when(ks == 0)
    def _():
        m_ref[...] = jnp.full_like(m_ref, NEG)
        l_ref[...] = jnp.zeros_like(l_ref)
        acc_ref[...] = jnp.zeros_like(acc_ref)

    q4 = jnp.concatenate([q_ref[:, r * HEAD_DIM:(r + 1) * HEAD_DIM] for r in range(GROUP_SIZE)], axis=0)
    rows = []
    q_sub = tq // ATT_TILE
    for c in range(SLC_SUB):
        tbi = [jnp.clip(qt * q_sub + qs - (ks * SLC_SUB + c), 0, 2) for qs in range(q_sub)]
        madd = [mask_ref[c * ATT_TILE:(c + 1) * ATT_TILE, qs * ATT_TILE:(qs + 1) * ATT_TILE].astype(F32)
                for qs in range(q_sub)]
        rows.append(jnp.concatenate([tbt_ref[tbi[qs], g * GROUP_SIZE + r] + madd[qs]
                                     for r in range(GROUP_SIZE) for qs in range(q_sub)], axis=1))
    st = (lax.dot_general(k_ref[...].astype(BF16), q4, NT_DIMS, preferred_element_type=F32) * SCALE
          + jnp.concatenate(rows, axis=0))
    m_prev = m_ref[...]
    m_new = jnp.maximum(m_prev, jnp.max(st, axis=0, keepdims=True))
    alpha = jnp.exp(m_prev - m_new)
    pt = jnp.exp(st - m_new)
    l_ref[...] = alpha * l_ref[...] + jnp.sum(pt, axis=0, keepdims=True)
    acc_ref[...] = alpha * acc_ref[...] + jnp.dot(vt_ref[...], pt.astype(BF16), preferred_element_type=F32)
    m_ref[...] = m_new

    @pl.when(ks == ((qt + 1) * tq - 1) // SLC_KEYS)
    def _():
        ot = acc_ref[...] / l_ref[...]
        for r in range(GROUP_SIZE):
            o_ref[:, r * HEAD_DIM:(r + 1) * HEAD_DIM] = ot[:, r * tq:(r + 1) * tq].T


def slc_attention(q, kv, vt, mask, tbt, bsz, seq):
    tq = SLC_QUERIES
    nqt = seq // tq
    nks = seq // SLC_KEYS
    gw = GROUP_SIZE * HEAD_DIM
    pairs = [(t, s) for t in range(nqt) for s in range(((t + 1) * tq - 1) // SLC_KEYS + 1)]
    qt_of = jnp.asarray([p[0] for p in pairs], jnp.int32)
    ks_of = jnp.asarray([p[1] for p in pairs], jnp.int32)
    return pl.pallas_call(
        _slc_kernel,
        grid_spec=pltpu.PrefetchScalarGridSpec(
            num_scalar_prefetch=2,
            grid=(bsz, KV_GROUPS, len(pairs)),
            in_specs=[pl.BlockSpec((tq, gw), lambda b, g, p, qt, ks: (b * nqt + qt[p], g)),
                      pl.BlockSpec((SLC_KEYS, HEAD_DIM), lambda b, g, p, qt, ks: (b * nks + ks[p], 2 * g)),
                      pl.BlockSpec((HEAD_DIM, SLC_KEYS), lambda b, g, p, qt, ks: (g, b * nks + ks[p])),
                      pl.BlockSpec(tbt.shape, lambda b, g, p, qt, ks: (0, 0, 0, 0)),
                      pl.BlockSpec((None, None, SLC_KEYS, tq), lambda b, g, p, qt, ks: (b, g, ks[p], qt[p]))],
            out_specs=pl.BlockSpec((tq, gw), lambda b, g, p, qt, ks: (b * nqt + qt[p], g)),
            scratch_shapes=[pltpu.VMEM((1, GROUP_SIZE * tq), F32), pltpu.VMEM((1, GROUP_SIZE * tq), F32),
                            pltpu.VMEM((HEAD_DIM, GROUP_SIZE * tq), F32)]),
        out_shape=jax.ShapeDtypeStruct((bsz * seq, D_MODEL), F32),
        compiler_params=_params("parallel", "parallel", "arbitrary"),
        name="slc_attention",
    )(qt_of, ks_of, q, kv, vt, tbt, mask)


WIN_TILES = WINDOW // ATT_TILE + 1
WIN_SPAN = WIN_TILES * ATT_TILE


def _win_kernel(q_ref, *refs):
    kv_refs, (bias_ref, oc_ref, os_ref, gate_ref, o_ref) = refs[:WIN_TILES], refs[WIN_TILES:]
    gate = gate_ref[...]
    qt = pl.program_id(1)
    tq = q_ref.shape[0]
    row = lax.broadcasted_iota(jnp.int32, (tq, WIN_SPAN), 0)
    col = lax.broadcasted_iota(jnp.int32, (tq, WIN_SPAN), 1)
    back = row + WINDOW - col
    mask = (back >= 0) & (back <= WINDOW) & (qt * tq - WINDOW + col >= 0)
    madd = jnp.where(mask, 0.0, NEG)
    for g in range(KV_GROUPS):
        k = jnp.concatenate([r[:, g * 2 * HEAD_DIM:(g * 2 + 1) * HEAD_DIM].astype(BF16) for r in kv_refs], axis=0)
        v = jnp.concatenate([r[:, (g * 2 + 1) * HEAD_DIM:(g * 2 + 2) * HEAD_DIM].astype(BF16) for r in kv_refs], axis=0)
        heads = range(g * GROUP_SIZE, (g + 1) * GROUP_SIZE)
        q4 = jnp.concatenate([q_ref[:, h * HEAD_DIM:(h + 1) * HEAD_DIM] for h in heads], axis=0)
        badd = jnp.concatenate([bias_ref[h] + madd for h in heads], axis=0)
        s = lax.dot_general(q4, k, NT_DIMS, preferred_element_type=F32) * SCALE + badd
        e = jnp.exp(s - jnp.max(s, axis=-1, keepdims=True))
        l = jnp.sum(e, axis=-1, keepdims=True)
        o = jnp.dot(e.astype(BF16), v, preferred_element_type=F32) / l
        for r, h in enumerate(heads):
            cols = slice(h * HEAD_DIM, (h + 1) * HEAD_DIM)
            merged = (oc_ref[:, cols] * gate[:, 3 * h:3 * h + 1] + os_ref[:, cols] * gate[:, 3 * h + 1:3 * h + 2]
                      + o[r * tq:(r + 1) * tq, :] * gate[:, 3 * h + 2:3 * h + 3])
            o_ref[:, cols] = merged.astype(o_ref.dtype)


def win_attention(q, kv, bias_win, o_cmp, o_slc, gate, bsz, seq):
    tq = ATT_TILE
    nqt = seq // tq
    row_spec = pl.BlockSpec((tq, D_MODEL), lambda b, t: (b * nqt + t, 0))

    def kv_spec(i):
        return pl.BlockSpec((tq, KV_COLS), lambda b, t: (b * nqt + jnp.maximum(t - (WIN_TILES - 1) + i, 0), 0))

    return pl.pallas_call(
        _win_kernel,
        grid=(bsz, nqt),
        in_specs=[row_spec]
        + [kv_spec(i) for i in range(WIN_TILES)]
        + [pl.BlockSpec(bias_win.shape, lambda b, t: (0, 0, 0)), row_spec, row_spec,
           pl.BlockSpec((tq, LANES), lambda b, t: (b * nqt + t, 0))],
        out_specs=row_spec,
        out_shape=jax.ShapeDtypeStruct((bsz * seq, D_MODEL), BF16),
        compiler_params=_params("parallel", "arbitrary"),
        name="win_attention",
    )(q, *([kv] * WIN_TILES), bias_win, o_cmp, o_slc, gate)


def _combine_kernel(oc_ref, os_ref, ow_ref, g_ref, o_ref):
    gate = g_ref[...]
    for h in range(N_HEADS):
        cols = slice(h * HEAD_DIM, (h + 1) * HEAD_DIM)
        o = (oc_ref[:, cols] * gate[:, 3 * h:3 * h + 1] + os_ref[:, cols] * gate[:, 3 * h + 1:3 * h + 2]
             + ow_ref[:, cols] * gate[:, 3 * h + 2:3 * h + 3])
        o_ref[:, cols] = o.astype(o_ref.dtype)


def nsa_combine(o_cmp, o_slc, o_win, gate, tm):
    r = o_cmp.shape[0]
    spec = pl.BlockSpec((tm, D_MODEL), lambda m: (m, 0))
    return pl.pallas_call(
        _combine_kernel,
        grid=(r // tm,),
        in_specs=[spec, spec, spec, pl.BlockSpec((tm, LANES), lambda m: (m, 0))],
        out_specs=spec,
        out_shape=jax.ShapeDtypeStruct((r, D_MODEL), BF16),
        compiler_params=_params("parallel"),
        name="nsa_combine",
    )(o_cmp, o_slc, o_win, gate)


N_SEL_S = -(-(PAST_LEN + 1) // SEL_BLOCK)
N_SEL_S_PAD = 384
N_CMP_S = PAST_LEN // CMP_STRIDE


def _group_rows(parts, hgrp):
    out = parts[0]
    for g in range(1, KV_GROUPS):
        out = jnp.where(hgrp == g, parts[g], out)
    return out


def _sample_attn_kernel(q_ref, kc_ref, bc_ref, gsum_ref, msel_ref, win_ref, new_ref, bw_ref, bn_ref,
                        oc_ref, ow_ref, idx_ref):
    q = q_ref[...]
    hgrp = lax.broadcasted_iota(jnp.int32, (N_HEADS, 1), 0) // GROUP_SIZE

    def kcol(g):
        return slice(g * 2 * HEAD_DIM, (g * 2 + 1) * HEAD_DIM)

    def vcol(g):
        return slice((g * 2 + 1) * HEAD_DIM, (g * 2 + 2) * HEAD_DIM)

    s = _group_rows([lax.dot_general(q, kc_ref[:, kcol(g)], NT_DIMS, preferred_element_type=F32)
                     for g in range(KV_GROUPS)], hgrp)
    s = s * SCALE + bc_ref[...]
    n = lax.broadcasted_iota(jnp.int32, s.shape, 1)
    p = _softmax_rows(s, n * CMP_STRIDE + (CMP_BLOCK - 1) <= PAST_LEN)
    pb = p.astype(BF16)
    oc_ref[...] = _group_rows([jnp.dot(pb, kc_ref[:, vcol(g)], preferred_element_type=F32)
                               for g in range(KV_GROUPS)], hgrp)
    imp = jnp.dot(gsum_ref[...], p, precision=HIGHEST, preferred_element_type=F32)
    p_sel = jnp.dot(imp, msel_ref[...], precision=HIGHEST, preferred_element_type=F32)
    j = lax.broadcasted_iota(jnp.int32, p_sel.shape, 1)
    cur = PAST_LEN // SEL_BLOCK
    forced = (j == 0) | (j == cur) | (j == cur - 1)
    score = jnp.where(j <= cur, p_sel + jnp.where(forced, FORCE_BONUS, 0.0), -jnp.inf)
    lane = lax.broadcasted_iota(jnp.int32, idx_ref.shape, 1)
    idx = jnp.zeros(idx_ref.shape, F32)
    jf = j.astype(F32)
    for kk in range(N_SELECT):
        mx = jnp.max(score, axis=-1, keepdims=True)
        pick = jnp.min(jnp.where(score == mx, jf, float(N_SEL_S_PAD)), axis=-1, keepdims=True)
        idx = jnp.where(lane == kk, pick, idx)
        score = jnp.where(jf == pick, -jnp.inf, score)
    idx_ref[...] = idx.astype(jnp.int32)
    wb = win_ref.shape[0] // N_KV_SLABS

    def win_slab(slab):
        return win_ref[pl.ds(slab, wb, stride=N_KV_SLABS), :].astype(BF16)

    sw = _group_rows([lax.dot_general(q, win_slab(2 * g), NT_DIMS, preferred_element_type=F32)
                      for g in range(KV_GROUPS)], hgrp)
    sw = sw * SCALE + bw_ref[...]
    qf = q.astype(F32)
    sn = _group_rows([jnp.sum(qf * new_ref[:, kcol(g)].astype(BF16).astype(F32), axis=-1, keepdims=True)
                      for g in range(KV_GROUPS)], hgrp)
    sn = sn * SCALE + bn_ref[:, 0:1]
    m = jnp.maximum(jnp.max(sw, axis=-1, keepdims=True), sn)
    ew, en = jnp.exp(sw - m), jnp.exp(sn - m)
    l = jnp.sum(ew, axis=-1, keepdims=True) + en
    pw, pn = (ew / l).astype(BF16), (en / l).astype(BF16).astype(F32)
    ow = _group_rows([jnp.dot(pw, win_slab(2 * g + 1), preferred_element_type=F32)
                      + pn * new_ref[:, vcol(g)].astype(BF16).astype(F32) for g in range(KV_GROUPS)], hgrp)
    ow_ref[...] = ow


def sample_attention(q3, kc, bias_c, gsum, msel, win_pool, layer, kv_win_new, bias_w, bias_new):
    bsz = q3.shape[0]
    wb = win_pool.shape[1]
    o = jax.ShapeDtypeStruct((bsz, N_HEADS, HEAD_DIM), F32)
    full2 = lambda b: (0, 0)
    return pl.pallas_call(
        _sample_attn_kernel,
        grid=(bsz,),
        in_specs=[pl.BlockSpec((None, N_HEADS, HEAD_DIM), lambda b: (b, 0, 0)),
                  pl.BlockSpec((None, N_CMP_S, KV_COLS), lambda b: (b, 0, 0)),
                  pl.BlockSpec(bias_c.shape, full2),
                  pl.BlockSpec(gsum.shape, full2),
                  pl.BlockSpec(msel.shape, full2),
                  pl.BlockSpec((None, wb, HEAD_DIM), lambda b: (layer * bsz + b, 0, 0)),
                  pl.BlockSpec((None, 1, KV_COLS), lambda b: (b, 0, 0)),
                  pl.BlockSpec(bias_w.shape, full2),
                  pl.BlockSpec(bias_new.shape, full2)],
        out_specs=[pl.BlockSpec((None, N_HEADS, HEAD_DIM), lambda b: (b, 0, 0)),
                   pl.BlockSpec((None, N_HEADS, HEAD_DIM), lambda b: (b, 0, 0)),
                   pl.BlockSpec((None, SUBLANES, LANES), lambda b: (b, 0, 0))],
        out_shape=[o, o, jax.ShapeDtypeStruct((bsz, SUBLANES, LANES), jnp.int32)],
        compiler_params=_params("parallel"),
        name="sample_attention",
    )(q3, kc, bias_c, gsum, msel, win_pool, kv_win_new, bias_w, bias_new)


def _sample_slc_kernel(idx_ref, pt_ref, q_ref, *refs):
    blk_refs, (new_ref, bias_ref, o_ref) = refs[:N_SELECT], refs[N_SELECT:]
    b, g = pl.program_id(0), pl.program_id(1)
    ks, vs, biases = [], [], []
    lane = lax.broadcasted_iota(jnp.int32, (N_HEADS, SEL_BLOCK), 1)
    for kk, blk_ref in enumerate(blk_refs):
        j = idx_ref[b, g, kk]
        is_new = j >= PAST_LEN // SEL_BLOCK
        k_blk = blk_ref[pl.ds(2 * g, SEL_BLOCK, stride=N_KV_SLABS), :]
        v_blk = blk_ref[pl.ds(2 * g + 1, SEL_BLOCK, stride=N_KV_SLABS), :]
        ks.append(jnp.where(is_new, jnp.broadcast_to(new_ref[:, :HEAD_DIM], k_blk.shape), k_blk).astype(BF16))
        vs.append(jnp.where(is_new, jnp.broadcast_to(new_ref[:, HEAD_DIM:], v_blk.shape), v_blk).astype(BF16))
        biases.append(bias_ref[j] + jnp.where(j * SEL_BLOCK + lane <= PAST_LEN, 0.0, NEG))
    k, v = jnp.concatenate(ks, axis=0), jnp.concatenate(vs, axis=0)
    s = lax.dot_general(q_ref[...], k, NT_DIMS, preferred_element_type=F32) * SCALE + jnp.concatenate(biases, axis=1)
    p = _softmax_rows(s, s > 0.5 * NEG)
    o_ref[...] = jnp.dot(p.astype(BF16), v, preferred_element_type=F32)


def sample_slc_attention(idx, page_table, q3, pool, n_phys, layer, kv_slc_new, bias_blk):
    bsz = q3.shape[0]
    half_per_page = PAGE_SIZE // SEL_BLOCK
    n_half = n_phys * half_per_page
    last_past = PAST_LEN // SEL_BLOCK - 1

    def blk_spec(kk):
        def blk_map(b, g, idx_r, pt_r):
            j = jnp.minimum(idx_r[b, g, kk], last_past)
            return (layer * n_half + pt_r[b, j // half_per_page] * half_per_page + j % half_per_page, 0, 0)
        return pl.BlockSpec((None, SEL_BLOCK * N_KV_SLABS, HEAD_DIM), blk_map)

    return pl.pallas_call(
        _sample_slc_kernel,
        grid_spec=pltpu.PrefetchScalarGridSpec(
            num_scalar_prefetch=2,
            grid=(bsz, KV_GROUPS),
            in_specs=[pl.BlockSpec((None, N_HEADS, HEAD_DIM), lambda b, g, i, p: (b, 0, 0))]
            + [blk_spec(kk) for kk in range(N_SELECT)]
            + [pl.BlockSpec((None, 1, 2 * HEAD_DIM), lambda b, g, i, p: (b, 0, g)),
               pl.BlockSpec(bias_blk.shape, lambda b, g, i, p: (0, 0, 0))],
            out_specs=pl.BlockSpec((None, None, N_HEADS, HEAD_DIM), lambda b, g, i, p: (b, g, 0, 0))),
        out_shape=jax.ShapeDtypeStruct((bsz, KV_GROUPS, N_HEADS, HEAD_DIM), F32),
        compiler_params=_params("parallel", "arbitrary"),
        name="sample_slc_attention",
    )(idx, page_table, q3, *([pool] * N_SELECT), kv_slc_new, bias_blk)


CONV_TILE = 128
CONV_HALO = 32
CONV_ROWS = 64


def _dwconv_ln_kernel(u_ref, halo_ref, w_ref, b_ref, g_ref, beta_ref, o_ref, buf_ref, y_ref, sh_ref, *, tiles_per_seq):
    first = pl.program_id(0) % tiles_per_seq == 0
    buf_ref[0:CONV_HALO, :] = jnp.where(first, 0.0, halo_ref[...])
    buf_ref[CONV_HALO:, :] = u_ref[...]
    lead = CONV_HALO - (CONV_WIDTH - 1)
    def lane_slab(ci, carry):
        cols = pl.ds(pl.multiple_of(ci * LANES, LANES), LANES)
        for r0 in range(0, CONV_TILE, CONV_ROWS):
            acc = jnp.broadcast_to(b_ref[:, cols], (CONV_ROWS, LANES))
            for res in range(SUBLANES):
                taps = [k for k in range(CONV_WIDTH) if (lead + k) % SUBLANES == res]
                span = max(lead + k - res for k in taps) + CONV_ROWS
                sh_ref[res, 0:span, :] = buf_ref[r0 + res:r0 + res + span, cols]
                for k in taps:
                    off = lead + k - res
                    acc = acc + w_ref[k:k + 1, cols] * sh_ref[res, off:off + CONV_ROWS, :]
            y_ref[r0:r0 + CONV_ROWS, cols] = acc
        return carry

    lax.fori_loop(0, D_MODEL // LANES, lane_slab, 0)
    y = y_ref[...]
    mu = jnp.mean(y, axis=-1, keepdims=True)
    var = jnp.mean(jnp.square(y - mu), axis=-1, keepdims=True)
    y = (y - mu) * lax.rsqrt(var + LN_EPS) * g_ref[...] + beta_ref[...]
    o_ref[...] = (y * jax.nn.sigmoid(y)).astype(o_ref.dtype)


def dwconv_ln(u, w, b, ln_g, ln_b, seq):
    r, d = u.shape
    tiles_per_seq = seq // CONV_TILE
    ratio = CONV_TILE // CONV_HALO
    vec = lambda m: (0, 0)
    return pl.pallas_call(
        functools.partial(_dwconv_ln_kernel, tiles_per_seq=tiles_per_seq),
        grid=(r // CONV_TILE,),
        in_specs=[pl.BlockSpec((CONV_TILE, d), lambda m: (m, 0)),
                  pl.BlockSpec((CONV_HALO, d), lambda m: (jnp.maximum(m * ratio - 1, 0), 0)),
                  pl.BlockSpec((CONV_WIDTH, d), vec), pl.BlockSpec((1, d), vec),
                  pl.BlockSpec((1, d), vec), pl.BlockSpec((1, d), vec)],
        out_specs=pl.BlockSpec((CONV_TILE, d), lambda m: (m, 0)),
        out_shape=jax.ShapeDtypeStruct((r, d), BF16),
        scratch_shapes=[pltpu.VMEM((CONV_HALO + CONV_TILE, d), F32), pltpu.VMEM((CONV_TILE, d), F32),
                        pltpu.VMEM((SUBLANES, CONV_HALO + CONV_ROWS, LANES), F32)],
        compiler_params=_params("parallel"),
        name="dwconv_ln",
    )(u, u, w, b.reshape(1, d), ln_g.reshape(1, d), ln_b.reshape(1, d))


def _dwconv_ln_decode_kernel(u_ref, hist_ref, w_ref, b_ref, g_ref, beta_ref, o_ref):
    y = b_ref[...] + w_ref[CONV_WIDTH - 1:CONV_WIDTH, :] * u_ref[...]
    for k in range(CONV_WIDTH - 1):
        y = y + w_ref[k:k + 1, :] * hist_ref[k]
    mu = jnp.mean(y, axis=-1, keepdims=True)
    var = jnp.mean(jnp.square(y - mu), axis=-1, keepdims=True)
    y = (y - mu) * lax.rsqrt(var + LN_EPS) * g_ref[...] + beta_ref[...]
    o_ref[...] = (y * jax.nn.sigmoid(y)).astype(o_ref.dtype)


def dwconv_ln_decode(u, hist_t, w, b, ln_g, ln_b):
    r, d = u.shape
    vec = lambda i: (0, 0)
    return pl.pallas_call(
        _dwconv_ln_decode_kernel,
        grid=(1,),
        in_specs=[pl.BlockSpec((r, d), vec), pl.BlockSpec(hist_t.shape, lambda i: (0, 0, 0)),
                  pl.BlockSpec((CONV_WIDTH, d), vec), pl.BlockSpec((1, d), vec),
                  pl.BlockSpec((1, d), vec), pl.BlockSpec((1, d), vec)],
        out_specs=pl.BlockSpec((r, d), vec),
        out_shape=jax.ShapeDtypeStruct((r, d), BF16),
        compiler_params=_params("arbitrary"),
        name="dwconv_ln_decode",
    )(u, hist_t, w, b.reshape(1, d), ln_g.reshape(1, d), ln_b.reshape(1, d))


S5_CHUNK = 256
S5_PITCH = S5_CHUNK + 4
IN_SLABS = D_MODEL // LANES
STATE_PER_IN = N_SLAB // IN_SLABS


def _s5_project_in(hb, wb_ref, store):
    half = STATE_PER_IN * LANES
    for i in range(IN_SLABS):
        res = jnp.dot(hb[:, i * LANES:(i + 1) * LANES], wb_ref[i], preferred_element_type=F32)
        for jj in range(STATE_PER_IN):
            store(i * STATE_PER_IN + jj, res[:, jj * LANES:(jj + 1) * LANES],
                  res[:, half + jj * LANES:half + (jj + 1) * LANES])


def _s5_project_out(load, cre_ref, cim_ref, hn, d_ref, y_ref):
    for i in range(IN_SLABS):
        cols = slice(i * LANES, (i + 1) * LANES)
        acc = d_ref[:, cols] * hn[:, cols]
        for jj in range(STATE_PER_IN):
            j = i * STATE_PER_IN + jj
            re, im = load(j)
            acc = acc + jnp.dot(re.astype(BF16), cre_ref[j], preferred_element_type=F32)
            acc = acc - jnp.dot(im.astype(BF16), cim_ref[j], preferred_element_type=F32)
        y_ref[:, cols] = acc.astype(y_ref.dtype)


def _s5_scan_kernel(x_ref, g_ref, wb_ref, ar_ref, ai_ref, cre_ref, cim_ref, d_ref,
                    y_ref, sr_ref, si_ref, bur_ref, bui_ref, hr_ref, hi_ref):
    tc, pitch = S5_CHUNK, S5_PITCH

    @pl.when(pl.program_id(1) == 0)
    def _():
        hr_ref[...] = jnp.zeros_like(hr_ref)
        hi_ref[...] = jnp.zeros_like(hi_ref)

    hn = _rms(x_ref[...], g_ref[...])

    def store(j, re, im):
        bur_ref[j * pitch:j * pitch + tc, :] = re
        bui_ref[j * pitch:j * pitch + tc, :] = im

    _s5_project_in(hn.astype(BF16), wb_ref, store)

    ar = [ar_ref[j8] for j8 in range(SLAB_GRP)]
    ai = [ai_ref[j8] for j8 in range(SLAB_GRP)]

    def step(t, carry):
        out = []
        for j8 in range(SLAB_GRP):
            hr, hi = carry[2 * j8], carry[2 * j8 + 1]
            rows = pl.ds(j8 * SUBLANES * pitch + t, SUBLANES, stride=pitch)
            nr = ar[j8] * hr - ai[j8] * hi + bur_ref[rows, :]
            ni = ar[j8] * hi + ai[j8] * hr + bui_ref[rows, :]
            bur_ref[rows, :] = nr
            bui_ref[rows, :] = ni
            out += [nr, ni]
        return tuple(out)

    init = []
    for j8 in range(SLAB_GRP):
        init += [hr_ref[j8], hi_ref[j8]]
    fin = lax.fori_loop(0, tc, step, tuple(init))
    for j8 in range(SLAB_GRP):
        hr_ref[j8] = fin[2 * j8]
        hi_ref[j8] = fin[2 * j8 + 1]
    sr_ref[...] = hr_ref[...]
    si_ref[...] = hi_ref[...]

    def load(j):
        return bur_ref[j * pitch:j * pitch + tc, :], bui_ref[j * pitch:j * pitch + tc, :]

    _s5_project_out(load, cre_ref, cim_ref, hn, d_ref, y_ref)


def s5_scan(x, g, wb, ar, ai, cre, cim, d_skip, bsz, seq):
    n_chunks = seq // S5_CHUNK
    st = jax.ShapeDtypeStruct((bsz, SLAB_GRP, SUBLANES, LANES), F32)
    st_spec = pl.BlockSpec((None, SLAB_GRP, SUBLANES, LANES), lambda b, c: (b, 0, 0, 0))
    vec = lambda b, c: (0, 0)
    c3 = lambda b, c: (0, 0, 0)
    return pl.pallas_call(
        _s5_scan_kernel,
        grid=(bsz, n_chunks),
        in_specs=[pl.BlockSpec((S5_CHUNK, D_MODEL), lambda b, c: (b * n_chunks + c, 0)),
                  pl.BlockSpec((1, D_MODEL), vec),
                  pl.BlockSpec(wb.shape, c3), pl.BlockSpec(ar.shape, c3), pl.BlockSpec(ai.shape, c3),
                  pl.BlockSpec(cre.shape, c3), pl.BlockSpec(cim.shape, c3),
                  pl.BlockSpec((1, D_MODEL), vec)],
        out_specs=[pl.BlockSpec((S5_CHUNK, D_MODEL), lambda b, c: (b * n_chunks + c, 0)), st_spec, st_spec],
        out_shape=[jax.ShapeDtypeStruct((bsz * seq, D_MODEL), BF16), st, st],
        scratch_shapes=[pltpu.VMEM((N_SLAB * S5_PITCH, LANES), F32), pltpu.VMEM((N_SLAB * S5_PITCH, LANES), F32),
                        pltpu.VMEM((SLAB_GRP, SUBLANES, LANES), F32), pltpu.VMEM((SLAB_GRP, SUBLANES, LANES), F32)],
        compiler_params=_params("parallel", "arbitrary"),
        name="s5_scan",
    )(x, g.reshape(1, -1), wb, ar, ai, cre, cim, d_skip.reshape(1, -1))


def _s5_decode_kernel(x_ref, g_ref, wb_ref, ar_ref, ai_ref, cre_ref, cim_ref, d_ref, h0r_ref, h0i_ref,
                      y_ref, sr_ref, si_ref):
    hn = _rms(x_ref[...], g_ref[...])

    def store(j, re, im):
        cols = slice(j * LANES, (j + 1) * LANES)
        ar, ai = ar_ref[:, cols], ai_ref[:, cols]
        hr, hi = h0r_ref[:, cols], h0i_ref[:, cols]
        sr_ref[:, cols] = ar * hr - ai * hi + re
        si_ref[:, cols] = ar * hi + ai * hr + im

    _s5_project_in(hn.astype(BF16), wb_ref, store)

    def load(j):
        cols = slice(j * LANES, (j + 1) * LANES)
        return sr_ref[:, cols], si_ref[:, cols]

    _s5_project_out(load, cre_ref, cim_ref, hn, d_ref, y_ref)


def s5_decode(x, g, wb, ar_row, ai_row, cre, cim, d_skip, h0r, h0i):
    r = x.shape[0]
    st = jax.ShapeDtypeStruct((r, SSM_DIM), F32)
    vec = lambda i: (0, 0)
    c3 = lambda i: (0, 0, 0)
    return pl.pallas_call(
        _s5_decode_kernel,
        grid=(1,),
        in_specs=[pl.BlockSpec((r, D_MODEL), vec), pl.BlockSpec((1, D_MODEL), vec),
                  pl.BlockSpec(wb.shape, c3), pl.BlockSpec((1, SSM_DIM), vec), pl.BlockSpec((1, SSM_DIM), vec),
                  pl.BlockSpec(cre.shape, c3), pl.BlockSpec(cim.shape, c3), pl.BlockSpec((1, D_MODEL), vec),
                  pl.BlockSpec((r, SSM_DIM), vec), pl.BlockSpec((r, SSM_DIM), vec)],
        out_specs=[pl.BlockSpec((r, D_MODEL), vec), pl.BlockSpec((r, SSM_DIM), vec), pl.BlockSpec((r, SSM_DIM), vec)],
        out_shape=[jax.ShapeDtypeStruct((r, D_MODEL), BF16), st, st],
        compiler_params=_params("arbitrary"),
        name="s5_decode",
    )(x, g.reshape(1, -1), wb, ar_row, ai_row, cre, cim, d_skip.reshape(1, -1), h0r, h0i)


def _t5_bucket(rel):
    n = jnp.maximum(rel, 0)
    nf = jnp.maximum(n, MAX_EXACT).astype(F32)
    big = MAX_EXACT + (jnp.log(nf / MAX_EXACT) / math.log(REL_MAX_DIST / MAX_EXACT)
                       * (N_BUCKETS - MAX_EXACT)).astype(jnp.int32)
    return jnp.where(n < MAX_EXACT, n, jnp.minimum(big, N_BUCKETS - 1))


def _bias_of(rel_bias, rel):
    onehot = (_t5_bucket(rel)[..., None] == jnp.arange(N_BUCKETS, dtype=jnp.int32)).astype(F32)
    return jnp.einsum('...k,kh->h...', onehot, rel_bias.astype(F32), precision=HIGHEST)


def _selection_matrix(n_cmp, n_sel_pad):
    coef = np.convolve(np.ones(SEL_RATIO), np.ones(CMP_BLOCK // CMP_STRIDE)).astype(np.float32)
    m = np.zeros((n_cmp, n_sel_pad), np.float32)
    for j in range(n_sel_pad):
        for o in range(coef.shape[0]):
            n = SEL_RATIO * j + o - (CMP_BLOCK // CMP_STRIDE - 1)
            if 0 <= n < n_cmp:
                m[n, j] = coef[o]
    return m


def _s5_params(a_re, a_im, log_dt, b_re, b_im, c_re, c_im):
    dt = jnp.exp(log_dt.astype(F32))[:, None]
    ar, ai = a_re.astype(F32), a_im.astype(F32)
    mag = jnp.exp(ar * dt)
    abar_re, abar_im = mag * jnp.cos(ai * dt), mag * jnp.sin(ai * dt)
    den = ar * ar + ai * ai
    coef_re = ((abar_re - 1.0) * ar + abar_im * ai) / den
    coef_im = (abar_im * ar - (abar_re - 1.0) * ai) / den
    br, bim = b_re.astype(F32), b_im.astype(F32)
    bb_re = coef_re[..., None] * br - coef_im[..., None] * bim
    bb_im = coef_re[..., None] * bim + coef_im[..., None] * br
    gpi = LANES // SSM_GROUP_CH
    eye = jnp.eye(gpi, dtype=F32)

    def in_blocks(bb):
        t = bb.reshape(IN_SLABS, gpi, SSM_STATE, SSM_GROUP_CH)
        blk = jnp.einsum('sgpc,gh->sgchp', t, eye)
        return blk.reshape(IN_SLABS, LANES, gpi * SSM_STATE)

    wb = jnp.concatenate([in_blocks(bb_re), in_blocks(bb_im)], axis=-1).astype(BF16)
    gps = LANES // SSM_STATE
    ch_per_in = LANES

    def out_blocks(c):
        t = c.astype(F32).reshape(IN_SLABS, STATE_PER_IN, gps, SSM_GROUP_CH, SSM_STATE)
        sel = jnp.eye(STATE_PER_IN * gps, dtype=F32).reshape(STATE_PER_IN, gps, STATE_PER_IN * gps)
        blk = jnp.einsum('ijgcp,jgh->ijgphc', t, sel)
        return blk.reshape(N_SLAB, LANES, ch_per_in).astype(BF16)

    return abar_re, abar_im, wb, out_blocks(c_re), out_blocks(c_im)


PROMPT_TM = 512
TN = 512
FFN_UP_TM = 1024
FFN_DOWN_TK = D_FF // 4
GLU_TM = 256
GLU_TK = D_MODEL


def _row_tile(r):
    return PROMPT_TM if r % PROMPT_TM == 0 else r


def _nsa_project(h, wq, wkv, wg):
    tm = _row_tile(h.shape[0])
    q = matmul(h, wq, n_split=1, out_dtype=BF16, act=None, tm=tm, tn=TN, name="nsa_q")[0]
    kv = matmul(h, wkv, n_split=3, out_dtype=F32, act=None, tm=tm, tn=TN, name="nsa_kv")
    gate = matmul(h, wg, n_split=1, out_dtype=F32, act="sigmoid", tm=tm, tn=LANES, name="nsa_gate")[0]
    return q, kv, gate


def _nsa_layer(hp, hs, xp, xs, g_post, g_next, wts, tabs, caches, ret, layer, page_table, bsz, seq):
    wq, wkv, wg, wo, pe, w1, w2 = wts
    cmp_pool, slc_pool, win_pool, n_phys = caches
    dec = hs.shape[0]
    q = matmul(hp, wq, n_split=1, out_dtype=BF16, act=None, tm=PROMPT_TM, tn=2 * TN, name="nsa_q")[0]
    kv, ret_cmp, ret_slc, vt = kv_project(hp, wkv, ret[0], ret[1], layer, tm=PROMPT_TM)
    gate = matmul(hp, wg, n_split=1, out_dtype=F32, act="sigmoid", tm=PROMPT_TM, tn=LANES, name="nsa_gate")[0]
    part_a, part_b = cmp_partial_rows(kv[0], pe, w1, seq)
    kc = cmp_finish(part_a, part_b, jnp.zeros((bsz, SUBLANES, KV_COLS), F32), w2)
    o_cmp, mask = cmp_attention(q, kc, tabs["cmp"], tabs["msel"], tabs["expand"], bsz, seq)
    o_slc = slc_attention(q, kv[1], vt, mask, tabs["tile_t"], bsz, seq)
    o = win_attention(q, kv[2], tabs["win"], o_cmp, o_slc, gate, bsz, seq)
    xp, hp = proj_res(o, wo, xp, g_post, g_next, glu=False, tm=PROMPT_TM, tk=D_MODEL, name="nsa_out")
    qs, kvs, gate_s = _nsa_project(hs, wq, wkv, wg)
    past_a, past_b = cmp_partial_paged(cmp_pool, n_phys, page_table, layer, pe, w1)
    tail = jnp.pad(kvs[0][:, None, :], ((0, 0), (0, CMP_STRIDE - 1), (0, 0))).reshape(dec * CMP_STRIDE, KV_COLS)
    _, tail_b = cmp_partial_rows(tail, pe, w1, dec * CMP_STRIDE)
    b_next = jnp.pad(tail_b[0][:, None, :], ((0, 0), (0, SUBLANES - 1), (0, 0)))
    kc_s = cmp_finish(past_a, past_b, b_next, w2)
    oc_s, ow_s, idx = sample_attention(qs.reshape(dec, N_HEADS, HEAD_DIM), kc_s, tabs["cmp_s"], tabs["gsum"],
                                       tabs["msel_s"], win_pool, layer, kvs[2][:, None, :],
                                       tabs["win_s"], tabs["new_s"])
    q3 = qs.reshape(dec, N_HEADS, HEAD_DIM)
    os_all = sample_slc_attention(idx[:, :KV_GROUPS, :N_SELECT], page_table, q3, slc_pool, n_phys, layer,
                                  kvs[1][:, None, :], tabs["slc_s"])
    os_s = jnp.stack([os_all[:, h // GROUP_SIZE, h] for h in range(N_HEADS)], axis=1)
    o_s = nsa_combine(oc_s.reshape(dec, D_MODEL), os_s.reshape(dec, D_MODEL), ow_s.reshape(dec, D_MODEL), gate_s, dec)
    xs, hs = proj_res(o_s, wo, xs, g_post, g_next, glu=False, tm=dec, tk=D_MODEL, name="nsa_out_s")
    return xp, hp, xs, hs, kv, kvs, (ret_cmp, ret_slc)


def _conv_layer(hp, hs, xp, xs, g_post, g_next, wts, state, bsz, seq):
    w_pw1, dw, dw_b, ln_g, ln_b, w_pw2 = wts
    dec = hs.shape[0]
    u = glu_matmul(hp, w_pw1, tm=PROMPT_TM, tn=TN, name="conv_pw1")
    hc = dwconv_ln(u, dw, dw_b, ln_g, ln_b, seq)
    xp, hp = proj_res(hc, w_pw2, xp, g_post, g_next, glu=False, tm=PROMPT_TM, tk=D_MODEL, name="conv_pw2")
    hist_p = u.reshape(bsz, seq, D_MODEL)[:, seq - (CONV_WIDTH - 1):]
    us = glu_matmul(hs, w_pw1, tm=dec, tn=TN, name="conv_pw1_s")
    hc_s = dwconv_ln_decode(us, jnp.swapaxes(state, 0, 1), dw, dw_b, ln_g, ln_b)
    xs, hs = proj_res(hc_s, w_pw2, xs, g_post, g_next, glu=False, tm=dec, tk=D_MODEL, name="conv_pw2_s")
    hist_s = jnp.concatenate([state[:, 1:], us[:, None, :]], axis=1)
    return xp, hp, xs, hs, hist_p, hist_s


def _s5_layer(xp, xs, g_pre, g_post, g_next, wts, state_re, state_im, bsz, seq):
    a_re, a_im, log_dt, b_re, b_im, c_re, c_im, d_skip, w_glu = wts
    dec = xs.shape[0]
    abar_re, abar_im, wb, cre, cim = _s5_params(a_re, a_im, log_dt, b_re, b_im, c_re, c_im)
    slab_shape = (SLAB_GRP, SUBLANES, LANES)
    y, sr, si = s5_scan(xp, g_pre, wb, abar_re.reshape(slab_shape), abar_im.reshape(slab_shape), cre, cim,
                        d_skip, bsz, seq)
    xp, hp = proj_res(y, w_glu, xp, g_post, g_next, glu=True, tm=GLU_TM, tk=GLU_TK, name="s5_glu")
    ys, sr_s, si_s = s5_decode(xs, g_pre, wb, abar_re.reshape(1, SSM_DIM), abar_im.reshape(1, SSM_DIM), cre, cim,
                               d_skip, state_re.reshape(dec, SSM_DIM), state_im.reshape(dec, SSM_DIM))
    xs, hs = proj_res(ys, w_glu, xs, g_post, g_next, glu=True, tm=dec, tk=GLU_TK, name="s5_glu_s")
    gp = (SSM_GROUPS, SSM_STATE)
    return (xp, hp, xs, hs, sr.reshape((bsz,) + gp), si.reshape((bsz,) + gp),
            sr_s.reshape((dec,) + gp), si_s.reshape((dec,) + gp))


def _ffn_layer(hp, hs, xp, xs, g_post, g_next, wts, state, bsz, seq):
    w_up, layer, dw, dw_b, w_down = wts
    dec = hs.shape[0]
    act, hist = ffn_up(hp, w_up, layer, dw, dw_b, seq_len=seq, tm=FFN_UP_TM, tn=TN)
    xp, hp = proj_res(act, w_down, xp, g_post, g_next, glu=False, tm=PROMPT_TM, tk=FFN_DOWN_TK, name="ffn_down")
    tiles = seq // FFN_UP_TM
    hist_p = hist.reshape(bsz, tiles, SUBLANES, D_FF)[:, tiles - 1, SUBLANES - 2:, :]
    act_s, gate_s = ffn_up(hs, w_up, layer, dw, dw_b, hist=state, seq_len=1, tm=dec, tn=TN)
    xs, hs = proj_res(act_s, w_down, xs, g_post, g_next, glu=False, tm=dec, tk=FFN_DOWN_TK, name="ffn_down_s")
    hist_s = jnp.concatenate([state[:, 1:], gate_s.reshape(dec, 1, D_FF)], axis=1)
    return xp, hp, xs, hs, hist_p, hist_s


def _bias_tables(rel_bias, seq):
    tq = ATT_TILE
    i = jnp.arange(tq, dtype=jnp.int32)
    tile = jnp.stack([_bias_of(rel_bias, d * tq + i[:, None] - i[None, :]) for d in range(3)])
    n_cmp = seq // CMP_STRIDE
    cpos = jnp.arange(n_cmp, dtype=jnp.int32) * CMP_STRIDE + (CMP_BLOCK - 1)
    qpos = jnp.arange(seq, dtype=jnp.int32)
    n_sel = -(-seq // SEL_BLOCK)
    key = np.arange(seq)
    expand = (key[:, None] // SEL_BLOCK == np.arange(LANES)[None, :]).astype(np.float32)
    cpos_s = jnp.arange(N_CMP_S, dtype=jnp.int32) * CMP_STRIDE + (CMP_BLOCK - 1)
    wb = min(WINDOW, PAST_LEN)
    kpos_s = jnp.arange(N_SEL_S * SEL_BLOCK, dtype=jnp.int32)
    slc_s = _bias_of(rel_bias, PAST_LEN - kpos_s).reshape(N_HEADS, N_SEL_S, SEL_BLOCK)
    gsum = (np.arange(N_HEADS)[None, :] // GROUP_SIZE == np.arange(SUBLANES)[:, None]).astype(np.float32)
    return {
        "tile_t": jnp.swapaxes(tile, -1, -2),
        "win": _bias_of(rel_bias, i[:, None] + WINDOW - jnp.arange(WIN_SPAN, dtype=jnp.int32)[None, :]),
        "cmp": _bias_of(rel_bias, qpos[:, None] - cpos[None, :]),
        "msel": jnp.asarray(_selection_matrix(n_cmp, n_sel).T),
        "expand": jnp.asarray(expand, BF16),
        "cmp_s": _bias_of(rel_bias, PAST_LEN - cpos_s),
        "msel_s": jnp.asarray(_selection_matrix(N_CMP_S, N_SEL_S_PAD)),
        "gsum": jnp.asarray(gsum),
        "win_s": _bias_of(rel_bias, wb - jnp.arange(wb, dtype=jnp.int32)),
        "new_s": jnp.broadcast_to(_bias_of(rel_bias, jnp.zeros((1,), jnp.int32)), (N_HEADS, LANES)),
        "slc_s": jnp.swapaxes(slc_s, 0, 1),
    }


def kernel(x_prompt, x_sample, cache_cmp_kv, cache_slc_kv, cache_win_kv, state_conv, state_ssm_re, state_ssm_im, state_ffn_conv, page_table, norm_gain, rel_bias, nsa_w_q, nsa_w_kv, nsa_cmp_pe, nsa_cmp_w1, nsa_cmp_w2, nsa_w_gate, nsa_w_o, conv_w_pw1, conv_dw, conv_dw_b, conv_ln_g, conv_ln_b, conv_w_pw2, ssm_a_re, ssm_a_im, ssm_log_dt, ssm_b_re, ssm_b_im, ssm_c_re, ssm_c_im, ssm_d, ssm_w_glu, ffn_w_up, ffn_dw, ffn_dw_b, ffn_w_down):
    bsz, seq, d = x_prompt.shape
    dec, dec_seq, _ = x_sample.shape
    assert dec_seq == 1 and d == D_MODEL and seq % PROMPT_TM == 0 and seq // CMP_STRIDE == LANES
    n_nsa = cache_cmp_kv.shape[0]
    n_phys = cache_cmp_kv.shape[1]
    xp = x_prompt.reshape(bsz * seq, d)
    xs = x_sample.reshape(dec, d)
    tabs = _bias_tables(rel_bias, seq)
    cmp_pool = cache_cmp_kv.reshape(n_nsa * n_phys, PAGE_SIZE * N_KV_SLABS, HEAD_DIM)
    slc_pool = cache_slc_kv.reshape(n_nsa * n_phys * (PAGE_SIZE // SEL_BLOCK), SEL_BLOCK * N_KV_SLABS, HEAD_DIM)
    win_pool = cache_win_kv.reshape(n_nsa * dec, cache_win_kv.shape[2] * N_KV_SLABS, HEAD_DIM)
    g_all = norm_gain.astype(F32)

    hp = rmsnorm_cast(xp, g_all[0, 0], PROMPT_TM)
    hs = rmsnorm_cast(xs, g_all[0, 0], dec)
    ret = tuple(jnp.zeros((n_nsa, bsz * seq * N_KV_SLABS, HEAD_DIM), F32) for _ in range(2))
    ret_shape = (n_nsa, bsz, seq, KV_GROUPS, 2, HEAD_DIM)
    out = {k: [] for k in ("cmp_s", "slc_s", "win_p", "win_s", "conv_p", "conv_s",
                           "re_p", "re_s", "im_p", "im_s", "ffn_p", "ffn_s")}
    counts = [0, 0, 0]
    for i in range(DEPTH):
        m = i % 3
        j = counts[m]
        counts[m] += 1
        g_post, g_ffn, g_ffn_post = g_all[i, 1], g_all[i, 2], g_all[i, 3]
        g_next = g_all[i + 1, 0] if i + 1 < DEPTH else g_all[i, 0]
        if m == 0:
            wg = jnp.pad(nsa_w_gate[j], ((0, 0), (0, LANES - nsa_w_gate.shape[-1]))).astype(BF16)
            wts = (nsa_w_q[j].astype(BF16), nsa_w_kv[j].astype(BF16), wg, nsa_w_o[j].astype(BF16),
                   jnp.swapaxes(nsa_cmp_pe[j], 0, 1).astype(F32), nsa_cmp_w1[j].astype(BF16),
                   nsa_cmp_w2[j].astype(BF16))
            xp, hp, xs, hs, kv, kvs, ret = _nsa_layer(hp, hs, xp, xs, g_post, g_ffn, wts, tabs,
                                                      (cmp_pool, slc_pool, win_pool, n_phys), ret, j, page_table,
                                                      bsz, seq)
            shp = (bsz, seq, KV_GROUPS, 2, HEAD_DIM)
            shs = (dec, 1, KV_GROUPS, 2, HEAD_DIM)
            out["cmp_s"].append(kvs[0].reshape(shs))
            out["slc_s"].append(kvs[1].reshape(shs))
            out["win_p"].append(kv[2].reshape(shp)[:, seq - min(WINDOW, seq):])
            win_full = jnp.concatenate([cache_win_kv[j], kvs[2].reshape(shs)], axis=1)
            out["win_s"].append(win_full[:, win_full.shape[1] - min(WINDOW, PAST_LEN + 1):])
        elif m == 1:
            wts = (conv_w_pw1[j].astype(BF16), conv_dw[j], conv_dw_b[j], conv_ln_g[j], conv_ln_b[j],
                   conv_w_pw2[j].astype(BF16))
            xp, hp, xs, hs, cp, cs = _conv_layer(hp, hs, xp, xs, g_post, g_ffn, wts, state_conv[j], bsz, seq)
            out["conv_p"].append(cp); out["conv_s"].append(cs)
        else:
            wts = (ssm_a_re[j], ssm_a_im[j], ssm_log_dt[j], ssm_b_re[j], ssm_b_im[j], ssm_c_re[j], ssm_c_im[j],
                   ssm_d[j], ssm_w_glu[j].astype(BF16))
            xp, hp, xs, hs, rp, ip, rs, is_ = _s5_layer(xp, xs, g_all[i, 0], g_post, g_ffn, wts,
                                                        state_ssm_re[j], state_ssm_im[j], bsz, seq)
            out["re_p"].append(rp); out["im_p"].append(ip); out["re_s"].append(rs); out["im_s"].append(is_)
        wts = (ffn_w_up, i, ffn_dw[i], ffn_dw_b[i], ffn_w_down[i].astype(BF16))
        xp, hp, xs, hs, fp, fs = _ffn_layer(hp, hs, xp, xs, g_ffn_post, g_next, wts, state_ffn_conv[i], bsz, seq)
        out["ffn_p"].append(fp); out["ffn_s"].append(fs)
    st = lambda k: jnp.stack(out[k])
    return (xp.reshape(bsz, seq, d), xs.reshape(dec, 1, d),
            ret[0].reshape(ret_shape), st("cmp_s"), ret[1].reshape(ret_shape), st("slc_s"), st("win_p"), st("win_s"),
            st("conv_p"), st("conv_s"), st("re_p"), st("re_s"), st("im_p"), st("im_s"),
            st("ffn_p"), st("ffn_s"))
```

```python
import functools
import math

import numpy as np
import jax
import jax.numpy as jnp
from jax import lax
from jax.experimental import pallas as pl
from jax.experimental.pallas import tpu as pltpu

F32 = jnp.float32
BF16 = jnp.bfloat16

D_MODEL = 2048
DEPTH = 4
PAST_LEN = 16384
PAGE_SIZE = 128
N_HEADS = 16
HEAD_DIM = 128
KV_GROUPS = 4
GROUP_SIZE = 4
KV_COLS = KV_GROUPS * 2 * HEAD_DIM
CMP_BLOCK = 32
CMP_STRIDE = 16
SEL_BLOCK = 64
SEL_RATIO = SEL_BLOCK // CMP_STRIDE
N_SELECT = 16
WINDOW = 512
FORCE_BONUS = 1.0e4
N_BUCKETS = 32
MAX_EXACT = 16
REL_MAX_DIST = 128
CONV_WIDTH = 31
SSM_GROUP_CH = 16
SSM_GROUPS = 128
SSM_STATE = 64
SSM_DIM = SSM_GROUPS * SSM_STATE
D_FF = 5632
RMS_EPS = 1e-6
LN_EPS = 1e-5
SCALE = HEAD_DIM ** -0.5
NEG = -1e30

LANES = 128
SUBLANES = 8
VMEM_LIMIT = 56 * 1024 * 1024
ATT_TILE = 128
N_SLAB = SSM_DIM // LANES
SLAB_GRP = N_SLAB // SUBLANES

HIGHEST = lax.Precision.HIGHEST
NT_DIMS = (((1,), (1,)), ((), ()))


def _params(*sem):
    return pltpu.CompilerParams(dimension_semantics=sem, vmem_limit_bytes=VMEM_LIMIT)


def _rms(x, g):
    return x * lax.rsqrt(jnp.mean(x * x, axis=-1, keepdims=True) + RMS_EPS) * g


def _rmsnorm_kernel(x_ref, g_ref, o_ref):
    o_ref[...] = _rms(x_ref[...], g_ref[...]).astype(o_ref.dtype)


def rmsnorm_cast(x, g, tm):
    r, d = x.shape
    return pl.pallas_call(
        _rmsnorm_kernel,
        grid=(r // tm,),
        in_specs=[pl.BlockSpec((tm, d), lambda m: (m, 0)), pl.BlockSpec((1, d), lambda m: (0, 0))],
        out_specs=pl.BlockSpec((tm, d), lambda m: (m, 0)),
        out_shape=jax.ShapeDtypeStruct((r, d), BF16),
        compiler_params=_params("parallel"),
        name="rmsnorm",
    )(x, g.reshape(1, d))


def _mm_kernel(a_ref, w_ref, o_ref, *, act):
    r = jnp.dot(a_ref[...], w_ref[...], preferred_element_type=F32)
    if act == "sigmoid":
        r = jax.nn.sigmoid(r)
    o_ref[...] = r.astype(o_ref.dtype)


def matmul(a, w, *, n_split, out_dtype, act, tm, tn, name):
    r, k = a.shape
    n = w.shape[1]
    per = (n // n_split) // tn
    return pl.pallas_call(
        functools.partial(_mm_kernel, act=act),
        grid=(r // tm, n // tn),
        in_specs=[pl.BlockSpec((tm, k), lambda m, j: (m, 0)), pl.BlockSpec((k, tn), lambda m, j: (0, j))],
        out_specs=pl.BlockSpec((None, tm, tn), lambda m, j: (j // per, m, j % per)),
        out_shape=jax.ShapeDtypeStruct((n_split, r, n // n_split), out_dtype),
        compiler_params=_params("parallel", "arbitrary"),
        name=name,
    )(a, w)


KV_TN = 512


def _kv_project_kernel(a_ref, w_ref, prev_cmp_ref, prev_slc_ref, kv_ref, cmp_ref, slc_ref, vt_ref):
    del prev_cmp_ref, prev_slc_ref
    n = pl.program_id(1)
    res = jnp.dot(a_ref[...], w_ref[...], preferred_element_type=F32)
    kv_ref[...] = res
    tm = res.shape[0]
    slabs = KV_TN // HEAD_DIM
    first = (n % 2) * slabs

    @pl.when((n >= 2) & (n < 4))
    def _():
        for i in range(slabs // 2):
            v = res[:, (2 * i + 1) * HEAD_DIM:(2 * i + 2) * HEAD_DIM]
            vt_ref[i * HEAD_DIM:(i + 1) * HEAD_DIM, :] = v.T.astype(vt_ref.dtype)

    def scatter(out_ref):
        for sl in range(slabs):
            out_ref[pl.ds(first + sl, tm, stride=N_KV_SLABS), :] = res[:, sl * HEAD_DIM:(sl + 1) * HEAD_DIM]

    @pl.when(n < 2)
    def _():
        scatter(cmp_ref)

    @pl.when((n >= 2) & (n < 4))
    def _():
        scatter(slc_ref)


def kv_project(a, w_kv, prev_cmp, prev_slc, layer, *, tm):
    r, k = a.shape
    n_tiles = w_kv.shape[1] // KV_TN
    per = KV_COLS // KV_TN
    nat_spec = pl.BlockSpec((None, tm * N_KV_SLABS, HEAD_DIM), lambda m, j: (layer, m, 0))
    any_spec = pl.BlockSpec(memory_space=pl.ANY)
    return pl.pallas_call(
        _kv_project_kernel,
        grid=(r // tm, n_tiles),
        in_specs=[pl.BlockSpec((tm, k), lambda m, j: (m, 0)), pl.BlockSpec((k, KV_TN), lambda m, j: (0, j)),
                  any_spec, any_spec],
        out_specs=[pl.BlockSpec((None, tm, KV_TN), lambda m, j: (j // per, m, j % per)), nat_spec, nat_spec,
                   pl.BlockSpec((KV_TN // 2, tm), lambda m, j: (jnp.clip(j - 2, 0, 1), m))],
        out_shape=[jax.ShapeDtypeStruct((3, r, KV_COLS), F32),
                   jax.ShapeDtypeStruct(prev_cmp.shape, F32), jax.ShapeDtypeStruct(prev_slc.shape, F32),
                   jax.ShapeDtypeStruct((KV_GROUPS * HEAD_DIM, r), BF16)],
        input_output_aliases={2: 1, 3: 2},
        compiler_params=_params("parallel", "arbitrary"),
        name="nsa_kv",
    )(a, w_kv, prev_cmp, prev_slc)


def _glu_mm_kernel(a_ref, wa_ref, wb_ref, o_ref):
    a = a_ref[...]
    lin = jnp.dot(a, wa_ref[...], preferred_element_type=F32)
    gate = jnp.dot(a, wb_ref[...], preferred_element_type=F32)
    o_ref[...] = lin * jax.nn.sigmoid(gate)


def glu_matmul(a, w, *, tm, tn, name):
    r, k = a.shape
    n = w.shape[1] // 2
    nb = n // tn
    return pl.pallas_call(
        _glu_mm_kernel,
        grid=(r // tm, nb),
        in_specs=[pl.BlockSpec((tm, k), lambda m, j: (m, 0)),
                  pl.BlockSpec((k, tn), lambda m, j: (0, j)),
                  pl.BlockSpec((k, tn), lambda m, j: (0, j + nb))],
        out_specs=pl.BlockSpec((tm, tn), lambda m, j: (m, j)),
        out_shape=jax.ShapeDtypeStruct((r, n), F32),
        compiler_params=_params("parallel", "arbitrary"),
        name=name,
    )(a, w, w)


def _proj_res_kernel(a_ref, w_ref, x_ref, gp_ref, gn_ref, xo_ref, ho_ref, *acc, glu, nk):
    def finish(y):
        if glu:
            d = y.shape[1] // 2
            y = y[:, :d] * jax.nn.sigmoid(y[:, d:])
        xn = x_ref[...] + _rms(y, gp_ref[...])
        xo_ref[...] = xn
        ho_ref[...] = _rms(xn, gn_ref[...]).astype(ho_ref.dtype)

    part = jnp.dot(a_ref[...], w_ref[...], preferred_element_type=F32)
    if nk == 1:
        finish(part)
        return
    acc_ref, = acc
    k = pl.program_id(1)

    @pl.when(k == 0)
    def _():
        acc_ref[...] = part

    @pl.when((k > 0) & (k < nk - 1))
    def _():
        acc_ref[...] += part

    @pl.when(k == nk - 1)
    def _():
        finish(acc_ref[...] + part)


def proj_res(a, w, x, g_post, g_next, *, glu, tm, tk, name):
    r, kdim = a.shape
    n = w.shape[1]
    d = x.shape[1]
    nk = kdim // tk
    return pl.pallas_call(
        functools.partial(_proj_res_kernel, glu=glu, nk=nk),
        grid=(r // tm, nk),
        in_specs=[pl.BlockSpec((tm, tk), lambda m, k: (m, k)),
                  pl.BlockSpec((tk, n), lambda m, k: (k, 0)),
                  pl.BlockSpec((tm, d), lambda m, k: (m, 0)),
                  pl.BlockSpec((1, d), lambda m, k: (0, 0)),
                  pl.BlockSpec((1, d), lambda m, k: (0, 0))],
        out_specs=[pl.BlockSpec((tm, d), lambda m, k: (m, 0)),
                   pl.BlockSpec((tm, d), lambda m, k: (m, 0))],
        out_shape=[jax.ShapeDtypeStruct((r, d), F32), jax.ShapeDtypeStruct((r, d), BF16)],
        scratch_shapes=[pltpu.VMEM((tm, n), F32)] if nk > 1 else [],
        compiler_params=_params("parallel", "arbitrary"),
        name=name,
    )(a, w, x, g_post.reshape(1, d), g_next.reshape(1, d))


def _ffn_up_kernel(a_ref, p1_ref, p2_ref, wg_ref, wv_ref, dw_ref, db_ref, act_ref, hist_ref, *, decode, tiles_per_seq):
    a = a_ref[...]
    wg = wg_ref[...].astype(BF16)
    gate = jnp.dot(a, wg, preferred_element_type=F32)
    val = jnp.dot(a, wv_ref[...].astype(BF16), preferred_element_type=F32)
    tm = gate.shape[0]
    if decode:
        g2, g1 = p2_ref[...], p1_ref[...]
        hist_ref[...] = gate
    else:
        halo = jnp.dot(p1_ref[...], wg, preferred_element_type=F32)
        halo = jnp.where(pl.program_id(0) % tiles_per_seq == 0, 0.0, halo)
        h7, h6 = halo[7:8, :], halo[6:7, :]
        row = lax.broadcasted_iota(jnp.int32, gate.shape, 0)
        g1 = jnp.where(row == 0, h7, pltpu.roll(gate, 1, 0))
        g2 = jnp.where(row == 0, h6, jnp.where(row == 1, h7, pltpu.roll(gate, 2, 0)))
        hist_ref[...] = gate[tm - SUBLANES:, :]
    g = dw_ref[0:1, :] * g2 + dw_ref[1:2, :] * g1 + dw_ref[2:3, :] * gate + db_ref[...]
    act_ref[...] = (jax.nn.gelu(g) * val).astype(act_ref.dtype)


def ffn_up(h, w_up, layer, dw, db, *, hist=None, seq_len, tm, tn):
    r, k = h.shape
    nb = D_FF // tn
    decode = hist is not None
    if decode:
        p1, p2 = hist[:, 1, :], hist[:, 0, :]
        p_specs = [pl.BlockSpec((tm, tn), lambda m, j: (m, j)), pl.BlockSpec((tm, tn), lambda m, j: (m, j))]
        hrows = tm
    else:
        p1 = p2 = h
        blk = tm // SUBLANES
        p_specs = [pl.BlockSpec((SUBLANES, k), lambda m, j: (jnp.maximum(m * blk - 1, 0), 0)),
                   pl.BlockSpec((SUBLANES, k), lambda m, j: (0, 0))]
        hrows = SUBLANES
    return pl.pallas_call(
        functools.partial(_ffn_up_kernel, decode=decode, tiles_per_seq=max(seq_len // tm, 1)),
        grid=(r // tm, nb),
        in_specs=[pl.BlockSpec((tm, k), lambda m, j: (m, 0))] + p_specs + [
            pl.BlockSpec((None, k, tn), lambda m, j: (layer, 0, j)),
            pl.BlockSpec((None, k, tn), lambda m, j: (layer, 0, j + nb)),
            pl.BlockSpec((3, tn), lambda m, j: (0, j)),
            pl.BlockSpec((1, tn), lambda m, j: (0, j))],
        out_specs=[pl.BlockSpec((tm, tn), lambda m, j: (m, j)),
                   pl.BlockSpec((None, hrows, tn), lambda m, j: (m, 0, j))],
        out_shape=[jax.ShapeDtypeStruct((r, D_FF), BF16),
                   jax.ShapeDtypeStruct((r // tm, hrows, D_FF), F32)],
        compiler_params=_params("parallel", "arbitrary"),
        name="ffn_up",
    )(h, p1, p2, w_up, w_up, dw, db.reshape(1, D_FF))


N_KV_SLABS = KV_GROUPS * 2


def _cmp_partial_slab(x_ref, c, pe_ref, w1_ref, n_chunks, first=0, pitch=1, chunk_pitch=None):
    chunk_pitch = CMP_STRIDE * pitch if chunk_pitch is None else chunk_pitch
    acc_a = jnp.zeros((n_chunks, HEAD_DIM), F32)
    acc_b = jnp.zeros((n_chunks, HEAD_DIM), F32)
    for s in range(CMP_STRIDE):
        xs = x_ref[pl.ds(first + s * pitch, n_chunks, stride=chunk_pitch), :]
        xa = (xs + pe_ref[c, s:s + 1, :]).astype(BF16)
        xb = (xs + pe_ref[c, CMP_STRIDE + s:CMP_STRIDE + s + 1, :]).astype(BF16)
        acc_a += jnp.dot(xa, w1_ref[c, s], preferred_element_type=F32)
        acc_b += jnp.dot(xb, w1_ref[c, CMP_STRIDE + s], preferred_element_type=F32)
    return acc_a, acc_b


def _cmp1_kernel(x_ref, pe_ref, w1_ref, a_ref, b_ref, *, n_chunks):
    c = pl.program_id(1) % 2
    a_ref[...], b_ref[...] = _cmp_partial_slab(x_ref, c, pe_ref, w1_ref, n_chunks)


def cmp_partial_rows(x, pe, w1, rows):
    nb = x.shape[0] // rows
    n_chunks = rows // CMP_STRIDE
    out = jax.ShapeDtypeStruct((nb, n_chunks, KV_COLS), F32)
    return pl.pallas_call(
        functools.partial(_cmp1_kernel, n_chunks=n_chunks),
        grid=(nb, N_KV_SLABS),
        in_specs=[pl.BlockSpec((rows, HEAD_DIM), lambda b, sl: (b, sl)),
                  pl.BlockSpec(pe.shape, lambda b, sl: (0, 0, 0)),
                  pl.BlockSpec(w1.shape, lambda b, sl: (0, 0, 0, 0))],
        out_specs=[pl.BlockSpec((None, n_chunks, HEAD_DIM), lambda b, sl: (b, 0, sl))] * 2,
        out_shape=[out, out],
        compiler_params=_params("parallel", "arbitrary"),
        name="cmp_partial",
    )(x, pe, w1)


PAGES_PER_STEP = 16
CHUNKS_PER_PAGE = PAGE_SIZE // CMP_STRIDE
CHUNK_ROWS = CMP_STRIDE * N_KV_SLABS
CHUNK_PITCH = CHUNK_ROWS + 4
STEP_CHUNKS = PAGES_PER_STEP * CHUNKS_PER_PAGE


def _cmp1_paged_kernel(pt_ref, *refs):
    page_refs = refs[:PAGES_PER_STEP]
    pe_ref, w1_ref, a_ref, b_ref, buf_ref = refs[PAGES_PER_STEP:]
    for i, p_ref in enumerate(page_refs):
        for ch in range(CHUNKS_PER_PAGE):
            dst = (i * CHUNKS_PER_PAGE + ch) * CHUNK_PITCH
            buf_ref[dst:dst + CHUNK_ROWS, :] = p_ref[ch * CHUNK_ROWS:(ch + 1) * CHUNK_ROWS, :]
    for slab in range(N_KV_SLABS):
        cols = slice(slab * HEAD_DIM, (slab + 1) * HEAD_DIM)
        a_ref[:, cols], b_ref[:, cols] = _cmp_partial_slab(buf_ref, slab % 2, pe_ref, w1_ref, STEP_CHUNKS,
                                                           first=slab, pitch=N_KV_SLABS, chunk_pitch=CHUNK_PITCH)


def cmp_partial_paged(pool, n_phys, page_table, layer, pe, w1):
    bsz, n_pages = page_table.shape
    n_steps = n_pages // PAGES_PER_STEP
    n_chunks = STEP_CHUNKS
    out = jax.ShapeDtypeStruct((bsz, n_steps, n_chunks, KV_COLS), F32)

    def page_spec(i):
        return pl.BlockSpec((None, PAGE_SIZE * N_KV_SLABS, HEAD_DIM),
                            lambda b, s, pt: (layer * n_phys + pt[b, s * PAGES_PER_STEP + i], 0, 0))

    out_spec = pl.BlockSpec((None, None, n_chunks, KV_COLS), lambda b, s, pt: (b, s, 0, 0))
    a, b = pl.pallas_call(
        _cmp1_paged_kernel,
        grid_spec=pltpu.PrefetchScalarGridSpec(
            num_scalar_prefetch=1,
            grid=(bsz, n_steps),
            in_specs=[page_spec(i) for i in range(PAGES_PER_STEP)] + [
                pl.BlockSpec(pe.shape, lambda b, s, pt: (0, 0, 0)),
                pl.BlockSpec(w1.shape, lambda b, s, pt: (0, 0, 0, 0))],
            out_specs=[out_spec, out_spec],
            scratch_shapes=[pltpu.VMEM((STEP_CHUNKS * CHUNK_PITCH, HEAD_DIM), F32)]),
        out_shape=[out, out],
        compiler_params=_params("parallel", "arbitrary"),
        name="cmp_partial_paged",
    )(page_table, *([pool] * PAGES_PER_STEP), pe, w1)
    return a.reshape(bsz, -1, KV_COLS), b.reshape(bsz, -1, KV_COLS)


def _cmp2_kernel(a_ref, b_ref, bx_ref, w2_ref, o_ref):
    n = a_ref.shape[0]
    row = lax.broadcasted_iota(jnp.int32, (n, HEAD_DIM), 0)
    for slab in range(KV_GROUPS * 2):
        c = slab % 2
        cols = slice(slab * HEAD_DIM, (slab + 1) * HEAD_DIM)
        nxt = pltpu.roll(b_ref[:, cols], n - 1, 0)
        nxt = jnp.where(row == n - 1, bx_ref[0:1, cols], nxt)
        h = jax.nn.gelu(a_ref[:, cols] + nxt).astype(BF16)
        o_ref[:, cols] = jnp.dot(h, w2_ref[c], preferred_element_type=F32).astype(o_ref.dtype)


def cmp_finish(a, b, b_next, w2):
    nb, n, _ = a.shape
    return pl.pallas_call(
        _cmp2_kernel,
        grid=(nb,),
        in_specs=[pl.BlockSpec((None, n, KV_COLS), lambda i: (i, 0, 0)),
                  pl.BlockSpec((None, n, KV_COLS), lambda i: (i, 0, 0)),
                  pl.BlockSpec((None, SUBLANES, KV_COLS), lambda i: (i, 0, 0)),
                  pl.BlockSpec(w2.shape, lambda i: (0, 0, 0))],
        out_specs=pl.BlockSpec((None, n, KV_COLS), lambda i: (i, 0, 0)),
        out_shape=jax.ShapeDtypeStruct((nb, n, KV_COLS), BF16),
        compiler_params=_params("parallel"),
        name="cmp_finish",
    )(a, b, b_next, w2)


def _softmax_rows(s, mask):
    s = jnp.where(mask, s, NEG)
    m = jnp.max(s, axis=-1, keepdims=True)
    e = jnp.where(mask, jnp.exp(s - m), 0.0)
    l = jnp.sum(e, axis=-1, keepdims=True)
    return jnp.where(l > 0.0, e / jnp.where(l > 0.0, l, 1.0), 0.0)


def _cmp_attn_kernel(q_ref, kc_ref, bias_ref, msel_ref, exp_ref, o_ref, mask_ref, *, n_sel):
    qt = pl.program_id(1)
    tq, n_cmp = q_ref.shape[0], kc_ref.shape[0]
    qpos = qt * tq + lax.broadcasted_iota(jnp.int32, (tq, n_cmp), 0)
    cpos = lax.broadcasted_iota(jnp.int32, (tq, n_cmp), 1) * CMP_STRIDE + (CMP_BLOCK - 1)
    cadd = jnp.where(cpos <= qpos, 0.0, NEG)
    j = lax.broadcasted_iota(jnp.int32, (n_sel, tq), 0)
    cur = (qt * tq + lax.broadcasted_iota(jnp.int32, (n_sel, tq), 1)) // SEL_BLOCK
    valid = j <= cur
    forced = (j == 0) | (j == cur) | (j == cur - 1)
    pad_rows = jnp.zeros((exp_ref.shape[1] - n_sel, tq), F32)
    diag = pl.ds(pl.multiple_of(qt * tq, tq), tq)
    causal = lax.broadcasted_iota(jnp.int32, (tq, tq), 0) <= lax.broadcasted_iota(jnp.int32, (tq, tq), 1)
    for g in range(KV_GROUPS):
        k = kc_ref[:, g * 2 * HEAD_DIM:(g * 2 + 1) * HEAD_DIM]
        v = kc_ref[:, (g * 2 + 1) * HEAD_DIM:(g * 2 + 2) * HEAD_DIM]
        heads = range(g * GROUP_SIZE, (g + 1) * GROUP_SIZE)
        q4 = jnp.concatenate([q_ref[:, h * HEAD_DIM:(h + 1) * HEAD_DIM] for h in heads], axis=0)
        badd = jnp.concatenate([bias_ref[h] + cadd for h in heads], axis=0)
        s = lax.dot_general(q4, k, NT_DIMS, preferred_element_type=F32) * SCALE + badd
        p = _softmax_rows(s, s > 0.5 * NEG)
        o4 = jnp.dot(p.astype(BF16), v, preferred_element_type=F32)
        imp = jnp.zeros((tq, n_cmp), F32)
        for r, h in enumerate(heads):
            o_ref[:, h * HEAD_DIM:(h + 1) * HEAD_DIM] = o4[r * tq:(r + 1) * tq, :]
            imp = imp + p[r * tq:(r + 1) * tq, :]
        p_sel = lax.dot_general(msel_ref[...], imp, NT_DIMS, precision=HIGHEST, preferred_element_type=F32)
        score = jnp.where(valid, p_sel + jnp.where(forced, FORCE_BONUS, 0.0), -jnp.inf)
        rank = jnp.zeros((n_sel, tq), jnp.int32)
        for i in range(n_sel):
            si = score[i:i + 1, :]
            beats = (si > score) | ((si == score) & (i < j))
            rank = rank + beats.astype(jnp.int32)
        sel_t = jnp.where((rank < N_SELECT) & valid, 1.0, 0.0)
        sel = jnp.concatenate([sel_t, pad_rows], axis=0).astype(BF16)
        allowed = jnp.dot(exp_ref[...], sel, preferred_element_type=F32)
        mask_ref[g] = ((allowed - 1.0) * -NEG).astype(mask_ref.dtype)
        allowed_d = jnp.dot(exp_ref[diag, :], sel, preferred_element_type=F32)
        mask_ref[g, diag, :] = jnp.where(causal, (allowed_d - 1.0) * -NEG, NEG).astype(mask_ref.dtype)


def cmp_attention(q, kc, bias_cmp, msel, expand, bsz, seq):
    tq = 2 * ATT_TILE
    nqt = seq // tq
    n_cmp = kc.shape[1]
    n_sel = msel.shape[0]
    return pl.pallas_call(
        functools.partial(_cmp_attn_kernel, n_sel=n_sel),
        grid=(bsz, nqt),
        in_specs=[pl.BlockSpec((tq, D_MODEL), lambda b, t: (b * nqt + t, 0)),
                  pl.BlockSpec((None, n_cmp, KV_COLS), lambda b, t: (b, 0, 0)),
                  pl.BlockSpec((N_HEADS, tq, n_cmp), lambda b, t: (0, t, 0)),
                  pl.BlockSpec(msel.shape, lambda b, t: (0, 0)),
                  pl.BlockSpec(expand.shape, lambda b, t: (0, 0))],
        out_specs=[pl.BlockSpec((tq, D_MODEL), lambda b, t: (b * nqt + t, 0)),
                   pl.BlockSpec((None, KV_GROUPS, seq, tq), lambda b, t: (b, 0, 0, t))],
        out_shape=[jax.ShapeDtypeStruct((bsz * seq, D_MODEL), F32),
                   jax.ShapeDtypeStruct((bsz, KV_GROUPS, seq, seq), BF16)],
        compiler_params=_params("parallel", "arbitrary"),
        name="cmp_attention",
    )(q, kc, bias_cmp, msel, expand)


SLC_KEYS = 1024
SLC_SUB = SLC_KEYS // ATT_TILE
SLC_QUERIES = 256


def _slc_kernel(qt_ref, ks_ref, q_ref, k_ref, vt_ref, tbt_ref, mask_ref, o_ref, m_ref, l_ref, acc_ref):
    g, pair = pl.program_id(1), pl.program_id(2)
    qt, ks = qt_ref[pair], ks_ref[pair]
    tq = q_ref.shape[0]

    @pl.when(ks == 0)
    def _():
        m_ref[...] = jnp.full_like(m_ref, NEG)
        l_ref[...] = jnp.zeros_like(l_ref)
        acc_ref[...] = jnp.zeros_like(acc_ref)

    q4 = jnp.concatenate([q_ref[:, r * HEAD_DIM:(r + 1) * HEAD_DIM] for r in range(GROUP_SIZE)], axis=0)
    rows = []
    q_sub = tq // ATT_TILE
    for c in range(SLC_SUB):
        tbi = [jnp.clip(qt * q_sub + qs - (ks * SLC_SUB + c), 0, 2) for qs in range(q_sub)]
        madd = [mask_ref[c * ATT_TILE:(c + 1) * ATT_TILE, qs * ATT_TILE:(qs + 1) * ATT_TILE].astype(F32)
                for qs in range(q_sub)]
        rows.append(jnp.concatenate([tbt_ref[tbi[qs], g * GROUP_SIZE + r] + madd[qs]
                                     for r in range(GROUP_SIZE) for qs in range(q_sub)], axis=1))
    st = (lax.dot_general(k_ref[...].astype(BF16), q4, NT_DIMS, preferred_element_type=F32) * SCALE
          + jnp.concatenate(rows, axis=0))
    m_prev = m_ref[...]
    m_new = jnp.maximum(m_prev, jnp.max(st, axis=0, keepdims=True))
    alpha = jnp.exp(m_prev - m_new)
    pt = jnp.exp(st - m_new)
    l_ref[...] = alpha * l_ref[...] + jnp.sum(pt, axis=0, keepdims=True)
    acc_ref[...] = alpha * acc_ref[...] + jnp.dot(vt_ref[...], pt.astype(BF16), preferred_element_type=F32)
    m_ref[...] = m_new

    @pl.when(ks == ((qt + 1) * tq - 1) // SLC_KEYS)
    def _():
        ot = acc_ref[...] / l_ref[...]
        for r in range(GROUP_SIZE):
            o_ref[:, r * HEAD_DIM:(r + 1) * HEAD_DIM] = ot[:, r * tq:(r + 1) * tq].T


def slc_attention(q, kv, vt, mask, tbt, bsz, seq):
    tq = SLC_QUERIES
    nqt = seq // tq
    nks = seq // SLC_KEYS
    gw = GROUP_SIZE * HEAD_DIM
    pairs = [(t, s) for t in range(nqt) for s in range(((t + 1) * tq - 1) // SLC_KEYS + 1)]
    qt_of = jnp.asarray([p[0] for p in pairs], jnp.int32)
    ks_of = jnp.asarray([p[1] for p in pairs], jnp.int32)
    return pl.pallas_call(
        _slc_kernel,
        grid_spec=pltpu.PrefetchScalarGridSpec(
            num_scalar_prefetch=2,
            grid=(bsz, KV_GROUPS, len(pairs)),
            in_specs=[pl.BlockSpec((tq, gw), lambda b, g, p, qt, ks: (b * nqt + qt[p], g)),
                      pl.BlockSpec((SLC_KEYS, HEAD_DIM), lambda b, g, p, qt, ks: (b * nks + ks[p], 2 * g)),
                      pl.BlockSpec((HEAD_DIM, SLC_KEYS), lambda b, g, p, qt, ks: (g, b * nks + ks[p])),
                      pl.BlockSpec(tbt.shape, lambda b, g, p, qt, ks: (0, 0, 0, 0)),
                      pl.BlockSpec((None, None, SLC_KEYS, tq), lambda b, g, p, qt, ks: (b, g, ks[p], qt[p]))],
            out_specs=pl.BlockSpec((tq, gw), lambda b, g, p, qt, ks: (b * nqt + qt[p], g)),
            scratch_shapes=[pltpu.VMEM((1, GROUP_SIZE * tq), F32), pltpu.VMEM((1, GROUP_SIZE * tq), F32),
                            pltpu.VMEM((HEAD_DIM, GROUP_SIZE * tq), F32)]),
        out_shape=jax.ShapeDtypeStruct((bsz * seq, D_MODEL), F32),
        compiler_params=_params("parallel", "parallel", "arbitrary"),
        name="slc_attention",
    )(qt_of, ks_of, q, kv, vt, tbt, mask)


WIN_TILES = WINDOW // ATT_TILE + 1
WIN_SPAN = WIN_TILES * ATT_TILE


def _win_kernel(q_ref, *refs):
    kv_refs, (bias_ref, oc_ref, os_ref, gate_ref, o_ref) = refs[:WIN_TILES], refs[WIN_TILES:]
    gate = gate_ref[...]
    qt = pl.program_id(1)
    tq = q_ref.shape[0]
    row = lax.broadcasted_iota(jnp.int32, (tq, WIN_SPAN), 0)
    col = lax.broadcasted_iota(jnp.int32, (tq, WIN_SPAN), 1)
    back = row + WINDOW - col
    mask = (back >= 0) & (back <= WINDOW) & (qt * tq - WINDOW + col >= 0)
    madd = jnp.where(mask, 0.0, NEG)
    for g in range(KV_GROUPS):
        k = jnp.concatenate([r[:, g * 2 * HEAD_DIM:(g * 2 + 1) * HEAD_DIM].astype(BF16) for r in kv_refs], axis=0)
        v = jnp.concatenate([r[:, (g * 2 + 1) * HEAD_DIM:(g * 2 + 2) * HEAD_DIM].astype(BF16) for r in kv_refs], axis=0)
        heads = range(g * GROUP_SIZE, (g + 1) * GROUP_SIZE)
        q4 = jnp.concatenate([q_ref[:, h * HEAD_DIM:(h + 1) * HEAD_DIM] for h in heads], axis=0)
        badd = jnp.concatenate([bias_ref[h] + madd for h in heads], axis=0)
        s = lax.dot_general(q4, k, NT_DIMS, preferred_element_type=F32) * SCALE + badd
        e = jnp.exp(s - jnp.max(s, axis=-1, keepdims=True))
        l = jnp.sum(e, axis=-1, keepdims=True)
        o = jnp.dot(e.astype(BF16), v, preferred_element_type=F32) / l
        for r, h in enumerate(heads):
            cols = slice(h * HEAD_DIM, (h + 1) * HEAD_DIM)
            merged = (oc_ref[:, cols] * gate[:, 3 * h:3 * h + 1] + os_ref[:, cols] * gate[:, 3 * h + 1:3 * h + 2]
                      + o[r * tq:(r + 1) * tq, :] * gate[:, 3 * h + 2:3 * h + 3])
            o_ref[:, cols] = merged.astype(o_ref.dtype)


def win_attention(q, kv, bias_win, o_cmp, o_slc, gate, bsz, seq):
    tq = ATT_TILE
    nqt = seq // tq
    row_spec = pl.BlockSpec((tq, D_MODEL), lambda b, t: (b * nqt + t, 0))

    def kv_spec(i):
        return pl.BlockSpec((tq, KV_COLS), lambda b, t: (b * nqt + jnp.maximum(t - (WIN_TILES - 1) + i, 0), 0))

    return pl.pallas_call(
        _win_kernel,
        grid=(bsz, nqt),
        in_specs=[row_spec]
        + [kv_spec(i) for i in range(WIN_TILES)]
        + [pl.BlockSpec(bias_win.shape, lambda b, t: (0, 0, 0)), row_spec, row_spec,
           pl.BlockSpec((tq, LANES), lambda b, t: (b * nqt + t, 0))],
        out_specs=row_spec,
        out_shape=jax.ShapeDtypeStruct((bsz * seq, D_MODEL), BF16),
        compiler_params=_params("parallel", "arbitrary"),
        name="win_attention",
    )(q, *([kv] * WIN_TILES), bias_win, o_cmp, o_slc, gate)


def _combine_kernel(oc_ref, os_ref, ow_ref, g_ref, o_ref):
    gate = g_ref[...]
    for h in range(N_HEADS):
        cols = slice(h * HEAD_DIM, (h + 1) * HEAD_DIM)
        o = (oc_ref[:, cols] * gate[:, 3 * h:3 * h + 1] + os_ref[:, cols] * gate[:, 3 * h + 1:3 * h + 2]
             + ow_ref[:, cols] * gate[:, 3 * h + 2:3 * h + 3])
        o_ref[:, cols] = o.astype(o_ref.dtype)


def nsa_combine(o_cmp, o_slc, o_win, gate, tm):
    r = o_cmp.shape[0]
    spec = pl.BlockSpec((tm, D_MODEL), lambda m: (m, 0))
    return pl.pallas_call(
        _combine_kernel,
        grid=(r // tm,),
        in_specs=[spec, spec, spec, pl.BlockSpec((tm, LANES), lambda m: (m, 0))],
        out_specs=spec,
        out_shape=jax.ShapeDtypeStruct((r, D_MODEL), BF16),
        compiler_params=_params("parallel"),
        name="nsa_combine",
    )(o_cmp, o_slc, o_win, gate)


N_SEL_S = -(-(PAST_LEN + 1) // SEL_BLOCK)
N_SEL_S_PAD = 384
N_CMP_S = PAST_LEN // CMP_STRIDE


def _group_rows(parts, hgrp):
    out = parts[0]
    for g in range(1, KV_GROUPS):
        out = jnp.where(hgrp == g, parts[g], out)
    return out


def _sample_attn_kernel(q_ref, kc_ref, bc_ref, gsum_ref, msel_ref, win_ref, new_ref, bw_ref, bn_ref,
                        oc_ref, ow_ref, idx_ref):
    q = q_ref[...]
    hgrp = lax.broadcasted_iota(jnp.int32, (N_HEADS, 1), 0) // GROUP_SIZE

    def kcol(g):
        return slice(g * 2 * HEAD_DIM, (g * 2 + 1) * HEAD_DIM)

    def vcol(g):
        return slice((g * 2 + 1) * HEAD_DIM, (g * 2 + 2) * HEAD_DIM)

    s = _group_rows([lax.dot_general(q, kc_ref[:, kcol(g)], NT_DIMS, preferred_element_type=F32)
                     for g in range(KV_GROUPS)], hgrp)
    s = s * SCALE + bc_ref[...]
    n = lax.broadcasted_iota(jnp.int32, s.shape, 1)
    p = _softmax_rows(s, n * CMP_STRIDE + (CMP_BLOCK - 1) <= PAST_LEN)
    pb = p.astype(BF16)
    oc_ref[...] = _group_rows([jnp.dot(pb, kc_ref[:, vcol(g)], preferred_element_type=F32)
                               for g in range(KV_GROUPS)], hgrp)
    imp = jnp.dot(gsum_ref[...], p, precision=HIGHEST, preferred_element_type=F32)
    p_sel = jnp.dot(imp, msel_ref[...], precision=HIGHEST, preferred_element_type=F32)
    j = lax.broadcasted_iota(jnp.int32, p_sel.shape, 1)
    cur = PAST_LEN // SEL_BLOCK
    forced = (j == 0) | (j == cur) | (j == cur - 1)
    score = jnp.where(j <= cur, p_sel + jnp.where(forced, FORCE_BONUS, 0.0), -jnp.inf)
    lane = lax.broadcasted_iota(jnp.int32, idx_ref.shape, 1)
    idx = jnp.zeros(idx_ref.shape, F32)
    jf = j.astype(F32)
    for kk in range(N_SELECT):
        mx = jnp.max(score, axis=-1, keepdims=True)
        pick = jnp.min(jnp.where(score == mx, jf, float(N_SEL_S_PAD)), axis=-1, keepdims=True)
        idx = jnp.where(lane == kk, pick, idx)
        score = jnp.where(jf == pick, -jnp.inf, score)
    idx_ref[...] = idx.astype(jnp.int32)
    wb = win_ref.shape[0] // N_KV_SLABS

    def win_slab(slab):
        return win_ref[pl.ds(slab, wb, stride=N_KV_SLABS), :].astype(BF16)

    sw = _group_rows([lax.dot_general(q, win_slab(2 * g), NT_DIMS, preferred_element_type=F32)
                      for g in range(KV_GROUPS)], hgrp)
    sw = sw * SCALE + bw_ref[...]
    qf = q.astype(F32)
    sn = _group_rows([jnp.sum(qf * new_ref[:, kcol(g)].astype(BF16).astype(F32), axis=-1, keepdims=True)
                      for g in range(KV_GROUPS)], hgrp)
    sn = sn * SCALE + bn_ref[:, 0:1]
    m = jnp.maximum(jnp.max(sw, axis=-1, keepdims=True), sn)
    ew, en = jnp.exp(sw - m), jnp.exp(sn - m)
    l = jnp.sum(ew, axis=-1, keepdims=True) + en
    pw, pn = (ew / l).astype(BF16), (en / l).astype(BF16).astype(F32)
    ow = _group_rows([jnp.dot(pw, win_slab(2 * g + 1), preferred_element_type=F32)
                      + pn * new_ref[:, vcol(g)].astype(BF16).astype(F32) for g in range(KV_GROUPS)], hgrp)
    ow_ref[...] = ow


def sample_attention(q3, kc, bias_c, gsum, msel, win_pool, layer, kv_win_new, bias_w, bias_new):
    bsz = q3.shape[0]
    wb = win_pool.shape[1]
    o = jax.ShapeDtypeStruct((bsz, N_HEADS, HEAD_DIM), F32)
    full2 = lambda b: (0, 0)
    return pl.pallas_call(
        _sample_attn_kernel,
        grid=(bsz,),
        in_specs=[pl.BlockSpec((None, N_HEADS, HEAD_DIM), lambda b: (b, 0, 0)),
                  pl.BlockSpec((None, N_CMP_S, KV_COLS), lambda b: (b, 0, 0)),
                  pl.BlockSpec(bias_c.shape, full2),
                  pl.BlockSpec(gsum.shape, full2),
                  pl.BlockSpec(msel.shape, full2),
                  pl.BlockSpec((None, wb, HEAD_DIM), lambda b: (layer * bsz + b, 0, 0)),
                  pl.BlockSpec((None, 1, KV_COLS), lambda b: (b, 0, 0)),
                  pl.BlockSpec(bias_w.shape, full2),
                  pl.BlockSpec(bias_new.shape, full2)],
        out_specs=[pl.BlockSpec((None, N_HEADS, HEAD_DIM), lambda b: (b, 0, 0)),
                   pl.BlockSpec((None, N_HEADS, HEAD_DIM), lambda b: (b, 0, 0)),
                   pl.BlockSpec((None, SUBLANES, LANES), lambda b: (b, 0, 0))],
        out_shape=[o, o, jax.ShapeDtypeStruct((bsz, SUBLANES, LANES), jnp.int32)],
        compiler_params=_params("parallel"),
        name="sample_attention",
    )(q3, kc, bias_c, gsum, msel, win_pool, kv_win_new, bias_w, bias_new)


def _sample_slc_kernel(idx_ref, pt_ref, q_ref, *refs):
    blk_refs, (new_ref, bias_ref, o_ref) = refs[:N_SELECT], refs[N_SELECT:]
    b, g = pl.program_id(0), pl.program_id(1)
    ks, vs, biases = [], [], []
    lane = lax.broadcasted_iota(jnp.int32, (N_HEADS, SEL_BLOCK), 1)
    for kk, blk_ref in enumerate(blk_refs):
        j = idx_ref[b, g, kk]
        is_new = j >= PAST_LEN // SEL_BLOCK
        k_blk = blk_ref[pl.ds(2 * g, SEL_BLOCK, stride=N_KV_SLABS), :]
        v_blk = blk_ref[pl.ds(2 * g + 1, SEL_BLOCK, stride=N_KV_SLABS), :]
        ks.append(jnp.where(is_new, jnp.broadcast_to(new_ref[:, :HEAD_DIM], k_blk.shape), k_blk).astype(BF16))
        vs.append(jnp.where(is_new, jnp.broadcast_to(new_ref[:, HEAD_DIM:], v_blk.shape), v_blk).astype(BF16))
        biases.append(bias_ref[j] + jnp.where(j * SEL_BLOCK + lane <= PAST_LEN, 0.0, NEG))
    k, v = jnp.concatenate(ks, axis=0), jnp.concatenate(vs, axis=0)
    s = lax.dot_general(q_ref[...], k, NT_DIMS, preferred_element_type=F32) * SCALE + jnp.concatenate(biases, axis=1)
    p = _softmax_rows(s, s > 0.5 * NEG)
    o_ref[...] = jnp.dot(p.astype(BF16), v, preferred_element_type=F32)


def sample_slc_attention(idx, page_table, q3, pool, n_phys, layer, kv_slc_new, bias_blk):
    bsz = q3.shape[0]
    half_per_page = PAGE_SIZE // SEL_BLOCK
    n_half = n_phys * half_per_page
    last_past = PAST_LEN // SEL_BLOCK - 1

    def blk_spec(kk):
        def blk_map(b, g, idx_r, pt_r):
            j = jnp.minimum(idx_r[b, g, kk], last_past)
            return (layer * n_half + pt_r[b, j // half_per_page] * half_per_page + j % half_per_page, 0, 0)
        return pl.BlockSpec((None, SEL_BLOCK * N_KV_SLABS, HEAD_DIM), blk_map)

    return pl.pallas_call(
        _sample_slc_kernel,
        grid_spec=pltpu.PrefetchScalarGridSpec(
            num_scalar_prefetch=2,
            grid=(bsz, KV_GROUPS),
            in_specs=[pl.BlockSpec((None, N_HEADS, HEAD_DIM), lambda b, g, i, p: (b, 0, 0))]
            + [blk_spec(kk) for kk in range(N_SELECT)]
            + [pl.BlockSpec((None, 1, 2 * HEAD_DIM), lambda b, g, i, p: (b, 0, g)),
               pl.BlockSpec(bias_blk.shape, lambda b, g, i, p: (0, 0, 0))],
            out_specs=pl.BlockSpec((None, None, N_HEADS, HEAD_DIM), lambda b, g, i, p: (b, g, 0, 0))),
        out_shape=jax.ShapeDtypeStruct((bsz, KV_GROUPS, N_HEADS, HEAD_DIM), F32),
        compiler_params=_params("parallel", "arbitrary"),
        name="sample_slc_attention",
    )(idx, page_table, q3, *([pool] * N_SELECT), kv_slc_new, bias_blk)


CONV_TILE = 128
CONV_HALO = 32
CONV_ROWS = 64


def _dwconv_ln_kernel(u_ref, halo_ref, w_ref, b_ref, g_ref, beta_ref, o_ref, buf_ref, y_ref, sh_ref, *, tiles_per_seq):
    first = pl.program_id(0) % tiles_per_seq == 0
    buf_ref[0:CONV_HALO, :] = jnp.where(first, 0.0, halo_ref[...])
    buf_ref[CONV_HALO:, :] = u_ref[...]
    lead = CONV_HALO - (CONV_WIDTH - 1)
    def lane_slab(ci, carry):
        cols = pl.ds(pl.multiple_of(ci * LANES, LANES), LANES)
        for r0 in range(0, CONV_TILE, CONV_ROWS):
            acc = jnp.broadcast_to(b_ref[:, cols], (CONV_ROWS, LANES))
            for res in range(SUBLANES):
                taps = [k for k in range(CONV_WIDTH) if (lead + k) % SUBLANES == res]
                span = max(lead + k - res for k in taps) + CONV_ROWS
                sh_ref[res, 0:span, :] = buf_ref[r0 + res:r0 + res + span, cols]
                for k in taps:
                    off = lead + k - res
                    acc = acc + w_ref[k:k + 1, cols] * sh_ref[res, off:off + CONV_ROWS, :]
            y_ref[r0:r0 + CONV_ROWS, cols] = acc
        return carry

    lax.fori_loop(0, D_MODEL // LANES, lane_slab, 0)
    y = y_ref[...]
    mu = jnp.mean(y, axis=-1, keepdims=True)
    var = jnp.mean(jnp.square(y - mu), axis=-1, keepdims=True)
    y = (y - mu) * lax.rsqrt(var + LN_EPS) * g_ref[...] + beta_ref[...]
    o_ref[...] = (y * jax.nn.sigmoid(y)).astype(o_ref.dtype)


def dwconv_ln(u, w, b, ln_g, ln_b, seq):
    r, d = u.shape
    tiles_per_seq = seq // CONV_TILE
    ratio = CONV_TILE // CONV_HALO
    vec = lambda m: (0, 0)
    return pl.pallas_call(
        functools.partial(_dwconv_ln_kernel, tiles_per_seq=tiles_per_seq),
        grid=(r // CONV_TILE,),
        in_specs=[pl.BlockSpec((CONV_TILE, d), lambda m: (m, 0)),
                  pl.BlockSpec((CONV_HALO, d), lambda m: (jnp.maximum(m * ratio - 1, 0), 0)),
                  pl.BlockSpec((CONV_WIDTH, d), vec), pl.BlockSpec((1, d), vec),
                  pl.BlockSpec((1, d), vec), pl.BlockSpec((1, d), vec)],
        out_specs=pl.BlockSpec((CONV_TILE, d), lambda m: (m, 0)),
        out_shape=jax.ShapeDtypeStruct((r, d), BF16),
        scratch_shapes=[pltpu.VMEM((CONV_HALO + CONV_TILE, d), F32), pltpu.VMEM((CONV_TILE, d), F32),
                        pltpu.VMEM((SUBLANES, CONV_HALO + CONV_ROWS, LANES), F32)],
        compiler_params=_params("parallel"),
        name="dwconv_ln",
    )(u, u, w, b.reshape(1, d), ln_g.reshape(1, d), ln_b.reshape(1, d))


def _dwconv_ln_decode_kernel(u_ref, hist_ref, w_ref, b_ref, g_ref, beta_ref, o_ref):
    y = b_ref[...] + w_ref[CONV_WIDTH - 1:CONV_WIDTH, :] * u_ref[...]
    for k in range(CONV_WIDTH - 1):
        y = y + w_ref[k:k + 1, :] * hist_ref[k]
    mu = jnp.mean(y, axis=-1, keepdims=True)
    var = jnp.mean(jnp.square(y - mu), axis=-1, keepdims=True)
    y = (y - mu) * lax.rsqrt(var + LN_EPS) * g_ref[...] + beta_ref[...]
    o_ref[...] = (y * jax.nn.sigmoid(y)).astype(o_ref.dtype)


def dwconv_ln_decode(u, hist_t, w, b, ln_g, ln_b):
    r, d = u.shape
    vec = lambda i: (0, 0)
    return pl.pallas_call(
        _dwconv_ln_decode_kernel,
        grid=(1,),
        in_specs=[pl.BlockSpec((r, d), vec), pl.BlockSpec(hist_t.shape, lambda i: (0, 0, 0)),
                  pl.BlockSpec((CONV_WIDTH, d), vec), pl.BlockSpec((1, d), vec),
                  pl.BlockSpec((1, d), vec), pl.BlockSpec((1, d), vec)],
        out_specs=pl.BlockSpec((r, d), vec),
        out_shape=jax.ShapeDtypeStruct((r, d), BF16),
        compiler_params=_params("arbitrary"),
        name="dwconv_ln_decode",
    )(u, hist_t, w, b.reshape(1, d), ln_g.reshape(1, d), ln_b.reshape(1, d))


S5_CHUNK = 256
S5_PITCH = S5_CHUNK + 4
IN_SLABS = D_MODEL // LANES
STATE_PER_IN = N_SLAB // IN_SLABS


def _s5_project_in(hb, wb_ref, store):
    half = STATE_PER_IN * LANES
    for i in range(IN_SLABS):
        res = jnp.dot(hb[:, i * LANES:(i + 1) * LANES], wb_ref[i], preferred_element_type=F32)
        for jj in range(STATE_PER_IN):
            store(i * STATE_PER_IN + jj, res[:, jj * LANES:(jj + 1) * LANES],
                  res[:, half + jj * LANES:half + (jj + 1) * LANES])


def _s5_project_out(load, cre_ref, cim_ref, hn, d_ref, y_ref):
    for i in range(IN_SLABS):
        cols = slice(i * LANES, (i + 1) * LANES)
        acc = d_ref[:, cols] * hn[:, cols]
        for jj in range(STATE_PER_IN):
            j = i * STATE_PER_IN + jj
            re, im = load(j)
            acc = acc + jnp.dot(re.astype(BF16), cre_ref[j], preferred_element_type=F32)
            acc = acc - jnp.dot(im.astype(BF16), cim_ref[j], preferred_element_type=F32)
        y_ref[:, cols] = acc.astype(y_ref.dtype)


def _s5_scan_kernel(x_ref, g_ref, wb_ref, ar_ref, ai_ref, cre_ref, cim_ref, d_ref,
                    y_ref, sr_ref, si_ref, bur_ref, bui_ref, hr_ref, hi_ref):
    tc, pitch = S5_CHUNK, S5_PITCH

    @pl.when(pl.program_id(1) == 0)
    def _():
        hr_ref[...] = jnp.zeros_like(hr_ref)
        hi_ref[...] = jnp.zeros_like(hi_ref)

    hn = _rms(x_ref[...], g_ref[...])

    def store(j, re, im):
        bur_ref[j * pitch:j * pitch + tc, :] = re
        bui_ref[j * pitch:j * pitch + tc, :] = im

    _s5_project_in(hn.astype(BF16), wb_ref, store)

    ar = [ar_ref[j8] for j8 in range(SLAB_GRP)]
    ai = [ai_ref[j8] for j8 in range(SLAB_GRP)]

    def step(t, carry):
        out = []
        for j8 in range(SLAB_GRP):
            hr, hi = carry[2 * j8], carry[2 * j8 + 1]
            rows = pl.ds(j8 * SUBLANES * pitch + t, SUBLANES, stride=pitch)
            nr = ar[j8] * hr - ai[j8] * hi + bur_ref[rows, :]
            ni = ar[j8] * hi + ai[j8] * hr + bui_ref[rows, :]
            bur_ref[rows, :] = nr
            bui_ref[rows, :] = ni
            out += [nr, ni]
        return tuple(out)

    init = []
    for j8 in range(SLAB_GRP):
        init += [hr_ref[j8], hi_ref[j8]]
    fin = lax.fori_loop(0, tc, step, tuple(init))
    for j8 in range(SLAB_GRP):
        hr_ref[j8] = fin[2 * j8]
        hi_ref[j8] = fin[2 * j8 + 1]
    sr_ref[...] = hr_ref[...]
    si_ref[...] = hi_ref[...]

    def load(j):
        return bur_ref[j * pitch:j * pitch + tc, :], bui_ref[j * pitch:j * pitch + tc, :]

    _s5_project_out(load, cre_ref, cim_ref, hn, d_ref, y_ref)


def s5_scan(x, g, wb, ar, ai, cre, cim, d_skip, bsz, seq):
    n_chunks = seq // S5_CHUNK
    st = jax.ShapeDtypeStruct((bsz, SLAB_GRP, SUBLANES, LANES), F32)
    st_spec = pl.BlockSpec((None, SLAB_GRP, SUBLANES, LANES), lambda b, c: (b, 0, 0, 0))
    vec = lambda b, c: (0, 0)
    c3 = lambda b, c: (0, 0, 0)
    return pl.pallas_call(
        _s5_scan_kernel,
        grid=(bsz, n_chunks),
        in_specs=[pl.BlockSpec((S5_CHUNK, D_MODEL), lambda b, c: (b * n_chunks + c, 0)),
                  pl.BlockSpec((1, D_MODEL), vec),
                  pl.BlockSpec(wb.shape, c3), pl.BlockSpec(ar.shape, c3), pl.BlockSpec(ai.shape, c3),
                  pl.BlockSpec(cre.shape, c3), pl.BlockSpec(cim.shape, c3),
                  pl.BlockSpec((1, D_MODEL), vec)],
        out_specs=[pl.BlockSpec((S5_CHUNK, D_MODEL), lambda b, c: (b * n_chunks + c, 0)), st_spec, st_spec],
        out_shape=[jax.ShapeDtypeStruct((bsz * seq, D_MODEL), BF16), st, st],
        scratch_shapes=[pltpu.VMEM((N_SLAB * S5_PITCH, LANES), F32), pltpu.VMEM((N_SLAB * S5_PITCH, LANES), F32),
                        pltpu.VMEM((SLAB_GRP, SUBLANES, LANES), F32), pltpu.VMEM((SLAB_GRP, SUBLANES, LANES), F32)],
        compiler_params=_params("parallel", "arbitrary"),
        name="s5_scan",
    )(x, g.reshape(1, -1), wb, ar, ai, cre, cim, d_skip.reshape(1, -1))


def _s5_decode_kernel(x_ref, g_ref, wb_ref, ar_ref, ai_ref, cre_ref, cim_ref, d_ref, h0r_ref, h0i_ref,
                      y_ref, sr_ref, si_ref):
    hn = _rms(x_ref[...], g_ref[...])

    def store(j, re, im):
        cols = slice(j * LANES, (j + 1) * LANES)
        ar, ai = ar_ref[:, cols], ai_ref[:, cols]
        hr, hi = h0r_ref[:, cols], h0i_ref[:, cols]
        sr_ref[:, cols] = ar * hr - ai * hi + re
        si_ref[:, cols] = ar * hi + ai * hr + im

    _s5_project_in(hn.astype(BF16), wb_ref, store)

    def load(j):
        cols = slice(j * LANES, (j + 1) * LANES)
        return sr_ref[:, cols], si_ref[:, cols]

    _s5_project_out(load, cre_ref, cim_ref, hn, d_ref, y_ref)


def s5_decode(x, g, wb, ar_row, ai_row, cre, cim, d_skip, h0r, h0i):
    r = x.shape[0]
    st = jax.ShapeDtypeStruct((r, SSM_DIM), F32)
    vec = lambda i: (0, 0)
    c3 = lambda i: (0, 0, 0)
    return pl.pallas_call(
        _s5_decode_kernel,
        grid=(1,),
        in_specs=[pl.BlockSpec((r, D_MODEL), vec), pl.BlockSpec((1, D_MODEL), vec),
                  pl.BlockSpec(wb.shape, c3), pl.BlockSpec((1, SSM_DIM), vec), pl.BlockSpec((1, SSM_DIM), vec),
                  pl.BlockSpec(cre.shape, c3), pl.BlockSpec(cim.shape, c3), pl.BlockSpec((1, D_MODEL), vec),
                  pl.BlockSpec((r, SSM_DIM), vec), pl.BlockSpec((r, SSM_DIM), vec)],
        out_specs=[pl.BlockSpec((r, D_MODEL), vec), pl.BlockSpec((r, SSM_DIM), vec), pl.BlockSpec((r, SSM_DIM), vec)],
        out_shape=[jax.ShapeDtypeStruct((r, D_MODEL), BF16), st, st],
        compiler_params=_params("arbitrary"),
        name="s5_decode",
    )(x, g.reshape(1, -1), wb, ar_row, ai_row, cre, cim, d_skip.reshape(1, -1), h0r, h0i)


def _t5_bucket(rel):
    n = jnp.maximum(rel, 0)
    nf = jnp.maximum(n, MAX_EXACT).astype(F32)
    big = MAX_EXACT + (jnp.log(nf / MAX_EXACT) / math.log(REL_MAX_DIST / MAX_EXACT)
                       * (N_BUCKETS - MAX_EXACT)).astype(jnp.int32)
    return jnp.where(n < MAX_EXACT, n, jnp.minimum(big, N_BUCKETS - 1))


def _bias_of(rel_bias, rel):
    onehot = (_t5_bucket(rel)[..., None] == jnp.arange(N_BUCKETS, dtype=jnp.int32)).astype(F32)
    return jnp.einsum('...k,kh->h...', onehot, rel_bias.astype(F32), precision=HIGHEST)


def _selection_matrix(n_cmp, n_sel_pad):
    coef = np.convolve(np.ones(SEL_RATIO), np.ones(CMP_BLOCK // CMP_STRIDE)).astype(np.float32)
    m = np.zeros((n_cmp, n_sel_pad), np.float32)
    for j in range(n_sel_pad):
        for o in range(coef.shape[0]):
            n = SEL_RATIO * j + o - (CMP_BLOCK // CMP_STRIDE - 1)
            if 0 <= n < n_cmp:
                m[n, j] = coef[o]
    return m


def _s5_params(a_re, a_im, log_dt, b_re, b_im, c_re, c_im):
    dt = jnp.exp(log_dt.astype(F32))[:, None]
    ar, ai = a_re.astype(F32), a_im.astype(F32)
    mag = jnp.exp(ar * dt)
    abar_re, abar_im = mag * jnp.cos(ai * dt), mag * jnp.sin(ai * dt)
    den = ar * ar + ai * ai
    coef_re = ((abar_re - 1.0) * ar + abar_im * ai) / den
    coef_im = (abar_im * ar - (abar_re - 1.0) * ai) / den
    br, bim = b_re.astype(F32), b_im.astype(F32)
    bb_re = coef_re[..., None] * br - coef_im[..., None] * bim
    bb_im = coef_re[..., None] * bim + coef_im[..., None] * br
    gpi = LANES // SSM_GROUP_CH
    eye = jnp.eye(gpi, dtype=F32)

    def in_blocks(bb):
        t = bb.reshape(IN_SLABS, gpi, SSM_STATE, SSM_GROUP_CH)
        blk = jnp.einsum('sgpc,gh->sgchp', t, eye)
        return blk.reshape(IN_SLABS, LANES, gpi * SSM_STATE)

    wb = jnp.concatenate([in_blocks(bb_re), in_blocks(bb_im)], axis=-1).astype(BF16)
    gps = LANES // SSM_STATE
    ch_per_in = LANES

    def out_blocks(c):
        t = c.astype(F32).reshape(IN_SLABS, STATE_PER_IN, gps, SSM_GROUP_CH, SSM_STATE)
        sel = jnp.eye(STATE_PER_IN * gps, dtype=F32).reshape(STATE_PER_IN, gps, STATE_PER_IN * gps)
        blk = jnp.einsum('ijgcp,jgh->ijgphc', t, sel)
        return blk.reshape(N_SLAB, LANES, ch_per_in).astype(BF16)

    return abar_re, abar_im, wb, out_blocks(c_re), out_blocks(c_im)


PROMPT_TM = 512
TN = 512
FFN_UP_TM = 1024
FFN_DOWN_TK = D_FF // 4
GLU_TM = 256
GLU_TK = D_MODEL


def _row_tile(r):
    return PROMPT_TM if r % PROMPT_TM == 0 else r


def _nsa_project(h, wq, wkv, wg):
    tm = _row_tile(h.shape[0])
    q = matmul(h, wq, n_split=1, out_dtype=BF16, act=None, tm=tm, tn=TN, name="nsa_q")[0]
    kv = matmul(h, wkv, n_split=3, out_dtype=F32, act=None, tm=tm, tn=TN, name="nsa_kv")
    gate = matmul(h, wg, n_split=1, out_dtype=F32, act="sigmoid", tm=tm, tn=LANES, name="nsa_gate")[0]
    return q, kv, gate


def _nsa_layer(hp, hs, xp, xs, g_post, g_next, wts, tabs, caches, ret, layer, page_table, bsz, seq):
    wq, wkv, wg, wo, pe, w1, w2 = wts
    cmp_pool, slc_pool, win_pool, n_phys = caches
    dec = hs.shape[0]
    q = matmul(hp, wq, n_split=1, out_dtype=BF16, act=None, tm=PROMPT_TM, tn=2 * TN, name="nsa_q")[0]
    kv, ret_cmp, ret_slc, vt = kv_project(hp, wkv, ret[0], ret[1], layer, tm=PROMPT_TM)
    gate = matmul(hp, wg, n_split=1, out_dtype=F32, act="sigmoid", tm=PROMPT_TM, tn=LANES, name="nsa_gate")[0]
    part_a, part_b = cmp_partial_rows(kv[0], pe, w1, seq)
    kc = cmp_finish(part_a, part_b, jnp.zeros((bsz, SUBLANES, KV_COLS), F32), w2)
    o_cmp, mask = cmp_attention(q, kc, tabs["cmp"], tabs["msel"], tabs["expand"], bsz, seq)
    o_slc = slc_attention(q, kv[1], vt, mask, tabs["tile_t"], bsz, seq)
    o = win_attention(q, kv[2], tabs["win"], o_cmp, o_slc, gate, bsz, seq)
    xp, hp = proj_res(o, wo, xp, g_post, g_next, glu=False, tm=PROMPT_TM, tk=D_MODEL, name="nsa_out")
    qs, kvs, gate_s = _nsa_project(hs, wq, wkv, wg)
    past_a, past_b = cmp_partial_paged(cmp_pool, n_phys, page_table, layer, pe, w1)
    tail = jnp.pad(kvs[0][:, None, :], ((0, 0), (0, CMP_STRIDE - 1), (0, 0))).reshape(dec * CMP_STRIDE, KV_COLS)
    _, tail_b = cmp_partial_rows(tail, pe, w1, dec * CMP_STRIDE)
    b_next = jnp.pad(tail_b[0][:, None, :], ((0, 0), (0, SUBLANES - 1), (0, 0)))
    kc_s = cmp_finish(past_a, past_b, b_next, w2)
    oc_s, ow_s, idx = sample_attention(qs.reshape(dec, N_HEADS, HEAD_DIM), kc_s, tabs["cmp_s"], tabs["gsum"],
                                       tabs["msel_s"], win_pool, layer, kvs[2][:, None, :],
                                       tabs["win_s"], tabs["new_s"])
    q3 = qs.reshape(dec, N_HEADS, HEAD_DIM)
    os_all = sample_slc_attention(idx[:, :KV_GROUPS, :N_SELECT], page_table, q3, slc_pool, n_phys, layer,
                                  kvs[1][:, None, :], tabs["slc_s"])
    os_s = jnp.stack([os_all[:, h // GROUP_SIZE, h] for h in range(N_HEADS)], axis=1)
    o_s = nsa_combine(oc_s.reshape(dec, D_MODEL), os_s.reshape(dec, D_MODEL), ow_s.reshape(dec, D_MODEL), gate_s, dec)
    xs, hs = proj_res(o_s, wo, xs, g_post, g_next, glu=False, tm=dec, tk=D_MODEL, name="nsa_out_s")
    return xp, hp, xs, hs, kv, kvs, (ret_cmp, ret_slc)


def _conv_layer(hp, hs, xp, xs, g_post, g_next, wts, state, bsz, seq):
    w_pw1, dw, dw_b, ln_g, ln_b, w_pw2 = wts
    dec = hs.shape[0]
    u = glu_matmul(hp, w_pw1, tm=PROMPT_TM, tn=TN, name="conv_pw1")
    hc = dwconv_ln(u, dw, dw_b, ln_g, ln_b, seq)
    xp, hp = proj_res(hc, w_pw2, xp, g_post, g_next, glu=False, tm=PROMPT_TM, tk=D_MODEL, name="conv_pw2")
    hist_p = u.reshape(bsz, seq, D_MODEL)[:, seq - (CONV_WIDTH - 1):]
    us = glu_matmul(hs, w_pw1, tm=dec, tn=TN, name="conv_pw1_s")
    hc_s = dwconv_ln_decode(us, jnp.swapaxes(state, 0, 1), dw, dw_b, ln_g, ln_b)
    xs, hs = proj_res(hc_s, w_pw2, xs, g_post, g_next, glu=False, tm=dec, tk=D_MODEL, name="conv_pw2_s")
    hist_s = jnp.concatenate([state[:, 1:], us[:, None, :]], axis=1)
    return xp, hp, xs, hs, hist_p, hist_s


def _s5_layer(xp, xs, g_pre, g_post, g_next, wts, state_re, state_im, bsz, seq):
    a_re, a_im, log_dt, b_re, b_im, c_re, c_im, d_skip, w_glu = wts
    dec = xs.shape[0]
    abar_re, abar_im, wb, cre, cim = _s5_params(a_re, a_im, log_dt, b_re, b_im, c_re, c_im)
    slab_shape = (SLAB_GRP, SUBLANES, LANES)
    y, sr, si = s5_scan(xp, g_pre, wb, abar_re.reshape(slab_shape), abar_im.reshape(slab_shape), cre, cim,
                        d_skip, bsz, seq)
    xp, hp = proj_res(y, w_glu, xp, g_post, g_next, glu=True, tm=GLU_TM, tk=GLU_TK, name="s5_glu")
    ys, sr_s, si_s = s5_decode(xs, g_pre, wb, abar_re.reshape(1, SSM_DIM), abar_im.reshape(1, SSM_DIM), cre, cim,
                               d_skip, state_re.reshape(dec, SSM_DIM), state_im.reshape(dec, SSM_DIM))
    xs, hs = proj_res(ys, w_glu, xs, g_post, g_next, glu=True, tm=dec, tk=GLU_TK, name="s5_glu_s")
    gp = (SSM_GROUPS, SSM_STATE)
    return (xp, hp, xs, hs, sr.reshape((bsz,) + gp), si.reshape((bsz,) + gp),
            sr_s.reshape((dec,) + gp), si_s.reshape((dec,) + gp))


def _ffn_layer(hp, hs, xp, xs, g_post, g_next, wts, state, bsz, seq):
    w_up, layer, dw, dw_b, w_down = wts
    dec = hs.shape[0]
    act, hist = ffn_up(hp, w_up, layer, dw, dw_b, seq_len=seq, tm=FFN_UP_TM, tn=TN)
    xp, hp = proj_res(act, w_down, xp, g_post, g_next, glu=False, tm=PROMPT_TM, tk=FFN_DOWN_TK, name="ffn_down")
    tiles = seq // FFN_UP_TM
    hist_p = hist.reshape(bsz, tiles, SUBLANES, D_FF)[:, tiles - 1, SUBLANES - 2:, :]
    act_s, gate_s = ffn_up(hs, w_up, layer, dw, dw_b, hist=state, seq_len=1, tm=dec, tn=TN)
    xs, hs = proj_res(act_s, w_down, xs, g_post, g_next, glu=False, tm=dec, tk=FFN_DOWN_TK, name="ffn_down_s")
    hist_s = jnp.concatenate([state[:, 1:], gate_s.reshape(dec, 1, D_FF)], axis=1)
    return xp, hp, xs, hs, hist_p, hist_s


def _bias_tables(rel_bias, seq):
    tq = ATT_TILE
    i = jnp.arange(tq, dtype=jnp.int32)
    tile = jnp.stack([_bias_of(rel_bias, d * tq + i[:, None] - i[None, :]) for d in range(3)])
    n_cmp = seq // CMP_STRIDE
    cpos = jnp.arange(n_cmp, dtype=jnp.int32) * CMP_STRIDE + (CMP_BLOCK - 1)
    qpos = jnp.arange(seq, dtype=jnp.int32)
    n_sel = -(-seq // SEL_BLOCK)
    key = np.arange(seq)
    expand = (key[:, None] // SEL_BLOCK == np.arange(LANES)[None, :]).astype(np.float32)
    cpos_s = jnp.arange(N_CMP_S, dtype=jnp.int32) * CMP_STRIDE + (CMP_BLOCK - 1)
    wb = min(WINDOW, PAST_LEN)
    kpos_s = jnp.arange(N_SEL_S * SEL_BLOCK, dtype=jnp.int32)
    slc_s = _bias_of(rel_bias, PAST_LEN - kpos_s).reshape(N_HEADS, N_SEL_S, SEL_BLOCK)
    gsum = (np.arange(N_HEADS)[None, :] // GROUP_SIZE == np.arange(SUBLANES)[:, None]).astype(np.float32)
    return {
        "tile_t": jnp.swapaxes(tile, -1, -2),
        "win": _bias_of(rel_bias, i[:, None] + WINDOW - jnp.arange(WIN_SPAN, dtype=jnp.int32)[None, :]),
        "cmp": _bias_of(rel_bias, qpos[:, None] - cpos[None, :]),
        "msel": jnp.asarray(_selection_matrix(n_cmp, n_sel).T),
        "expand": jnp.asarray(expand, BF16),
        "cmp_s": _bias_of(rel_bias, PAST_LEN - cpos_s),
        "msel_s": jnp.asarray(_selection_matrix(N_CMP_S, N_SEL_S_PAD)),
        "gsum": jnp.asarray(gsum),
        "win_s": _bias_of(rel_bias, wb - jnp.arange(wb, dtype=jnp.int32)),
        "new_s": jnp.broadcast_to(_bias_of(rel_bias, jnp.zeros((1,), jnp.int32)), (N_HEADS, LANES)),
        "slc_s": jnp.swapaxes(slc_s, 0, 1),
    }


def kernel(x_prompt, x_sample, cache_cmp_kv, cache_slc_kv, cache_win_kv, state_conv, state_ssm_re, state_ssm_im, state_ffn_conv, page_table, norm_gain, rel_bias, nsa_w_q, nsa_w_kv, nsa_cmp_pe, nsa_cmp_w1, nsa_cmp_w2, nsa_w_gate, nsa_w_o, conv_w_pw1, conv_dw, conv_dw_b, conv_ln_g, conv_ln_b, conv_w_pw2, ssm_a_re, ssm_a_im, ssm_log_dt, ssm_b_re, ssm_b_im, ssm_c_re, ssm_c_im, ssm_d, ssm_w_glu, ffn_w_up, ffn_dw, ffn_dw_b, ffn_w_down):
    bsz, seq, d = x_prompt.shape
    dec, dec_seq, _ = x_sample.shape
    assert dec_seq == 1 and d == D_MODEL and seq % PROMPT_TM == 0 and seq // CMP_STRIDE == LANES
    n_nsa = cache_cmp_kv.shape[0]
    n_phys = cache_cmp_kv.shape[1]
    xp = x_prompt.reshape(bsz * seq, d)
    xs = x_sample.reshape(dec, d)
    tabs = _bias_tables(rel_bias, seq)
    cmp_pool = cache_cmp_kv.reshape(n_nsa * n_phys, PAGE_SIZE * N_KV_SLABS, HEAD_DIM)
    slc_pool = cache_slc_kv.reshape(n_nsa * n_phys * (PAGE_SIZE // SEL_BLOCK), SEL_BLOCK * N_KV_SLABS, HEAD_DIM)
    win_pool = cache_win_kv.reshape(n_nsa * dec, cache_win_kv.shape[2] * N_KV_SLABS, HEAD_DIM)
    g_all = norm_gain.astype(F32)

    hp = rmsnorm_cast(xp, g_all[0, 0], PROMPT_TM)
    hs = rmsnorm_cast(xs, g_all[0, 0], dec)
    ret = tuple(jnp.zeros((n_nsa, bsz * seq * N_KV_SLABS, HEAD_DIM), F32) for _ in range(2))
    ret_shape = (n_nsa, bsz, seq, KV_GROUPS, 2, HEAD_DIM)
    out = {k: [] for k in ("cmp_s", "slc_s", "win_p", "win_s", "conv_p", "conv_s",
                           "re_p", "re_s", "im_p", "im_s", "ffn_p", "ffn_s")}
    counts = [0, 0, 0]
    for i in range(DEPTH):
        m = i % 3
        j = counts[m]
        counts[m] += 1
        g_post, g_ffn, g_ffn_post = g_all[i, 1], g_all[i, 2], g_all[i, 3]
        g_next = g_all[i + 1, 0] if i + 1 < DEPTH else g_all[i, 0]
        if m == 0:
            wg = jnp.pad(nsa_w_gate[j], ((0, 0), (0, LANES - nsa_w_gate.shape[-1]))).astype(BF16)
            wts = (nsa_w_q[j].astype(BF16), nsa_w_kv[j].astype(BF16), wg, nsa_w_o[j].astype(BF16),
                   jnp.swapaxes(nsa_cmp_pe[j], 0, 1).astype(F32), nsa_cmp_w1[j].astype(BF16),
                   nsa_cmp_w2[j].astype(BF16))
            xp, hp, xs, hs, kv, kvs, ret = _nsa_layer(hp, hs, xp, xs, g_post, g_ffn, wts, tabs,
                                                      (cmp_pool, slc_pool, win_pool, n_phys), ret, j, page_table,
                                                      bsz, seq)
            shp = (bsz, seq, KV_GROUPS, 2, HEAD_DIM)
            shs = (dec, 1, KV_GROUPS, 2, HEAD_DIM)
            out["cmp_s"].append(kvs[0].reshape(shs))
            out["slc_s"].append(kvs[1].reshape(shs))
            out["win_p"].append(kv[2].reshape(shp)[:, seq - min(WINDOW, seq):])
            win_full = jnp.concatenate([cache_win_kv[j], kvs[2].reshape(shs)], axis=1)
            out["win_s"].append(win_full[:, win_full.shape[1] - min(WINDOW, PAST_LEN + 1):])
        elif m == 1:
            wts = (conv_w_pw1[j].astype(BF16), conv_dw[j], conv_dw_b[j], conv_ln_g[j], conv_ln_b[j],
                   conv_w_pw2[j].astype(BF16))
            xp, hp, xs, hs, cp, cs = _conv_layer(hp, hs, xp, xs, g_post, g_ffn, wts, state_conv[j], bsz, seq)
            out["conv_p"].append(cp); out["conv_s"].append(cs)
        else:
            wts = (ssm_a_re[j], ssm_a_im[j], ssm_log_dt[j], ssm_b_re[j], ssm_b_im[j], ssm_c_re[j], ssm_c_im[j],
                   ssm_d[j], ssm_w_glu[j].astype(BF16))
            xp, hp, xs, hs, rp, ip, rs, is_ = _s5_layer(xp, xs, g_all[i, 0], g_post, g_ffn, wts,
                                                        state_ssm_re[j], state_ssm_im[j], bsz, seq)
            out["re_p"].append(rp); out["im_p"].append(ip); out["re_s"].append(rs); out["im_s"].append(is_)
        wts = (ffn_w_up, i, ffn_dw[i], ffn_dw_b[i], ffn_w_down[i].astype(BF16))
        xp, hp, xs, hs, fp, fs = _ffn_layer(hp, hs, xp, xs, g_ffn_post, g_next, wts, state_ffn_conv[i], bsz, seq)
        out["ffn_p"].append(fp); out["ffn_s"].append(fs)
    st = lambda k: jnp.stack(out[k])
    return (xp.reshape(bsz, seq, d), xs.reshape(dec, 1, d),
            ret[0].reshape(ret_shape), st("cmp_s"), ret[1].reshape(ret_shape), st("slc_s"), st("win_p"), st("win_s"),
            st("conv_p"), st("conv_s"), st("re_p"), st("re_s"), st("im_p"), st("im_s"),
            st("ffn_p"), st("ffn_s"))
```
